```python
import math
import jax, jax.numpy as jnp
from jax import lax
import numpy as np

D_MODEL = 1024
BATCH = 4
SEQ = 8192
DEPTH = 2

GRID_W = 64
CTX_LEN = 256
N_MIXERS = 2
ATT_HEADS = 8
ATT_HEAD_DIM = D_MODEL // ATT_HEADS // 2
ATT_V_DIM = 2 * ATT_HEAD_DIM
ROPE_THETA = 10000.0
Q_BLOCK = 128
SGU_CHUNK = 128
SGU_DFF = 2 * D_MODEL
SGU_GROUPS = 8
N_EXPERTS = 256
TOP_K = 8
N_GROUPS = 8
TOPK_GROUPS = 4
EXPERT_DFF = D_MODEL // 4
SHARED_DFF = EXPERT_DFF
ROUTED_SCALE = 2.5
MOE_BLOCK = 128
DEEPNORM_ALPHA = (2 * DEPTH) ** 0.25
DEEPNORM_BETA = (8 * DEPTH) ** -0.25
LN_EPS = 1e-5
N_ATTN_LAYERS = (DEPTH + 1) // 2
N_SGU_LAYERS = DEPTH // 2

kernel_name = "hybrid_diffattn_sgu_moe_deepnorm_prefix"


def _layer_norm(x, g, b):
    xf = x.astype(jnp.float32)
    mu = jnp.mean(xf, axis=-1, keepdims=True)
    var = jnp.mean(jnp.square(xf - mu), axis=-1, keepdims=True)
    return ((xf - mu) * lax.rsqrt(var + LN_EPS) * g.astype(jnp.float32) + b.astype(jnp.float32)).astype(x.dtype)


def _rms_norm(x, g):
    xf = x.astype(jnp.float32)
    return (xf * lax.rsqrt(jnp.mean(jnp.square(xf), axis=-1, keepdims=True) + LN_EPS) * g.astype(jnp.float32)).astype(x.dtype)


def _rope_axis(xh, pos):
    half = xh.shape[-1]
    inv = ROPE_THETA ** (-jnp.arange(0, half, 2, dtype=jnp.float32) / half)
    ang = pos.astype(jnp.float32)[:, None] * inv[None, :]
    ang = jnp.concatenate([ang, ang], axis=-1)[None, :, None, None, :]
    x1, x2 = jnp.split(xh.astype(jnp.float32), 2, axis=-1)
    rot = jnp.concatenate([-x2, x1], axis=-1)
    return (xh.astype(jnp.float32) * jnp.cos(ang) + rot * jnp.sin(ang)).astype(xh.dtype)


def _rope_2d(x, row_pos, col_pos):
    xr, xc = jnp.split(x, 2, axis=-1)
    return jnp.concatenate([_rope_axis(xr, row_pos), _rope_axis(xc, col_pos)], axis=-1)


def _diff_attn_core(q, k, v, lam):
    s = jnp.einsum('bqhmd,bkhmd->bhmqk', q, k, preferred_element_type=jnp.float32) * (ATT_HEAD_DIM ** -0.5)
    p = jax.nn.softmax(s, axis=-1)
    pd = p[:, :, 0] - lam * p[:, :, 1]
    o = jnp.einsum('bhqk,bkhe->bqhe', pd, v.astype(jnp.float32))
    return o.astype(v.dtype)


def _diff_attn_out(o, subln_g, lam_init, w_out):
    o = _rms_norm(o, subln_g) * (1.0 - lam_init)
    return o.reshape(o.shape[0], o.shape[1], D_MODEL) @ w_out


def _diff_attention(h, hc, w_in, w_out, lam_p, subln_g, lam_init, row_pos, col_pos, ctx_out):
    B, N, _ = h.shape
    C = hc.shape[1]
    lam_p = lam_p.astype(jnp.float32)
    lam = jnp.exp(jnp.sum(lam_p[0] * lam_p[1])) - jnp.exp(jnp.sum(lam_p[2] * lam_p[3])) + lam_init
    qkv = h @ w_in
    q = _rope_2d(qkv[..., :D_MODEL].reshape(B, N, ATT_HEADS, 2, ATT_HEAD_DIM), row_pos, col_pos)
    k = _rope_2d(qkv[..., D_MODEL:2 * D_MODEL].reshape(B, N, ATT_HEADS, 2, ATT_HEAD_DIM), row_pos, col_pos)
    v = qkv[..., 2 * D_MODEL:].reshape(B, N, ATT_HEADS, ATT_V_DIM)
    kvc = hc @ w_in[:, D_MODEL:]
    kc = kvc[..., :D_MODEL].reshape(B, C, ATT_HEADS, 2, ATT_HEAD_DIM)
    vc = kvc[..., D_MODEL:].reshape(B, C, ATT_HEADS, ATT_V_DIM)
    k_all = jnp.concatenate([k, kc], axis=1)
    v_all = jnp.concatenate([v, vc], axis=1)
    qb = q.reshape(B, N // Q_BLOCK, Q_BLOCK, ATT_HEADS, 2, ATT_HEAD_DIM).transpose(1, 0, 2, 3, 4, 5)
    ob = lax.map(lambda qi: _diff_attn_core(qi, k_all, v_all, lam), qb)
    o = ob.transpose(1, 0, 2, 3, 4).reshape(B, N, ATT_HEADS, ATT_V_DIM)
    y = _diff_attn_out(o, subln_g, lam_init, w_out)
    yc = None
    if ctx_out:
        qc = (hc @ w_in[:, :D_MODEL]).reshape(B, C, ATT_HEADS, 2, ATT_HEAD_DIM)
        yc = _diff_attn_out(_diff_attn_core(qc, kc, vc, lam), subln_g, lam_init, w_out)
    return y, yc


def _sgu_mixer(h, w_in, b_in, norm_g, norm_b, w_s, b_s, w_out):
    B, N, _ = h.shape
    z = jax.nn.gelu(h @ w_in + b_in, approximate=False)
    u, v = jnp.split(z, 2, axis=-1)
    v = _layer_norm(v, norm_g, norm_b)
    vc = v.reshape(B, N // SGU_CHUNK, SGU_CHUNK, SGU_GROUPS, SGU_DFF // SGU_GROUPS)
    vm = jnp.einsum('gpq,bnqgc->bnpgc', w_s, vc) + b_s.T[:, :, None]
    return (u * vm.reshape(B, N, SGU_DFF)) @ w_out


def _route(xt, w_r, r_bias):
    T = xt.shape[0]
    scores = jax.nn.sigmoid(xt.astype(jnp.float32) @ w_r.astype(jnp.float32))
    choice = scores + r_bias.astype(jnp.float32)
    grp = choice.reshape(T, N_GROUPS, N_EXPERTS // N_GROUPS)
    gscore = jnp.sum(lax.top_k(grp, 2)[0], axis=-1)
    _, gidx = lax.top_k(gscore, TOPK_GROUPS)
    gmask = jnp.any(jax.nn.one_hot(gidx, N_GROUPS, dtype=jnp.bool_), axis=1)
    emask = jnp.repeat(gmask, N_EXPERTS // N_GROUPS, axis=1)
    _, idx = lax.top_k(jnp.where(emask, choice, -jnp.inf), TOP_K)
    w = jnp.take_along_axis(scores, idx, axis=1)
    w = w / jnp.sum(w, axis=-1, keepdims=True) * ROUTED_SCALE
    return idx.astype(jnp.int32), w


def _moe(xt, w_r, r_bias, wg, wu, wd, sg, su, sd):
    T = xt.shape[0]
    idx, w = _route(xt, w_r, r_bias)
    TK = T * TOP_K
    e = idx.reshape(-1)
    tok = jnp.arange(TK, dtype=jnp.int32) // TOP_K
    wf = w.reshape(-1)
    order = jnp.argsort(e)
    se = e[order]
    counts = jnp.bincount(e, length=N_EXPERTS).astype(jnp.int32)
    starts = jnp.cumsum(counts) - counts
    padded = (counts + MOE_BLOCK - 1) // MOE_BLOCK * MOE_BLOCK
    pends = jnp.cumsum(padded)
    pstarts = pends - padded
    dest = pstarts[se] + jnp.arange(TK, dtype=jnp.int32) - starts[se]
    n_blocks = -(-TK // MOE_BLOCK) + N_EXPERTS
    P = n_blocks * MOE_BLOCK
    slot_tok = jnp.zeros((P,), jnp.int32).at[dest].set(tok[order])
    slot_w = jnp.zeros((P,), jnp.float32).at[dest].set(wf[order])
    block_e = jnp.clip(jnp.searchsorted(pends, jnp.arange(n_blocks, dtype=jnp.int32) * MOE_BLOCK, side='right'), 0, N_EXPERTS - 1)

    def step(acc, blk):
        bt, bw, be = blk
        xb = xt[bt]
        hb = jax.nn.silu(xb @ wg[be]) * (xb @ wu[be])
        yb = (hb @ wd[be]) * bw[:, None]
        return acc.at[bt].add(yb.astype(acc.dtype)), None

    routed, _ = lax.scan(step, jnp.zeros_like(xt),
                         (slot_tok.reshape(n_blocks, MOE_BLOCK), slot_w.reshape(n_blocks, MOE_BLOCK), block_e))
    shared = (jax.nn.silu(xt @ sg) * (xt @ su)) @ sd
    return routed + shared


def setup_inputs(seed: int = 0) -> dict:
    key = jax.random.key(seed)
    ks = jax.random.split(key, 32)
    f32 = jnp.float32
    D, dh, dv = D_MODEL, ATT_HEAD_DIM, ATT_V_DIM
    nrm = lambda k, shape, s: jax.random.normal(k, shape, f32) * s
    beta = DEEPNORM_BETA
    return {
        'x': nrm(ks[0], (BATCH, SEQ, D), 1.0),
        'c': nrm(ks[1], (BATCH, D), 1.0),
        'ctx': nrm(ks[2], (BATCH, CTX_LEN, D), 1.0),
        'c_ctx': nrm(ks[3], (D,), 0.5),
        'w_mod': nrm(ks[4], (DEPTH, D, 6 * D), 0.5 * D ** -0.5),
        'b_mod': nrm(ks[5], (DEPTH, 6 * D), 0.02),
        'ln_g': 1.0 + nrm(ks[6], (DEPTH, 2, D), 0.02),
        'ln_b': nrm(ks[7], (DEPTH, 2, D), 0.02),
        'attn_w_in': nrm(ks[8], (N_ATTN_LAYERS, D, 3 * D), D ** -0.5),
        'attn_w_out': nrm(ks[9], (N_ATTN_LAYERS, D, D), beta * D ** -0.5),
        'attn_lambda': nrm(ks[10], (N_ATTN_LAYERS, 4, dh), 0.1),
        'attn_subln_g': 1.0 + nrm(ks[11], (N_ATTN_LAYERS, dv), 0.02),
        'sgu_w_in': nrm(ks[12], (N_SGU_LAYERS, D, 2 * SGU_DFF), D ** -0.5),
        'sgu_b_in': nrm(ks[13], (N_SGU_LAYERS, 2 * SGU_DFF), 0.02),
        'sgu_norm_g': 1.0 + nrm(ks[14], (N_SGU_LAYERS, SGU_DFF), 0.02),
        'sgu_norm_b': nrm(ks[15], (N_SGU_LAYERS, SGU_DFF), 0.02),
        'sgu_w_s': nrm(ks[16], (N_SGU_LAYERS, SGU_GROUPS, SGU_CHUNK, SGU_CHUNK), SGU_CHUNK ** -0.5),
        'sgu_b_s': 1.0 + nrm(ks[17], (N_SGU_LAYERS, SGU_GROUPS, SGU_CHUNK), 0.02),
        'sgu_w_out': nrm(ks[18], (N_SGU_LAYERS, SGU_DFF, D), beta * SGU_DFF ** -0.5),
        'router_w': nrm(ks[19], (DEPTH, D, N_EXPERTS), D ** -0.5),
        'router_bias': nrm(ks[20], (DEPTH, N_EXPERTS), 0.01),
        'exp_w_gate': nrm(ks[21], (DEPTH, N_EXPERTS, D, EXPERT_DFF), D ** -0.5),
        'exp_w_up': nrm(ks[22], (DEPTH, N_EXPERTS, D, EXPERT_DFF), D ** -0.5),
        'exp_w_down': nrm(ks[23], (DEPTH, N_EXPERTS, EXPERT_DFF, D), beta * EXPERT_DFF ** -0.5),
        'sh_w_gate': nrm(ks[24], (DEPTH, D, SHARED_DFF), D ** -0.5),
        'sh_w_up': nrm(ks[25], (DEPTH, D, SHARED_DFF), D ** -0.5),
        'sh_w_down': nrm(ks[26], (DEPTH, SHARED_DFF, D), beta * SHARED_DFF ** -0.5),
    }


def reference(x, c, ctx, c_ctx, w_mod, b_mod, ln_g, ln_b, attn_w_in, attn_w_out, attn_lambda, attn_subln_g,
              sgu_w_in, sgu_b_in, sgu_norm_g, sgu_norm_b, sgu_w_s, sgu_b_s, sgu_w_out,
              router_w, router_bias, exp_w_gate, exp_w_up, exp_w_down, sh_w_gate, sh_w_up, sh_w_down):
    B, N, D = x.shape
    C = ctx.shape[1]
    ROWS = N // GRID_W
    row_pos = jnp.repeat(jnp.arange(ROWS, dtype=jnp.int32), GRID_W, total_repeat_length=ROWS * GRID_W)
    col_pos = jnp.tile(jnp.arange(GRID_W, dtype=jnp.int32), ROWS)
    last_ctx_read = max(i for i in range(DEPTH) if i % N_MIXERS == 0)
    s_c = jax.nn.silu(c)
    s_cc = jax.nn.silu(c_ctx)
    xc = ctx
    for i in range(DEPTH):
        mod = (s_c @ w_mod[i] + b_mod[i])[:, None, :]
        sh_m, sc_m, g_m, sh_f, sc_f, g_f = jnp.split(mod, 6, axis=-1)
        modc = s_cc @ w_mod[i] + b_mod[i]
        shc_m, scc_m, gc_m, shc_f, scc_f, gc_f = jnp.split(modc, 6, axis=-1)
        is_attn = (i % N_MIXERS == 0)
        ctx_update = i < last_ctx_read
        h = x * (1.0 + sc_m) + sh_m
        if is_attn:
            a = i // N_MIXERS
            lam_init = 0.8 - 0.6 * math.exp(-0.3 * i)
            hc = xc * (1.0 + scc_m) + shc_m
            y, yc = _diff_attention(h, hc, attn_w_in[a], attn_w_out[a], attn_lambda[a], attn_subln_g[a],
                                    lam_init, row_pos, col_pos, ctx_update)
        else:
            s = i // N_MIXERS
            sgu_p = (sgu_w_in[s], sgu_b_in[s], sgu_norm_g[s], sgu_norm_b[s], sgu_w_s[s], sgu_b_s[s], sgu_w_out[s])
            y = _sgu_mixer(h, *sgu_p)
            if ctx_update:
                yc = _sgu_mixer(xc * (1.0 + scc_m) + shc_m, *sgu_p)
        x = _layer_norm(DEEPNORM_ALPHA * x + g_m * y, ln_g[i, 0], ln_b[i, 0])
        moe_p = (router_w[i], router_bias[i], exp_w_gate[i], exp_w_up[i], exp_w_down[i],
                 sh_w_gate[i], sh_w_up[i], sh_w_down[i])
        h = (x * (1.0 + sc_f) + sh_f).reshape(B * N, D)
        if ctx_update:
            xc = _layer_norm(DEEPNORM_ALPHA * xc + gc_m * yc, ln_g[i, 0], ln_b[i, 0])
            hc = (xc * (1.0 + scc_f) + shc_f).reshape(B * C, D)
            yf_all = _moe(jnp.concatenate([h, hc], axis=0), *moe_p)
            yf = yf_all[:B * N].reshape(B, N, D)
            yfc = yf_all[B * N:].reshape(B, C, D)
            xc = _layer_norm(DEEPNORM_ALPHA * xc + gc_f * yfc, ln_g[i, 1], ln_b[i, 1])
        else:
            yf = _moe(h, *moe_p).reshape(B, N, D)
        x = _layer_norm(DEEPNORM_ALPHA * x + g_f * yf, ln_g[i, 1], ln_b[i, 1])
    return x
```

```python
import functools
import math

import jax
import jax.numpy as jnp
from jax import lax
from jax.experimental import pallas as pl
from jax.experimental.pallas import tpu as pltpu

F32 = jnp.float32
BF16 = jnp.bfloat16

GRID_W = 64
ATT_HEADS = 8
ROPE_THETA = 10000.0
SGU_CHUNK = 128
SGU_GROUPS = 8
TOP_K = 8
N_GROUPS = 8
TOPK_GROUPS = 4
ROUTED_SCALE = 2.5
LN_EPS = 1e-5
N_MIXERS = 2

LANES = 128
MOE_ROWS = 256
GATHER_ROWS = 256
VMEM_LIMIT = 56 * 1024 * 1024
NEG_INF = float("-inf")


def _cparams(*sem):
    return pltpu.CompilerParams(dimension_semantics=sem, vmem_limit_bytes=VMEM_LIMIT)


def _layer_norm(z, g, b):
    mu = jnp.mean(z, axis=-1, keepdims=True)
    zc = z - mu
    var = jnp.mean(zc * zc, axis=-1, keepdims=True)
    return zc * lax.rsqrt(var + LN_EPS) * g + b


def _silu(x):
    return x * (1.0 / (1.0 + jnp.exp(-x)))


def _mod_kernel(cs_ref, w_ref, b_ref, o_ref):
    s = _silu(cs_ref[...])
    o_ref[0] = jnp.dot(s, w_ref[0], precision=lax.Precision.HIGHEST,
                       preferred_element_type=F32) + b_ref[0]


def _modulation(cs, w_mod, b_mod):
    depth, d, n6 = w_mod.shape
    tn = n6 // 4
    return pl.pallas_call(
        _mod_kernel,
        out_shape=jax.ShapeDtypeStruct((depth, 8, n6), F32),
        grid=(depth, n6 // tn),
        in_specs=[
            pl.BlockSpec((8, d), lambda l, j: (0, 0)),
            pl.BlockSpec((1, d, tn), lambda l, j: (l, 0, j)),
            pl.BlockSpec((1, 1, tn), lambda l, j: (l, 0, j)),
        ],
        out_specs=pl.BlockSpec((1, 8, tn), lambda l, j: (l, 0, j)),
        compiler_params=_cparams("parallel", "parallel"),
        name="modulation",
    )(cs, w_mod, b_mod.reshape(depth, 1, n6))


def _rope(xh, cos, sin_lo, sin_hi):
    return xh * cos + pltpu.roll(xh, LANES - 16, 1) * sin_lo + pltpu.roll(xh, 16, 1) * sin_hi


def _qkv_kernel(x_ref, sc_ref, sh_ref, w_ref, cos_ref, slo_ref, shi_ref, q_ref, k_ref, v_ref, *, q_scale):
    d = x_ref.shape[-1]
    h = (x_ref[0] * (1.0 + sc_ref[0]) + sh_ref[0]).astype(BF16)
    cos, slo, shi = cos_ref[...], slo_ref[...], shi_ref[...]
    q = jnp.dot(h, w_ref[:, 0:d], preferred_element_type=F32)
    for hd in range(ATT_HEADS):
        q_ref[0, hd] = (_rope(q[:, hd * LANES:(hd + 1) * LANES], cos, slo, shi) * q_scale).astype(BF16)
    k = jnp.dot(h, w_ref[:, d:2 * d], preferred_element_type=F32)
    for hd in range(ATT_HEADS):
        k_ref[0, hd] = _rope(k[:, hd * LANES:(hd + 1) * LANES], cos, slo, shi).astype(BF16)
    v = jnp.dot(h, w_ref[:, 2 * d:3 * d], preferred_element_type=F32)
    for hd in range(ATT_HEADS):
        v_ref[0, hd] = v[:, hd * LANES:(hd + 1) * LANES].astype(BF16)


def _qkv_proj(x, sc, sh, w_bf, cos, slo, shi, q_scale, tn):
    b, n, d = x.shape
    hd_shape = jax.ShapeDtypeStruct((b, ATT_HEADS, n, LANES), BF16)
    vec = pl.BlockSpec((1, 1, d), lambda bi, i: (bi, 0, 0))
    tab = pl.BlockSpec((tn, LANES), lambda bi, i: (i, 0))
    out = pl.BlockSpec((1, ATT_HEADS, tn, LANES), lambda bi, i: (bi, 0, i, 0))
    return pl.pallas_call(
        functools.partial(_qkv_kernel, q_scale=q_scale),
        out_shape=(hd_shape, hd_shape, hd_shape),
        grid=(b, n // tn),
        in_specs=[
            pl.BlockSpec((1, tn, d), lambda bi, i: (bi, i, 0)),
            vec, vec,
            pl.BlockSpec((d, 3 * d), lambda bi, i: (0, 0)),
            tab, tab, tab,
        ],
        out_specs=(out, out, out),
        compiler_params=_cparams("parallel", "parallel"),
        name="qkv_proj",
    )(x, sc, sh, w_bf, cos, slo, shi)


def _kv_ctx_kernel(x_ref, sc_ref, sh_ref, w_ref, k_ref, v_ref):
    d = x_ref.shape[-1]
    h = (x_ref[0] * (1.0 + sc_ref[0]) + sh_ref[0]).astype(BF16)
    k = jnp.dot(h, w_ref[:, 0:d], preferred_element_type=F32)
    v = jnp.dot(h, w_ref[:, d:2 * d], preferred_element_type=F32)
    for hd in range(ATT_HEADS):
        k_ref[0, hd] = k[:, hd * LANES:(hd + 1) * LANES].astype(BF16)
        v_ref[0, hd] = v[:, hd * LANES:(hd + 1) * LANES].astype(BF16)


def _kv_ctx_proj(ctx, sc, sh, w_kv_bf):
    b, c, d = ctx.shape
    hd_shape = jax.ShapeDtypeStruct((b, ATT_HEADS, c, LANES), BF16)
    vec = pl.BlockSpec((1, 1, d), lambda bi: (0, 0, 0))
    out = pl.BlockSpec((1, ATT_HEADS, c, LANES), lambda bi: (bi, 0, 0, 0))
    return pl.pallas_call(
        _kv_ctx_kernel,
        out_shape=(hd_shape, hd_shape),
        grid=(b,),
        in_specs=[
            pl.BlockSpec((1, c, d), lambda bi: (bi, 0, 0)),
            vec, vec,
            pl.BlockSpec((d, 2 * d), lambda bi: (0, 0)),
        ],
        out_specs=(out, out),
        compiler_params=_cparams("parallel"),
        name="kv_ctx_proj",
    )(ctx, sc, sh, w_kv_bf)


def _attn_kernel(q_ref, k_ref, v_ref, lamp_ref, g_ref, o_ref, *, tk, lam_init):
    q = q_ref[0, 0]
    tq = q.shape[0]
    half = LANES // 2
    lane = lax.broadcasted_iota(jnp.int32, q.shape, 1)
    zero = jnp.zeros_like(q)
    qs = (jnp.where(lane < half, q, zero), jnp.where(lane >= half, q, zero))
    nk = k_ref.shape[2] // tk

    def body(j, carry):
        start = pl.multiple_of(j * tk, tk)
        kc = k_ref[0, 0, pl.ds(start, tk), :]
        vc = v_ref[0, 0, pl.ds(start, tk), :]
        new = []
        for mp in range(2):
            m, l, acc = carry[mp]
            s = lax.dot_general(qs[mp], kc, (((1,), (1,)), ((), ())), preferred_element_type=F32)
            mn = jnp.maximum(m, jnp.max(s, axis=-1, keepdims=True))
            a = jnp.exp2(m - mn)
            p = jnp.exp2(s - mn)
            l = a * l + jnp.sum(p, axis=-1, keepdims=True)
            acc = a * acc + jnp.dot(p.astype(BF16), vc, preferred_element_type=F32)
            new.append((mn, l, acc))
        return tuple(new)

    init = tuple((jnp.full((tq, 1), NEG_INF, F32), jnp.zeros((tq, 1), F32), jnp.zeros((tq, LANES), F32))
                 for _ in range(2))
    (m1, l1, a1), (m2, l2, a2) = lax.fori_loop(0, nk, body, init)

    lp = lamp_ref[...]
    lam = (jnp.exp(jnp.sum(lp[0:1] * lp[1:2], axis=-1, keepdims=True))
           - jnp.exp(jnp.sum(lp[2:3] * lp[3:4], axis=-1, keepdims=True)) + lam_init)
    o = a1 / l1 - lam * (a2 / l2)
    o = o * lax.rsqrt(jnp.mean(o * o, axis=-1, keepdims=True) + LN_EPS) * g_ref[...] * (1.0 - lam_init)
    o_ref[0] = o.astype(BF16)


def _diff_attention(q, k, v, lam_p, subln_g, lam_init, tq, tk):
    b, h, n, _ = q.shape
    nk = k.shape[2]
    return pl.pallas_call(
        functools.partial(_attn_kernel, tk=tk, lam_init=lam_init),
        out_shape=jax.ShapeDtypeStruct((b, n, h * LANES), BF16),
        grid=(b, h, n // tq),
        in_specs=[
            pl.BlockSpec((1, 1, tq, LANES), lambda bi, hi, i: (bi, hi, i, 0)),
            pl.BlockSpec((1, 1, nk, LANES), lambda bi, hi, i: (bi, hi, 0, 0)),
            pl.BlockSpec((1, 1, nk, LANES), lambda bi, hi, i: (bi, hi, 0, 0)),
            pl.BlockSpec(lam_p.shape, lambda bi, hi, i: (0, 0)),
            pl.BlockSpec((1, LANES), lambda bi, hi, i: (0, 0)),
        ],
        out_specs=pl.BlockSpec((1, tq, LANES), lambda bi, hi, i: (bi, i, hi)),
        compiler_params=_cparams("parallel", "parallel", "parallel"),
        name="diff_attention",
    )(q, k, v, lam_p, subln_g.reshape(1, LANES))


def _sgu_kernel(x_ref, sc_ref, sh_ref, w_ref, b_ref, ng_ref, nb_ref, ws_ref, bs_ref, t_ref):
    f = t_ref.shape[-1]
    cg = f // SGU_GROUPS
    tm = x_ref.shape[1]
    h = (x_ref[0] * (1.0 + sc_ref[0]) + sh_ref[0]).astype(BF16)
    z = jnp.dot(h, w_ref[...], preferred_element_type=F32) + b_ref[...]
    z = 0.5 * z * (1.0 + lax.erf(z * (2.0 ** -0.5)))
    u = z[:, :f]
    v = _layer_norm(z[:, f:], ng_ref[...], nb_ref[...]).astype(BF16)
    for c in range(tm // SGU_CHUNK):
        rows = slice(c * SGU_CHUNK, (c + 1) * SGU_CHUNK)
        for g in range(SGU_GROUPS):
            cols = slice(g * cg, (g + 1) * cg)
            vm = jnp.dot(ws_ref[g], v[rows, cols], preferred_element_type=F32) + bs_ref[:, g:g + 1]
            t_ref[0, rows, cols] = (u[rows, cols] * vm).astype(BF16)


def _sgu_mixer(x, sc, sh, w_in_bf, b_in, norm_g, norm_b, w_s_bf, b_s_t, tm):
    b, n, d = x.shape
    f2 = w_in_bf.shape[1]
    f = f2 // 2
    vec = pl.BlockSpec((1, 1, d), lambda bi, i: (bi, 0, 0))
    full2 = lambda a: pl.BlockSpec(a.shape, lambda bi, i: (0,) * a.ndim)
    b_in2, ng2, nb2 = b_in.reshape(1, f2), norm_g.reshape(1, f), norm_b.reshape(1, f)
    return pl.pallas_call(
        _sgu_kernel,
        out_shape=jax.ShapeDtypeStruct((b, n, f), BF16),
        grid=(b, n // tm),
        in_specs=[
            pl.BlockSpec((1, tm, d), lambda bi, i: (bi, i, 0)),
            vec, vec,
            full2(w_in_bf), full2(b_in2), full2(ng2), full2(nb2), full2(w_s_bf), full2(b_s_t),
        ],
        out_specs=pl.BlockSpec((1, tm, f), lambda bi, i: (bi, i, 0)),
        compiler_params=_cparams("parallel", "parallel"),
        name="sgu_mixer",
    )(x, sc, sh, w_in_bf, b_in2, ng2, nb2, w_s_bf, b_s_t)


def _post_kernel(pre_ref, w_ref, x_ref, gm_ref, lg_ref, lb_ref, scf_ref, shf_ref, x1_ref, hf_ref, *, alpha):
    y = jnp.dot(pre_ref[0], w_ref[...], preferred_element_type=F32)
    x1 = _layer_norm(alpha * x_ref[0] + gm_ref[0] * y, lg_ref[...], lb_ref[...])
    x1_ref[0] = x1
    hf_ref[0] = (x1 * (1.0 + scf_ref[0]) + shf_ref[0]).astype(BF16)


def _post_mixer(pre, w_bf, x, gm, ln_g, ln_b, scf, shf, alpha, tm):
    b, n, d = x.shape
    kd = pre.shape[-1]
    vec = pl.BlockSpec((1, 1, d), lambda bi, i: (bi, 0, 0))
    row = pl.BlockSpec((1, d), lambda bi, i: (0, 0))
    tile = pl.BlockSpec((1, tm, d), lambda bi, i: (bi, i, 0))
    return pl.pallas_call(
        functools.partial(_post_kernel, alpha=alpha),
        out_shape=(jax.ShapeDtypeStruct((b, n, d), F32), jax.ShapeDtypeStruct((b, n, d), BF16)),
        grid=(b, n // tm),
        in_specs=[
            pl.BlockSpec((1, tm, kd), lambda bi, i: (bi, i, 0)),
            pl.BlockSpec((kd, d), lambda bi, i: (0, 0)),
            tile, vec, row, row, vec, vec,
        ],
        out_specs=(tile, tile),
        compiler_params=_cparams("parallel", "parallel"),
        name="post_mixer",
    )(pre, w_bf, x, gm, ln_g.reshape(1, d), ln_b.reshape(1, d), scf, shf)


def _route_kernel(x_ref, sc_ref, sh_ref, wr_ref, rb_ref, idx_ref, w_ref, rank_ref, cnt_ref, carry_ref):
    i = pl.program_id(0)
    e = wr_ref.shape[0]
    tm = x_ref.shape[0]
    ge = e // N_GROUPS

    @pl.when(i == 0)
    def _():
        carry_ref[...] = jnp.zeros_like(carry_ref)

    h = x_ref[...] * (1.0 + sc_ref[0]) + sh_ref[0]
    logits = lax.dot_general(wr_ref[...], h, (((1,), (1,)), ((), ())),
                             precision=lax.Precision.HIGHEST, preferred_element_type=F32)
    scores = 1.0 / (1.0 + jnp.exp(-logits))
    choice = scores + rb_ref[...]

    g3 = choice.reshape(N_GROUPS, ge, tm)
    ri = lax.broadcasted_iota(jnp.int32, g3.shape, 1).astype(F32)
    m1 = jnp.max(g3, axis=1, keepdims=True)
    first = jnp.min(jnp.where(g3 == m1, ri, float(ge)), axis=1, keepdims=True)
    m2 = jnp.max(jnp.where(ri == first, NEG_INF, g3), axis=1, keepdims=True)
    gs = m1 + m2

    gi = lax.broadcasted_iota(jnp.int32, gs.shape, 0).astype(F32)
    gsel = jnp.zeros(gs.shape, F32)
    cur = gs
    for _ in range(TOPK_GROUPS):
        m = jnp.max(cur, axis=0, keepdims=True)
        f = jnp.min(jnp.where(cur == m, gi, float(N_GROUPS)), axis=0, keepdims=True)
        hit = gi == f
        gsel = jnp.where(hit, 1.0, gsel)
        cur = jnp.where(hit, NEG_INF, cur)
    emask = jnp.broadcast_to(gsel, g3.shape).reshape(e, tm)
    masked = jnp.where(emask > 0.5, choice, NEG_INF)

    ei = lax.broadcasted_iota(jnp.int32, (e, tm), 0).astype(F32)
    onehot = jnp.zeros((e, tm), F32)
    idxs, ws = [], []
    for _ in range(TOP_K):
        m = jnp.max(masked, axis=0, keepdims=True)
        f = jnp.min(jnp.where(masked == m, ei, float(e)), axis=0, keepdims=True)
        hit = ei == f
        idxs.append(f)
        ws.append(jnp.sum(jnp.where(hit, scores, 0.0), axis=0, keepdims=True))
        masked = jnp.where(hit, NEG_INF, masked)
        onehot = jnp.where(hit, 1.0, onehot)
    wsum = ws[0]
    for k in range(1, TOP_K):
        wsum = wsum + ws[k]

    r_i = lax.broadcasted_iota(jnp.int32, (tm, tm), 0)
    c_i = lax.broadcasted_iota(jnp.int32, (tm, tm), 1)
    upper = jnp.where(r_i < c_i, 1.0, 0.0).astype(BF16)
    rk = jnp.dot(onehot.astype(BF16), upper, preferred_element_type=F32) + carry_ref[...]
    carry_ref[...] += jnp.sum(onehot, axis=1, keepdims=True)

    for k in range(TOP_K):
        idx_ref[k:k + 1, :] = idxs[k].astype(jnp.int32)
        w_ref[k:k + 1, :] = ws[k] / wsum * ROUTED_SCALE
        rank_ref[k:k + 1, :] = jnp.sum(jnp.where(ei == idxs[k], rk, 0.0), axis=0, keepdims=True).astype(jnp.int32)
    cnt_ref[...] = jnp.broadcast_to(carry_ref[...], cnt_ref.shape).astype(jnp.int32)


def _route(x1, scf, shf, wr_t, rbias, tm):
    b, n, d = x1.shape
    t = b * n
    e = wr_t.shape[0]
    per_b = n // tm
    vec = pl.BlockSpec((1, 1, d), lambda i: (i // per_b, 0, 0))
    out_t = pl.BlockSpec((TOP_K, tm), lambda i: (0, i))
    return pl.pallas_call(
        _route_kernel,
        out_shape=(jax.ShapeDtypeStruct((TOP_K, t), jnp.int32), jax.ShapeDtypeStruct((TOP_K, t), F32),
                   jax.ShapeDtypeStruct((TOP_K, t), jnp.int32), jax.ShapeDtypeStruct((e, LANES), jnp.int32)),
        grid=(t // tm,),
        in_specs=[
            pl.BlockSpec((tm, d), lambda i: (i, 0)),
            vec, vec,
            pl.BlockSpec((e, d), lambda i: (0, 0)),
            pl.BlockSpec((e, 1), lambda i: (0, 0)),
        ],
        out_specs=(out_t, out_t, out_t, pl.BlockSpec((e, LANES), lambda i: (0, 0))),
        scratch_shapes=[pltpu.VMEM((e, 1), F32)],
        compiler_params=_cparams("arbitrary"),
        name="route",
    )(x1.reshape(t, d), scf, shf, wr_t, rbias.reshape(e, 1))


def _gather_kernel(idx_ref, src_ref, out_ref, sem):
    g = idx_ref.shape[-1]
    base = pl.program_id(0) * g

    def issue(r, c):
        pltpu.make_async_copy(src_ref.at[idx_ref[0, 0, r]], out_ref.at[base + r], sem).start()
        return c

    lax.fori_loop(0, g, issue, 0)

    def drain(r, c):
        pltpu.make_async_copy(src_ref.at[0], out_ref.at[base + r], sem).wait()
        return c

    lax.fori_loop(0, g, drain, 0)


def _gather_rows(src, idx):
    r, d = src.shape
    m = idx.shape[0]
    g = GATHER_ROWS
    src3 = src.reshape(r, d // LANES, LANES)
    out = pl.pallas_call(
        _gather_kernel,
        out_shape=jax.ShapeDtypeStruct((m, d // LANES, LANES), src.dtype),
        grid=(m // g,),
        in_specs=[
            pl.BlockSpec((1, 1, g), lambda i: (i, 0, 0), memory_space=pltpu.SMEM),
            pl.BlockSpec(memory_space=pl.ANY),
        ],
        out_specs=pl.BlockSpec(memory_space=pl.ANY),
        scratch_shapes=[pltpu.SemaphoreType.DMA],
        compiler_params=pltpu.CompilerParams(dimension_semantics=("arbitrary",), has_side_effects=True),
        name="gather_rows",
    )(idx.reshape(m // g, 1, g), src3)
    return out.reshape(m, d)


def _experts_kernel(be_ref, nu_ref, xs_ref, wg_ref, wu_ref, wd_ref, y_ref):
    i = pl.program_id(0)

    @pl.when(i < nu_ref[0])
    def _():
        x = xs_ref[...]
        g = jnp.dot(x, wg_ref[0].astype(BF16), preferred_element_type=F32)
        u = jnp.dot(x, wu_ref[0].astype(BF16), preferred_element_type=F32)
        hb = (_silu(g) * u).astype(BF16)
        y_ref[...] = jnp.dot(hb, wd_ref[0].astype(BF16), preferred_element_type=F32).astype(y_ref.dtype)

    @pl.when(i >= nu_ref[0])
    def _():
        y_ref[...] = jnp.zeros_like(y_ref)


def _routed_experts(xs, wg, wu, wd, block_e, n_used):
    p, d = xs.shape
    e, _, f = wg.shape
    nb = p // MOE_ROWS
    grid_spec = pltpu.PrefetchScalarGridSpec(
        num_scalar_prefetch=2,
        grid=(nb,),
        in_specs=[
            pl.BlockSpec((MOE_ROWS, d), lambda i, be, nu: (i, 0)),
            pl.BlockSpec((1, d, f), lambda i, be, nu: (be[i], 0, 0)),
            pl.BlockSpec((1, d, f), lambda i, be, nu: (be[i], 0, 0)),
            pl.BlockSpec((1, f, d), lambda i, be, nu: (be[i], 0, 0)),
        ],
        out_specs=pl.BlockSpec((MOE_ROWS, d), lambda i, be, nu: (i, 0)),
    )
    return pl.pallas_call(
        _experts_kernel,
        out_shape=jax.ShapeDtypeStruct((p, d), BF16),
        grid_spec=grid_spec,
        compiler_params=_cparams("arbitrary"),
        name="routed_experts",
    )(block_e, n_used, xs, wg, wu, wd)


def _combine_kernel(yg_ref, w_ref, hf_ref, sg_ref, su_ref, sd_ref, x_ref, gf_ref, lg_ref, lb_ref, o_ref, *, alpha):
    w = w_ref[...]
    routed = w[:, 0:1] * yg_ref[0].astype(F32)
    for k in range(1, TOP_K):
        routed = routed + w[:, k:k + 1] * yg_ref[k].astype(F32)
    hf = hf_ref[...]
    g = jnp.dot(hf, sg_ref[...], preferred_element_type=F32)
    u = jnp.dot(hf, su_ref[...], preferred_element_type=F32)
    shared = jnp.dot((_silu(g) * u).astype(BF16), sd_ref[...], preferred_element_type=F32)
    o_ref[...] = _layer_norm(alpha * x_ref[...] + gf_ref[0] * (routed + shared), lg_ref[...], lb_ref[...])


def _combine(yg, w_tk, hf, sg_bf, su_bf, sd_bf, x1, gf, ln_g, ln_b, alpha, tm, per_b):
    t, d = x1.shape
    f = sg_bf.shape[1]
    row = pl.BlockSpec((1, d), lambda i: (0, 0))
    tile = pl.BlockSpec((tm, d), lambda i: (i, 0))
    return pl.pallas_call(
        functools.partial(_combine_kernel, alpha=alpha),
        out_shape=jax.ShapeDtypeStruct((t, d), F32),
        grid=(t // tm,),
        in_specs=[
            pl.BlockSpec((TOP_K, tm, d), lambda i: (0, i, 0)),
            pl.BlockSpec((tm, TOP_K), lambda i: (i, 0)),
            tile,
            pl.BlockSpec((d, f), lambda i: (0, 0)),
            pl.BlockSpec((d, f), lambda i: (0, 0)),
            pl.BlockSpec((f, d), lambda i: (0, 0)),
            tile,
            pl.BlockSpec((1, 1, d), lambda i: (i // per_b, 0, 0)),
            row, row,
        ],
        out_specs=tile,
        compiler_params=_cparams("parallel"),
        name="moe_combine",
    )(yg, w_tk, hf, sg_bf, su_bf, sd_bf, x1, gf, ln_g.reshape(1, d), ln_b.reshape(1, d))


def _moe_layer(x1, hf, scf, shf, gf, router_w, router_bias, wg, wu, wd, sg, su, sd, ln_g, ln_b, alpha, tm):
    b, n, d = x1.shape
    t = b * n
    e = router_w.shape[1]
    idx_t, w_t, rank_t, cnt = _route(x1, scf, shf, router_w.T, router_bias, tm)

    counts = cnt[:, 0]
    padded = (counts + MOE_ROWS - 1) // MOE_ROWS * MOE_ROWS
    pends = jnp.cumsum(padded)
    pstarts = pends - padded
    pos_t = jnp.take(pstarts, idx_t) + rank_t
    n_blocks = t * TOP_K // MOE_ROWS + e
    p = n_blocks * MOE_ROWS
    tok = jnp.broadcast_to(jnp.arange(t, dtype=jnp.int32)[None, :], (TOP_K, t))
    slot_tok = jnp.zeros((p,), jnp.int32).at[pos_t.reshape(-1)].set(tok.reshape(-1))
    block_e = jnp.clip(jnp.searchsorted(pends, jnp.arange(n_blocks, dtype=jnp.int32) * MOE_ROWS, side="right"),
                       0, e - 1).astype(jnp.int32)
    n_used = (pends[-1:] // MOE_ROWS).astype(jnp.int32)

    hf2 = hf.reshape(t, d)
    xs = _gather_rows(hf2, slot_tok)
    yb = _routed_experts(xs, wg, wu, wd, block_e, n_used)
    yg = _gather_rows(yb, pos_t.reshape(-1)).reshape(TOP_K, t, d)
    out = _combine(yg, w_t.T, hf2, sg.astype(BF16), su.astype(BF16), sd.astype(BF16), x1.reshape(t, d), gf,
                   ln_g, ln_b, alpha, tm, n // tm)
    return out.reshape(b, n, d)


def _rope_tables(n):
    rows = n // GRID_W
    row_pos = jnp.repeat(jnp.arange(rows, dtype=F32), GRID_W)
    col_pos = jnp.tile(jnp.arange(GRID_W, dtype=F32), rows)
    half = LANES // 4
    lane = jnp.arange(LANES)
    in_blk = lane % half
    freq = ROPE_THETA ** (-(2.0 * (in_blk % (half // 2)).astype(F32)) / half)
    use_col = (lane // half) % 2 == 1
    pos = jnp.where(use_col[None, :], col_pos[:, None], row_pos[:, None])
    ang = pos * freq[None, :]
    lo = (in_blk < half // 2)[None, :]
    sin = jnp.sin(ang)
    return jnp.cos(ang), jnp.where(lo, -sin, 0.0), jnp.where(lo, 0.0, sin)


def kernel(x, c, ctx, c_ctx, w_mod, b_mod, ln_g, ln_b, attn_w_in, attn_w_out, attn_lambda, attn_subln_g,
           sgu_w_in, sgu_b_in, sgu_norm_g, sgu_norm_b, sgu_w_s, sgu_b_s, sgu_w_out,
           router_w, router_bias, exp_w_gate, exp_w_up, exp_w_down, sh_w_gate, sh_w_up, sh_w_down):
    b, n, d = x.shape
    depth = w_mod.shape[0]
    assert b <= 7 and d == ATT_HEADS * LANES and n % GRID_W == 0
    alpha = (2 * depth) ** 0.25
    head_dim = d // ATT_HEADS // 2
    tm = 512 if n % 512 == 0 else 256

    cs = jnp.zeros((8, d), F32).at[:b].set(c).at[b].set(c_ctx)
    mods = _modulation(cs, w_mod, b_mod)

    def mod_vec(i, j):
        return mods[i, :, j * d:(j + 1) * d].reshape(8, 1, d)

    for i in range(depth):
        sh_m, sc_m, g_m, sh_f, sc_f, g_f = (mod_vec(i, j) for j in range(6))
        if i % N_MIXERS == 0:
            a = i // N_MIXERS
            lam_init = 0.8 - 0.6 * math.exp(-0.3 * i)
            w_in_bf = attn_w_in[a].astype(BF16)
            cos, slo, shi = _rope_tables(n)
            q_scale = head_dim ** -0.5 * math.log2(math.e)
            q, k, v = _qkv_proj(x, sc_m, sh_m, w_in_bf, cos, slo, shi, q_scale, tm)
            kc, vc = _kv_ctx_proj(ctx, sc_m[b:b + 1], sh_m[b:b + 1], w_in_bf[:, d:])
            k_all = jnp.concatenate([k, kc], axis=2)
            v_all = jnp.concatenate([v, vc], axis=2)
            nk = k_all.shape[2]
            tk = next(t for t in (768, 512, 256, 128) if nk % t == 0)
            pre = _diff_attention(q, k_all, v_all, attn_lambda[a], attn_subln_g[a], lam_init, 256, tk)
            w_out_bf = attn_w_out[a].astype(BF16)
        else:
            s = i // N_MIXERS
            pre = _sgu_mixer(x, sc_m, sh_m, sgu_w_in[s].astype(BF16), sgu_b_in[s], sgu_norm_g[s], sgu_norm_b[s],
                             sgu_w_s[s].astype(BF16), sgu_b_s[s].T, 256)
            w_out_bf = sgu_w_out[s].astype(BF16)
        x1, hf = _post_mixer(pre, w_out_bf, x, g_m, ln_g[i, 0], ln_b[i, 0], sc_f, sh_f, alpha, tm)
        x = _moe_layer(x1, hf, sc_f, sh_f, g_f, router_w[i], router_bias[i], exp_w_gate[i], exp_w_up[i],
                       exp_w_down[i], sh_w_gate[i], sh_w_up[i], sh_w_down[i], ln_g[i, 1], ln_b[i, 1], alpha, tm)
    return x
```

```python
import functools
import math

import jax
import jax.numpy as jnp
from jax import lax
from jax.experimental import pallas as pl
from jax.experimental.pallas import tpu as pltpu
from jax.experimental.pallas import tpu_sc as plsc

F32 = jnp.float32
BF16 = jnp.bfloat16
I32 = jnp.int32

GRID_W = 64
ATT_HEADS = 8
ROPE_THETA = 10000.0
SGU_CHUNK = 128
SGU_GROUPS = 8
TOP_K = 8
N_GROUPS = 8
TOPK_GROUPS = 4
ROUTED_SCALE = 2.5
LN_EPS = 1e-5
N_MIXERS = 2

LANES = 128
MOE_ROWS = 256
SC_CORES = 2
SC_SUBCORES = 16
SC_CHUNK = 64
VMEM_LIMIT = 56 * 1024 * 1024
NEG_INF = float("-inf")


def _cparams(*sem):
    return pltpu.CompilerParams(dimension_semantics=sem, vmem_limit_bytes=VMEM_LIMIT)


def _layer_norm(z, g, b):
    mu = jnp.mean(z, axis=-1, keepdims=True)
    zc = z - mu
    var = jnp.mean(zc * zc, axis=-1, keepdims=True)
    return zc * lax.rsqrt(var + LN_EPS) * g + b


def _silu(x):
    return x * (1.0 / (1.0 + jnp.exp(-x)))


_HIGH_HALF = -65536


def _pack_bf16_pairs(y):
    w = y.shape[1] // 2
    bits = lax.bitcast_convert_type(y.astype(BF16).astype(F32), I32)
    return lax.shift_right_logical(bits[:, :w], 16) | (bits[:, w:] & _HIGH_HALF)


def _unpack_bf16_pairs(p):
    return (lax.bitcast_convert_type(lax.shift_left(p, 16), F32),
            lax.bitcast_convert_type(p & _HIGH_HALF, F32))


def _mod_kernel(cs_ref, w_ref, b_ref, o_ref):
    s = _silu(cs_ref[...])
    o_ref[0] = jnp.dot(s, w_ref[0], precision=lax.Precision.HIGHEST,
                       preferred_element_type=F32) + b_ref[0]


def _modulation(cs, w_mod, b_mod):
    depth, d, n6 = w_mod.shape
    tn = n6 // 4
    return pl.pallas_call(
        _mod_kernel,
        out_shape=jax.ShapeDtypeStruct((depth, 8, n6), F32),
        grid=(depth, n6 // tn),
        in_specs=[
            pl.BlockSpec((8, d), lambda l, j: (0, 0)),
            pl.BlockSpec((1, d, tn), lambda l, j: (l, 0, j)),
            pl.BlockSpec((1, 1, tn), lambda l, j: (l, 0, j)),
        ],
        out_specs=pl.BlockSpec((1, 8, tn), lambda l, j: (l, 0, j)),
        compiler_params=_cparams("parallel", "parallel"),
        name="modulation",
    )(cs, w_mod, b_mod.reshape(depth, 1, n6))


def _rope(xh, cos, sin_lo, sin_hi):
    return xh * cos + pltpu.roll(xh, LANES - 16, 1) * sin_lo + pltpu.roll(xh, 16, 1) * sin_hi


def _qkv_kernel(x_ref, sc_ref, sh_ref, w_ref, cos_ref, slo_ref, shi_ref, q_ref, k_ref, v_ref, *, q_scale):
    d = x_ref.shape[-1]
    h = (x_ref[0] * (1.0 + sc_ref[0]) + sh_ref[0]).astype(BF16)
    cos, slo, shi = cos_ref[...], slo_ref[...], shi_ref[...]
    q = jnp.dot(h, w_ref[:, 0:d], preferred_element_type=F32)
    for hd in range(ATT_HEADS):
        q_ref[0, hd] = (_rope(q[:, hd * LANES:(hd + 1) * LANES], cos, slo, shi) * q_scale).astype(BF16)
    k = jnp.dot(h, w_ref[:, d:2 * d], preferred_element_type=F32)
    for hd in range(ATT_HEADS):
        k_ref[0, hd] = _rope(k[:, hd * LANES:(hd + 1) * LANES], cos, slo, shi).astype(BF16)
    v = jnp.dot(h, w_ref[:, 2 * d:3 * d], preferred_element_type=F32)
    for hd in range(ATT_HEADS):
        v_ref[0, hd] = v[:, hd * LANES:(hd + 1) * LANES].astype(BF16)


def _qkv_proj(x, sc, sh, w_bf, cos, slo, shi, q_scale, tn):
    b, n, d = x.shape
    hd_shape = jax.ShapeDtypeStruct((b, ATT_HEADS, n, LANES), BF16)
    vec = pl.BlockSpec((1, 1, d), lambda bi, i: (bi, 0, 0))
    tab = pl.BlockSpec((tn, LANES), lambda bi, i: (i, 0))
    out = pl.BlockSpec((1, ATT_HEADS, tn, LANES), lambda bi, i: (bi, 0, i, 0))
    return pl.pallas_call(
        functools.partial(_qkv_kernel, q_scale=q_scale),
        out_shape=(hd_shape, hd_shape, hd_shape),
        grid=(b, n // tn),
        in_specs=[
            pl.BlockSpec((1, tn, d), lambda bi, i: (bi, i, 0)),
            vec, vec,
            pl.BlockSpec((d, 3 * d), lambda bi, i: (0, 0)),
            tab, tab, tab,
        ],
        out_specs=(out, out, out),
        compiler_params=_cparams("parallel", "parallel"),
        name="qkv_proj",
    )(x, sc, sh, w_bf, cos, slo, shi)


def _kv_ctx_kernel(x_ref, sc_ref, sh_ref, w_ref, k_ref, v_ref):
    d = x_ref.shape[-1]
    h = (x_ref[0] * (1.0 + sc_ref[0]) + sh_ref[0]).astype(BF16)
    k = jnp.dot(h, w_ref[:, 0:d], preferred_element_type=F32)
    v = jnp.dot(h, w_ref[:, d:2 * d], preferred_element_type=F32)
    for hd in range(ATT_HEADS):
        k_ref[0, hd] = k[:, hd * LANES:(hd + 1) * LANES].astype(BF16)
        v_ref[0, hd] = v[:, hd * LANES:(hd + 1) * LANES].astype(BF16)


def _kv_ctx_proj(ctx, sc, sh, w_kv_bf):
    b, c, d = ctx.shape
    hd_shape = jax.ShapeDtypeStruct((b, ATT_HEADS, c, LANES), BF16)
    vec = pl.BlockSpec((1, 1, d), lambda bi: (0, 0, 0))
    out = pl.BlockSpec((1, ATT_HEADS, c, LANES), lambda bi: (bi, 0, 0, 0))
    return pl.pallas_call(
        _kv_ctx_kernel,
        out_shape=(hd_shape, hd_shape),
        grid=(b,),
        in_specs=[
            pl.BlockSpec((1, c, d), lambda bi: (bi, 0, 0)),
            vec, vec,
            pl.BlockSpec((d, 2 * d), lambda bi: (0, 0)),
        ],
        out_specs=(out, out),
        compiler_params=_cparams("parallel"),
        name="kv_ctx_proj",
    )(ctx, sc, sh, w_kv_bf)


def _attn_kernel(q_ref, k_ref, v_ref, lamp_ref, g_ref, o_ref, *, tk, lam_init):
    q = q_ref[0, 0]
    tq = q.shape[0]
    half = LANES // 2
    lane = lax.broadcasted_iota(jnp.int32, q.shape, 1)
    zero = jnp.zeros_like(q)
    qs = (jnp.where(lane < half, q, zero), jnp.where(lane >= half, q, zero))
    nk = k_ref.shape[2] // tk

    def body(j, carry):
        start = pl.multiple_of(j * tk, tk)
        kc = k_ref[0, 0, pl.ds(start, tk), :]
        vc = v_ref[0, 0, pl.ds(start, tk), :]
        new = []
        for mp in range(2):
            m, l, acc = carry[mp]
            s = lax.dot_general(qs[mp], kc, (((1,), (1,)), ((), ())), preferred_element_type=F32)
            mn = jnp.maximum(m, jnp.max(s, axis=-1, keepdims=True))
            a = jnp.exp2(m - mn)
            p = jnp.exp2(s - mn)
            l = a * l + jnp.sum(p, axis=-1, keepdims=True)
            acc = a * acc + jnp.dot(p.astype(BF16), vc, preferred_element_type=F32)
            new.append((mn, l, acc))
        return tuple(new)

    init = tuple((jnp.full((tq, 1), NEG_INF, F32), jnp.zeros((tq, 1), F32), jnp.zeros((tq, LANES), F32))
                 for _ in range(2))
    (m1, l1, a1), (m2, l2, a2) = lax.fori_loop(0, nk, body, init)

    lp = lamp_ref[...]
    lam = (jnp.exp(jnp.sum(lp[0:1] * lp[1:2], axis=-1, keepdims=True))
           - jnp.exp(jnp.sum(lp[2:3] * lp[3:4], axis=-1, keepdims=True)) + lam_init)
    o = a1 / l1 - lam * (a2 / l2)
    o = o * lax.rsqrt(jnp.mean(o * o, axis=-1, keepdims=True) + LN_EPS) * g_ref[...] * (1.0 - lam_init)
    o_ref[0] = o.astype(BF16)


def _diff_attention(q, k, v, lam_p, subln_g, lam_init, tq, tk):
    b, h, n, _ = q.shape
    nk = k.shape[2]
    return pl.pallas_call(
        functools.partial(_attn_kernel, tk=tk, lam_init=lam_init),
        out_shape=jax.ShapeDtypeStruct((b, n, h * LANES), BF16),
        grid=(b, h, n // tq),
        in_specs=[
            pl.BlockSpec((1, 1, tq, LANES), lambda bi, hi, i: (bi, hi, i, 0)),
            pl.BlockSpec((1, 1, nk, LANES), lambda bi, hi, i: (bi, hi, 0, 0)),
            pl.BlockSpec((1, 1, nk, LANES), lambda bi, hi, i: (bi, hi, 0, 0)),
            pl.BlockSpec(lam_p.shape, lambda bi, hi, i: (0, 0)),
            pl.BlockSpec((1, LANES), lambda bi, hi, i: (0, 0)),
        ],
        out_specs=pl.BlockSpec((1, tq, LANES), lambda bi, hi, i: (bi, i, hi)),
        compiler_params=_cparams("parallel", "parallel", "parallel"),
        name="diff_attention",
    )(q, k, v, lam_p, subln_g.reshape(1, LANES))


def _sgu_kernel(x_ref, sc_ref, sh_ref, w_ref, b_ref, ng_ref, nb_ref, ws_ref, bs_ref, t_ref):
    f = t_ref.shape[-1]
    cg = f // SGU_GROUPS
    tm = x_ref.shape[1]
    h = (x_ref[0] * (1.0 + sc_ref[0]) + sh_ref[0]).astype(BF16)
    z = jnp.dot(h, w_ref[...], preferred_element_type=F32) + b_ref[...]
    z = 0.5 * z * (1.0 + lax.erf(z * (2.0 ** -0.5)))
    u = z[:, :f]
    v = _layer_norm(z[:, f:], ng_ref[...], nb_ref[...]).astype(BF16)
    for c in range(tm // SGU_CHUNK):
        rows = slice(c * SGU_CHUNK, (c + 1) * SGU_CHUNK)
        for g in range(SGU_GROUPS):
            cols = slice(g * cg, (g + 1) * cg)
            vm = jnp.dot(ws_ref[g], v[rows, cols], preferred_element_type=F32) + bs_ref[:, g:g + 1]
            t_ref[0, rows, cols] = (u[rows, cols] * vm).astype(BF16)


def _sgu_mixer(x, sc, sh, w_in_bf, b_in, norm_g, norm_b, w_s_bf, b_s_t, tm):
    b, n, d = x.shape
    f2 = w_in_bf.shape[1]
    f = f2 // 2
    vec = pl.BlockSpec((1, 1, d), lambda bi, i: (bi, 0, 0))
    full2 = lambda a: pl.BlockSpec(a.shape, lambda bi, i: (0,) * a.ndim)
    b_in2, ng2, nb2 = b_in.reshape(1, f2), norm_g.reshape(1, f), norm_b.reshape(1, f)
    return pl.pallas_call(
        _sgu_kernel,
        out_shape=jax.ShapeDtypeStruct((b, n, f), BF16),
        grid=(b, n // tm),
        in_specs=[
            pl.BlockSpec((1, tm, d), lambda bi, i: (bi, i, 0)),
            vec, vec,
            full2(w_in_bf), full2(b_in2), full2(ng2), full2(nb2), full2(w_s_bf), full2(b_s_t),
        ],
        out_specs=pl.BlockSpec((1, tm, f), lambda bi, i: (bi, i, 0)),
        compiler_params=_cparams("parallel", "parallel"),
        name="sgu_mixer",
    )(x, sc, sh, w_in_bf, b_in2, ng2, nb2, w_s_bf, b_s_t)


def _post_kernel(pre_ref, w_ref, x_ref, gm_ref, lg_ref, lb_ref, scf_ref, shf_ref, x1_ref, hf_ref, *, alpha):
    y = jnp.dot(pre_ref[0], w_ref[...], preferred_element_type=F32)
    x1 = _layer_norm(alpha * x_ref[0] + gm_ref[0] * y, lg_ref[...], lb_ref[...])
    x1_ref[0] = x1
    hf_ref[0] = _pack_bf16_pairs(x1 * (1.0 + scf_ref[0]) + shf_ref[0])


def _post_mixer(pre, w_bf, x, gm, ln_g, ln_b, scf, shf, alpha, tm):
    b, n, d = x.shape
    kd = pre.shape[-1]
    vec = pl.BlockSpec((1, 1, d), lambda bi, i: (bi, 0, 0))
    row = pl.BlockSpec((1, d), lambda bi, i: (0, 0))
    tile = pl.BlockSpec((1, tm, d), lambda bi, i: (bi, i, 0))
    return pl.pallas_call(
        functools.partial(_post_kernel, alpha=alpha),
        out_shape=(jax.ShapeDtypeStruct((b, n, d), F32), jax.ShapeDtypeStruct((b, n, d // 2), I32)),
        grid=(b, n // tm),
        in_specs=[
            pl.BlockSpec((1, tm, kd), lambda bi, i: (bi, i, 0)),
            pl.BlockSpec((kd, d), lambda bi, i: (0, 0)),
            tile, vec, row, row, vec, vec,
        ],
        out_specs=(tile, pl.BlockSpec((1, tm, d // 2), lambda bi, i: (bi, i, 0))),
        compiler_params=_cparams("parallel", "parallel"),
        name="post_mixer",
    )(pre, w_bf, x, gm, ln_g.reshape(1, d), ln_b.reshape(1, d), scf, shf)


def _route_kernel(x_ref, sc_ref, sh_ref, wr_ref, rb_ref, idx_ref, w_ref, rank_ref, cnt_ref, carry_ref):
    i = pl.program_id(0)
    e = wr_ref.shape[0]
    tm = x_ref.shape[0]
    ge = e // N_GROUPS

    @pl.when(i == 0)
    def _():
        carry_ref[...] = jnp.zeros_like(carry_ref)

    h = x_ref[...] * (1.0 + sc_ref[0]) + sh_ref[0]
    logits = lax.dot_general(wr_ref[...], h, (((1,), (1,)), ((), ())),
                             precision=lax.Precision.HIGHEST, preferred_element_type=F32)
    scores = 1.0 / (1.0 + jnp.exp(-logits))
    choice = scores + rb_ref[...]

    g3 = choice.reshape(N_GROUPS, ge, tm)
    ri = lax.broadcasted_iota(jnp.int32, g3.shape, 1).astype(F32)
    m1 = jnp.max(g3, axis=1, keepdims=True)
    first = jnp.min(jnp.where(g3 == m1, ri, float(ge)), axis=1, keepdims=True)
    m2 = jnp.max(jnp.where(ri == first, NEG_INF, g3), axis=1, keepdims=True)
    gs = m1 + m2

    gi = lax.broadcasted_iota(jnp.int32, gs.shape, 0).astype(F32)
    gsel = jnp.zeros(gs.shape, F32)
    cur = gs
    for _ in range(TOPK_GROUPS):
        m = jnp.max(cur, axis=0, keepdims=True)
        f = jnp.min(jnp.where(cur == m, gi, float(N_GROUPS)), axis=0, keepdims=True)
        hit = gi == f
        gsel = jnp.where(hit, 1.0, gsel)
        cur = jnp.where(hit, NEG_INF, cur)
    emask = jnp.broadcast_to(gsel, g3.shape).reshape(e, tm)
    masked = jnp.where(emask > 0.5, choice, NEG_INF)

    ei = lax.broadcasted_iota(jnp.int32, (e, tm), 0).astype(F32)
    onehot = jnp.zeros((e, tm), F32)
    idxs, ws = [], []
    for _ in range(TOP_K):
        m = jnp.max(masked, axis=0, keepdims=True)
        f = jnp.min(jnp.where(masked == m, ei, float(e)), axis=0, keepdims=True)
        hit = ei == f
        idxs.append(f)
        ws.append(jnp.sum(jnp.where(hit, scores, 0.0), axis=0, keepdims=True))
        masked = jnp.where(hit, NEG_INF, masked)
        onehot = jnp.where(hit, 1.0, onehot)
    wsum = ws[0]
    for k in range(1, TOP_K):
        wsum = wsum + ws[k]

    r_i = lax.broadcasted_iota(jnp.int32, (tm, tm), 0)
    c_i = lax.broadcasted_iota(jnp.int32, (tm, tm), 1)
    upper = jnp.where(r_i < c_i, 1.0, 0.0).astype(BF16)
    rk = jnp.dot(onehot.astype(BF16), upper, preferred_element_type=F32) + carry_ref[...]
    carry_ref[...] += jnp.sum(onehot, axis=1, keepdims=True)

    for k in range(TOP_K):
        idx_ref[k:k + 1, :] = idxs[k].astype(jnp.int32)
        w_ref[k:k + 1, :] = ws[k] / wsum * ROUTED_SCALE
        rank_ref[k:k + 1, :] = jnp.sum(jnp.where(ei == idxs[k], rk, 0.0), axis=0, keepdims=True).astype(jnp.int32)
    cnt_ref[...] = jnp.broadcast_to(carry_ref[...], cnt_ref.shape).astype(jnp.int32)


def _route(x1, scf, shf, wr_t, rbias, tm):
    b, n, d = x1.shape
    t = b * n
    e = wr_t.shape[0]
    per_b = n // tm
    vec = pl.BlockSpec((1, 1, d), lambda i: (i // per_b, 0, 0))
    out_t = pl.BlockSpec((TOP_K, tm), lambda i: (0, i))
    return pl.pallas_call(
        _route_kernel,
        out_shape=(jax.ShapeDtypeStruct((TOP_K, t), jnp.int32), jax.ShapeDtypeStruct((TOP_K, t), F32),
                   jax.ShapeDtypeStruct((TOP_K, t), jnp.int32), jax.ShapeDtypeStruct((e, LANES), jnp.int32)),
        grid=(t // tm,),
        in_specs=[
            pl.BlockSpec((tm, d), lambda i: (i, 0)),
            vec, vec,
            pl.BlockSpec((e, d), lambda i: (0, 0)),
            pl.BlockSpec((e, 1), lambda i: (0, 0)),
        ],
        out_specs=(out_t, out_t, out_t, pl.BlockSpec((e, LANES), lambda i: (0, 0))),
        scratch_shapes=[pltpu.VMEM((e, 1), F32)],
        compiler_params=_cparams("arbitrary"),
        name="route",
    )(x1.reshape(t, d), scf, shf, wr_t, rbias.reshape(e, 1))


def _gather_rows(table, idx):
    m = idx.shape[0]
    w = table.shape[1]
    workers = SC_CORES * SC_SUBCORES
    per_w = m // workers
    assert m % (workers * SC_CHUNK) == 0
    mesh = plsc.VectorSubcoreMesh(core_axis_name="c", subcore_axis_name="s",
                                  num_cores=SC_CORES, num_subcores=SC_SUBCORES)

    @functools.partial(
        pl.kernel, mesh=mesh,
        out_type=jax.ShapeDtypeStruct((m, w), table.dtype),
        scratch_types=[
            pltpu.VMEM((SC_CHUNK,), I32),
            pltpu.VMEM((SC_CHUNK, w), table.dtype),
            pltpu.SemaphoreType.DMA,
        ],
        name="sc_gather_rows",
    )
    def gather(table_hbm, idx_hbm, out_hbm, idx_v, rows_v, sem):
        base = (lax.axis_index("s") * SC_CORES + lax.axis_index("c")) * per_w

        @pl.loop(0, per_w // SC_CHUNK)
        def _(j):
            off = pl.multiple_of(base + j * SC_CHUNK, SC_CHUNK)
            pltpu.sync_copy(idx_hbm.at[pl.ds(off, SC_CHUNK)], idx_v)
            pltpu.async_copy(table_hbm.at[idx_v], rows_v, sem).wait()
            pltpu.sync_copy(rows_v, out_hbm.at[pl.ds(off, SC_CHUNK)])

    return gather(table, idx)


def _experts_kernel(be_ref, nu_ref, xs_ref, wg_ref, wu_ref, wd_ref, y_ref):
    i = pl.program_id(0)

    @pl.when(i < nu_ref[0])
    def _():
        x_lo, x_hi = (v.astype(BF16) for v in _unpack_bf16_pairs(xs_ref[...]))
        half = x_lo.shape[1]

        def up(w_ref):
            return (jnp.dot(x_lo, w_ref[0, :half, :].astype(BF16), preferred_element_type=F32)
                    + jnp.dot(x_hi, w_ref[0, half:, :].astype(BF16), preferred_element_type=F32))

        hb = (_silu(up(wg_ref)) * up(wu_ref)).astype(BF16)
        y_ref[...] = _pack_bf16_pairs(jnp.dot(hb, wd_ref[0].astype(BF16), preferred_element_type=F32))

    @pl.when(i >= nu_ref[0])
    def _():
        y_ref[...] = jnp.zeros_like(y_ref)


def _routed_experts(xs, wg, wu, wd, block_e, n_used):
    p, dp = xs.shape
    e, d, f = wg.shape
    nb = p // MOE_ROWS
    grid_spec = pltpu.PrefetchScalarGridSpec(
        num_scalar_prefetch=2,
        grid=(nb,),
        in_specs=[
            pl.BlockSpec((MOE_ROWS, dp), lambda i, be, nu: (i, 0)),
            pl.BlockSpec((1, d, f), lambda i, be, nu: (be[i], 0, 0)),
            pl.BlockSpec((1, d, f), lambda i, be, nu: (be[i], 0, 0)),
            pl.BlockSpec((1, f, d), lambda i, be, nu: (be[i], 0, 0)),
        ],
        out_specs=pl.BlockSpec((MOE_ROWS, dp), lambda i, be, nu: (i, 0)),
    )
    return pl.pallas_call(
        _experts_kernel,
        out_shape=jax.ShapeDtypeStruct((p, dp), I32),
        grid_spec=grid_spec,
        compiler_params=_cparams("arbitrary"),
        name="routed_experts",
    )(block_e, n_used, xs, wg, wu, wd)


def _combine_kernel(yg_ref, w_ref, hf_ref, sg_ref, su_ref, sd_ref, x_ref, gf_ref, lg_ref, lb_ref, o_ref, *, alpha):
    w = w_ref[...]
    r_lo, r_hi = _unpack_bf16_pairs(yg_ref[0])
    r_lo, r_hi = w[:, 0:1] * r_lo, w[:, 0:1] * r_hi
    for k in range(1, TOP_K):
        y_lo, y_hi = _unpack_bf16_pairs(yg_ref[k])
        r_lo, r_hi = r_lo + w[:, k:k + 1] * y_lo, r_hi + w[:, k:k + 1] * y_hi
    routed = jnp.concatenate([r_lo, r_hi], axis=1)
    hf = jnp.concatenate(_unpack_bf16_pairs(hf_ref[...]), axis=1).astype(BF16)
    g = jnp.dot(hf, sg_ref[...], preferred_element_type=F32)
    u = jnp.dot(hf, su_ref[...], preferred_element_type=F32)
    shared = jnp.dot((_silu(g) * u).astype(BF16), sd_ref[...], preferred_element_type=F32)
    o_ref[...] = _layer_norm(alpha * x_ref[...] + gf_ref[0] * (routed + shared), lg_ref[...], lb_ref[...])


def _combine(yg, w_tk, hf, sg_bf, su_bf, sd_bf, x1, gf, ln_g, ln_b, alpha, tm, per_b):
    t, d = x1.shape
    f = sg_bf.shape[1]
    row = pl.BlockSpec((1, d), lambda i: (0, 0))
    tile = pl.BlockSpec((tm, d), lambda i: (i, 0))
    return pl.pallas_call(
        functools.partial(_combine_kernel, alpha=alpha),
        out_shape=jax.ShapeDtypeStruct((t, d), F32),
        grid=(t // tm,),
        in_specs=[
            pl.BlockSpec((TOP_K, tm, d // 2), lambda i: (0, i, 0)),
            pl.BlockSpec((tm, TOP_K), lambda i: (i, 0)),
            pl.BlockSpec((tm, d // 2), lambda i: (i, 0)),
            pl.BlockSpec((d, f), lambda i: (0, 0)),
            pl.BlockSpec((d, f), lambda i: (0, 0)),
            pl.BlockSpec((f, d), lambda i: (0, 0)),
            tile,
            pl.BlockSpec((1, 1, d), lambda i: (i // per_b, 0, 0)),
            row, row,
        ],
        out_specs=tile,
        compiler_params=_cparams("parallel"),
        name="moe_combine",
    )(yg, w_tk, hf, sg_bf, su_bf, sd_bf, x1, gf, ln_g.reshape(1, d), ln_b.reshape(1, d))


def _moe_layer(x1, hf, scf, shf, gf, router_w, router_bias, wg, wu, wd, sg, su, sd, ln_g, ln_b, alpha, tm):
    b, n, d = x1.shape
    t = b * n
    e = router_w.shape[1]
    idx_t, w_t, rank_t, cnt = _route(x1, scf, shf, router_w.T, router_bias, tm)

    counts = cnt[:, 0]
    padded = (counts + MOE_ROWS - 1) // MOE_ROWS * MOE_ROWS
    pends = jnp.cumsum(padded)
    pstarts = pends - padded
    sel = idx_t[:, :, None] == jnp.arange(e, dtype=I32)
    pos_t = jnp.sum(jnp.where(sel, pstarts, 0), axis=-1) + rank_t
    n_blocks = t * TOP_K // MOE_ROWS + e
    p = n_blocks * MOE_ROWS
    tok = jnp.broadcast_to(jnp.arange(t, dtype=jnp.int32)[None, :], (TOP_K, t))
    slot_tok = jnp.zeros((p,), jnp.int32).at[pos_t.reshape(-1)].set(tok.reshape(-1))
    block_e = jnp.clip(jnp.searchsorted(pends, jnp.arange(n_blocks, dtype=jnp.int32) * MOE_ROWS, side="right"),
                       0, e - 1).astype(jnp.int32)
    n_used = (pends[-1:] // MOE_ROWS).astype(jnp.int32)

    hf2 = hf.reshape(t, d // 2)
    xs = _gather_rows(hf2, slot_tok)
    yb = _routed_experts(xs, wg, wu, wd, block_e, n_used)
    yg = _gather_rows(yb, pos_t.reshape(-1)).reshape(TOP_K, t, d // 2)
    out = _combine(yg, w_t.T, hf2, sg.astype(BF16), su.astype(BF16), sd.astype(BF16), x1.reshape(t, d), gf,
                   ln_g, ln_b, alpha, tm, n // tm)
    return out.reshape(b, n, d)


def _rope_tables(n):
    rows = n // GRID_W
    row_pos = jnp.repeat(jnp.arange(rows, dtype=F32), GRID_W)
    col_pos = jnp.tile(jnp.arange(GRID_W, dtype=F32), rows)
    half = LANES // 4
    lane = jnp.arange(LANES)
    in_blk = lane % half
    freq = ROPE_THETA ** (-(2.0 * (in_blk % (half // 2)).astype(F32)) / half)
    use_col = (lane // half) % 2 == 1
    pos = jnp.where(use_col[None, :], col_pos[:, None], row_pos[:, None])
    ang = pos * freq[None, :]
    lo = (in_blk < half // 2)[None, :]
    sin = jnp.sin(ang)
    return jnp.cos(ang), jnp.where(lo, -sin, 0.0), jnp.where(lo, 0.0, sin)


def kernel(x, c, ctx, c_ctx, w_mod, b_mod, ln_g, ln_b, attn_w_in, attn_w_out, attn_lambda, attn_subln_g,
           sgu_w_in, sgu_b_in, sgu_norm_g, sgu_norm_b, sgu_w_s, sgu_b_s, sgu_w_out,
           router_w, router_bias, exp_w_gate, exp_w_up, exp_w_down, sh_w_gate, sh_w_up, sh_w_down):
    b, n, d = x.shape
    depth = w_mod.shape[0]
    assert b <= 7 and d == ATT_HEADS * LANES and n % GRID_W == 0
    alpha = (2 * depth) ** 0.25
    head_dim = d // ATT_HEADS // 2
    tm = 512 if n % 512 == 0 else 256

    cs = jnp.zeros((8, d), F32).at[:b].set(c).at[b].set(c_ctx)
    mods = _modulation(cs, w_mod, b_mod)

    def mod_vec(i, j):
        return mods[i, :, j * d:(j + 1) * d].reshape(8, 1, d)

    for i in range(depth):
        sh_m, sc_m, g_m, sh_f, sc_f, g_f = (mod_vec(i, j) for j in range(6))
        if i % N_MIXERS == 0:
            a = i // N_MIXERS
            lam_init = 0.8 - 0.6 * math.exp(-0.3 * i)
            w_in_bf = attn_w_in[a].astype(BF16)
            cos, slo, shi = _rope_tables(n)
            q_scale = head_dim ** -0.5 * math.log2(math.e)
            q, k, v = _qkv_proj(x, sc_m, sh_m, w_in_bf, cos, slo, shi, q_scale, tm)
            kc, vc = _kv_ctx_proj(ctx, sc_m[b:b + 1], sh_m[b:b + 1], w_in_bf[:, d:])
            k_all = jnp.concatenate([k, kc], axis=2)
            v_all = jnp.concatenate([v, vc], axis=2)
            nk = k_all.shape[2]
            tk = next(t for t in (768, 512, 256, 128) if nk % t == 0)
            pre = _diff_attention(q, k_all, v_all, attn_lambda[a], attn_subln_g[a], lam_init, 256, tk)
            w_out_bf = attn_w_out[a].astype(BF16)
        else:
            s = i // N_MIXERS
            pre = _sgu_mixer(x, sc_m, sh_m, sgu_w_in[s].astype(BF16), sgu_b_in[s], sgu_norm_g[s], sgu_norm_b[s],
                             sgu_w_s[s].astype(BF16), sgu_b_s[s].T, 256)
            w_out_bf = sgu_w_out[s].astype(BF16)
        x1, hf = _post_mixer(pre, w_out_bf, x, g_m, ln_g[i, 0], ln_b[i, 0], sc_f, sh_f, alpha, tm)
        x = _moe_layer(x1, hf, sc_f, sh_f, g_f, router_w[i], router_bias[i], exp_w_gate[i], exp_w_up[i],
                       exp_w_down[i], sh_w_gate[i], sh_w_up[i], sh_w_down[i], ln_g[i, 1], ln_b[i, 1], alpha, tm)
    return x
```

```python
import functools
import math

import jax
import jax.numpy as jnp
from jax import lax
from jax.experimental import pallas as pl
from jax.experimental.pallas import tpu as pltpu
from jax.experimental.pallas import tpu_sc as plsc

F32 = jnp.float32
BF16 = jnp.bfloat16
I32 = jnp.int32

GRID_W = 64
ATT_HEADS = 8
ROPE_THETA = 10000.0
SGU_CHUNK = 128
SGU_GROUPS = 8
TOP_K = 8
N_GROUPS = 8
TOPK_GROUPS = 4
ROUTED_SCALE = 2.5
LN_EPS = 1e-5
N_MIXERS = 2

LANES = 128
MOE_ROWS = 256
ATT_STRIP = 32
SC_CORES = 2
SC_SUBCORES = 16
SC_CHUNK = 64
VMEM_LIMIT = 56 * 1024 * 1024
NEG_INF = float("-inf")


def _cparams(*sem):
    return pltpu.CompilerParams(dimension_semantics=sem, vmem_limit_bytes=VMEM_LIMIT)


def _layer_norm(z, g, b):
    mu = jnp.mean(z, axis=-1, keepdims=True)
    zc = z - mu
    var = jnp.mean(zc * zc, axis=-1, keepdims=True)
    return zc * lax.rsqrt(var + LN_EPS) * g + b


def _silu(x):
    return x * (1.0 / (1.0 + jnp.exp(-x)))


_HIGH_HALF = -65536


def _pack_bf16_pairs(y):
    w = y.shape[1] // 2
    bits = lax.bitcast_convert_type(y.astype(BF16).astype(F32), I32)
    return lax.shift_right_logical(bits[:, :w], 16) | (bits[:, w:] & _HIGH_HALF)


def _unpack_bf16_pairs(p):
    return (lax.bitcast_convert_type(lax.shift_left(p, 16), F32),
            lax.bitcast_convert_type(p & _HIGH_HALF, F32))


def _mod_kernel(cs_ref, w_ref, b_ref, o_ref):
    s = _silu(cs_ref[...])
    o_ref[0] = jnp.dot(s, w_ref[0], precision=lax.Precision.HIGHEST,
                       preferred_element_type=F32) + b_ref[0]


def _modulation(cs, w_mod, b_mod):
    depth, d, n6 = w_mod.shape
    tn = n6 // 4
    return pl.pallas_call(
        _mod_kernel,
        out_shape=jax.ShapeDtypeStruct((depth, 8, n6), F32),
        grid=(depth, n6 // tn),
        in_specs=[
            pl.BlockSpec((8, d), lambda l, j: (0, 0)),
            pl.BlockSpec((1, d, tn), lambda l, j: (l, 0, j)),
            pl.BlockSpec((1, 1, tn), lambda l, j: (l, 0, j)),
        ],
        out_specs=pl.BlockSpec((1, 8, tn), lambda l, j: (l, 0, j)),
        compiler_params=_cparams("parallel", "parallel"),
        name="modulation",
    )(cs, w_mod, b_mod.reshape(depth, 1, n6))


def _rope(xh, cos, sin_lo, sin_hi):
    return xh * cos + pltpu.roll(xh, LANES - 16, 1) * sin_lo + pltpu.roll(xh, 16, 1) * sin_hi


def _qkv_kernel(x_ref, sc_ref, sh_ref, w_ref, cos_ref, slo_ref, shi_ref, q_ref, k_ref, v_ref, *, q_scale):
    d = x_ref.shape[-1]
    h = (x_ref[0] * (1.0 + sc_ref[0]) + sh_ref[0]).astype(BF16)
    cos, slo, shi = cos_ref[...], slo_ref[...], shi_ref[...]
    q = jnp.dot(h, w_ref[:, 0:d], preferred_element_type=F32)
    for hd in range(ATT_HEADS):
        q_ref[0, hd] = (_rope(q[:, hd * LANES:(hd + 1) * LANES], cos, slo, shi) * q_scale).astype(BF16)
    k = jnp.dot(h, w_ref[:, d:2 * d], preferred_element_type=F32)
    for hd in range(ATT_HEADS):
        k_ref[0, hd] = _rope(k[:, hd * LANES:(hd + 1) * LANES], cos, slo, shi).astype(BF16)
    v = jnp.dot(h, w_ref[:, 2 * d:3 * d], preferred_element_type=F32)
    for hd in range(ATT_HEADS):
        v_ref[0, hd] = v[:, hd * LANES:(hd + 1) * LANES].astype(BF16)


def _qkv_proj(x, sc, sh, w_bf, cos, slo, shi, q_scale, tn):
    b, n, d = x.shape
    hd_shape = jax.ShapeDtypeStruct((b, ATT_HEADS, n, LANES), BF16)
    vec = pl.BlockSpec((1, 1, d), lambda bi, i: (bi, 0, 0))
    tab = pl.BlockSpec((tn, LANES), lambda bi, i: (i, 0))
    out = pl.BlockSpec((1, ATT_HEADS, tn, LANES), lambda bi, i: (bi, 0, i, 0))
    return pl.pallas_call(
        functools.partial(_qkv_kernel, q_scale=q_scale),
        out_shape=(hd_shape, hd_shape, hd_shape),
        grid=(b, n // tn),
        in_specs=[
            pl.BlockSpec((1, tn, d), lambda bi, i: (bi, i, 0)),
            vec, vec,
            pl.BlockSpec((d, 3 * d), lambda bi, i: (0, 0)),
            tab, tab, tab,
        ],
        out_specs=(out, out, out),
        compiler_params=_cparams("parallel", "parallel"),
        name="qkv_proj",
    )(x, sc, sh, w_bf, cos, slo, shi)


def _kv_ctx_kernel(x_ref, sc_ref, sh_ref, w_ref, k_ref, v_ref):
    d = x_ref.shape[-1]
    h = (x_ref[0] * (1.0 + sc_ref[0]) + sh_ref[0]).astype(BF16)
    k = jnp.dot(h, w_ref[:, 0:d], preferred_element_type=F32)
    v = jnp.dot(h, w_ref[:, d:2 * d], preferred_element_type=F32)
    for hd in range(ATT_HEADS):
        k_ref[0, hd] = k[:, hd * LANES:(hd + 1) * LANES].astype(BF16)
        v_ref[0, hd] = v[:, hd * LANES:(hd + 1) * LANES].astype(BF16)


def _kv_ctx_proj(ctx, sc, sh, w_kv_bf):
    b, c, d = ctx.shape
    hd_shape = jax.ShapeDtypeStruct((b, ATT_HEADS, c, LANES), BF16)
    vec = pl.BlockSpec((1, 1, d), lambda bi: (0, 0, 0))
    out = pl.BlockSpec((1, ATT_HEADS, c, LANES), lambda bi: (bi, 0, 0, 0))
    return pl.pallas_call(
        _kv_ctx_kernel,
        out_shape=(hd_shape, hd_shape),
        grid=(b,),
        in_specs=[
            pl.BlockSpec((1, c, d), lambda bi: (bi, 0, 0)),
            vec, vec,
            pl.BlockSpec((d, 2 * d), lambda bi: (0, 0)),
        ],
        out_specs=(out, out),
        compiler_params=_cparams("parallel"),
        name="kv_ctx_proj",
    )(ctx, sc, sh, w_kv_bf)


def _attn_finish(acc_a, l_a, acc_b, l_b, lamp_ref, g_ref, lam_init):
    lp = lamp_ref[...]
    lam = (jnp.exp(jnp.sum(lp[0:1] * lp[1:2], axis=-1, keepdims=True))
           - jnp.exp(jnp.sum(lp[2:3] * lp[3:4], axis=-1, keepdims=True)) + lam_init)
    o = acc_a / l_a - lam * (acc_b / l_b)
    o = o * lax.rsqrt(jnp.mean(o * o, axis=-1, keepdims=True) + LN_EPS) * g_ref[...] * (1.0 - lam_init)
    return o.astype(BF16)


def _attn_kernel(q_ref, k_ref, v_ref, kc_ref, vc_ref, lamp_ref, g_ref, o_ref,
                 s00, s01, s10, s11, p0, p1, mb0, mb1, lp0, lp1, acc0, acc1, *, tk, lam_init):
    q = q_ref[0, 0]
    tq = q.shape[0]
    half = LANES // 2
    lane = lax.broadcasted_iota(jnp.int32, q.shape, 1)
    zero = jnp.zeros_like(q)
    qs = (jnp.where(lane < half, q, zero), jnp.where(lane >= half, q, zero))
    n_chunks = k_ref.shape[2] // tk
    s_scr = ((s00, s01), (s10, s11))
    p_scr, mb_scr, lp_scr, acc_scr = (p0, p1), (mb0, mb1), (lp0, lp1), (acc0, acc1)
    nt = (((1,), (1,)), ((), ()))

    def scores(slot, kc):
        width = kc.shape[0]
        for mp in range(2):
            s_scr[slot][mp][:, 0:width] = lax.dot_general(qs[mp], kc, nt, preferred_element_type=F32)

    def absorb(slot, vc):
        width = vc.shape[0]
        for mp in range(2):
            for r in range(tq // ATT_STRIP):
                rows = slice(r * ATT_STRIP, (r + 1) * ATT_STRIP)
                shift = mb_scr[mp][rows, :]
                tiles = [jnp.exp2(s_scr[slot][mp][rows, t * LANES:(t + 1) * LANES] - shift)
                         for t in range(width // LANES)]
                p_scr[mp][rows, 0:width] = jnp.concatenate(tiles, axis=1).astype(BF16)
                lp_scr[mp][rows, :] += functools.reduce(lambda x, y: x + y, tiles)
            acc_scr[mp][...] += jnp.dot(p_scr[mp][:, 0:width], vc, preferred_element_type=F32)

    def k_chunk(j):
        return k_ref[0, 0, pl.ds(pl.multiple_of(j * tk, tk), tk), :]

    def v_chunk(j):
        return v_ref[0, 0, pl.ds(pl.multiple_of(j * tk, tk), tk), :]

    scores(0, k_chunk(0))
    for mp in range(2):
        mb_scr[mp][...] = jnp.broadcast_to(jnp.max(s_scr[0][mp][...], axis=-1, keepdims=True), (tq, LANES))
        lp_scr[mp][...] = jnp.zeros((tq, LANES), F32)
        acc_scr[mp][...] = jnp.zeros((tq, LANES), F32)

    def pair(jj, c):
        j0 = 2 * jj
        scores(1, k_chunk(j0 + 1))
        absorb(0, v_chunk(j0))
        scores(0, k_chunk(j0 + 2))
        absorb(1, v_chunk(j0 + 1))
        return c

    lax.fori_loop(0, n_chunks // 2 - 1, pair, 0)
    scores(1, k_chunk(n_chunks - 1))
    absorb(0, v_chunk(n_chunks - 2))
    scores(0, kc_ref[0, 0])
    absorb(1, v_chunk(n_chunks - 1))
    absorb(0, vc_ref[0, 0])

    sums = [jnp.sum(lp_scr[mp][...], axis=-1, keepdims=True) for mp in range(2)]
    bad = sum(jnp.sum(jnp.where(jnp.isfinite(x), 0.0, 1.0)) for x in (sums[0], sums[1], acc0[...], acc1[...]))

    @pl.when(bad == 0.0)
    def _():
        o_ref[0] = _attn_finish(acc0[...], sums[0], acc1[...], sums[1], lamp_ref, g_ref, lam_init)

    @pl.when(bad != 0.0)
    def _():
        def update(carry, kc, vc):
            new = []
            for mp in range(2):
                m, l, acc = carry[mp]
                s = lax.dot_general(qs[mp], kc, nt, preferred_element_type=F32)
                mn = jnp.maximum(m, jnp.max(s, axis=-1, keepdims=True))
                a = jnp.exp2(m - mn)
                p = jnp.exp2(s - mn)
                new.append((mn, a * l + jnp.sum(p, axis=-1, keepdims=True),
                            a * acc + jnp.dot(p.astype(BF16), vc, preferred_element_type=F32)))
            return tuple(new)

        init = tuple((jnp.full((tq, 1), NEG_INF, F32), jnp.zeros((tq, 1), F32), jnp.zeros((tq, LANES), F32))
                     for _ in range(2))
        carry = lax.fori_loop(0, n_chunks, lambda j, c: update(c, k_chunk(j), v_chunk(j)), init)
        (_, l_a, acc_a), (_, l_b, acc_b) = update(carry, kc_ref[0, 0], vc_ref[0, 0])
        o_ref[0] = _attn_finish(acc_a, l_a, acc_b, l_b, lamp_ref, g_ref, lam_init)


def _diff_attention(q, k, v, kc, vc, lam_p, subln_g, lam_init, tq, tk):
    b, h, n, _ = q.shape
    c = kc.shape[2]
    assert n % tk == 0 and (n // tk) % 2 == 0 and c <= tk and tq % ATT_STRIP == 0
    kv = pl.BlockSpec((1, 1, n, LANES), lambda bi, hi, i: (bi, hi, 0, 0))
    kvc = pl.BlockSpec((1, 1, c, LANES), lambda bi, hi, i: (bi, hi, 0, 0))
    return pl.pallas_call(
        functools.partial(_attn_kernel, tk=tk, lam_init=lam_init),
        out_shape=jax.ShapeDtypeStruct((b, n, h * LANES), BF16),
        grid=(b, h, n // tq),
        in_specs=[
            pl.BlockSpec((1, 1, tq, LANES), lambda bi, hi, i: (bi, hi, i, 0)),
            kv, kv, kvc, kvc,
            pl.BlockSpec(lam_p.shape, lambda bi, hi, i: (0, 0)),
            pl.BlockSpec((1, LANES), lambda bi, hi, i: (0, 0)),
        ],
        out_specs=pl.BlockSpec((1, tq, LANES), lambda bi, hi, i: (bi, i, hi)),
        scratch_shapes=[
            *[pltpu.VMEM((tq, tk), F32)] * 4,
            *[pltpu.VMEM((tq, tk), BF16)] * 2,
            *[pltpu.VMEM((tq, LANES), F32)] * 2,
            *[pltpu.VMEM((tq, LANES), F32)] * 2,
            *[pltpu.VMEM((tq, LANES), F32)] * 2,
        ],
        compiler_params=_cparams("parallel", "parallel", "parallel"),
        name="diff_attention",
    )(q, k, v, kc, vc, lam_p, subln_g.reshape(1, LANES))


def _sgu_kernel(x_ref, sc_ref, sh_ref, w_ref, b_ref, ng_ref, nb_ref, ws_ref, bs_ref, t_ref):
    f = t_ref.shape[-1]
    cg = f // SGU_GROUPS
    tm = x_ref.shape[1]
    h = (x_ref[0] * (1.0 + sc_ref[0]) + sh_ref[0]).astype(BF16)
    z = jnp.dot(h, w_ref[...], preferred_element_type=F32) + b_ref[...]
    z = 0.5 * z * (1.0 + lax.erf(z * (2.0 ** -0.5)))
    u = z[:, :f]
    v = _layer_norm(z[:, f:], ng_ref[...], nb_ref[...]).astype(BF16)
    for c in range(tm // SGU_CHUNK):
        rows = slice(c * SGU_CHUNK, (c + 1) * SGU_CHUNK)
        for g in range(SGU_GROUPS):
            cols = slice(g * cg, (g + 1) * cg)
            vm = jnp.dot(ws_ref[g], v[rows, cols], preferred_element_type=F32) + bs_ref[:, g:g + 1]
            t_ref[0, rows, cols] = (u[rows, cols] * vm).astype(BF16)


def _sgu_mixer(x, sc, sh, w_in_bf, b_in, norm_g, norm_b, w_s_bf, b_s_t, tm):
    b, n, d = x.shape
    f2 = w_in_bf.shape[1]
    f = f2 // 2
    vec = pl.BlockSpec((1, 1, d), lambda bi, i: (bi, 0, 0))
    full2 = lambda a: pl.BlockSpec(a.shape, lambda bi, i: (0,) * a.ndim)
    b_in2, ng2, nb2 = b_in.reshape(1, f2), norm_g.reshape(1, f), norm_b.reshape(1, f)
    return pl.pallas_call(
        _sgu_kernel,
        out_shape=jax.ShapeDtypeStruct((b, n, f), BF16),
        grid=(b, n // tm),
        in_specs=[
            pl.BlockSpec((1, tm, d), lambda bi, i: (bi, i, 0)),
            vec, vec,
            full2(w_in_bf), full2(b_in2), full2(ng2), full2(nb2), full2(w_s_bf), full2(b_s_t),
        ],
        out_specs=pl.BlockSpec((1, tm, f), lambda bi, i: (bi, i, 0)),
        compiler_params=_cparams("parallel", "parallel"),
        name="sgu_mixer",
    )(x, sc, sh, w_in_bf, b_in2, ng2, nb2, w_s_bf, b_s_t)


def _post_kernel(pre_ref, w_ref, x_ref, gm_ref, lg_ref, lb_ref, scf_ref, shf_ref, x1_ref, hf_ref, *, alpha):
    y = jnp.dot(pre_ref[0], w_ref[...], preferred_element_type=F32)
    x1 = _layer_norm(alpha * x_ref[0] + gm_ref[0] * y, lg_ref[...], lb_ref[...])
    x1_ref[0] = x1
    hf_ref[0] = _pack_bf16_pairs(x1 * (1.0 + scf_ref[0]) + shf_ref[0])


def _post_mixer(pre, w_bf, x, gm, ln_g, ln_b, scf, shf, alpha, tm):
    b, n, d = x.shape
    kd = pre.shape[-1]
    vec = pl.BlockSpec((1, 1, d), lambda bi, i: (bi, 0, 0))
    row = pl.BlockSpec((1, d), lambda bi, i: (0, 0))
    tile = pl.BlockSpec((1, tm, d), lambda bi, i: (bi, i, 0))
    return pl.pallas_call(
        functools.partial(_post_kernel, alpha=alpha),
        out_shape=(jax.ShapeDtypeStruct((b, n, d), F32), jax.ShapeDtypeStruct((b, n, d // 2), I32)),
        grid=(b, n // tm),
        in_specs=[
            pl.BlockSpec((1, tm, kd), lambda bi, i: (bi, i, 0)),
            pl.BlockSpec((kd, d), lambda bi, i: (0, 0)),
            tile, vec, row, row, vec, vec,
        ],
        out_specs=(tile, pl.BlockSpec((1, tm, d // 2), lambda bi, i: (bi, i, 0))),
        compiler_params=_cparams("parallel", "parallel"),
        name="post_mixer",
    )(pre, w_bf, x, gm, ln_g.reshape(1, d), ln_b.reshape(1, d), scf, shf)


def _route_kernel(x_ref, sc_ref, sh_ref, wr_ref, rb_ref, idx_ref, w_ref, rank_ref, cnt_ref, carry_ref):
    i = pl.program_id(0)
    e = wr_ref.shape[0]
    tm = x_ref.shape[0]
    ge = e // N_GROUPS

    @pl.when(i == 0)
    def _():
        carry_ref[...] = jnp.zeros_like(carry_ref)

    h = x_ref[...] * (1.0 + sc_ref[0]) + sh_ref[0]
    logits = lax.dot_general(wr_ref[...], h, (((1,), (1,)), ((), ())),
                             precision=lax.Precision.HIGHEST, preferred_element_type=F32)
    scores = 1.0 / (1.0 + jnp.exp(-logits))
    choice = scores + rb_ref[...]

    g3 = choice.reshape(N_GROUPS, ge, tm)
    ri = lax.broadcasted_iota(jnp.int32, g3.shape, 1).astype(F32)
    m1 = jnp.max(g3, axis=1, keepdims=True)
    first = jnp.min(jnp.where(g3 == m1, ri, float(ge)), axis=1, keepdims=True)
    m2 = jnp.max(jnp.where(ri == first, NEG_INF, g3), axis=1, keepdims=True)
    gs = m1 + m2

    gi = lax.broadcasted_iota(jnp.int32, gs.shape, 0).astype(F32)
    gsel = jnp.zeros(gs.shape, F32)
    cur = gs
    for _ in range(TOPK_GROUPS):
        m = jnp.max(cur, axis=0, keepdims=True)
        f = jnp.min(jnp.where(cur == m, gi, float(N_GROUPS)), axis=0, keepdims=True)
        hit = gi == f
        gsel = jnp.where(hit, 1.0, gsel)
        cur = jnp.where(hit, NEG_INF, cur)
    emask = jnp.broadcast_to(gsel, g3.shape).reshape(e, tm)
    masked = jnp.where(emask > 0.5, choice, NEG_INF)

    ei = lax.broadcasted_iota(jnp.int32, (e, tm), 0).astype(F32)
    onehot = jnp.zeros((e, tm), F32)
    idxs, ws = [], []
    for _ in range(TOP_K):
        m = jnp.max(masked, axis=0, keepdims=True)
        f = jnp.min(jnp.where(masked == m, ei, float(e)), axis=0, keepdims=True)
        hit = ei == f
        idxs.append(f)
        ws.append(jnp.sum(jnp.where(hit, scores, 0.0), axis=0, keepdims=True))
        masked = jnp.where(hit, NEG_INF, masked)
        onehot = jnp.where(hit, 1.0, onehot)
    wsum = ws[0]
    for k in range(1, TOP_K):
        wsum = wsum + ws[k]

    r_i = lax.broadcasted_iota(jnp.int32, (tm, tm), 0)
    c_i = lax.broadcasted_iota(jnp.int32, (tm, tm), 1)
    upper = jnp.where(r_i < c_i, 1.0, 0.0).astype(BF16)
    rk = jnp.dot(onehot.astype(BF16), upper, preferred_element_type=F32) + carry_ref[...]
    carry_ref[...] += jnp.sum(onehot, axis=1, keepdims=True)

    for k in range(TOP_K):
        idx_ref[k:k + 1, :] = idxs[k].astype(jnp.int32)
        w_ref[k:k + 1, :] = ws[k] / wsum * ROUTED_SCALE
        rank_ref[k:k + 1, :] = jnp.sum(jnp.where(ei == idxs[k], rk, 0.0), axis=0, keepdims=True).astype(jnp.int32)
    cnt_ref[...] = jnp.broadcast_to(carry_ref[...], cnt_ref.shape).astype(jnp.int32)


def _route(x1, scf, shf, wr_t, rbias, tm):
    b, n, d = x1.shape
    t = b * n
    e = wr_t.shape[0]
    per_b = n // tm
    vec = pl.BlockSpec((1, 1, d), lambda i: (i // per_b, 0, 0))
    out_t = pl.BlockSpec((TOP_K, tm), lambda i: (0, i))
    return pl.pallas_call(
        _route_kernel,
        out_shape=(jax.ShapeDtypeStruct((TOP_K, t), jnp.int32), jax.ShapeDtypeStruct((TOP_K, t), F32),
                   jax.ShapeDtypeStruct((TOP_K, t), jnp.int32), jax.ShapeDtypeStruct((e, LANES), jnp.int32)),
        grid=(t // tm,),
        in_specs=[
            pl.BlockSpec((tm, d), lambda i: (i, 0)),
            vec, vec,
            pl.BlockSpec((e, d), lambda i: (0, 0)),
            pl.BlockSpec((e, 1), lambda i: (0, 0)),
        ],
        out_specs=(out_t, out_t, out_t, pl.BlockSpec((e, LANES), lambda i: (0, 0))),
        scratch_shapes=[pltpu.VMEM((e, 1), F32)],
        compiler_params=_cparams("arbitrary"),
        name="route",
    )(x1.reshape(t, d), scf, shf, wr_t, rbias.reshape(e, 1))


def _gather_rows(table, idx):
    m = idx.shape[0]
    w = table.shape[1]
    workers = SC_CORES * SC_SUBCORES
    per_w = m // workers
    assert m % (workers * SC_CHUNK) == 0
    mesh = plsc.VectorSubcoreMesh(core_axis_name="c", subcore_axis_name="s",
                                  num_cores=SC_CORES, num_subcores=SC_SUBCORES)

    @functools.partial(
        pl.kernel, mesh=mesh,
        out_type=jax.ShapeDtypeStruct((m, w), table.dtype),
        scratch_types=[
            pltpu.VMEM((SC_CHUNK,), I32),
            pltpu.VMEM((SC_CHUNK, w), table.dtype),
            pltpu.SemaphoreType.DMA,
        ],
        name="sc_gather_rows",
    )
    def gather(table_hbm, idx_hbm, out_hbm, idx_v, rows_v, sem):
        base = (lax.axis_index("s") * SC_CORES + lax.axis_index("c")) * per_w

        @pl.loop(0, per_w // SC_CHUNK)
        def _(j):
            off = pl.multiple_of(base + j * SC_CHUNK, SC_CHUNK)
            pltpu.sync_copy(idx_hbm.at[pl.ds(off, SC_CHUNK)], idx_v)
            pltpu.async_copy(table_hbm.at[idx_v], rows_v, sem).wait()
            pltpu.sync_copy(rows_v, out_hbm.at[pl.ds(off, SC_CHUNK)])

    return gather(table, idx)


def _experts_kernel(be_ref, nu_ref, xs_ref, wg_ref, wu_ref, wd_ref, y_ref, wg_bf, wu_bf, wd_bf):
    i = pl.program_id(0)
    used = i < nu_ref[0]

    @pl.when(used & ((i == 0) | (be_ref[i] != be_ref[jnp.maximum(i - 1, 0)])))
    def _():
        wg_bf[...] = wg_ref[0, 0].astype(BF16)
        wu_bf[...] = wu_ref[0, 0].astype(BF16)
        wd_bf[...] = wd_ref[0, 0].astype(BF16)

    @pl.when(used)
    def _():
        x_lo, x_hi = (v.astype(BF16) for v in _unpack_bf16_pairs(xs_ref[...]))
        half = x_lo.shape[1]

        def up(w_bf):
            return (jnp.dot(x_lo, w_bf[:half, :], preferred_element_type=F32)
                    + jnp.dot(x_hi, w_bf[half:, :], preferred_element_type=F32))

        hb = (_silu(up(wg_bf)) * up(wu_bf)).astype(BF16)
        y_ref[...] = _pack_bf16_pairs(jnp.dot(hb, wd_bf[...], preferred_element_type=F32))

    @pl.when(jnp.logical_not(used))
    def _():
        y_ref[...] = jnp.zeros_like(y_ref)


def _routed_experts(xs, wg, wu, wd, layer, block_e, n_used):
    p, dp = xs.shape
    _, e, d, f = wg.shape
    nb = p // MOE_ROWS
    grid_spec = pltpu.PrefetchScalarGridSpec(
        num_scalar_prefetch=2,
        grid=(nb,),
        in_specs=[
            pl.BlockSpec((MOE_ROWS, dp), lambda i, be, nu: (i, 0)),
            pl.BlockSpec((1, 1, d, f), lambda i, be, nu: (layer, be[i], 0, 0)),
            pl.BlockSpec((1, 1, d, f), lambda i, be, nu: (layer, be[i], 0, 0)),
            pl.BlockSpec((1, 1, f, d), lambda i, be, nu: (layer, be[i], 0, 0)),
        ],
        out_specs=pl.BlockSpec((MOE_ROWS, dp), lambda i, be, nu: (i, 0)),
        scratch_shapes=[pltpu.VMEM((d, f), BF16), pltpu.VMEM((d, f), BF16), pltpu.VMEM((f, d), BF16)],
    )
    return pl.pallas_call(
        _experts_kernel,
        out_shape=jax.ShapeDtypeStruct((p, dp), I32),
        grid_spec=grid_spec,
        compiler_params=_cparams("arbitrary"),
        name="routed_experts",
    )(block_e, n_used, xs, wg, wu, wd)


def _combine_kernel(yg_ref, w_ref, hf_ref, sg_ref, su_ref, sd_ref, x_ref, gf_ref, lg_ref, lb_ref, o_ref, *, alpha):
    w = w_ref[...]
    r_lo, r_hi = _unpack_bf16_pairs(yg_ref[0])
    r_lo, r_hi = w[:, 0:1] * r_lo, w[:, 0:1] * r_hi
    for k in range(1, TOP_K):
        y_lo, y_hi = _unpack_bf16_pairs(yg_ref[k])
        r_lo, r_hi = r_lo + w[:, k:k + 1] * y_lo, r_hi + w[:, k:k + 1] * y_hi
    routed = jnp.concatenate([r_lo, r_hi], axis=1)
    hf = jnp.concatenate(_unpack_bf16_pairs(hf_ref[...]), axis=1).astype(BF16)
    g = jnp.dot(hf, sg_ref[...], preferred_element_type=F32)
    u = jnp.dot(hf, su_ref[...], preferred_element_type=F32)
    shared = jnp.dot((_silu(g) * u).astype(BF16), sd_ref[...], preferred_element_type=F32)
    o_ref[...] = _layer_norm(alpha * x_ref[...] + gf_ref[0] * (routed + shared), lg_ref[...], lb_ref[...])


def _combine(yg, w_tk, hf, sg_bf, su_bf, sd_bf, x1, gf, ln_g, ln_b, alpha, tm, per_b):
    t, d = x1.shape
    f = sg_bf.shape[1]
    row = pl.BlockSpec((1, d), lambda i: (0, 0))
    tile = pl.BlockSpec((tm, d), lambda i: (i, 0))
    return pl.pallas_call(
        functools.partial(_combine_kernel, alpha=alpha),
        out_shape=jax.ShapeDtypeStruct((t, d), F32),
        grid=(t // tm,),
        in_specs=[
            pl.BlockSpec((TOP_K, tm, d // 2), lambda i: (0, i, 0)),
            pl.BlockSpec((tm, TOP_K), lambda i: (i, 0)),
            pl.BlockSpec((tm, d // 2), lambda i: (i, 0)),
            pl.BlockSpec((d, f), lambda i: (0, 0)),
            pl.BlockSpec((d, f), lambda i: (0, 0)),
            pl.BlockSpec((f, d), lambda i: (0, 0)),
            tile,
            pl.BlockSpec((1, 1, d), lambda i: (i // per_b, 0, 0)),
            row, row,
        ],
        out_specs=tile,
        compiler_params=_cparams("parallel"),
        name="moe_combine",
    )(yg, w_tk, hf, sg_bf, su_bf, sd_bf, x1, gf, ln_g.reshape(1, d), ln_b.reshape(1, d))


def _moe_layer(x1, hf, scf, shf, gf, router_w, router_bias, wg, wu, wd, layer, sg, su, sd, ln_g, ln_b, alpha, tm):
    b, n, d = x1.shape
    t = b * n
    e = router_w.shape[1]
    idx_t, w_t, rank_t, cnt = _route(x1, scf, shf, router_w.T, router_bias, tm)

    counts = cnt[:, 0]
    padded = (counts + MOE_ROWS - 1) // MOE_ROWS * MOE_ROWS
    pends = jnp.cumsum(padded)
    pstarts = pends - padded
    sel = idx_t[:, :, None] == jnp.arange(e, dtype=I32)
    pos_t = jnp.sum(jnp.where(sel, pstarts, 0), axis=-1) + rank_t
    n_blocks = t * TOP_K // MOE_ROWS + e
    p = n_blocks * MOE_ROWS
    tok = jnp.broadcast_to(jnp.arange(t, dtype=jnp.int32)[None, :], (TOP_K, t))
    slot_tok = (jnp.arange(p, dtype=I32) % t).at[pos_t.reshape(-1)].set(tok.reshape(-1))
    block_e = jnp.clip(jnp.searchsorted(pends, jnp.arange(n_blocks, dtype=jnp.int32) * MOE_ROWS, side="right"),
                       0, e - 1).astype(jnp.int32)
    n_used = (pends[-1:] // MOE_ROWS).astype(jnp.int32)

    hf2 = hf.reshape(t, d // 2)
    xs = _gather_rows(hf2, slot_tok)
    yb = _routed_experts(xs, wg, wu, wd, layer, block_e, n_used)
    yg = _gather_rows(yb, pos_t.reshape(-1)).reshape(TOP_K, t, d // 2)
    out = _combine(yg, w_t.T, hf2, sg.astype(BF16), su.astype(BF16), sd.astype(BF16), x1.reshape(t, d), gf,
                   ln_g, ln_b, alpha, tm, n // tm)
    return out.reshape(b, n, d)


def _rope_tables(n):
    rows = n // GRID_W
    row_pos = jnp.repeat(jnp.arange(rows, dtype=F32), GRID_W)
    col_pos = jnp.tile(jnp.arange(GRID_W, dtype=F32), rows)
    half = LANES // 4
    lane = jnp.arange(LANES)
    in_blk = lane % half
    freq = ROPE_THETA ** (-(2.0 * (in_blk % (half // 2)).astype(F32)) / half)
    use_col = (lane // half) % 2 == 1
    pos = jnp.where(use_col[None, :], col_pos[:, None], row_pos[:, None])
    ang = pos * freq[None, :]
    lo = (in_blk < half // 2)[None, :]
    sin = jnp.sin(ang)
    return jnp.cos(ang), jnp.where(lo, -sin, 0.0), jnp.where(lo, 0.0, sin)


def kernel(x, c, ctx, c_ctx, w_mod, b_mod, ln_g, ln_b, attn_w_in, attn_w_out, attn_lambda, attn_subln_g,
           sgu_w_in, sgu_b_in, sgu_norm_g, sgu_norm_b, sgu_w_s, sgu_b_s, sgu_w_out,
           router_w, router_bias, exp_w_gate, exp_w_up, exp_w_down, sh_w_gate, sh_w_up, sh_w_down):
    b, n, d = x.shape
    depth = w_mod.shape[0]
    assert b <= 7 and d == ATT_HEADS * LANES and n % GRID_W == 0
    alpha = (2 * depth) ** 0.25
    head_dim = d // ATT_HEADS // 2
    tm = 512 if n % 512 == 0 else 256

    cs = jnp.zeros((8, d), F32).at[:b].set(c).at[b].set(c_ctx)
    mods = _modulation(cs, w_mod, b_mod)

    def mod_vec(i, j):
        return mods[i, :, j * d:(j + 1) * d].reshape(8, 1, d)

    for i in range(depth):
        sh_m, sc_m, g_m, sh_f, sc_f, g_f = (mod_vec(i, j) for j in range(6))
        if i % N_MIXERS == 0:
            a = i // N_MIXERS
            lam_init = 0.8 - 0.6 * math.exp(-0.3 * i)
            w_in_bf = attn_w_in[a].astype(BF16)
            cos, slo, shi = _rope_tables(n)
            q_scale = head_dim ** -0.5 * math.log2(math.e)
            q, k, v = _qkv_proj(x, sc_m, sh_m, w_in_bf, cos, slo, shi, q_scale, tm)
            kc, vc = _kv_ctx_proj(ctx, sc_m[b:b + 1], sh_m[b:b + 1], w_in_bf[:, d:])
            tk = min(512, n // 2)
            pre = _diff_attention(q, k, v, kc, vc, attn_lambda[a], attn_subln_g[a], lam_init, min(512, n), tk)
            w_out_bf = attn_w_out[a].astype(BF16)
        else:
            s = i // N_MIXERS
            pre = _sgu_mixer(x, sc_m, sh_m, sgu_w_in[s].astype(BF16), sgu_b_in[s], sgu_norm_g[s], sgu_norm_b[s],
                             sgu_w_s[s].astype(BF16), sgu_b_s[s].T, 256)
            w_out_bf = sgu_w_out[s].astype(BF16)
        x1, hf = _post_mixer(pre, w_out_bf, x, g_m, ln_g[i, 0], ln_b[i, 0], sc_f, sh_f, alpha, tm)
        x = _moe_layer(x1, hf, sc_f, sh_f, g_f, router_w[i], router_bias[i], exp_w_gate, exp_w_up, exp_w_down, i,
                       sh_w_gate[i], sh_w_up[i], sh_w_down[i], ln_g[i, 1], ln_b[i, 1], alpha, tm)
    return x
```

```python
import functools
import math

import jax
import jax.numpy as jnp
from jax import lax
from jax.experimental import pallas as pl
from jax.experimental.pallas import tpu as pltpu
from jax.experimental.pallas import tpu_sc as plsc

F32 = jnp.float32
BF16 = jnp.bfloat16
I32 = jnp.int32

GRID_W = 64
ATT_HEADS = 8
ROPE_THETA = 10000.0
SGU_CHUNK = 128
SGU_GROUPS = 8
TOP_K = 8
N_GROUPS = 8
TOPK_GROUPS = 4
ROUTED_SCALE = 2.5
LN_EPS = 1e-5
N_MIXERS = 2

LANES = 128
MOE_ROWS = 256
ATT_STRIP = 32
SC_CORES = 2
SC_SUBCORES = 16
SC_CHUNK = 64
VMEM_LIMIT = 56 * 1024 * 1024
NEG_INF = float("-inf")


def _cparams(*sem):
    return pltpu.CompilerParams(dimension_semantics=sem, vmem_limit_bytes=VMEM_LIMIT)


def _layer_norm(z, g, b):
    mu = jnp.mean(z, axis=-1, keepdims=True)
    zc = z - mu
    var = jnp.mean(zc * zc, axis=-1, keepdims=True)
    return zc * lax.rsqrt(var + LN_EPS) * g + b


def _silu(x):
    return x * (1.0 / (1.0 + jnp.exp(-x)))


_HIGH_HALF = -65536


def _pack_bf16_pairs(y):
    w = y.shape[1] // 2
    bits = lax.bitcast_convert_type(y.astype(BF16).astype(F32), I32)
    return lax.shift_right_logical(bits[:, :w], 16) | (bits[:, w:] & _HIGH_HALF)


def _unpack_bf16_pairs(p):
    return (lax.bitcast_convert_type(lax.shift_left(p, 16), F32),
            lax.bitcast_convert_type(p & _HIGH_HALF, F32))


def _mod_kernel(cs_ref, w_ref, b_ref, o_ref):
    s = _silu(cs_ref[...])
    o_ref[0] = jnp.dot(s, w_ref[0], precision=lax.Precision.HIGHEST,
                       preferred_element_type=F32) + b_ref[0]


def _modulation(cs, w_mod, b_mod):
    depth, d, n6 = w_mod.shape
    tn = n6 // 4
    return pl.pallas_call(
        _mod_kernel,
        out_shape=jax.ShapeDtypeStruct((depth, 8, n6), F32),
        grid=(depth, n6 // tn),
        in_specs=[
            pl.BlockSpec((8, d), lambda l, j: (0, 0)),
            pl.BlockSpec((1, d, tn), lambda l, j: (l, 0, j)),
            pl.BlockSpec((1, 1, tn), lambda l, j: (l, 0, j)),
        ],
        out_specs=pl.BlockSpec((1, 8, tn), lambda l, j: (l, 0, j)),
        compiler_params=_cparams("parallel", "parallel"),
        name="modulation",
    )(cs, w_mod, b_mod.reshape(depth, 1, n6))


def _rope(xh, cos, sin_lo, sin_hi):
    return xh * cos + pltpu.roll(xh, LANES - 16, 1) * sin_lo + pltpu.roll(xh, 16, 1) * sin_hi


def _qkv_kernel(x_ref, sc_ref, sh_ref, w_ref, cos_ref, slo_ref, shi_ref, q_ref, k_ref, v_ref, *, q_scale):
    d = x_ref.shape[-1]
    h = (x_ref[0] * (1.0 + sc_ref[0]) + sh_ref[0]).astype(BF16)
    cos, slo, shi = cos_ref[...], slo_ref[...], shi_ref[...]
    q = jnp.dot(h, w_ref[:, 0:d], preferred_element_type=F32)
    for hd in range(ATT_HEADS):
        q_ref[0, hd] = (_rope(q[:, hd * LANES:(hd + 1) * LANES], cos, slo, shi) * q_scale).astype(BF16)
    k = jnp.dot(h, w_ref[:, d:2 * d], preferred_element_type=F32)
    for hd in range(ATT_HEADS):
        k_ref[0, hd] = _rope(k[:, hd * LANES:(hd + 1) * LANES], cos, slo, shi).astype(BF16)
    v = jnp.dot(h, w_ref[:, 2 * d:3 * d], preferred_element_type=F32)
    for hd in range(ATT_HEADS):
        v_ref[0, hd] = v[:, hd * LANES:(hd + 1) * LANES].astype(BF16)


def _qkv_proj(x, sc, sh, w_bf, cos, slo, shi, q_scale, tn):
    b, n, d = x.shape
    hd_shape = jax.ShapeDtypeStruct((b, ATT_HEADS, n, LANES), BF16)
    vec = pl.BlockSpec((1, 1, d), lambda bi, i: (bi, 0, 0))
    tab = pl.BlockSpec((tn, LANES), lambda bi, i: (i, 0))
    out = pl.BlockSpec((1, ATT_HEADS, tn, LANES), lambda bi, i: (bi, 0, i, 0))
    return pl.pallas_call(
        functools.partial(_qkv_kernel, q_scale=q_scale),
        out_shape=(hd_shape, hd_shape, hd_shape),
        grid=(b, n // tn),
        in_specs=[
            pl.BlockSpec((1, tn, d), lambda bi, i: (bi, i, 0)),
            vec, vec,
            pl.BlockSpec((d, 3 * d), lambda bi, i: (0, 0)),
            tab, tab, tab,
        ],
        out_specs=(out, out, out),
        compiler_params=_cparams("parallel", "parallel"),
        name="qkv_proj",
    )(x, sc, sh, w_bf, cos, slo, shi)


def _kv_ctx_kernel(x_ref, sc_ref, sh_ref, w_ref, k_ref, v_ref):
    d = x_ref.shape[-1]
    h = (x_ref[0] * (1.0 + sc_ref[0]) + sh_ref[0]).astype(BF16)
    k = jnp.dot(h, w_ref[:, 0:d], preferred_element_type=F32)
    v = jnp.dot(h, w_ref[:, d:2 * d], preferred_element_type=F32)
    for hd in range(ATT_HEADS):
        k_ref[0, hd] = k[:, hd * LANES:(hd + 1) * LANES].astype(BF16)
        v_ref[0, hd] = v[:, hd * LANES:(hd + 1) * LANES].astype(BF16)


def _kv_ctx_proj(ctx, sc, sh, w_kv_bf):
    b, c, d = ctx.shape
    hd_shape = jax.ShapeDtypeStruct((b, ATT_HEADS, c, LANES), BF16)
    vec = pl.BlockSpec((1, 1, d), lambda bi: (0, 0, 0))
    out = pl.BlockSpec((1, ATT_HEADS, c, LANES), lambda bi: (bi, 0, 0, 0))
    return pl.pallas_call(
        _kv_ctx_kernel,
        out_shape=(hd_shape, hd_shape),
        grid=(b,),
        in_specs=[
            pl.BlockSpec((1, c, d), lambda bi: (bi, 0, 0)),
            vec, vec,
            pl.BlockSpec((d, 2 * d), lambda bi: (0, 0)),
        ],
        out_specs=(out, out),
        compiler_params=_cparams("parallel"),
        name="kv_ctx_proj",
    )(ctx, sc, sh, w_kv_bf)


def _attn_finish(acc_a, l_a, acc_b, l_b, lamp_ref, g_ref, lam_init):
    lp = lamp_ref[...]
    lam = (jnp.exp(jnp.sum(lp[0:1] * lp[1:2], axis=-1, keepdims=True))
           - jnp.exp(jnp.sum(lp[2:3] * lp[3:4], axis=-1, keepdims=True)) + lam_init)
    o = acc_a / l_a - lam * (acc_b / l_b)
    o = o * lax.rsqrt(jnp.mean(o * o, axis=-1, keepdims=True) + LN_EPS) * g_ref[...] * (1.0 - lam_init)
    return o.astype(BF16)


def _attn_kernel(q_ref, k_ref, v_ref, kc_ref, vc_ref, lamp_ref, g_ref, o_ref,
                 s00, s01, s10, s11, p0, p1, mb0, mb1, lp0, lp1, acc0, acc1, *, tk, lam_init):
    q = q_ref[0, 0]
    tq = q.shape[0]
    half = LANES // 2
    lane = lax.broadcasted_iota(jnp.int32, q.shape, 1)
    zero = jnp.zeros_like(q)
    qs = (jnp.where(lane < half, q, zero), jnp.where(lane >= half, q, zero))
    n_chunks = k_ref.shape[2] // tk
    s_scr = ((s00, s01), (s10, s11))
    p_scr, mb_scr, lp_scr, acc_scr = (p0, p1), (mb0, mb1), (lp0, lp1), (acc0, acc1)
    nt = (((1,), (1,)), ((), ()))

    def scores(slot, kc):
        width = kc.shape[0]
        for mp in range(2):
            s_scr[slot][mp][:, 0:width] = lax.dot_general(qs[mp], kc, nt, preferred_element_type=F32)

    def absorb(slot, vc):
        width = vc.shape[0]
        for mp in range(2):
            for r in range(tq // ATT_STRIP):
                rows = slice(r * ATT_STRIP, (r + 1) * ATT_STRIP)
                shift = mb_scr[mp][rows, :]
                tiles = [jnp.exp2(s_scr[slot][mp][rows, t * LANES:(t + 1) * LANES] - shift)
                         for t in range(width // LANES)]
                p_scr[mp][rows, 0:width] = jnp.concatenate(tiles, axis=1).astype(BF16)
                lp_scr[mp][rows, :] += functools.reduce(lambda x, y: x + y, tiles)
            acc_scr[mp][...] += jnp.dot(p_scr[mp][:, 0:width], vc, preferred_element_type=F32)

    def k_chunk(j):
        return k_ref[0, 0, pl.ds(pl.multiple_of(j * tk, tk), tk), :]

    def v_chunk(j):
        return v_ref[0, 0, pl.ds(pl.multiple_of(j * tk, tk), tk), :]

    scores(0, k_chunk(0))
    for mp in range(2):
        mb_scr[mp][...] = jnp.broadcast_to(jnp.max(s_scr[0][mp][...], axis=-1, keepdims=True), (tq, LANES))
        lp_scr[mp][...] = jnp.zeros((tq, LANES), F32)
        acc_scr[mp][...] = jnp.zeros((tq, LANES), F32)

    def pair(jj, c):
        j0 = 2 * jj
        scores(1, k_chunk(j0 + 1))
        absorb(0, v_chunk(j0))
        scores(0, k_chunk(j0 + 2))
        absorb(1, v_chunk(j0 + 1))
        return c

    lax.fori_loop(0, n_chunks // 2 - 1, pair, 0)
    scores(1, k_chunk(n_chunks - 1))
    absorb(0, v_chunk(n_chunks - 2))
    scores(0, kc_ref[0, 0])
    absorb(1, v_chunk(n_chunks - 1))
    absorb(0, vc_ref[0, 0])

    sums = [jnp.sum(lp_scr[mp][...], axis=-1, keepdims=True) for mp in range(2)]
    bad = sum(jnp.sum(jnp.where(jnp.isfinite(x), 0.0, 1.0)) for x in (sums[0], sums[1], acc0[...], acc1[...]))

    @pl.when(bad == 0.0)
    def _():
        o_ref[0] = _attn_finish(acc0[...], sums[0], acc1[...], sums[1], lamp_ref, g_ref, lam_init)

    @pl.when(bad != 0.0)
    def _():
        def update(carry, kc, vc):
            new = []
            for mp in range(2):
                m, l, acc = carry[mp]
                s = lax.dot_general(qs[mp], kc, nt, preferred_element_type=F32)
                mn = jnp.maximum(m, jnp.max(s, axis=-1, keepdims=True))
                a = jnp.exp2(m - mn)
                p = jnp.exp2(s - mn)
                new.append((mn, a * l + jnp.sum(p, axis=-1, keepdims=True),
                            a * acc + jnp.dot(p.astype(BF16), vc, preferred_element_type=F32)))
            return tuple(new)

        init = tuple((jnp.full((tq, 1), NEG_INF, F32), jnp.zeros((tq, 1), F32), jnp.zeros((tq, LANES), F32))
                     for _ in range(2))
        carry = lax.fori_loop(0, n_chunks, lambda j, c: update(c, k_chunk(j), v_chunk(j)), init)
        (_, l_a, acc_a), (_, l_b, acc_b) = update(carry, kc_ref[0, 0], vc_ref[0, 0])
        o_ref[0] = _attn_finish(acc_a, l_a, acc_b, l_b, lamp_ref, g_ref, lam_init)


def _diff_attention(q, k, v, kc, vc, lam_p, subln_g, lam_init, tq, tk):
    b, h, n, _ = q.shape
    c = kc.shape[2]
    assert n % tk == 0 and (n // tk) % 2 == 0 and c <= tk and tq % ATT_STRIP == 0
    kv = pl.BlockSpec((1, 1, n, LANES), lambda bi, hi, i: (bi, hi, 0, 0))
    kvc = pl.BlockSpec((1, 1, c, LANES), lambda bi, hi, i: (bi, hi, 0, 0))
    return pl.pallas_call(
        functools.partial(_attn_kernel, tk=tk, lam_init=lam_init),
        out_shape=jax.ShapeDtypeStruct((b, n, h * LANES), BF16),
        grid=(b, h, n // tq),
        in_specs=[
            pl.BlockSpec((1, 1, tq, LANES), lambda bi, hi, i: (bi, hi, i, 0)),
            kv, kv, kvc, kvc,
            pl.BlockSpec(lam_p.shape, lambda bi, hi, i: (0, 0)),
            pl.BlockSpec((1, LANES), lambda bi, hi, i: (0, 0)),
        ],
        out_specs=pl.BlockSpec((1, tq, LANES), lambda bi, hi, i: (bi, i, hi)),
        scratch_shapes=[
            *[pltpu.VMEM((tq, tk), F32)] * 4,
            *[pltpu.VMEM((tq, tk), BF16)] * 2,
            *[pltpu.VMEM((tq, LANES), F32)] * 2,
            *[pltpu.VMEM((tq, LANES), F32)] * 2,
            *[pltpu.VMEM((tq, LANES), F32)] * 2,
        ],
        compiler_params=_cparams("parallel", "parallel", "parallel"),
        name="diff_attention",
    )(q, k, v, kc, vc, lam_p, subln_g.reshape(1, LANES))


def _sgu_kernel(x_ref, sc_ref, sh_ref, w_ref, b_ref, ng_ref, nb_ref, ws_ref, bs_ref, t_ref):
    f = t_ref.shape[-1]
    cg = f // SGU_GROUPS
    tm = x_ref.shape[1]
    h = (x_ref[0] * (1.0 + sc_ref[0]) + sh_ref[0]).astype(BF16)
    z = jnp.dot(h, w_ref[...], preferred_element_type=F32) + b_ref[...]
    z = 0.5 * z * (1.0 + lax.erf(z * (2.0 ** -0.5)))
    u = z[:, :f]
    v = _layer_norm(z[:, f:], ng_ref[...], nb_ref[...]).astype(BF16)
    for c in range(tm // SGU_CHUNK):
        rows = slice(c * SGU_CHUNK, (c + 1) * SGU_CHUNK)
        for g in range(SGU_GROUPS):
            cols = slice(g * cg, (g + 1) * cg)
            vm = jnp.dot(ws_ref[g], v[rows, cols], preferred_element_type=F32) + bs_ref[:, g:g + 1]
            t_ref[0, rows, cols] = (u[rows, cols] * vm).astype(BF16)


def _sgu_mixer(x, sc, sh, w_in_bf, b_in, norm_g, norm_b, w_s_bf, b_s_t, tm):
    b, n, d = x.shape
    f2 = w_in_bf.shape[1]
    f = f2 // 2
    vec = pl.BlockSpec((1, 1, d), lambda bi, i: (bi, 0, 0))
    full2 = lambda a: pl.BlockSpec(a.shape, lambda bi, i: (0,) * a.ndim)
    b_in2, ng2, nb2 = b_in.reshape(1, f2), norm_g.reshape(1, f), norm_b.reshape(1, f)
    return pl.pallas_call(
        _sgu_kernel,
        out_shape=jax.ShapeDtypeStruct((b, n, f), BF16),
        grid=(b, n // tm),
        in_specs=[
            pl.BlockSpec((1, tm, d), lambda bi, i: (bi, i, 0)),
            vec, vec,
            full2(w_in_bf), full2(b_in2), full2(ng2), full2(nb2), full2(w_s_bf), full2(b_s_t),
        ],
        out_specs=pl.BlockSpec((1, tm, f), lambda bi, i: (bi, i, 0)),
        compiler_params=_cparams("parallel", "parallel"),
        name="sgu_mixer",
    )(x, sc, sh, w_in_bf, b_in2, ng2, nb2, w_s_bf, b_s_t)


def _post_kernel(pre_ref, w_ref, x_ref, gm_ref, lg_ref, lb_ref, scf_ref, shf_ref, x1_ref, hf_ref, *, alpha):
    y = jnp.dot(pre_ref[0], w_ref[...], preferred_element_type=F32)
    x1 = _layer_norm(alpha * x_ref[0] + gm_ref[0] * y, lg_ref[...], lb_ref[...])
    x1_ref[0] = x1
    hf_ref[0] = _pack_bf16_pairs(x1 * (1.0 + scf_ref[0]) + shf_ref[0])


def _post_mixer(pre, w_bf, x, gm, ln_g, ln_b, scf, shf, alpha, tm):
    b, n, d = x.shape
    kd = pre.shape[-1]
    vec = pl.BlockSpec((1, 1, d), lambda bi, i: (bi, 0, 0))
    row = pl.BlockSpec((1, d), lambda bi, i: (0, 0))
    tile = pl.BlockSpec((1, tm, d), lambda bi, i: (bi, i, 0))
    return pl.pallas_call(
        functools.partial(_post_kernel, alpha=alpha),
        out_shape=(jax.ShapeDtypeStruct((b, n, d), F32), jax.ShapeDtypeStruct((b, n, d // 2), I32)),
        grid=(b, n // tm),
        in_specs=[
            pl.BlockSpec((1, tm, kd), lambda bi, i: (bi, i, 0)),
            pl.BlockSpec((kd, d), lambda bi, i: (0, 0)),
            tile, vec, row, row, vec, vec,
        ],
        out_specs=(tile, pl.BlockSpec((1, tm, d // 2), lambda bi, i: (bi, i, 0))),
        compiler_params=_cparams("parallel", "parallel"),
        name="post_mixer",
    )(pre, w_bf, x, gm, ln_g.reshape(1, d), ln_b.reshape(1, d), scf, shf)


def _route_kernel(x_ref, sc_ref, sh_ref, wr_ref, rb_ref, idx_ref, w_ref, rank_ref, cnt_ref, carry_ref):
    i = pl.program_id(0)
    e = wr_ref.shape[0]
    tm = x_ref.shape[0]
    ge = e // N_GROUPS

    @pl.when(i == 0)
    def _():
        carry_ref[...] = jnp.zeros_like(carry_ref)

    h = x_ref[...] * (1.0 + sc_ref[0]) + sh_ref[0]
    logits = lax.dot_general(wr_ref[...], h, (((1,), (1,)), ((), ())),
                             precision=lax.Precision.HIGHEST, preferred_element_type=F32)
    scores = 1.0 / (1.0 + jnp.exp(-logits))
    choice = scores + rb_ref[...]

    g3 = choice.reshape(N_GROUPS, ge, tm)
    ri = lax.broadcasted_iota(jnp.int32, g3.shape, 1).astype(F32)
    m1 = jnp.max(g3, axis=1, keepdims=True)
    first = jnp.min(jnp.where(g3 == m1, ri, float(ge)), axis=1, keepdims=True)
    m2 = jnp.max(jnp.where(ri == first, NEG_INF, g3), axis=1, keepdims=True)
    gs = m1 + m2

    gi = lax.broadcasted_iota(jnp.int32, gs.shape, 0).astype(F32)
    gsel = jnp.zeros(gs.shape, F32)
    cur = gs
    for _ in range(TOPK_GROUPS):
        m = jnp.max(cur, axis=0, keepdims=True)
        f = jnp.min(jnp.where(cur == m, gi, float(N_GROUPS)), axis=0, keepdims=True)
        hit = gi == f
        gsel = jnp.where(hit, 1.0, gsel)
        cur = jnp.where(hit, NEG_INF, cur)
    emask = jnp.broadcast_to(gsel, g3.shape).reshape(e, tm)
    masked = jnp.where(emask > 0.5, choice, NEG_INF)

    ei = lax.broadcasted_iota(jnp.int32, (e, tm), 0).astype(F32)
    onehot = jnp.zeros((e, tm), F32)
    idxs, ws = [], []
    for _ in range(TOP_K):
        m = jnp.max(masked, axis=0, keepdims=True)
        f = jnp.min(jnp.where(masked == m, ei, float(e)), axis=0, keepdims=True)
        hit = ei == f
        idxs.append(f)
        ws.append(jnp.sum(jnp.where(hit, scores, 0.0), axis=0, keepdims=True))
        masked = jnp.where(hit, NEG_INF, masked)
        onehot = jnp.where(hit, 1.0, onehot)
    wsum = ws[0]
    for k in range(1, TOP_K):
        wsum = wsum + ws[k]

    r_i = lax.broadcasted_iota(jnp.int32, (tm, tm), 0)
    c_i = lax.broadcasted_iota(jnp.int32, (tm, tm), 1)
    upper = jnp.where(r_i < c_i, 1.0, 0.0).astype(BF16)
    rk = jnp.dot(onehot.astype(BF16), upper, preferred_element_type=F32) + carry_ref[...]
    carry_ref[...] += jnp.sum(onehot, axis=1, keepdims=True)

    for k in range(TOP_K):
        idx_ref[k:k + 1, :] = idxs[k].astype(jnp.int32)
        w_ref[k:k + 1, :] = ws[k] / wsum * ROUTED_SCALE
        rank_ref[k:k + 1, :] = jnp.sum(jnp.where(ei == idxs[k], rk, 0.0), axis=0, keepdims=True).astype(jnp.int32)
    cnt_ref[...] = jnp.broadcast_to(carry_ref[...], cnt_ref.shape).astype(jnp.int32)


def _route(x1, scf, shf, wr_t, rbias, tm):
    b, n, d = x1.shape
    t = b * n
    e = wr_t.shape[0]
    per_b = n // tm
    vec = pl.BlockSpec((1, 1, d), lambda i: (i // per_b, 0, 0))
    out_t = pl.BlockSpec((TOP_K, tm), lambda i: (0, i))
    return pl.pallas_call(
        _route_kernel,
        out_shape=(jax.ShapeDtypeStruct((TOP_K, t), jnp.int32), jax.ShapeDtypeStruct((TOP_K, t), F32),
                   jax.ShapeDtypeStruct((TOP_K, t), jnp.int32), jax.ShapeDtypeStruct((e, LANES), jnp.int32)),
        grid=(t // tm,),
        in_specs=[
            pl.BlockSpec((tm, d), lambda i: (i, 0)),
            vec, vec,
            pl.BlockSpec((e, d), lambda i: (0, 0)),
            pl.BlockSpec((e, 1), lambda i: (0, 0)),
        ],
        out_specs=(out_t, out_t, out_t, pl.BlockSpec((e, LANES), lambda i: (0, 0))),
        scratch_shapes=[pltpu.VMEM((e, 1), F32)],
        compiler_params=_cparams("arbitrary"),
        name="route",
    )(x1.reshape(t, d), scf, shf, wr_t, rbias.reshape(e, 1))


def _gather_rows(table, idx):
    m = idx.shape[0]
    w = table.shape[1]
    workers = SC_CORES * SC_SUBCORES
    per_w = m // workers
    assert m % (workers * SC_CHUNK) == 0
    mesh = plsc.VectorSubcoreMesh(core_axis_name="c", subcore_axis_name="s",
                                  num_cores=SC_CORES, num_subcores=SC_SUBCORES)

    @functools.partial(
        pl.kernel, mesh=mesh,
        out_type=jax.ShapeDtypeStruct((m, w), table.dtype),
        scratch_types=[
            pltpu.VMEM((SC_CHUNK,), I32),
            pltpu.VMEM((SC_CHUNK, w), table.dtype),
            pltpu.SemaphoreType.DMA,
        ],
        name="sc_gather_rows",
    )
    def gather(table_hbm, idx_hbm, out_hbm, idx_v, rows_v, sem):
        base = (lax.axis_index("s") * SC_CORES + lax.axis_index("c")) * per_w

        @pl.loop(0, per_w // SC_CHUNK)
        def _(j):
            off = pl.multiple_of(base + j * SC_CHUNK, SC_CHUNK)
            pltpu.sync_copy(idx_hbm.at[pl.ds(off, SC_CHUNK)], idx_v)
            pltpu.async_copy(table_hbm.at[idx_v], rows_v, sem).wait()
            pltpu.sync_copy(rows_v, out_hbm.at[pl.ds(off, SC_CHUNK)])

    return gather(table, idx)


def _scatter_rows(src, pos3, p):
    w = src.shape[1]
    n_chunks, k, ch = pos3.shape
    workers = SC_CORES * SC_SUBCORES
    per_w = n_chunks // workers
    assert ch == SC_CHUNK and n_chunks % workers == 0 and src.shape[0] == n_chunks * ch
    mesh = plsc.VectorSubcoreMesh(core_axis_name="c", subcore_axis_name="s",
                                  num_cores=SC_CORES, num_subcores=SC_SUBCORES)

    @functools.partial(
        pl.kernel, mesh=mesh,
        out_type=jax.ShapeDtypeStruct((p, w), src.dtype),
        scratch_types=[
            pltpu.VMEM((k, ch), I32),
            pltpu.VMEM((ch, w), src.dtype),
            pltpu.SemaphoreType.DMA,
        ],
        name="sc_scatter_rows",
    )
    def scatter(src_hbm, pos_hbm, out_hbm, idx_v, rows_v, sem):
        first = (lax.axis_index("s") * SC_CORES + lax.axis_index("c")) * per_w

        @pl.loop(0, per_w)
        def _(j):
            c = first + j
            pltpu.sync_copy(pos_hbm.at[c], idx_v)
            pltpu.sync_copy(src_hbm.at[pl.ds(pl.multiple_of(c * ch, ch), ch)], rows_v)
            copies = [pltpu.async_copy(rows_v, out_hbm.at[idx_v.at[kk]], sem) for kk in range(k)]
            for cp in copies:
                cp.wait()

    return scatter(src, pos3)


def _experts_kernel(ps_ref, nb_ref, cnt_ref, xs_hbm, wg_ref, wu_ref, wd_ref, y_hbm,
                    xbuf, ybuf, in_sem, out_sem, wg_bf, wu_bf, wd_bf):
    e = pl.program_id(0)
    nb, cnt = nb_ref[e], cnt_ref[e]
    base = pl.multiple_of(ps_ref[e], MOE_ROWS)

    def in_copy(b, slot):
        rows = pl.ds(pl.multiple_of(base + b * MOE_ROWS, MOE_ROWS), MOE_ROWS)
        return pltpu.make_async_copy(xs_hbm.at[rows], xbuf.at[slot], in_sem.at[slot])

    def out_copy(b, slot):
        rows = pl.ds(pl.multiple_of(base + b * MOE_ROWS, MOE_ROWS), MOE_ROWS)
        return pltpu.make_async_copy(ybuf.at[slot], y_hbm.at[rows], out_sem.at[slot])

    @pl.when(nb > 0)
    def _():
        in_copy(0, 0).start()
        wg_bf[...] = wg_ref[0, 0].astype(BF16)
        wu_bf[...] = wu_ref[0, 0].astype(BF16)
        wd_bf[...] = wd_ref[0, 0].astype(BF16)

        def body(b, c):
            slot = b % 2

            @pl.when(b + 1 < nb)
            def _():
                in_copy(b + 1, 1 - slot).start()

            in_copy(b, slot).wait()

            @pl.when(b >= 2)
            def _():
                out_copy(b - 2, slot).wait()

            row = lax.broadcasted_iota(I32, (MOE_ROWS, 1), 0) + b * MOE_ROWS
            x_lo, x_hi = (v.astype(BF16) for v in _unpack_bf16_pairs(jnp.where(row < cnt, xbuf[slot], 0)))
            half = x_lo.shape[1]

            def up(w_bf):
                return (jnp.dot(x_lo, w_bf[:half, :], preferred_element_type=F32)
                        + jnp.dot(x_hi, w_bf[half:, :], preferred_element_type=F32))

            hb = (_silu(up(wg_bf)) * up(wu_bf)).astype(BF16)
            ybuf[slot] = _pack_bf16_pairs(jnp.dot(hb, wd_bf[...], preferred_element_type=F32))
            out_copy(b, slot).start()
            return c

        lax.fori_loop(0, nb, body, 0)

        @pl.when(nb >= 2)
        def _():
            out_copy(nb - 2, nb % 2).wait()

        out_copy(nb - 1, (nb - 1) % 2).wait()


def _routed_experts(xs, wg, wu, wd, layer, pstarts, nblk, counts):
    p, dp = xs.shape
    _, e, d, f = wg.shape
    grid_spec = pltpu.PrefetchScalarGridSpec(
        num_scalar_prefetch=3,
        grid=(e,),
        in_specs=[
            pl.BlockSpec(memory_space=pl.ANY),
            pl.BlockSpec((1, 1, d, f), lambda i, ps, nb, cnt: (layer, i, 0, 0)),
            pl.BlockSpec((1, 1, d, f), lambda i, ps, nb, cnt: (layer, i, 0, 0)),
            pl.BlockSpec((1, 1, f, d), lambda i, ps, nb, cnt: (layer, i, 0, 0)),
        ],
        out_specs=pl.BlockSpec(memory_space=pl.ANY),
        scratch_shapes=[
            pltpu.VMEM((2, MOE_ROWS, dp), I32), pltpu.VMEM((2, MOE_ROWS, dp), I32),
            pltpu.SemaphoreType.DMA((2,)), pltpu.SemaphoreType.DMA((2,)),
            pltpu.VMEM((d, f), BF16), pltpu.VMEM((d, f), BF16), pltpu.VMEM((f, d), BF16),
        ],
    )
    return pl.pallas_call(
        _experts_kernel,
        out_shape=jax.ShapeDtypeStruct((p, dp), I32),
        grid_spec=grid_spec,
        compiler_params=_cparams("arbitrary"),
        name="routed_experts",
    )(pstarts, nblk, counts, xs, wg, wu, wd)


def _combine_kernel(yg_ref, w_ref, hf_ref, sg_ref, su_ref, sd_ref, x_ref, gf_ref, lg_ref, lb_ref, o_ref, *, alpha):
    w = w_ref[...]
    r_lo, r_hi = _unpack_bf16_pairs(yg_ref[0])
    r_lo, r_hi = w[:, 0:1] * r_lo, w[:, 0:1] * r_hi
    for k in range(1, TOP_K):
        y_lo, y_hi = _unpack_bf16_pairs(yg_ref[k])
        r_lo, r_hi = r_lo + w[:, k:k + 1] * y_lo, r_hi + w[:, k:k + 1] * y_hi
    routed = jnp.concatenate([r_lo, r_hi], axis=1)
    hf = jnp.concatenate(_unpack_bf16_pairs(hf_ref[...]), axis=1).astype(BF16)
    g = jnp.dot(hf, sg_ref[...], preferred_element_type=F32)
    u = jnp.dot(hf, su_ref[...], preferred_element_type=F32)
    shared = jnp.dot((_silu(g) * u).astype(BF16), sd_ref[...], preferred_element_type=F32)
    o_ref[...] = _layer_norm(alpha * x_ref[...] + gf_ref[0] * (routed + shared), lg_ref[...], lb_ref[...])


def _combine(yg, w_tk, hf, sg_bf, su_bf, sd_bf, x1, gf, ln_g, ln_b, alpha, tm, per_b):
    t, d = x1.shape
    f = sg_bf.shape[1]
    row = pl.BlockSpec((1, d), lambda i: (0, 0))
    tile = pl.BlockSpec((tm, d), lambda i: (i, 0))
    return pl.pallas_call(
        functools.partial(_combine_kernel, alpha=alpha),
        out_shape=jax.ShapeDtypeStruct((t, d), F32),
        grid=(t // tm,),
        in_specs=[
            pl.BlockSpec((TOP_K, tm, d // 2), lambda i: (0, i, 0)),
            pl.BlockSpec((tm, TOP_K), lambda i: (i, 0)),
            pl.BlockSpec((tm, d // 2), lambda i: (i, 0)),
            pl.BlockSpec((d, f), lambda i: (0, 0)),
            pl.BlockSpec((d, f), lambda i: (0, 0)),
            pl.BlockSpec((f, d), lambda i: (0, 0)),
            tile,
            pl.BlockSpec((1, 1, d), lambda i: (i // per_b, 0, 0)),
            row, row,
        ],
        out_specs=tile,
        compiler_params=_cparams("parallel"),
        name="moe_combine",
    )(yg, w_tk, hf, sg_bf, su_bf, sd_bf, x1, gf, ln_g.reshape(1, d), ln_b.reshape(1, d))


def _moe_layer(x1, hf, scf, shf, gf, router_w, router_bias, wg, wu, wd, layer, sg, su, sd, ln_g, ln_b, alpha, tm):
    b, n, d = x1.shape
    t = b * n
    e = router_w.shape[1]
    idx_t, w_t, rank_t, cnt = _route(x1, scf, shf, router_w.T, router_bias, tm)

    counts = cnt[:, 0]
    padded = (counts + MOE_ROWS - 1) // MOE_ROWS * MOE_ROWS
    pends = jnp.cumsum(padded)
    pstarts = pends - padded
    sel = idx_t[:, :, None] == jnp.arange(e, dtype=I32)
    pos_t = jnp.sum(jnp.where(sel, pstarts, 0), axis=-1) + rank_t
    p = t * TOP_K + e * MOE_ROWS
    pos3 = pos_t.reshape(TOP_K, t // SC_CHUNK, SC_CHUNK).transpose(1, 0, 2)

    hf2 = hf.reshape(t, d // 2)
    xs = _scatter_rows(hf2, pos3, p)
    yb = _routed_experts(xs, wg, wu, wd, layer, pstarts.astype(I32), (padded // MOE_ROWS).astype(I32), counts)
    yg = _gather_rows(yb, pos_t.reshape(-1)).reshape(TOP_K, t, d // 2)
    out = _combine(yg, w_t.T, hf2, sg.astype(BF16), su.astype(BF16), sd.astype(BF16), x1.reshape(t, d), gf,
                   ln_g, ln_b, alpha, tm, n // tm)
    return out.reshape(b, n, d)


def _rope_tables(n):
    rows = n // GRID_W
    row_pos = jnp.repeat(jnp.arange(rows, dtype=F32), GRID_W)
    col_pos = jnp.tile(jnp.arange(GRID_W, dtype=F32), rows)
    half = LANES // 4
    lane = jnp.arange(LANES)
    in_blk = lane % half
    freq = ROPE_THETA ** (-(2.0 * (in_blk % (half // 2)).astype(F32)) / half)
    use_col = (lane // half) % 2 == 1
    pos = jnp.where(use_col[None, :], col_pos[:, None], row_pos[:, None])
    ang = pos * freq[None, :]
    lo = (in_blk < half // 2)[None, :]
    sin = jnp.sin(ang)
    return jnp.cos(ang), jnp.where(lo, -sin, 0.0), jnp.where(lo, 0.0, sin)


def kernel(x, c, ctx, c_ctx, w_mod, b_mod, ln_g, ln_b, attn_w_in, attn_w_out, attn_lambda, attn_subln_g,
           sgu_w_in, sgu_b_in, sgu_norm_g, sgu_norm_b, sgu_w_s, sgu_b_s, sgu_w_out,
           router_w, router_bias, exp_w_gate, exp_w_up, exp_w_down, sh_w_gate, sh_w_up, sh_w_down):
    b, n, d = x.shape
    depth = w_mod.shape[0]
    assert b <= 7 and d == ATT_HEADS * LANES and n % GRID_W == 0
    alpha = (2 * depth) ** 0.25
    head_dim = d // ATT_HEADS // 2
    tm = 512 if n % 512 == 0 else 256

    cs = jnp.zeros((8, d), F32).at[:b].set(c).at[b].set(c_ctx)
    mods = _modulation(cs, w_mod, b_mod)

    def mod_vec(i, j):
        return mods[i, :, j * d:(j + 1) * d].reshape(8, 1, d)

    for i in range(depth):
        sh_m, sc_m, g_m, sh_f, sc_f, g_f = (mod_vec(i, j) for j in range(6))
        if i % N_MIXERS == 0:
            a = i // N_MIXERS
            lam_init = 0.8 - 0.6 * math.exp(-0.3 * i)
            w_in_bf = attn_w_in[a].astype(BF16)
            cos, slo, shi = _rope_tables(n)
            q_scale = head_dim ** -0.5 * math.log2(math.e)
            q, k, v = _qkv_proj(x, sc_m, sh_m, w_in_bf, cos, slo, shi, q_scale, tm)
            kc, vc = _kv_ctx_proj(ctx, sc_m[b:b + 1], sh_m[b:b + 1], w_in_bf[:, d:])
            tk = min(512, n // 2)
            pre = _diff_attention(q, k, v, kc, vc, attn_lambda[a], attn_subln_g[a], lam_init, min(512, n), tk)
            w_out_bf = attn_w_out[a].astype(BF16)
        else:
            s = i // N_MIXERS
            pre = _sgu_mixer(x, sc_m, sh_m, sgu_w_in[s].astype(BF16), sgu_b_in[s], sgu_norm_g[s], sgu_norm_b[s],
                             sgu_w_s[s].astype(BF16), sgu_b_s[s].T, 256)
            w_out_bf = sgu_w_out[s].astype(BF16)
        x1, hf = _post_mixer(pre, w_out_bf, x, g_m, ln_g[i, 0], ln_b[i, 0], sc_f, sh_f, alpha, tm)
        x = _moe_layer(x1, hf, sc_f, sh_f, g_f, router_w[i], router_bias[i], exp_w_gate, exp_w_up, exp_w_down, i,
                       sh_w_gate[i], sh_w_up[i], sh_w_down[i], ln_g[i, 1], ln_b[i, 1], alpha, tm)
    return x
```

```python
import functools
import math

import jax
import jax.numpy as jnp
from jax import lax
from jax.experimental import pallas as pl
from jax.experimental.pallas import tpu as pltpu
from jax.experimental.pallas import tpu_sc as plsc

F32 = jnp.float32
BF16 = jnp.bfloat16
I32 = jnp.int32

GRID_W = 64
ATT_HEADS = 8
ROPE_THETA = 10000.0
SGU_CHUNK = 128
SGU_GROUPS = 8
TOP_K = 8
N_GROUPS = 8
TOPK_GROUPS = 4
ROUTED_SCALE = 2.5
LN_EPS = 1e-5
N_MIXERS = 2

LANES = 128
MOE_ROWS = 256
MOE_LOOKAHEAD = 3
MOE_SLOTS = MOE_LOOKAHEAD + 1
ATT_STRIP = 32
SC_CORES = 2
SC_SUBCORES = 16
SC_CHUNK = 64
VMEM_LIMIT = 56 * 1024 * 1024
NEG_INF = float("-inf")


def _cparams(*sem):
    return pltpu.CompilerParams(dimension_semantics=sem, vmem_limit_bytes=VMEM_LIMIT)


def _layer_norm(z, g, b):
    mu = jnp.mean(z, axis=-1, keepdims=True)
    zc = z - mu
    var = jnp.mean(zc * zc, axis=-1, keepdims=True)
    return zc * lax.rsqrt(var + LN_EPS) * g + b


def _silu(x):
    return x * (1.0 / (1.0 + jnp.exp(-x)))


_HIGH_HALF = -65536


def _pack_bf16_pairs(y):
    w = y.shape[1] // 2
    bits = lax.bitcast_convert_type(y.astype(BF16).astype(F32), I32)
    return lax.shift_right_logical(bits[:, :w], 16) | (bits[:, w:] & _HIGH_HALF)


def _unpack_bf16_pairs(p):
    return (lax.bitcast_convert_type(lax.shift_left(p, 16), F32),
            lax.bitcast_convert_type(p & _HIGH_HALF, F32))


def _mod_kernel(cs_ref, w_ref, b_ref, o_ref):
    s = _silu(cs_ref[...])
    o_ref[0] = jnp.dot(s, w_ref[0], precision=lax.Precision.HIGHEST,
                       preferred_element_type=F32) + b_ref[0]


def _modulation(cs, w_mod, b_mod):
    depth, d, n6 = w_mod.shape
    tn = n6 // 4
    return pl.pallas_call(
        _mod_kernel,
        out_shape=jax.ShapeDtypeStruct((depth, 8, n6), F32),
        grid=(depth, n6 // tn),
        in_specs=[
            pl.BlockSpec((8, d), lambda l, j: (0, 0)),
            pl.BlockSpec((1, d, tn), lambda l, j: (l, 0, j)),
            pl.BlockSpec((1, 1, tn), lambda l, j: (l, 0, j)),
        ],
        out_specs=pl.BlockSpec((1, 8, tn), lambda l, j: (l, 0, j)),
        compiler_params=_cparams("parallel", "parallel"),
        name="modulation",
    )(cs, w_mod, b_mod.reshape(depth, 1, n6))


def _rope(xh, cos, sin_lo, sin_hi):
    return xh * cos + pltpu.roll(xh, LANES - 16, 1) * sin_lo + pltpu.roll(xh, 16, 1) * sin_hi


def _qkv_kernel(x_ref, sc_ref, sh_ref, w_ref, cos_ref, slo_ref, shi_ref, q_ref, k_ref, v_ref, *, q_scale):
    d = x_ref.shape[-1]
    h = (x_ref[0] * (1.0 + sc_ref[0]) + sh_ref[0]).astype(BF16)
    cos, slo, shi = cos_ref[...], slo_ref[...], shi_ref[...]
    q = jnp.dot(h, w_ref[:, 0:d], preferred_element_type=F32)
    for hd in range(ATT_HEADS):
        q_ref[0, hd] = (_rope(q[:, hd * LANES:(hd + 1) * LANES], cos, slo, shi) * q_scale).astype(BF16)
    k = jnp.dot(h, w_ref[:, d:2 * d], preferred_element_type=F32)
    for hd in range(ATT_HEADS):
        k_ref[0, hd] = _rope(k[:, hd * LANES:(hd + 1) * LANES], cos, slo, shi).astype(BF16)
    v = jnp.dot(h, w_ref[:, 2 * d:3 * d], preferred_element_type=F32)
    for hd in range(ATT_HEADS):
        v_ref[0, hd] = v[:, hd * LANES:(hd + 1) * LANES].astype(BF16)


def _qkv_proj(x, sc, sh, w_bf, cos, slo, shi, q_scale, tn):
    b, n, d = x.shape
    hd_shape = jax.ShapeDtypeStruct((b, ATT_HEADS, n, LANES), BF16)
    vec = pl.BlockSpec((1, 1, d), lambda bi, i: (bi, 0, 0))
    tab = pl.BlockSpec((tn, LANES), lambda bi, i: (i, 0))
    out = pl.BlockSpec((1, ATT_HEADS, tn, LANES), lambda bi, i: (bi, 0, i, 0))
    return pl.pallas_call(
        functools.partial(_qkv_kernel, q_scale=q_scale),
        out_shape=(hd_shape, hd_shape, hd_shape),
        grid=(b, n // tn),
        in_specs=[
            pl.BlockSpec((1, tn, d), lambda bi, i: (bi, i, 0)),
            vec, vec,
            pl.BlockSpec((d, 3 * d), lambda bi, i: (0, 0)),
            tab, tab, tab,
        ],
        out_specs=(out, out, out),
        compiler_params=_cparams("parallel", "parallel"),
        name="qkv_proj",
    )(x, sc, sh, w_bf, cos, slo, shi)


def _kv_ctx_kernel(x_ref, sc_ref, sh_ref, w_ref, k_ref, v_ref):
    d = x_ref.shape[-1]
    h = (x_ref[0] * (1.0 + sc_ref[0]) + sh_ref[0]).astype(BF16)
    k = jnp.dot(h, w_ref[:, 0:d], preferred_element_type=F32)
    v = jnp.dot(h, w_ref[:, d:2 * d], preferred_element_type=F32)
    for hd in range(ATT_HEADS):
        k_ref[0, hd] = k[:, hd * LANES:(hd + 1) * LANES].astype(BF16)
        v_ref[0, hd] = v[:, hd * LANES:(hd + 1) * LANES].astype(BF16)


def _kv_ctx_proj(ctx, sc, sh, w_kv_bf):
    b, c, d = ctx.shape
    hd_shape = jax.ShapeDtypeStruct((b, ATT_HEADS, c, LANES), BF16)
    vec = pl.BlockSpec((1, 1, d), lambda bi: (0, 0, 0))
    out = pl.BlockSpec((1, ATT_HEADS, c, LANES), lambda bi: (bi, 0, 0, 0))
    return pl.pallas_call(
        _kv_ctx_kernel,
        out_shape=(hd_shape, hd_shape),
        grid=(b,),
        in_specs=[
            pl.BlockSpec((1, c, d), lambda bi: (bi, 0, 0)),
            vec, vec,
            pl.BlockSpec((d, 2 * d), lambda bi: (0, 0)),
        ],
        out_specs=(out, out),
        compiler_params=_cparams("parallel"),
        name="kv_ctx_proj",
    )(ctx, sc, sh, w_kv_bf)


def _attn_finish(acc_a, l_a, acc_b, l_b, lamp_ref, g_ref, lam_init):
    lp = lamp_ref[...]
    lam = (jnp.exp(jnp.sum(lp[0:1] * lp[1:2], axis=-1, keepdims=True))
           - jnp.exp(jnp.sum(lp[2:3] * lp[3:4], axis=-1, keepdims=True)) + lam_init)
    o = acc_a / l_a - lam * (acc_b / l_b)
    o = o * lax.rsqrt(jnp.mean(o * o, axis=-1, keepdims=True) + LN_EPS) * g_ref[...] * (1.0 - lam_init)
    return o.astype(BF16)


def _attn_kernel(q_ref, k_ref, v_ref, kc_ref, vc_ref, lamp_ref, g_ref, o_ref,
                 s00, s01, s10, s11, p0, p1, mb0, mb1, lp0, lp1, acc0, acc1, *, tk, lam_init):
    q = q_ref[0, 0]
    tq = q.shape[0]
    half = LANES // 2
    lane = lax.broadcasted_iota(jnp.int32, q.shape, 1)
    zero = jnp.zeros_like(q)
    qs = (jnp.where(lane < half, q, zero), jnp.where(lane >= half, q, zero))
    n_chunks = k_ref.shape[2] // tk
    s_scr = ((s00, s01), (s10, s11))
    p_scr, mb_scr, lp_scr, acc_scr = (p0, p1), (mb0, mb1), (lp0, lp1), (acc0, acc1)
    nt = (((1,), (1,)), ((), ()))

    def scores(slot, kc):
        width = kc.shape[0]
        for mp in range(2):
            s_scr[slot][mp][:, 0:width] = lax.dot_general(qs[mp], kc, nt, preferred_element_type=F32)

    def absorb(slot, vc):
        width = vc.shape[0]
        for mp in range(2):
            for r in range(tq // ATT_STRIP):
                rows = slice(r * ATT_STRIP, (r + 1) * ATT_STRIP)
                shift = mb_scr[mp][rows, :]
                tiles = [jnp.exp2(s_scr[slot][mp][rows, t * LANES:(t + 1) * LANES] - shift)
                         for t in range(width // LANES)]
                p_scr[mp][rows, 0:width] = jnp.concatenate(tiles, axis=1).astype(BF16)
                lp_scr[mp][rows, :] += functools.reduce(lambda x, y: x + y, tiles)
            acc_scr[mp][...] += jnp.dot(p_scr[mp][:, 0:width], vc, preferred_element_type=F32)

    def k_chunk(j):
        return k_ref[0, 0, pl.ds(pl.multiple_of(j * tk, tk), tk), :]

    def v_chunk(j):
        return v_ref[0, 0, pl.ds(pl.multiple_of(j * tk, tk), tk), :]

    scores(0, k_chunk(0))
    for mp in range(2):
        mb_scr[mp][...] = jnp.broadcast_to(jnp.max(s_scr[0][mp][...], axis=-1, keepdims=True), (tq, LANES))
        lp_scr[mp][...] = jnp.zeros((tq, LANES), F32)
        acc_scr[mp][...] = jnp.zeros((tq, LANES), F32)

    def pair(jj, c):
        j0 = 2 * jj
        scores(1, k_chunk(j0 + 1))
        absorb(0, v_chunk(j0))
        scores(0, k_chunk(j0 + 2))
        absorb(1, v_chunk(j0 + 1))
        return c

    lax.fori_loop(0, n_chunks // 2 - 1, pair, 0)
    scores(1, k_chunk(n_chunks - 1))
    absorb(0, v_chunk(n_chunks - 2))
    scores(0, kc_ref[0, 0])
    absorb(1, v_chunk(n_chunks - 1))
    absorb(0, vc_ref[0, 0])

    sums = [jnp.sum(lp_scr[mp][...], axis=-1, keepdims=True) for mp in range(2)]
    bad = sum(jnp.sum(jnp.where(jnp.isfinite(x), 0.0, 1.0)) for x in (sums[0], sums[1], acc0[...], acc1[...]))

    @pl.when(bad == 0.0)
    def _():
        o_ref[0] = _attn_finish(acc0[...], sums[0], acc1[...], sums[1], lamp_ref, g_ref, lam_init)

    @pl.when(bad != 0.0)
    def _():
        def update(carry, kc, vc):
            new = []
            for mp in range(2):
                m, l, acc = carry[mp]
                s = lax.dot_general(qs[mp], kc, nt, preferred_element_type=F32)
                mn = jnp.maximum(m, jnp.max(s, axis=-1, keepdims=True))
                a = jnp.exp2(m - mn)
                p = jnp.exp2(s - mn)
                new.append((mn, a * l + jnp.sum(p, axis=-1, keepdims=True),
                            a * acc + jnp.dot(p.astype(BF16), vc, preferred_element_type=F32)))
            return tuple(new)

        init = tuple((jnp.full((tq, 1), NEG_INF, F32), jnp.zeros((tq, 1), F32), jnp.zeros((tq, LANES), F32))
                     for _ in range(2))
        carry = lax.fori_loop(0, n_chunks, lambda j, c: update(c, k_chunk(j), v_chunk(j)), init)
        (_, l_a, acc_a), (_, l_b, acc_b) = update(carry, kc_ref[0, 0], vc_ref[0, 0])
        o_ref[0] = _attn_finish(acc_a, l_a, acc_b, l_b, lamp_ref, g_ref, lam_init)


def _diff_attention(q, k, v, kc, vc, lam_p, subln_g, lam_init, tq, tk):
    b, h, n, _ = q.shape
    c = kc.shape[2]
    assert n % tk == 0 and (n // tk) % 2 == 0 and c <= tk and tq % ATT_STRIP == 0
    kv = pl.BlockSpec((1, 1, n, LANES), lambda bi, hi, i: (bi, hi, 0, 0))
    kvc = pl.BlockSpec((1, 1, c, LANES), lambda bi, hi, i: (bi, hi, 0, 0))
    return pl.pallas_call(
        functools.partial(_attn_kernel, tk=tk, lam_init=lam_init),
        out_shape=jax.ShapeDtypeStruct((b, n, h * LANES), BF16),
        grid=(b, h, n // tq),
        in_specs=[
            pl.BlockSpec((1, 1, tq, LANES), lambda bi, hi, i: (bi, hi, i, 0)),
            kv, kv, kvc, kvc,
            pl.BlockSpec(lam_p.shape, lambda bi, hi, i: (0, 0)),
            pl.BlockSpec((1, LANES), lambda bi, hi, i: (0, 0)),
        ],
        out_specs=pl.BlockSpec((1, tq, LANES), lambda bi, hi, i: (bi, i, hi)),
        scratch_shapes=[
            *[pltpu.VMEM((tq, tk), F32)] * 4,
            *[pltpu.VMEM((tq, tk), BF16)] * 2,
            *[pltpu.VMEM((tq, LANES), F32)] * 2,
            *[pltpu.VMEM((tq, LANES), F32)] * 2,
            *[pltpu.VMEM((tq, LANES), F32)] * 2,
        ],
        compiler_params=_cparams("parallel", "parallel", "parallel"),
        name="diff_attention",
    )(q, k, v, kc, vc, lam_p, subln_g.reshape(1, LANES))


def _sgu_kernel(x_ref, sc_ref, sh_ref, w_ref, b_ref, ng_ref, nb_ref, ws_ref, bs_ref, t_ref):
    f = t_ref.shape[-1]
    cg = f // SGU_GROUPS
    tm = x_ref.shape[1]
    h = (x_ref[0] * (1.0 + sc_ref[0]) + sh_ref[0]).astype(BF16)
    z = jnp.dot(h, w_ref[...], preferred_element_type=F32) + b_ref[...]
    z = 0.5 * z * (1.0 + lax.erf(z * (2.0 ** -0.5)))
    u = z[:, :f]
    v = _layer_norm(z[:, f:], ng_ref[...], nb_ref[...]).astype(BF16)
    for c in range(tm // SGU_CHUNK):
        rows = slice(c * SGU_CHUNK, (c + 1) * SGU_CHUNK)
        for g in range(SGU_GROUPS):
            cols = slice(g * cg, (g + 1) * cg)
            vm = jnp.dot(ws_ref[g], v[rows, cols], preferred_element_type=F32) + bs_ref[:, g:g + 1]
            t_ref[0, rows, cols] = (u[rows, cols] * vm).astype(BF16)


def _sgu_mixer(x, sc, sh, w_in_bf, b_in, norm_g, norm_b, w_s_bf, b_s_t, tm):
    b, n, d = x.shape
    f2 = w_in_bf.shape[1]
    f = f2 // 2
    vec = pl.BlockSpec((1, 1, d), lambda bi, i: (bi, 0, 0))
    full2 = lambda a: pl.BlockSpec(a.shape, lambda bi, i: (0,) * a.ndim)
    b_in2, ng2, nb2 = b_in.reshape(1, f2), norm_g.reshape(1, f), norm_b.reshape(1, f)
    return pl.pallas_call(
        _sgu_kernel,
        out_shape=jax.ShapeDtypeStruct((b, n, f), BF16),
        grid=(b, n // tm),
        in_specs=[
            pl.BlockSpec((1, tm, d), lambda bi, i: (bi, i, 0)),
            vec, vec,
            full2(w_in_bf), full2(b_in2), full2(ng2), full2(nb2), full2(w_s_bf), full2(b_s_t),
        ],
        out_specs=pl.BlockSpec((1, tm, f), lambda bi, i: (bi, i, 0)),
        compiler_params=_cparams("parallel", "parallel"),
        name="sgu_mixer",
    )(x, sc, sh, w_in_bf, b_in2, ng2, nb2, w_s_bf, b_s_t)


def _post_kernel(pre_ref, w_ref, x_ref, gm_ref, lg_ref, lb_ref, scf_ref, shf_ref, x1_ref, hf_ref, *, alpha):
    y = jnp.dot(pre_ref[0], w_ref[...], preferred_element_type=F32)
    x1 = _layer_norm(alpha * x_ref[0] + gm_ref[0] * y, lg_ref[...], lb_ref[...])
    x1_ref[0] = x1
    hf_ref[0] = _pack_bf16_pairs(x1 * (1.0 + scf_ref[0]) + shf_ref[0])


def _post_mixer(pre, w_bf, x, gm, ln_g, ln_b, scf, shf, alpha, tm):
    b, n, d = x.shape
    kd = pre.shape[-1]
    vec = pl.BlockSpec((1, 1, d), lambda bi, i: (bi, 0, 0))
    row = pl.BlockSpec((1, d), lambda bi, i: (0, 0))
    tile = pl.BlockSpec((1, tm, d), lambda bi, i: (bi, i, 0))
    return pl.pallas_call(
        functools.partial(_post_kernel, alpha=alpha),
        out_shape=(jax.ShapeDtypeStruct((b, n, d), F32), jax.ShapeDtypeStruct((b, n, d // 2), I32)),
        grid=(b, n // tm),
        in_specs=[
            pl.BlockSpec((1, tm, kd), lambda bi, i: (bi, i, 0)),
            pl.BlockSpec((kd, d), lambda bi, i: (0, 0)),
            tile, vec, row, row, vec, vec,
        ],
        out_specs=(tile, pl.BlockSpec((1, tm, d // 2), lambda bi, i: (bi, i, 0))),
        compiler_params=_cparams("parallel", "parallel"),
        name="post_mixer",
    )(pre, w_bf, x, gm, ln_g.reshape(1, d), ln_b.reshape(1, d), scf, shf)


def _route_kernel(x_ref, sc_ref, sh_ref, wr_ref, rb_ref, idx_ref, w_ref, rank_ref, cnt_ref, carry_ref):
    i = pl.program_id(0)
    e = wr_ref.shape[0]
    tm = x_ref.shape[0]
    ge = e // N_GROUPS

    @pl.when(i == 0)
    def _():
        carry_ref[...] = jnp.zeros_like(carry_ref)

    h = x_ref[...] * (1.0 + sc_ref[0]) + sh_ref[0]
    logits = lax.dot_general(wr_ref[...], h, (((1,), (1,)), ((), ())),
                             precision=lax.Precision.HIGHEST, preferred_element_type=F32)
    scores = 1.0 / (1.0 + jnp.exp(-logits))
    choice = scores + rb_ref[...]

    g3 = choice.reshape(N_GROUPS, ge, tm)
    ri = lax.broadcasted_iota(jnp.int32, g3.shape, 1).astype(F32)
    m1 = jnp.max(g3, axis=1, keepdims=True)
    first = jnp.min(jnp.where(g3 == m1, ri, float(ge)), axis=1, keepdims=True)
    m2 = jnp.max(jnp.where(ri == first, NEG_INF, g3), axis=1, keepdims=True)
    gs = m1 + m2

    gi = lax.broadcasted_iota(jnp.int32, gs.shape, 0).astype(F32)
    gsel = jnp.zeros(gs.shape, F32)
    cur = gs
    for _ in range(TOPK_GROUPS):
        m = jnp.max(cur, axis=0, keepdims=True)
        f = jnp.min(jnp.where(cur == m, gi, float(N_GROUPS)), axis=0, keepdims=True)
        hit = gi == f
        gsel = jnp.where(hit, 1.0, gsel)
        cur = jnp.where(hit, NEG_INF, cur)
    emask = jnp.broadcast_to(gsel, g3.shape).reshape(e, tm)
    masked = jnp.where(emask > 0.5, choice, NEG_INF)

    ei = lax.broadcasted_iota(jnp.int32, (e, tm), 0).astype(F32)
    onehot = jnp.zeros((e, tm), F32)
    idxs, ws = [], []
    for _ in range(TOP_K):
        m = jnp.max(masked, axis=0, keepdims=True)
        f = jnp.min(jnp.where(masked == m, ei, float(e)), axis=0, keepdims=True)
        hit = ei == f
        idxs.append(f)
        ws.append(jnp.sum(jnp.where(hit, scores, 0.0), axis=0, keepdims=True))
        masked = jnp.where(hit, NEG_INF, masked)
        onehot = jnp.where(hit, 1.0, onehot)
    wsum = ws[0]
    for k in range(1, TOP_K):
        wsum = wsum + ws[k]

    r_i = lax.broadcasted_iota(jnp.int32, (tm, tm), 0)
    c_i = lax.broadcasted_iota(jnp.int32, (tm, tm), 1)
    upper = jnp.where(r_i < c_i, 1.0, 0.0).astype(BF16)
    rk = jnp.dot(onehot.astype(BF16), upper, preferred_element_type=F32) + carry_ref[...]
    carry_ref[...] += jnp.sum(onehot, axis=1, keepdims=True)

    for k in range(TOP_K):
        idx_ref[k:k + 1, :] = idxs[k].astype(jnp.int32)
        w_ref[k:k + 1, :] = ws[k] / wsum * ROUTED_SCALE
        rank_ref[k:k + 1, :] = jnp.sum(jnp.where(ei == idxs[k], rk, 0.0), axis=0, keepdims=True).astype(jnp.int32)
    cnt_ref[...] = jnp.broadcast_to(carry_ref[...], cnt_ref.shape).astype(jnp.int32)


def _route(x1, scf, shf, wr_t, rbias, tm):
    b, n, d = x1.shape
    t = b * n
    e = wr_t.shape[0]
    per_b = n // tm
    vec = pl.BlockSpec((1, 1, d), lambda i: (i // per_b, 0, 0))
    out_t = pl.BlockSpec((TOP_K, tm), lambda i: (0, i))
    return pl.pallas_call(
        _route_kernel,
        out_shape=(jax.ShapeDtypeStruct((TOP_K, t), jnp.int32), jax.ShapeDtypeStruct((TOP_K, t), F32),
                   jax.ShapeDtypeStruct((TOP_K, t), jnp.int32), jax.ShapeDtypeStruct((e, LANES), jnp.int32)),
        grid=(t // tm,),
        in_specs=[
            pl.BlockSpec((tm, d), lambda i: (i, 0)),
            vec, vec,
            pl.BlockSpec((e, d), lambda i: (0, 0)),
            pl.BlockSpec((e, 1), lambda i: (0, 0)),
        ],
        out_specs=(out_t, out_t, out_t, pl.BlockSpec((e, LANES), lambda i: (0, 0))),
        scratch_shapes=[pltpu.VMEM((e, 1), F32)],
        compiler_params=_cparams("arbitrary"),
        name="route",
    )(x1.reshape(t, d), scf, shf, wr_t, rbias.reshape(e, 1))


def _gather_rows(table, idx):
    m = idx.shape[0]
    w = table.shape[1]
    workers = SC_CORES * SC_SUBCORES
    per_w = m // workers
    assert m % (workers * SC_CHUNK) == 0
    mesh = plsc.VectorSubcoreMesh(core_axis_name="c", subcore_axis_name="s",
                                  num_cores=SC_CORES, num_subcores=SC_SUBCORES)

    @functools.partial(
        pl.kernel, mesh=mesh,
        out_type=jax.ShapeDtypeStruct((m, w), table.dtype),
        scratch_types=[
            pltpu.VMEM((SC_CHUNK,), I32),
            pltpu.VMEM((SC_CHUNK, w), table.dtype),
            pltpu.SemaphoreType.DMA,
        ],
        name="sc_gather_rows",
    )
    def gather(table_hbm, idx_hbm, out_hbm, idx_v, rows_v, sem):
        base = (lax.axis_index("s") * SC_CORES + lax.axis_index("c")) * per_w

        @pl.loop(0, per_w // SC_CHUNK)
        def _(j):
            off = pl.multiple_of(base + j * SC_CHUNK, SC_CHUNK)
            pltpu.sync_copy(idx_hbm.at[pl.ds(off, SC_CHUNK)], idx_v)
            pltpu.async_copy(table_hbm.at[idx_v], rows_v, sem).wait()
            pltpu.sync_copy(rows_v, out_hbm.at[pl.ds(off, SC_CHUNK)])

    return gather(table, idx)


def _scatter_rows(src, pos3, p):
    w = src.shape[1]
    n_chunks, k, ch = pos3.shape
    workers = SC_CORES * SC_SUBCORES
    per_w = n_chunks // workers
    assert ch == SC_CHUNK and n_chunks % workers == 0 and src.shape[0] == n_chunks * ch
    mesh = plsc.VectorSubcoreMesh(core_axis_name="c", subcore_axis_name="s",
                                  num_cores=SC_CORES, num_subcores=SC_SUBCORES)

    @functools.partial(
        pl.kernel, mesh=mesh,
        out_type=jax.ShapeDtypeStruct((p, w), src.dtype),
        scratch_types=[
            pltpu.VMEM((k, ch), I32),
            pltpu.VMEM((ch, w), src.dtype),
            pltpu.SemaphoreType.DMA,
        ],
        name="sc_scatter_rows",
    )
    def scatter(src_hbm, pos_hbm, out_hbm, idx_v, rows_v, sem):
        first = (lax.axis_index("s") * SC_CORES + lax.axis_index("c")) * per_w

        @pl.loop(0, per_w)
        def _(j):
            c = first + j
            pltpu.sync_copy(pos_hbm.at[c], idx_v)
            pltpu.sync_copy(src_hbm.at[pl.ds(pl.multiple_of(c * ch, ch), ch)], rows_v)
            copies = [pltpu.async_copy(rows_v, out_hbm.at[idx_v.at[kk]], sem) for kk in range(k)]
            for cp in copies:
                cp.wait()

    return scatter(src, pos3)


def _experts_kernel(ps_ref, nb_ref, cnt_ref, nt_ref, xs_hbm, wg_ref, wu_ref, wd_ref, y_hbm,
                    xbuf, ybuf, in_sem, out_sem, wg_bf, wu_bf, wd_bf):
    e = pl.program_id(0)
    nb, cnt, n_total = nb_ref[e], cnt_ref[e], nt_ref[0]
    g0 = ps_ref[e] // MOE_ROWS

    def in_copy(g):
        rows = pl.ds(pl.multiple_of(g * MOE_ROWS, MOE_ROWS), MOE_ROWS)
        return pltpu.make_async_copy(xs_hbm.at[rows], xbuf.at[g % MOE_SLOTS], in_sem.at[g % MOE_SLOTS])

    def out_copy(g):
        rows = pl.ds(pl.multiple_of(g * MOE_ROWS, MOE_ROWS), MOE_ROWS)
        return pltpu.make_async_copy(ybuf.at[g % MOE_SLOTS], y_hbm.at[rows], out_sem.at[g % MOE_SLOTS])

    @pl.when(e == 0)
    def _():
        for j in range(MOE_LOOKAHEAD):
            @pl.when(j < n_total)
            def _():
                in_copy(j).start()

    @pl.when(nb > 0)
    def _():
        wg_bf[...] = wg_ref[0, 0].astype(BF16)
        wu_bf[...] = wu_ref[0, 0].astype(BF16)
        wd_bf[...] = wd_ref[0, 0].astype(BF16)

        def body(b, c):
            g = g0 + b
            slot = g % MOE_SLOTS

            @pl.when(g + MOE_LOOKAHEAD < n_total)
            def _():
                in_copy(g + MOE_LOOKAHEAD).start()

            in_copy(g).wait()

            @pl.when(g >= MOE_SLOTS)
            def _():
                out_copy(g - MOE_SLOTS).wait()

            row = lax.broadcasted_iota(I32, (MOE_ROWS, 1), 0) + b * MOE_ROWS
            x_lo, x_hi = (v.astype(BF16) for v in _unpack_bf16_pairs(jnp.where(row < cnt, xbuf[slot], 0)))
            half = x_lo.shape[1]

            def up(w_bf):
                return (jnp.dot(x_lo, w_bf[:half, :], preferred_element_type=F32)
                        + jnp.dot(x_hi, w_bf[half:, :], preferred_element_type=F32))

            hb = (_silu(up(wg_bf)) * up(wu_bf)).astype(BF16)
            ybuf[slot] = _pack_bf16_pairs(jnp.dot(hb, wd_bf[...], preferred_element_type=F32))
            out_copy(g).start()
            return c

        lax.fori_loop(0, nb, body, 0)

    @pl.when(e == pl.num_programs(0) - 1)
    def _():
        for j in range(MOE_SLOTS):
            @pl.when(n_total - 1 - j >= 0)
            def _():
                out_copy(n_total - 1 - j).wait()


def _routed_experts(xs, wg, wu, wd, layer, pstarts, nblk, counts, n_total):
    p, dp = xs.shape
    _, e, d, f = wg.shape
    wspec = lambda r, c: pl.BlockSpec((1, 1, r, c), lambda i, ps, nb, cnt, nt: (layer, i, 0, 0))
    grid_spec = pltpu.PrefetchScalarGridSpec(
        num_scalar_prefetch=4,
        grid=(e,),
        in_specs=[pl.BlockSpec(memory_space=pl.ANY), wspec(d, f), wspec(d, f), wspec(f, d)],
        out_specs=pl.BlockSpec(memory_space=pl.ANY),
        scratch_shapes=[
            pltpu.VMEM((MOE_SLOTS, MOE_ROWS, dp), I32), pltpu.VMEM((MOE_SLOTS, MOE_ROWS, dp), I32),
            pltpu.SemaphoreType.DMA((MOE_SLOTS,)), pltpu.SemaphoreType.DMA((MOE_SLOTS,)),
            pltpu.VMEM((d, f), BF16), pltpu.VMEM((d, f), BF16), pltpu.VMEM((f, d), BF16),
        ],
    )
    return pl.pallas_call(
        _experts_kernel,
        out_shape=jax.ShapeDtypeStruct((p, dp), I32),
        grid_spec=grid_spec,
        compiler_params=_cparams("arbitrary"),
        name="routed_experts",
    )(pstarts, nblk, counts, n_total, xs, wg, wu, wd)


def _combine_kernel(yg_ref, w_ref, hf_ref, sg_ref, su_ref, sd_ref, x_ref, gf_ref, lg_ref, lb_ref, o_ref, *, alpha):
    w = w_ref[...]
    r_lo, r_hi = _unpack_bf16_pairs(yg_ref[0])
    r_lo, r_hi = w[:, 0:1] * r_lo, w[:, 0:1] * r_hi
    for k in range(1, TOP_K):
        y_lo, y_hi = _unpack_bf16_pairs(yg_ref[k])
        r_lo, r_hi = r_lo + w[:, k:k + 1] * y_lo, r_hi + w[:, k:k + 1] * y_hi
    routed = jnp.concatenate([r_lo, r_hi], axis=1)
    hf = jnp.concatenate(_unpack_bf16_pairs(hf_ref[...]), axis=1).astype(BF16)
    g = jnp.dot(hf, sg_ref[...], preferred_element_type=F32)
    u = jnp.dot(hf, su_ref[...], preferred_element_type=F32)
    shared = jnp.dot((_silu(g) * u).astype(BF16), sd_ref[...], preferred_element_type=F32)
    o_ref[...] = _layer_norm(alpha * x_ref[...] + gf_ref[0] * (routed + shared), lg_ref[...], lb_ref[...])


def _combine(yg, w_tk, hf, sg_bf, su_bf, sd_bf, x1, gf, ln_g, ln_b, alpha, tm, per_b):
    t, d = x1.shape
    f = sg_bf.shape[1]
    row = pl.BlockSpec((1, d), lambda i: (0, 0))
    tile = pl.BlockSpec((tm, d), lambda i: (i, 0))
    return pl.pallas_call(
        functools.partial(_combine_kernel, alpha=alpha),
        out_shape=jax.ShapeDtypeStruct((t, d), F32),
        grid=(t // tm,),
        in_specs=[
            pl.BlockSpec((TOP_K, tm, d // 2), lambda i: (0, i, 0)),
            pl.BlockSpec((tm, TOP_K), lambda i: (i, 0)),
            pl.BlockSpec((tm, d // 2), lambda i: (i, 0)),
            pl.BlockSpec((d, f), lambda i: (0, 0)),
            pl.BlockSpec((d, f), lambda i: (0, 0)),
            pl.BlockSpec((f, d), lambda i: (0, 0)),
            tile,
            pl.BlockSpec((1, 1, d), lambda i: (i // per_b, 0, 0)),
            row, row,
        ],
        out_specs=tile,
        compiler_params=_cparams("parallel"),
        name="moe_combine",
    )(yg, w_tk, hf, sg_bf, su_bf, sd_bf, x1, gf, ln_g.reshape(1, d), ln_b.reshape(1, d))


def _moe_layer(x1, hf, scf, shf, gf, router_w, router_bias, wg, wu, wd, layer, sg, su, sd, ln_g, ln_b, alpha, tm):
    b, n, d = x1.shape
    t = b * n
    e = router_w.shape[1]
    idx_t, w_t, rank_t, cnt = _route(x1, scf, shf, router_w.T, router_bias, tm)

    counts = cnt[:, 0]
    padded = (counts + MOE_ROWS - 1) // MOE_ROWS * MOE_ROWS
    pends = jnp.cumsum(padded)
    pstarts = pends - padded
    sel = idx_t[:, :, None] == jnp.arange(e, dtype=I32)
    pos_t = jnp.sum(jnp.where(sel, pstarts, 0), axis=-1) + rank_t
    p = t * TOP_K + e * MOE_ROWS
    pos3 = pos_t.reshape(TOP_K, t // SC_CHUNK, SC_CHUNK).transpose(1, 0, 2)

    hf2 = hf.reshape(t, d // 2)
    xs = _scatter_rows(hf2, pos3, p)
    yb = _routed_experts(xs, wg, wu, wd, layer, pstarts.astype(I32), (padded // MOE_ROWS).astype(I32), counts,
                         (pends[-1:] // MOE_ROWS).astype(I32))
    yg = _gather_rows(yb, pos_t.reshape(-1)).reshape(TOP_K, t, d // 2)
    out = _combine(yg, w_t.T, hf2, sg.astype(BF16), su.astype(BF16), sd.astype(BF16), x1.reshape(t, d), gf,
                   ln_g, ln_b, alpha, tm, n // tm)
    return out.reshape(b, n, d)


def _rope_tables(n):
    rows = n // GRID_W
    row_pos = jnp.repeat(jnp.arange(rows, dtype=F32), GRID_W)
    col_pos = jnp.tile(jnp.arange(GRID_W, dtype=F32), rows)
    half = LANES // 4
    lane = jnp.arange(LANES)
    in_blk = lane % half
    freq = ROPE_THETA ** (-(2.0 * (in_blk % (half // 2)).astype(F32)) / half)
    use_col = (lane // half) % 2 == 1
    pos = jnp.where(use_col[None, :], col_pos[:, None], row_pos[:, None])
    ang = pos * freq[None, :]
    lo = (in_blk < half // 2)[None, :]
    sin = jnp.sin(ang)
    return jnp.cos(ang), jnp.where(lo, -sin, 0.0), jnp.where(lo, 0.0, sin)


def kernel(x, c, ctx, c_ctx, w_mod, b_mod, ln_g, ln_b, attn_w_in, attn_w_out, attn_lambda, attn_subln_g,
           sgu_w_in, sgu_b_in, sgu_norm_g, sgu_norm_b, sgu_w_s, sgu_b_s, sgu_w_out,
           router_w, router_bias, exp_w_gate, exp_w_up, exp_w_down, sh_w_gate, sh_w_up, sh_w_down):
    b, n, d = x.shape
    depth = w_mod.shape[0]
    assert b <= 7 and d == ATT_HEADS * LANES and n % GRID_W == 0
    alpha = (2 * depth) ** 0.25
    head_dim = d // ATT_HEADS // 2
    tm = 512 if n % 512 == 0 else 256

    cs = jnp.zeros((8, d), F32).at[:b].set(c).at[b].set(c_ctx)
    mods = _modulation(cs, w_mod, b_mod)

    def mod_vec(i, j):
        return mods[i, :, j * d:(j + 1) * d].reshape(8, 1, d)

    for i in range(depth):
        sh_m, sc_m, g_m, sh_f, sc_f, g_f = (mod_vec(i, j) for j in range(6))
        if i % N_MIXERS == 0:
            a = i // N_MIXERS
            lam_init = 0.8 - 0.6 * math.exp(-0.3 * i)
            w_in_bf = attn_w_in[a].astype(BF16)
            cos, slo, shi = _rope_tables(n)
            q_scale = head_dim ** -0.5 * math.log2(math.e)
            q, k, v = _qkv_proj(x, sc_m, sh_m, w_in_bf, cos, slo, shi, q_scale, tm)
            kc, vc = _kv_ctx_proj(ctx, sc_m[b:b + 1], sh_m[b:b + 1], w_in_bf[:, d:])
            tk = min(512, n // 2)
            pre = _diff_attention(q, k, v, kc, vc, attn_lambda[a], attn_subln_g[a], lam_init, min(1024, n), tk)
            w_out_bf = attn_w_out[a].astype(BF16)
        else:
            s = i // N_MIXERS
            pre = _sgu_mixer(x, sc_m, sh_m, sgu_w_in[s].astype(BF16), sgu_b_in[s], sgu_norm_g[s], sgu_norm_b[s],
                             sgu_w_s[s].astype(BF16), sgu_b_s[s].T, 256)
            w_out_bf = sgu_w_out[s].astype(BF16)
        x1, hf = _post_mixer(pre, w_out_bf, x, g_m, ln_g[i, 0], ln_b[i, 0], sc_f, sh_f, alpha, tm)
        x = _moe_layer(x1, hf, sc_f, sh_f, g_f, router_w[i], router_bias[i], exp_w_gate, exp_w_up, exp_w_down, i,
                       sh_w_gate[i], sh_w_up[i], sh_w_down[i], ln_g[i, 1], ln_b[i, 1], alpha, tm)
    return x
```

```python
import functools
import math

import jax
import jax.numpy as jnp
from jax import lax
from jax.experimental import pallas as pl
from jax.experimental.pallas import tpu as pltpu
from jax.experimental.pallas import tpu_sc as plsc

F32 = jnp.float32
BF16 = jnp.bfloat16
I32 = jnp.int32

GRID_W = 64
ATT_HEADS = 8
ROPE_THETA = 10000.0
SGU_CHUNK = 128
SGU_GROUPS = 8
TOP_K = 8
N_GROUPS = 8
TOPK_GROUPS = 4
ROUTED_SCALE = 2.5
LN_EPS = 1e-5
N_MIXERS = 2

LANES = 128
MOE_ROWS = 256
MOE_LOOKAHEAD = 3
MOE_SLOTS = MOE_LOOKAHEAD + 1
ATT_STRIP = 32
SC_CORES = 2
SC_SUBCORES = 16
SC_CHUNK = 64
VMEM_LIMIT = 56 * 1024 * 1024
NEG_INF = float("-inf")


def _cparams(*sem):
    return pltpu.CompilerParams(dimension_semantics=sem, vmem_limit_bytes=VMEM_LIMIT)


def _layer_norm(z, g, b):
    mu = jnp.mean(z, axis=-1, keepdims=True)
    zc = z - mu
    var = jnp.mean(zc * zc, axis=-1, keepdims=True)
    return zc * lax.rsqrt(var + LN_EPS) * g + b


def _silu(x):
    return x * (1.0 / (1.0 + jnp.exp(-x)))


_HIGH_HALF = -65536


def _pack_bf16_pairs(y):
    w = y.shape[1] // 2
    bits = lax.bitcast_convert_type(y.astype(BF16).astype(F32), I32)
    return lax.shift_right_logical(bits[:, :w], 16) | (bits[:, w:] & _HIGH_HALF)


def _unpack_bf16_pairs(p):
    return (lax.bitcast_convert_type(lax.shift_left(p, 16), F32),
            lax.bitcast_convert_type(p & _HIGH_HALF, F32))


def _mod_kernel(cs_ref, w_ref, b_ref, o_ref):
    s = _silu(cs_ref[...])
    o_ref[0] = jnp.dot(s, w_ref[0], precision=lax.Precision.HIGHEST,
                       preferred_element_type=F32) + b_ref[0]


def _modulation(cs, w_mod, b_mod):
    depth, d, n6 = w_mod.shape
    tn = n6 // 4
    return pl.pallas_call(
        _mod_kernel,
        out_shape=jax.ShapeDtypeStruct((depth, 8, n6), F32),
        grid=(depth, n6 // tn),
        in_specs=[
            pl.BlockSpec((8, d), lambda l, j: (0, 0)),
            pl.BlockSpec((1, d, tn), lambda l, j: (l, 0, j)),
            pl.BlockSpec((1, 1, tn), lambda l, j: (l, 0, j)),
        ],
        out_specs=pl.BlockSpec((1, 8, tn), lambda l, j: (l, 0, j)),
        compiler_params=_cparams("parallel", "parallel"),
        name="modulation",
    )(cs, w_mod, b_mod.reshape(depth, 1, n6))


def _rope(xh, cos, sin_lo, sin_hi):
    return xh * cos + pltpu.roll(xh, LANES - 16, 1) * sin_lo + pltpu.roll(xh, 16, 1) * sin_hi


def _qkv_kernel(x_ref, sc_ref, sh_ref, w_ref, cos_ref, slo_ref, shi_ref, q_ref, k_ref, v_ref, *, q_scale):
    d = x_ref.shape[-1]
    h = (x_ref[0] * (1.0 + sc_ref[0]) + sh_ref[0]).astype(BF16)
    cos, slo, shi = cos_ref[...], slo_ref[...], shi_ref[...]
    q = jnp.dot(h, w_ref[:, 0:d], preferred_element_type=F32)
    for hd in range(ATT_HEADS):
        q_ref[0, hd] = (_rope(q[:, hd * LANES:(hd + 1) * LANES], cos, slo, shi) * q_scale).astype(BF16)
    k = jnp.dot(h, w_ref[:, d:2 * d], preferred_element_type=F32)
    for hd in range(ATT_HEADS):
        k_ref[0, hd] = _rope(k[:, hd * LANES:(hd + 1) * LANES], cos, slo, shi).astype(BF16)
    v = jnp.dot(h, w_ref[:, 2 * d:3 * d], preferred_element_type=F32)
    for hd in range(ATT_HEADS):
        v_ref[0, hd] = v[:, hd * LANES:(hd + 1) * LANES].astype(BF16)


def _qkv_proj(x, sc, sh, w_bf, cos, slo, shi, q_scale, tn):
    b, n, d = x.shape
    hd_shape = jax.ShapeDtypeStruct((b, ATT_HEADS, n, LANES), BF16)
    vec = pl.BlockSpec((1, 1, d), lambda bi, i: (bi, 0, 0))
    tab = pl.BlockSpec((tn, LANES), lambda bi, i: (i, 0))
    out = pl.BlockSpec((1, ATT_HEADS, tn, LANES), lambda bi, i: (bi, 0, i, 0))
    return pl.pallas_call(
        functools.partial(_qkv_kernel, q_scale=q_scale),
        out_shape=(hd_shape, hd_shape, hd_shape),
        grid=(b, n // tn),
        in_specs=[
            pl.BlockSpec((1, tn, d), lambda bi, i: (bi, i, 0)),
            vec, vec,
            pl.BlockSpec((d, 3 * d), lambda bi, i: (0, 0)),
            tab, tab, tab,
        ],
        out_specs=(out, out, out),
        compiler_params=_cparams("parallel", "parallel"),
        name="qkv_proj",
    )(x, sc, sh, w_bf, cos, slo, shi)


def _kv_ctx_kernel(x_ref, sc_ref, sh_ref, w_ref, k_ref, v_ref):
    d = x_ref.shape[-1]
    h = (x_ref[0] * (1.0 + sc_ref[0]) + sh_ref[0]).astype(BF16)
    k = jnp.dot(h, w_ref[:, 0:d], preferred_element_type=F32)
    v = jnp.dot(h, w_ref[:, d:2 * d], preferred_element_type=F32)
    for hd in range(ATT_HEADS):
        k_ref[0, hd] = k[:, hd * LANES:(hd + 1) * LANES].astype(BF16)
        v_ref[0, hd] = v[:, hd * LANES:(hd + 1) * LANES].astype(BF16)


def _kv_ctx_proj(ctx, sc, sh, w_kv_bf):
    b, c, d = ctx.shape
    hd_shape = jax.ShapeDtypeStruct((b, ATT_HEADS, c, LANES), BF16)
    vec = pl.BlockSpec((1, 1, d), lambda bi: (0, 0, 0))
    out = pl.BlockSpec((1, ATT_HEADS, c, LANES), lambda bi: (bi, 0, 0, 0))
    return pl.pallas_call(
        _kv_ctx_kernel,
        out_shape=(hd_shape, hd_shape),
        grid=(b,),
        in_specs=[
            pl.BlockSpec((1, c, d), lambda bi: (bi, 0, 0)),
            vec, vec,
            pl.BlockSpec((d, 2 * d), lambda bi: (0, 0)),
        ],
        out_specs=(out, out),
        compiler_params=_cparams("parallel"),
        name="kv_ctx_proj",
    )(ctx, sc, sh, w_kv_bf)


def _attn_finish(acc_a, l_a, acc_b, l_b, lamp_ref, g_ref, lam_init):
    lp = lamp_ref[...]
    lam = (jnp.exp(jnp.sum(lp[0:1] * lp[1:2], axis=-1, keepdims=True))
           - jnp.exp(jnp.sum(lp[2:3] * lp[3:4], axis=-1, keepdims=True)) + lam_init)
    o = acc_a / l_a - lam * (acc_b / l_b)
    o = o * lax.rsqrt(jnp.mean(o * o, axis=-1, keepdims=True) + LN_EPS) * g_ref[...] * (1.0 - lam_init)
    return o.astype(BF16)


def _attn_kernel(q_ref, k_ref, v_ref, kc_ref, vc_ref, lamp_ref, g_ref, o_ref,
                 s00, s01, s10, s11, p0, p1, mb0, mb1, lp0, lp1, acc0, acc1, *, tk, lam_init):
    q = q_ref[0, 0]
    tq = q.shape[0]
    half = LANES // 2
    lane = lax.broadcasted_iota(jnp.int32, q.shape, 1)
    zero = jnp.zeros_like(q)
    qs = (jnp.where(lane < half, q, zero), jnp.where(lane >= half, q, zero))
    n_chunks = k_ref.shape[2] // tk
    s_scr = ((s00, s01), (s10, s11))
    p_scr, mb_scr, lp_scr, acc_scr = (p0, p1), (mb0, mb1), (lp0, lp1), (acc0, acc1)
    nt = (((1,), (1,)), ((), ()))

    def scores(slot, kc):
        width = kc.shape[0]
        for mp in range(2):
            s_scr[slot][mp][:, 0:width] = lax.dot_general(qs[mp], kc, nt, preferred_element_type=F32)

    def absorb(slot, vc):
        width = vc.shape[0]
        for mp in range(2):
            for r in range(tq // ATT_STRIP):
                rows = slice(r * ATT_STRIP, (r + 1) * ATT_STRIP)
                shift = mb_scr[mp][rows, :]
                tiles = [jnp.exp2(s_scr[slot][mp][rows, t * LANES:(t + 1) * LANES] - shift)
                         for t in range(width // LANES)]
                p_scr[mp][rows, 0:width] = jnp.concatenate(tiles, axis=1).astype(BF16)
                lp_scr[mp][rows, :] += functools.reduce(lambda x, y: x + y, tiles)
            acc_scr[mp][...] += jnp.dot(p_scr[mp][:, 0:width], vc, preferred_element_type=F32)

    def k_chunk(j):
        return k_ref[0, 0, pl.ds(pl.multiple_of(j * tk, tk), tk), :]

    def v_chunk(j):
        return v_ref[0, 0, pl.ds(pl.multiple_of(j * tk, tk), tk), :]

    scores(0, k_chunk(0))
    for mp in range(2):
        mb_scr[mp][...] = jnp.broadcast_to(jnp.max(s_scr[0][mp][...], axis=-1, keepdims=True), (tq, LANES))
        lp_scr[mp][...] = jnp.zeros((tq, LANES), F32)
        acc_scr[mp][...] = jnp.zeros((tq, LANES), F32)

    def pair(jj, c):
        j0 = 2 * jj
        scores(1, k_chunk(j0 + 1))
        absorb(0, v_chunk(j0))
        scores(0, k_chunk(j0 + 2))
        absorb(1, v_chunk(j0 + 1))
        return c

    lax.fori_loop(0, n_chunks // 2 - 1, pair, 0)
    scores(1, k_chunk(n_chunks - 1))
    absorb(0, v_chunk(n_chunks - 2))
    scores(0, kc_ref[0, 0])
    absorb(1, v_chunk(n_chunks - 1))
    absorb(0, vc_ref[0, 0])

    sums = [jnp.sum(lp_scr[mp][...], axis=-1, keepdims=True) for mp in range(2)]
    bad = sum(jnp.sum(jnp.where(jnp.isfinite(x), 0.0, 1.0)) for x in (sums[0], sums[1], acc0[...], acc1[...]))

    @pl.when(bad == 0.0)
    def _():
        o_ref[0] = _attn_finish(acc0[...], sums[0], acc1[...], sums[1], lamp_ref, g_ref, lam_init)

    @pl.when(bad != 0.0)
    def _():
        def update(carry, kc, vc):
            new = []
            for mp in range(2):
                m, l, acc = carry[mp]
                s = lax.dot_general(qs[mp], kc, nt, preferred_element_type=F32)
                mn = jnp.maximum(m, jnp.max(s, axis=-1, keepdims=True))
                a = jnp.exp2(m - mn)
                p = jnp.exp2(s - mn)
                new.append((mn, a * l + jnp.sum(p, axis=-1, keepdims=True),
                            a * acc + jnp.dot(p.astype(BF16), vc, preferred_element_type=F32)))
            return tuple(new)

        init = tuple((jnp.full((tq, 1), NEG_INF, F32), jnp.zeros((tq, 1), F32), jnp.zeros((tq, LANES), F32))
                     for _ in range(2))
        carry = lax.fori_loop(0, n_chunks, lambda j, c: update(c, k_chunk(j), v_chunk(j)), init)
        (_, l_a, acc_a), (_, l_b, acc_b) = update(carry, kc_ref[0, 0], vc_ref[0, 0])
        o_ref[0] = _attn_finish(acc_a, l_a, acc_b, l_b, lamp_ref, g_ref, lam_init)


def _diff_attention(q, k, v, kc, vc, lam_p, subln_g, lam_init, tq, tk):
    b, h, n, _ = q.shape
    c = kc.shape[2]
    assert n % tk == 0 and (n // tk) % 2 == 0 and c <= tk and tq % ATT_STRIP == 0
    kv = pl.BlockSpec((1, 1, n, LANES), lambda bi, hi, i: (bi, hi, 0, 0))
    kvc = pl.BlockSpec((1, 1, c, LANES), lambda bi, hi, i: (bi, hi, 0, 0))
    return pl.pallas_call(
        functools.partial(_attn_kernel, tk=tk, lam_init=lam_init),
        out_shape=jax.ShapeDtypeStruct((b, n, h * LANES), BF16),
        grid=(b, h, n // tq),
        in_specs=[
            pl.BlockSpec((1, 1, tq, LANES), lambda bi, hi, i: (bi, hi, i, 0)),
            kv, kv, kvc, kvc,
            pl.BlockSpec(lam_p.shape, lambda bi, hi, i: (0, 0)),
            pl.BlockSpec((1, LANES), lambda bi, hi, i: (0, 0)),
        ],
        out_specs=pl.BlockSpec((1, tq, LANES), lambda bi, hi, i: (bi, i, hi)),
        scratch_shapes=[
            *[pltpu.VMEM((tq, tk), F32)] * 4,
            *[pltpu.VMEM((tq, tk), BF16)] * 2,
            *[pltpu.VMEM((tq, LANES), F32)] * 2,
            *[pltpu.VMEM((tq, LANES), F32)] * 2,
            *[pltpu.VMEM((tq, LANES), F32)] * 2,
        ],
        compiler_params=_cparams("parallel", "parallel", "parallel"),
        name="diff_attention",
    )(q, k, v, kc, vc, lam_p, subln_g.reshape(1, LANES))


def _sgu_kernel(x_ref, sc_ref, sh_ref, w_ref, b_ref, ng_ref, nb_ref, ws_ref, bs_ref, t_ref):
    f = t_ref.shape[-1]
    cg = f // SGU_GROUPS
    tm = x_ref.shape[1]
    h = (x_ref[0] * (1.0 + sc_ref[0]) + sh_ref[0]).astype(BF16)
    z = jnp.dot(h, w_ref[...], preferred_element_type=F32) + b_ref[...]
    z = 0.5 * z * (1.0 + lax.erf(z * (2.0 ** -0.5)))
    u = z[:, :f]
    v = _layer_norm(z[:, f:], ng_ref[...], nb_ref[...]).astype(BF16)
    for c in range(tm // SGU_CHUNK):
        rows = slice(c * SGU_CHUNK, (c + 1) * SGU_CHUNK)
        for g in range(SGU_GROUPS):
            cols = slice(g * cg, (g + 1) * cg)
            vm = jnp.dot(ws_ref[g], v[rows, cols], preferred_element_type=F32) + bs_ref[:, g:g + 1]
            t_ref[0, rows, cols] = (u[rows, cols] * vm).astype(BF16)


def _sgu_mixer(x, sc, sh, w_in_bf, b_in, norm_g, norm_b, w_s_bf, b_s_t, tm):
    b, n, d = x.shape
    f2 = w_in_bf.shape[1]
    f = f2 // 2
    vec = pl.BlockSpec((1, 1, d), lambda bi, i: (bi, 0, 0))
    full2 = lambda a: pl.BlockSpec(a.shape, lambda bi, i: (0,) * a.ndim)
    b_in2, ng2, nb2 = b_in.reshape(1, f2), norm_g.reshape(1, f), norm_b.reshape(1, f)
    return pl.pallas_call(
        _sgu_kernel,
        out_shape=jax.ShapeDtypeStruct((b, n, f), BF16),
        grid=(b, n // tm),
        in_specs=[
            pl.BlockSpec((1, tm, d), lambda bi, i: (bi, i, 0)),
            vec, vec,
            full2(w_in_bf), full2(b_in2), full2(ng2), full2(nb2), full2(w_s_bf), full2(b_s_t),
        ],
        out_specs=pl.BlockSpec((1, tm, f), lambda bi, i: (bi, i, 0)),
        compiler_params=_cparams("parallel", "parallel"),
        name="sgu_mixer",
    )(x, sc, sh, w_in_bf, b_in2, ng2, nb2, w_s_bf, b_s_t)


def _post_kernel(pre_ref, w_ref, x_ref, gm_ref, lg_ref, lb_ref, scf_ref, shf_ref, x1_ref, hf_ref, *, alpha):
    y = jnp.dot(pre_ref[0], w_ref[...], preferred_element_type=F32)
    x1 = _layer_norm(alpha * x_ref[0] + gm_ref[0] * y, lg_ref[...], lb_ref[...])
    x1_ref[0] = x1
    hf_ref[0] = _pack_bf16_pairs(x1 * (1.0 + scf_ref[0]) + shf_ref[0])


def _post_mixer(pre, w_bf, x, gm, ln_g, ln_b, scf, shf, alpha, tm):
    b, n, d = x.shape
    kd = pre.shape[-1]
    vec = pl.BlockSpec((1, 1, d), lambda bi, i: (bi, 0, 0))
    row = pl.BlockSpec((1, d), lambda bi, i: (0, 0))
    tile = pl.BlockSpec((1, tm, d), lambda bi, i: (bi, i, 0))
    return pl.pallas_call(
        functools.partial(_post_kernel, alpha=alpha),
        out_shape=(jax.ShapeDtypeStruct((b, n, d), F32), jax.ShapeDtypeStruct((b, n, d // 2), I32)),
        grid=(b, n // tm),
        in_specs=[
            pl.BlockSpec((1, tm, kd), lambda bi, i: (bi, i, 0)),
            pl.BlockSpec((kd, d), lambda bi, i: (0, 0)),
            tile, vec, row, row, vec, vec,
        ],
        out_specs=(tile, pl.BlockSpec((1, tm, d // 2), lambda bi, i: (bi, i, 0))),
        compiler_params=_cparams("parallel", "parallel"),
        name="post_mixer",
    )(pre, w_bf, x, gm, ln_g.reshape(1, d), ln_b.reshape(1, d), scf, shf)


def _route_kernel(x_ref, sc_ref, sh_ref, wr_ref, rb_ref, idx_ref, w_ref, rank_ref, cnt_ref, carry_ref):
    i = pl.program_id(0)
    e = wr_ref.shape[0]
    tm = x_ref.shape[0]
    ge = e // N_GROUPS

    @pl.when(i == 0)
    def _():
        carry_ref[...] = jnp.zeros_like(carry_ref)

    h = x_ref[...] * (1.0 + sc_ref[0]) + sh_ref[0]
    logits = lax.dot_general(wr_ref[...], h, (((1,), (1,)), ((), ())),
                             precision=lax.Precision.HIGHEST, preferred_element_type=F32)
    scores = 1.0 / (1.0 + jnp.exp(-logits))
    choice = scores + rb_ref[...]

    g3 = choice.reshape(N_GROUPS, ge, tm)
    ri = lax.broadcasted_iota(jnp.int32, g3.shape, 1).astype(F32)
    m1 = jnp.max(g3, axis=1, keepdims=True)
    first = jnp.min(jnp.where(g3 == m1, ri, float(ge)), axis=1, keepdims=True)
    m2 = jnp.max(jnp.where(ri == first, NEG_INF, g3), axis=1, keepdims=True)
    gs = m1 + m2

    gi = lax.broadcasted_iota(jnp.int32, gs.shape, 0).astype(F32)
    gsel = jnp.zeros(gs.shape, F32)
    cur = gs
    for _ in range(TOPK_GROUPS):
        m = jnp.max(cur, axis=0, keepdims=True)
        f = jnp.min(jnp.where(cur == m, gi, float(N_GROUPS)), axis=0, keepdims=True)
        hit = gi == f
        gsel = jnp.where(hit, 1.0, gsel)
        cur = jnp.where(hit, NEG_INF, cur)
    emask = jnp.broadcast_to(gsel, g3.shape).reshape(e, tm)
    masked = jnp.where(emask > 0.5, choice, NEG_INF)

    ei = lax.broadcasted_iota(jnp.int32, (e, tm), 0).astype(F32)
    onehot = jnp.zeros((e, tm), F32)
    idxs, ws = [], []
    for _ in range(TOP_K):
        m = jnp.max(masked, axis=0, keepdims=True)
        f = jnp.min(jnp.where(masked == m, ei, float(e)), axis=0, keepdims=True)
        hit = ei == f
        idxs.append(f)
        ws.append(jnp.sum(jnp.where(hit, scores, 0.0), axis=0, keepdims=True))
        masked = jnp.where(hit, NEG_INF, masked)
        onehot = jnp.where(hit, 1.0, onehot)
    wsum = ws[0]
    for k in range(1, TOP_K):
        wsum = wsum + ws[k]

    r_i = lax.broadcasted_iota(jnp.int32, (tm, tm), 0)
    c_i = lax.broadcasted_iota(jnp.int32, (tm, tm), 1)
    upper = jnp.where(r_i < c_i, 1.0, 0.0).astype(BF16)
    rk = jnp.dot(onehot.astype(BF16), upper, preferred_element_type=F32) + carry_ref[...]
    carry_ref[...] += jnp.sum(onehot, axis=1, keepdims=True)

    for k in range(TOP_K):
        idx_ref[k:k + 1, :] = idxs[k].astype(jnp.int32)
        w_ref[k:k + 1, :] = ws[k] / wsum * ROUTED_SCALE
        rank_ref[k:k + 1, :] = jnp.sum(jnp.where(ei == idxs[k], rk, 0.0), axis=0, keepdims=True).astype(jnp.int32)
    cnt_ref[...] = jnp.broadcast_to(carry_ref[...], cnt_ref.shape).astype(jnp.int32)


def _route(x1, scf, shf, wr_t, rbias, tm):
    b, n, d = x1.shape
    t = b * n
    e = wr_t.shape[0]
    per_b = n // tm
    vec = pl.BlockSpec((1, 1, d), lambda i: (i // per_b, 0, 0))
    out_t = pl.BlockSpec((TOP_K, tm), lambda i: (0, i))
    return pl.pallas_call(
        _route_kernel,
        out_shape=(jax.ShapeDtypeStruct((TOP_K, t), jnp.int32), jax.ShapeDtypeStruct((TOP_K, t), F32),
                   jax.ShapeDtypeStruct((TOP_K, t), jnp.int32), jax.ShapeDtypeStruct((e, LANES), jnp.int32)),
        grid=(t // tm,),
        in_specs=[
            pl.BlockSpec((tm, d), lambda i: (i, 0)),
            vec, vec,
            pl.BlockSpec((e, d), lambda i: (0, 0)),
            pl.BlockSpec((e, 1), lambda i: (0, 0)),
        ],
        out_specs=(out_t, out_t, out_t, pl.BlockSpec((e, LANES), lambda i: (0, 0))),
        scratch_shapes=[pltpu.VMEM((e, 1), F32)],
        compiler_params=_cparams("arbitrary"),
        name="route",
    )(x1.reshape(t, d), scf, shf, wr_t, rbias.reshape(e, 1))


def _gather_rows(table, idx):
    m = idx.shape[0]
    w = table.shape[1]
    workers = SC_CORES * SC_SUBCORES
    n_ch = m // (workers * SC_CHUNK)
    assert m % (workers * SC_CHUNK) == 0 and n_ch % 2 == 0
    mesh = plsc.VectorSubcoreMesh(core_axis_name="c", subcore_axis_name="s",
                                  num_cores=SC_CORES, num_subcores=SC_SUBCORES)

    @functools.partial(
        pl.kernel, mesh=mesh,
        out_type=jax.ShapeDtypeStruct((m, w), table.dtype),
        scratch_types=[
            pltpu.VMEM((n_ch, SC_CHUNK), I32),
            pltpu.VMEM((SC_CHUNK, w), table.dtype), pltpu.VMEM((SC_CHUNK, w), table.dtype),
            pltpu.SemaphoreType.DMA, pltpu.SemaphoreType.DMA, pltpu.SemaphoreType.DMA, pltpu.SemaphoreType.DMA,
        ],
        name="sc_gather_rows",
    )
    def gather(table_hbm, idx_hbm, out_hbm, idx_all, buf0, buf1, gsem0, gsem1, wsem0, wsem1):
        first = (lax.axis_index("s") * SC_CORES + lax.axis_index("c")) * n_ch
        bufs, gsem, wsem = (buf0, buf1), (gsem0, gsem1), (wsem0, wsem1)
        pltpu.sync_copy(idx_hbm.at[pl.ds(first, n_ch)], idx_all)

        def gather_copy(j, s):
            return pltpu.make_async_copy(table_hbm.at[idx_all.at[j]], bufs[s], gsem[s])

        def write_copy(j, s):
            rows = pl.ds(pl.multiple_of((first + j) * SC_CHUNK, SC_CHUNK), SC_CHUNK)
            return pltpu.make_async_copy(bufs[s], out_hbm.at[rows], wsem[s])

        gather_copy(0, 0).start()

        @pl.loop(0, n_ch, step=2)
        def _(jj):
            for s in range(2):
                j = jj + s

                @pl.when(j >= 1)
                def _():
                    write_copy(j - 1, 1 - s).wait()

                @pl.when(j + 1 < n_ch)
                def _():
                    gather_copy(j + 1, 1 - s).start()

                gather_copy(j, s).wait()
                write_copy(j, s).start()

        write_copy(n_ch - 1, 1).wait()

    return gather(table, idx.reshape(m // SC_CHUNK, SC_CHUNK))


def _scatter_rows(src, pos3, p):
    w = src.shape[1]
    n_chunks, k, ch = pos3.shape
    workers = SC_CORES * SC_SUBCORES
    per_w = n_chunks // workers
    assert ch == SC_CHUNK and n_chunks % workers == 0 and src.shape[0] == n_chunks * ch
    mesh = plsc.VectorSubcoreMesh(core_axis_name="c", subcore_axis_name="s",
                                  num_cores=SC_CORES, num_subcores=SC_SUBCORES)

    @functools.partial(
        pl.kernel, mesh=mesh,
        out_type=jax.ShapeDtypeStruct((p, w), src.dtype),
        scratch_types=[
            pltpu.VMEM((k, ch), I32),
            pltpu.VMEM((ch, w), src.dtype),
            pltpu.SemaphoreType.DMA,
        ],
        name="sc_scatter_rows",
    )
    def scatter(src_hbm, pos_hbm, out_hbm, idx_v, rows_v, sem):
        first = (lax.axis_index("s") * SC_CORES + lax.axis_index("c")) * per_w

        @pl.loop(0, per_w)
        def _(j):
            c = first + j
            pltpu.sync_copy(pos_hbm.at[c], idx_v)
            pltpu.sync_copy(src_hbm.at[pl.ds(pl.multiple_of(c * ch, ch), ch)], rows_v)
            copies = [pltpu.async_copy(rows_v, out_hbm.at[idx_v.at[kk]], sem) for kk in range(k)]
            for cp in copies:
                cp.wait()

    return scatter(src, pos3)


def _experts_kernel(ps_ref, nb_ref, cnt_ref, nt_ref, xs_hbm, wg_ref, wu_ref, wd_ref, y_hbm,
                    xbuf, ybuf, in_sem, out_sem, wg_bf, wu_bf, wd_bf):
    e = pl.program_id(0)
    nb, cnt, n_total = nb_ref[e], cnt_ref[e], nt_ref[0]
    g0 = ps_ref[e] // MOE_ROWS

    def in_copy(g):
        rows = pl.ds(pl.multiple_of(g * MOE_ROWS, MOE_ROWS), MOE_ROWS)
        return pltpu.make_async_copy(xs_hbm.at[rows], xbuf.at[g % MOE_SLOTS], in_sem.at[g % MOE_SLOTS])

    def out_copy(g):
        rows = pl.ds(pl.multiple_of(g * MOE_ROWS, MOE_ROWS), MOE_ROWS)
        return pltpu.make_async_copy(ybuf.at[g % MOE_SLOTS], y_hbm.at[rows], out_sem.at[g % MOE_SLOTS])

    @pl.when(e == 0)
    def _():
        for j in range(MOE_LOOKAHEAD):
            @pl.when(j < n_total)
            def _():
                in_copy(j).start()

    @pl.when(nb > 0)
    def _():
        wg_bf[...] = wg_ref[0, 0].astype(BF16)
        wu_bf[...] = wu_ref[0, 0].astype(BF16)
        wd_bf[...] = wd_ref[0, 0].astype(BF16)

        def body(b, c):
            g = g0 + b
            slot = g % MOE_SLOTS

            @pl.when(g + MOE_LOOKAHEAD < n_total)
            def _():
                in_copy(g + MOE_LOOKAHEAD).start()

            in_copy(g).wait()

            @pl.when(g >= MOE_SLOTS)
            def _():
                out_copy(g - MOE_SLOTS).wait()

            row = lax.broadcasted_iota(I32, (MOE_ROWS, 1), 0) + b * MOE_ROWS
            x_lo, x_hi = (v.astype(BF16) for v in _unpack_bf16_pairs(jnp.where(row < cnt, xbuf[slot], 0)))
            half = x_lo.shape[1]

            def up(w_bf):
                return (jnp.dot(x_lo, w_bf[:half, :], preferred_element_type=F32)
                        + jnp.dot(x_hi, w_bf[half:, :], preferred_element_type=F32))

            hb = (_silu(up(wg_bf)) * up(wu_bf)).astype(BF16)
            ybuf[slot] = _pack_bf16_pairs(jnp.dot(hb, wd_bf[...], preferred_element_type=F32))
            out_copy(g).start()
            return c

        lax.fori_loop(0, nb, body, 0)

    @pl.when(e == pl.num_programs(0) - 1)
    def _():
        for j in range(MOE_SLOTS):
            @pl.when(n_total - 1 - j >= 0)
            def _():
                out_copy(n_total - 1 - j).wait()


def _routed_experts(xs, wg, wu, wd, layer, pstarts, nblk, counts, n_total):
    p, dp = xs.shape
    _, e, d, f = wg.shape
    wspec = lambda r, c: pl.BlockSpec((1, 1, r, c), lambda i, ps, nb, cnt, nt: (layer, i, 0, 0))
    grid_spec = pltpu.PrefetchScalarGridSpec(
        num_scalar_prefetch=4,
        grid=(e,),
        in_specs=[pl.BlockSpec(memory_space=pl.ANY), wspec(d, f), wspec(d, f), wspec(f, d)],
        out_specs=pl.BlockSpec(memory_space=pl.ANY),
        scratch_shapes=[
            pltpu.VMEM((MOE_SLOTS, MOE_ROWS, dp), I32), pltpu.VMEM((MOE_SLOTS, MOE_ROWS, dp), I32),
            pltpu.SemaphoreType.DMA((MOE_SLOTS,)), pltpu.SemaphoreType.DMA((MOE_SLOTS,)),
            pltpu.VMEM((d, f), BF16), pltpu.VMEM((d, f), BF16), pltpu.VMEM((f, d), BF16),
        ],
    )
    return pl.pallas_call(
        _experts_kernel,
        out_shape=jax.ShapeDtypeStruct((p, dp), I32),
        grid_spec=grid_spec,
        compiler_params=_cparams("arbitrary"),
        name="routed_experts",
    )(pstarts, nblk, counts, n_total, xs, wg, wu, wd)


def _combine_kernel(yg_ref, w_ref, hf_ref, sg_ref, su_ref, sd_ref, x_ref, gf_ref, lg_ref, lb_ref, o_ref, *, alpha):
    w = w_ref[...]
    r_lo, r_hi = _unpack_bf16_pairs(yg_ref[0])
    r_lo, r_hi = w[:, 0:1] * r_lo, w[:, 0:1] * r_hi
    for k in range(1, TOP_K):
        y_lo, y_hi = _unpack_bf16_pairs(yg_ref[k])
        r_lo, r_hi = r_lo + w[:, k:k + 1] * y_lo, r_hi + w[:, k:k + 1] * y_hi
    routed = jnp.concatenate([r_lo, r_hi], axis=1)
    hf = jnp.concatenate(_unpack_bf16_pairs(hf_ref[...]), axis=1).astype(BF16)
    g = jnp.dot(hf, sg_ref[...], preferred_element_type=F32)
    u = jnp.dot(hf, su_ref[...], preferred_element_type=F32)
    shared = jnp.dot((_silu(g) * u).astype(BF16), sd_ref[...], preferred_element_type=F32)
    o_ref[...] = _layer_norm(alpha * x_ref[...] + gf_ref[0] * (routed + shared), lg_ref[...], lb_ref[...])


def _combine(yg, w_tk, hf, sg_bf, su_bf, sd_bf, x1, gf, ln_g, ln_b, alpha, tm, per_b):
    t, d = x1.shape
    f = sg_bf.shape[1]
    row = pl.BlockSpec((1, d), lambda i: (0, 0))
    tile = pl.BlockSpec((tm, d), lambda i: (i, 0))
    return pl.pallas_call(
        functools.partial(_combine_kernel, alpha=alpha),
        out_shape=jax.ShapeDtypeStruct((t, d), F32),
        grid=(t // tm,),
        in_specs=[
            pl.BlockSpec((TOP_K, tm, d // 2), lambda i: (0, i, 0)),
            pl.BlockSpec((tm, TOP_K), lambda i: (i, 0)),
            pl.BlockSpec((tm, d // 2), lambda i: (i, 0)),
            pl.BlockSpec((d, f), lambda i: (0, 0)),
            pl.BlockSpec((d, f), lambda i: (0, 0)),
            pl.BlockSpec((f, d), lambda i: (0, 0)),
            tile,
            pl.BlockSpec((1, 1, d), lambda i: (i // per_b, 0, 0)),
            row, row,
        ],
        out_specs=tile,
        compiler_params=_cparams("parallel"),
        name="moe_combine",
    )(yg, w_tk, hf, sg_bf, su_bf, sd_bf, x1, gf, ln_g.reshape(1, d), ln_b.reshape(1, d))


def _moe_layer(x1, hf, scf, shf, gf, router_w, router_bias, wg, wu, wd, layer, sg, su, sd, ln_g, ln_b, alpha, tm):
    b, n, d = x1.shape
    t = b * n
    e = router_w.shape[1]
    idx_t, w_t, rank_t, cnt = _route(x1, scf, shf, router_w.T, router_bias, tm)

    counts = cnt[:, 0]
    padded = (counts + MOE_ROWS - 1) // MOE_ROWS * MOE_ROWS
    pends = jnp.cumsum(padded)
    pstarts = pends - padded
    sel = idx_t[:, :, None] == jnp.arange(e, dtype=I32)
    pos_t = jnp.sum(jnp.where(sel, pstarts, 0), axis=-1) + rank_t
    p = t * TOP_K + e * MOE_ROWS
    pos3 = pos_t.reshape(TOP_K, t // SC_CHUNK, SC_CHUNK).transpose(1, 0, 2)

    hf2 = hf.reshape(t, d // 2)
    xs = _scatter_rows(hf2, pos3, p)
    yb = _routed_experts(xs, wg, wu, wd, layer, pstarts.astype(I32), (padded // MOE_ROWS).astype(I32), counts,
                         (pends[-1:] // MOE_ROWS).astype(I32))
    yg = _gather_rows(yb, pos_t.reshape(-1)).reshape(TOP_K, t, d // 2)
    out = _combine(yg, w_t.T, hf2, sg.astype(BF16), su.astype(BF16), sd.astype(BF16), x1.reshape(t, d), gf,
                   ln_g, ln_b, alpha, tm, n // tm)
    return out.reshape(b, n, d)


def _rope_tables(n):
    rows = n // GRID_W
    row_pos = jnp.repeat(jnp.arange(rows, dtype=F32), GRID_W)
    col_pos = jnp.tile(jnp.arange(GRID_W, dtype=F32), rows)
    half = LANES // 4
    lane = jnp.arange(LANES)
    in_blk = lane % half
    freq = ROPE_THETA ** (-(2.0 * (in_blk % (half // 2)).astype(F32)) / half)
    use_col = (lane // half) % 2 == 1
    pos = jnp.where(use_col[None, :], col_pos[:, None], row_pos[:, None])
    ang = pos * freq[None, :]
    lo = (in_blk < half // 2)[None, :]
    sin = jnp.sin(ang)
    return jnp.cos(ang), jnp.where(lo, -sin, 0.0), jnp.where(lo, 0.0, sin)


def kernel(x, c, ctx, c_ctx, w_mod, b_mod, ln_g, ln_b, attn_w_in, attn_w_out, attn_lambda, attn_subln_g,
           sgu_w_in, sgu_b_in, sgu_norm_g, sgu_norm_b, sgu_w_s, sgu_b_s, sgu_w_out,
           router_w, router_bias, exp_w_gate, exp_w_up, exp_w_down, sh_w_gate, sh_w_up, sh_w_down):
    b, n, d = x.shape
    depth = w_mod.shape[0]
    assert b <= 7 and d == ATT_HEADS * LANES and n % GRID_W == 0
    alpha = (2 * depth) ** 0.25
    head_dim = d // ATT_HEADS // 2
    tm = 512 if n % 512 == 0 else 256

    cs = jnp.zeros((8, d), F32).at[:b].set(c).at[b].set(c_ctx)
    mods = _modulation(cs, w_mod, b_mod)

    def mod_vec(i, j):
        return mods[i, :, j * d:(j + 1) * d].reshape(8, 1, d)

    for i in range(depth):
        sh_m, sc_m, g_m, sh_f, sc_f, g_f = (mod_vec(i, j) for j in range(6))
        if i % N_MIXERS == 0:
            a = i // N_MIXERS
            lam_init = 0.8 - 0.6 * math.exp(-0.3 * i)
            w_in_bf = attn_w_in[a].astype(BF16)
            cos, slo, shi = _rope_tables(n)
            q_scale = head_dim ** -0.5 * math.log2(math.e)
            q, k, v = _qkv_proj(x, sc_m, sh_m, w_in_bf, cos, slo, shi, q_scale, tm)
            kc, vc = _kv_ctx_proj(ctx, sc_m[b:b + 1], sh_m[b:b + 1], w_in_bf[:, d:])
            tk = min(512, n // 2)
            pre = _diff_attention(q, k, v, kc, vc, attn_lambda[a], attn_subln_g[a], lam_init, min(1024, n), tk)
            w_out_bf = attn_w_out[a].astype(BF16)
        else:
            s = i // N_MIXERS
            pre = _sgu_mixer(x, sc_m, sh_m, sgu_w_in[s].astype(BF16), sgu_b_in[s], sgu_norm_g[s], sgu_norm_b[s],
                             sgu_w_s[s].astype(BF16), sgu_b_s[s].T, 256)
            w_out_bf = sgu_w_out[s].astype(BF16)
        x1, hf = _post_mixer(pre, w_out_bf, x, g_m, ln_g[i, 0], ln_b[i, 0], sc_f, sh_f, alpha, tm)
        x = _moe_layer(x1, hf, sc_f, sh_f, g_f, router_w[i], router_bias[i], exp_w_gate, exp_w_up, exp_w_down, i,
                       sh_w_gate[i], sh_w_up[i], sh_w_down[i], ln_g[i, 1], ln_b[i, 1], alpha, tm)
    return x
```

```python
import functools
import math

import jax
import jax.numpy as jnp
from jax import lax
from jax.experimental import pallas as pl
from jax.experimental.pallas import tpu as pltpu
from jax.experimental.pallas import tpu_sc as plsc

F32 = jnp.float32
BF16 = jnp.bfloat16
I32 = jnp.int32

GRID_W = 64
ATT_HEADS = 8
ROPE_THETA = 10000.0
SGU_CHUNK = 128
SGU_GROUPS = 8
TOP_K = 8
N_GROUPS = 8
TOPK_GROUPS = 4
ROUTED_SCALE = 2.5
LN_EPS = 1e-5
N_MIXERS = 2

LANES = 128
MOE_ROWS = 256
MOE_GROUP = 2
MOE_LOOKAHEAD = 4
MOE_SLOTS = MOE_LOOKAHEAD + MOE_GROUP
ATT_STRIP = 16
SC_CORES = 2
SC_SUBCORES = 16
SC_CHUNK = 64
VMEM_LIMIT = 56 * 1024 * 1024
NEG_INF = float("-inf")


def _cparams(*sem):
    return pltpu.CompilerParams(dimension_semantics=sem, vmem_limit_bytes=VMEM_LIMIT)


def _layer_norm(z, g, b):
    mu = jnp.mean(z, axis=-1, keepdims=True)
    zc = z - mu
    var = jnp.mean(zc * zc, axis=-1, keepdims=True)
    return zc * lax.rsqrt(var + LN_EPS) * g + b


def _silu(x):
    return x * (1.0 / (1.0 + jnp.exp(-x)))


_HIGH_HALF = -65536


def _pack_bf16_pairs(y):
    w = y.shape[1] // 2
    bits = lax.bitcast_convert_type(y.astype(BF16).astype(F32), I32)
    return lax.shift_right_logical(bits[:, :w], 16) | (bits[:, w:] & _HIGH_HALF)


def _unpack_bf16_pairs(p):
    return (lax.bitcast_convert_type(lax.shift_left(p, 16), F32),
            lax.bitcast_convert_type(p & _HIGH_HALF, F32))


def _mod_kernel(cs_ref, w_ref, b_ref, o_ref):
    s = _silu(cs_ref[...])
    o_ref[0] = jnp.dot(s, w_ref[0], precision=lax.Precision.HIGHEST,
                       preferred_element_type=F32) + b_ref[0]


def _modulation(cs, w_mod, b_mod):
    depth, d, n6 = w_mod.shape
    tn = n6 // 4
    return pl.pallas_call(
        _mod_kernel,
        out_shape=jax.ShapeDtypeStruct((depth, 8, n6), F32),
        grid=(depth, n6 // tn),
        in_specs=[
            pl.BlockSpec((8, d), lambda l, j: (0, 0)),
            pl.BlockSpec((1, d, tn), lambda l, j: (l, 0, j)),
            pl.BlockSpec((1, 1, tn), lambda l, j: (l, 0, j)),
        ],
        out_specs=pl.BlockSpec((1, 8, tn), lambda l, j: (l, 0, j)),
        compiler_params=_cparams("parallel", "parallel"),
        name="modulation",
    )(cs, w_mod, b_mod.reshape(depth, 1, n6))


def _rope(xh, cos, sin_lo, sin_hi):
    return xh * cos + pltpu.roll(xh, LANES - 16, 1) * sin_lo + pltpu.roll(xh, 16, 1) * sin_hi


def _qkv_kernel(x_ref, sc_ref, sh_ref, w_ref, cos_ref, slo_ref, shi_ref, q_ref, k_ref, v_ref, *, q_scale):
    d = x_ref.shape[-1]
    h = (x_ref[0] * (1.0 + sc_ref[0]) + sh_ref[0]).astype(BF16)
    cos, slo, shi = cos_ref[...], slo_ref[...], shi_ref[...]
    q = jnp.dot(h, w_ref[:, 0:d], preferred_element_type=F32)
    for hd in range(ATT_HEADS):
        q_ref[0, hd] = (_rope(q[:, hd * LANES:(hd + 1) * LANES], cos, slo, shi) * q_scale).astype(BF16)
    k = jnp.dot(h, w_ref[:, d:2 * d], preferred_element_type=F32)
    for hd in range(ATT_HEADS):
        k_ref[0, hd] = _rope(k[:, hd * LANES:(hd + 1) * LANES], cos, slo, shi).astype(BF16)
    v = jnp.dot(h, w_ref[:, 2 * d:3 * d], preferred_element_type=F32)
    for hd in range(ATT_HEADS):
        v_ref[0, hd] = v[:, hd * LANES:(hd + 1) * LANES].astype(BF16)


def _qkv_proj(x, sc, sh, w_bf, cos, slo, shi, q_scale, tn):
    b, n, d = x.shape
    hd_shape = jax.ShapeDtypeStruct((b, ATT_HEADS, n, LANES), BF16)
    vec = pl.BlockSpec((1, 1, d), lambda bi, i: (bi, 0, 0))
    tab = pl.BlockSpec((tn, LANES), lambda bi, i: (i, 0))
    out = pl.BlockSpec((1, ATT_HEADS, tn, LANES), lambda bi, i: (bi, 0, i, 0))
    return pl.pallas_call(
        functools.partial(_qkv_kernel, q_scale=q_scale),
        out_shape=(hd_shape, hd_shape, hd_shape),
        grid=(b, n // tn),
        in_specs=[
            pl.BlockSpec((1, tn, d), lambda bi, i: (bi, i, 0)),
            vec, vec,
            pl.BlockSpec((d, 3 * d), lambda bi, i: (0, 0)),
            tab, tab, tab,
        ],
        out_specs=(out, out, out),
        compiler_params=_cparams("parallel", "parallel"),
        name="qkv_proj",
    )(x, sc, sh, w_bf, cos, slo, shi)


def _kv_ctx_kernel(x_ref, sc_ref, sh_ref, w_ref, k_ref, v_ref):
    d = x_ref.shape[-1]
    h = (x_ref[0] * (1.0 + sc_ref[0]) + sh_ref[0]).astype(BF16)
    k = jnp.dot(h, w_ref[:, 0:d], preferred_element_type=F32)
    v = jnp.dot(h, w_ref[:, d:2 * d], preferred_element_type=F32)
    for hd in range(ATT_HEADS):
        k_ref[0, hd] = k[:, hd * LANES:(hd + 1) * LANES].astype(BF16)
        v_ref[0, hd] = v[:, hd * LANES:(hd + 1) * LANES].astype(BF16)


def _kv_ctx_proj(ctx, sc, sh, w_kv_bf):
    b, c, d = ctx.shape
    hd_shape = jax.ShapeDtypeStruct((b, ATT_HEADS, c, LANES), BF16)
    vec = pl.BlockSpec((1, 1, d), lambda bi: (0, 0, 0))
    out = pl.BlockSpec((1, ATT_HEADS, c, LANES), lambda bi: (bi, 0, 0, 0))
    return pl.pallas_call(
        _kv_ctx_kernel,
        out_shape=(hd_shape, hd_shape),
        grid=(b,),
        in_specs=[
            pl.BlockSpec((1, c, d), lambda bi: (bi, 0, 0)),
            vec, vec,
            pl.BlockSpec((d, 2 * d), lambda bi: (0, 0)),
        ],
        out_specs=(out, out),
        compiler_params=_cparams("parallel"),
        name="kv_ctx_proj",
    )(ctx, sc, sh, w_kv_bf)


def _attn_finish(acc_a, l_a, acc_b, l_b, lamp_ref, g_ref, lam_init):
    lp = lamp_ref[...]
    lam = (jnp.exp(jnp.sum(lp[0:1] * lp[1:2], axis=-1, keepdims=True))
           - jnp.exp(jnp.sum(lp[2:3] * lp[3:4], axis=-1, keepdims=True)) + lam_init)
    o = acc_a / l_a - lam * (acc_b / l_b)
    o = o * lax.rsqrt(jnp.mean(o * o, axis=-1, keepdims=True) + LN_EPS) * g_ref[...] * (1.0 - lam_init)
    return o.astype(BF16)


def _attn_kernel(q_ref, k_ref, v_ref, kc_ref, vc_ref, lamp_ref, g_ref, o_ref,
                 s00, s01, s10, s11, p0, p1, mb0, mb1, lp0, lp1, acc0, acc1, *, tk, lam_init):
    q = q_ref[0, 0]
    tq = q.shape[0]
    half = LANES // 2
    lane = lax.broadcasted_iota(jnp.int32, q.shape, 1)
    zero = jnp.zeros_like(q)
    qs = (jnp.where(lane < half, q, zero), jnp.where(lane >= half, q, zero))
    n_chunks = k_ref.shape[2] // tk
    s_scr = ((s00, s01), (s10, s11))
    p_scr, mb_scr, lp_scr, acc_scr = (p0, p1), (mb0, mb1), (lp0, lp1), (acc0, acc1)
    nt = (((1,), (1,)), ((), ()))

    def scores(slot, kc):
        width = kc.shape[0]
        for mp in range(2):
            s_scr[slot][mp][:, 0:width] = lax.dot_general(qs[mp], kc, nt, preferred_element_type=F32)

    def absorb(slot, vc):
        width = vc.shape[0]
        for mp in range(2):
            for r in range(tq // ATT_STRIP):
                rows = slice(r * ATT_STRIP, (r + 1) * ATT_STRIP)
                shift = mb_scr[mp][rows, :]
                tiles = [jnp.exp2(s_scr[slot][mp][rows, t * LANES:(t + 1) * LANES] - shift)
                         for t in range(width // LANES)]
                p_scr[mp][rows, 0:width] = jnp.concatenate(tiles, axis=1).astype(BF16)
                lp_scr[mp][rows, :] += functools.reduce(lambda x, y: x + y, tiles)
            acc_scr[mp][...] += jnp.dot(p_scr[mp][:, 0:width], vc, preferred_element_type=F32)

    def k_chunk(j):
        return k_ref[0, 0, pl.ds(pl.multiple_of(j * tk, tk), tk), :]

    def v_chunk(j):
        return v_ref[0, 0, pl.ds(pl.multiple_of(j * tk, tk), tk), :]

    scores(0, k_chunk(0))
    for mp in range(2):
        mb_scr[mp][...] = jnp.broadcast_to(jnp.max(s_scr[0][mp][...], axis=-1, keepdims=True), (tq, LANES))
        lp_scr[mp][...] = jnp.zeros((tq, LANES), F32)
        acc_scr[mp][...] = jnp.zeros((tq, LANES), F32)

    def pair(jj, c):
        j0 = 2 * jj
        scores(1, k_chunk(j0 + 1))
        absorb(0, v_chunk(j0))
        scores(0, k_chunk(j0 + 2))
        absorb(1, v_chunk(j0 + 1))
        return c

    lax.fori_loop(0, n_chunks // 2 - 1, pair, 0)
    scores(1, k_chunk(n_chunks - 1))
    absorb(0, v_chunk(n_chunks - 2))
    scores(0, kc_ref[0, 0])
    absorb(1, v_chunk(n_chunks - 1))
    absorb(0, vc_ref[0, 0])

    sums = [jnp.sum(lp_scr[mp][...], axis=-1, keepdims=True) for mp in range(2)]
    bad = sum(jnp.sum(jnp.where(jnp.isfinite(x), 0.0, 1.0)) for x in (sums[0], sums[1], acc0[...], acc1[...]))

    @pl.when(bad == 0.0)
    def _():
        o_ref[0] = _attn_finish(acc0[...], sums[0], acc1[...], sums[1], lamp_ref, g_ref, lam_init)

    @pl.when(bad != 0.0)
    def _():
        def update(carry, kc, vc):
            new = []
            for mp in range(2):
                m, l, acc = carry[mp]
                s = lax.dot_general(qs[mp], kc, nt, preferred_element_type=F32)
                mn = jnp.maximum(m, jnp.max(s, axis=-1, keepdims=True))
                a = jnp.exp2(m - mn)
                p = jnp.exp2(s - mn)
                new.append((mn, a * l + jnp.sum(p, axis=-1, keepdims=True),
                            a * acc + jnp.dot(p.astype(BF16), vc, preferred_element_type=F32)))
            return tuple(new)

        init = tuple((jnp.full((tq, 1), NEG_INF, F32), jnp.zeros((tq, 1), F32), jnp.zeros((tq, LANES), F32))
                     for _ in range(2))
        carry = lax.fori_loop(0, n_chunks, lambda j, c: update(c, k_chunk(j), v_chunk(j)), init)
        (_, l_a, acc_a), (_, l_b, acc_b) = update(carry, kc_ref[0, 0], vc_ref[0, 0])
        o_ref[0] = _attn_finish(acc_a, l_a, acc_b, l_b, lamp_ref, g_ref, lam_init)


def _diff_attention(q, k, v, kc, vc, lam_p, subln_g, lam_init, tq, tk):
    b, h, n, _ = q.shape
    c = kc.shape[2]
    assert n % tk == 0 and (n // tk) % 2 == 0 and c <= tk and tq % ATT_STRIP == 0
    kv = pl.BlockSpec((1, 1, n, LANES), lambda bi, hi, i: (bi, hi, 0, 0))
    kvc = pl.BlockSpec((1, 1, c, LANES), lambda bi, hi, i: (bi, hi, 0, 0))
    return pl.pallas_call(
        functools.partial(_attn_kernel, tk=tk, lam_init=lam_init),
        out_shape=jax.ShapeDtypeStruct((b, n, h * LANES), BF16),
        grid=(b, h, n // tq),
        in_specs=[
            pl.BlockSpec((1, 1, tq, LANES), lambda bi, hi, i: (bi, hi, i, 0)),
            kv, kv, kvc, kvc,
            pl.BlockSpec(lam_p.shape, lambda bi, hi, i: (0, 0)),
            pl.BlockSpec((1, LANES), lambda bi, hi, i: (0, 0)),
        ],
        out_specs=pl.BlockSpec((1, tq, LANES), lambda bi, hi, i: (bi, i, hi)),
        scratch_shapes=[
            *[pltpu.VMEM((tq, tk), F32)] * 4,
            *[pltpu.VMEM((tq, tk), BF16)] * 2,
            *[pltpu.VMEM((tq, LANES), F32)] * 2,
            *[pltpu.VMEM((tq, LANES), F32)] * 2,
            *[pltpu.VMEM((tq, LANES), F32)] * 2,
        ],
        compiler_params=_cparams("parallel", "parallel", "parallel"),
        name="diff_attention",
    )(q, k, v, kc, vc, lam_p, subln_g.reshape(1, LANES))


def _sgu_kernel(x_ref, sc_ref, sh_ref, w_ref, b_ref, ng_ref, nb_ref, ws_ref, bs_ref, t_ref):
    f = t_ref.shape[-1]
    cg = f // SGU_GROUPS
    tm = x_ref.shape[1]
    h = (x_ref[0] * (1.0 + sc_ref[0]) + sh_ref[0]).astype(BF16)
    z = jnp.dot(h, w_ref[...], preferred_element_type=F32) + b_ref[...]
    z = 0.5 * z * (1.0 + lax.erf(z * (2.0 ** -0.5)))
    u = z[:, :f]
    v = _layer_norm(z[:, f:], ng_ref[...], nb_ref[...]).astype(BF16)
    for c in range(tm // SGU_CHUNK):
        rows = slice(c * SGU_CHUNK, (c + 1) * SGU_CHUNK)
        for g in range(SGU_GROUPS):
            cols = slice(g * cg, (g + 1) * cg)
            vm = jnp.dot(ws_ref[g], v[rows, cols], preferred_element_type=F32) + bs_ref[:, g:g + 1]
            t_ref[0, rows, cols] = (u[rows, cols] * vm).astype(BF16)


def _sgu_mixer(x, sc, sh, w_in_bf, b_in, norm_g, norm_b, w_s_bf, b_s_t, tm):
    b, n, d = x.shape
    f2 = w_in_bf.shape[1]
    f = f2 // 2
    vec = pl.BlockSpec((1, 1, d), lambda bi, i: (bi, 0, 0))
    full2 = lambda a: pl.BlockSpec(a.shape, lambda bi, i: (0,) * a.ndim)
    b_in2, ng2, nb2 = b_in.reshape(1, f2), norm_g.reshape(1, f), norm_b.reshape(1, f)
    return pl.pallas_call(
        _sgu_kernel,
        out_shape=jax.ShapeDtypeStruct((b, n, f), BF16),
        grid=(b, n // tm),
        in_specs=[
            pl.BlockSpec((1, tm, d), lambda bi, i: (bi, i, 0)),
            vec, vec,
            full2(w_in_bf), full2(b_in2), full2(ng2), full2(nb2), full2(w_s_bf), full2(b_s_t),
        ],
        out_specs=pl.BlockSpec((1, tm, f), lambda bi, i: (bi, i, 0)),
        compiler_params=_cparams("parallel", "parallel"),
        name="sgu_mixer",
    )(x, sc, sh, w_in_bf, b_in2, ng2, nb2, w_s_bf, b_s_t)


def _post_kernel(pre_ref, w_ref, x_ref, gm_ref, lg_ref, lb_ref, scf_ref, shf_ref, x1_ref, hf_ref, *, alpha):
    y = jnp.dot(pre_ref[0], w_ref[...], preferred_element_type=F32)
    x1 = _layer_norm(alpha * x_ref[0] + gm_ref[0] * y, lg_ref[...], lb_ref[...])
    x1_ref[0] = x1
    hf_ref[0] = _pack_bf16_pairs(x1 * (1.0 + scf_ref[0]) + shf_ref[0])


def _post_mixer(pre, w_bf, x, gm, ln_g, ln_b, scf, shf, alpha, tm):
    b, n, d = x.shape
    kd = pre.shape[-1]
    vec = pl.BlockSpec((1, 1, d), lambda bi, i: (bi, 0, 0))
    row = pl.BlockSpec((1, d), lambda bi, i: (0, 0))
    tile = pl.BlockSpec((1, tm, d), lambda bi, i: (bi, i, 0))
    return pl.pallas_call(
        functools.partial(_post_kernel, alpha=alpha),
        out_shape=(jax.ShapeDtypeStruct((b, n, d), F32), jax.ShapeDtypeStruct((b, n, d // 2), I32)),
        grid=(b, n // tm),
        in_specs=[
            pl.BlockSpec((1, tm, kd), lambda bi, i: (bi, i, 0)),
            pl.BlockSpec((kd, d), lambda bi, i: (0, 0)),
            tile, vec, row, row, vec, vec,
        ],
        out_specs=(tile, pl.BlockSpec((1, tm, d // 2), lambda bi, i: (bi, i, 0))),
        compiler_params=_cparams("parallel", "parallel"),
        name="post_mixer",
    )(pre, w_bf, x, gm, ln_g.reshape(1, d), ln_b.reshape(1, d), scf, shf)


def _route_kernel(x_ref, sc_ref, sh_ref, wr_ref, rb_ref, idx_ref, w_ref, rank_ref, cnt_ref, carry_ref):
    i = pl.program_id(0)
    e = wr_ref.shape[0]
    tm = x_ref.shape[0]
    ge = e // N_GROUPS

    @pl.when(i == 0)
    def _():
        carry_ref[...] = jnp.zeros_like(carry_ref)

    h = x_ref[...] * (1.0 + sc_ref[0]) + sh_ref[0]
    logits = lax.dot_general(wr_ref[...], h, (((1,), (1,)), ((), ())),
                             precision=lax.Precision.HIGHEST, preferred_element_type=F32)
    scores = 1.0 / (1.0 + jnp.exp(-logits))
    choice = scores + rb_ref[...]

    g3 = choice.reshape(N_GROUPS, ge, tm)
    ri = lax.broadcasted_iota(jnp.int32, g3.shape, 1).astype(F32)
    m1 = jnp.max(g3, axis=1, keepdims=True)
    first = jnp.min(jnp.where(g3 == m1, ri, float(ge)), axis=1, keepdims=True)
    m2 = jnp.max(jnp.where(ri == first, NEG_INF, g3), axis=1, keepdims=True)
    gs = m1 + m2

    gi = lax.broadcasted_iota(jnp.int32, gs.shape, 0).astype(F32)
    gsel = jnp.zeros(gs.shape, F32)
    cur = gs
    for _ in range(TOPK_GROUPS):
        m = jnp.max(cur, axis=0, keepdims=True)
        f = jnp.min(jnp.where(cur == m, gi, float(N_GROUPS)), axis=0, keepdims=True)
        hit = gi == f
        gsel = jnp.where(hit, 1.0, gsel)
        cur = jnp.where(hit, NEG_INF, cur)
    emask = jnp.broadcast_to(gsel, g3.shape).reshape(e, tm)
    masked = jnp.where(emask > 0.5, choice, NEG_INF)

    ei = lax.broadcasted_iota(jnp.int32, (e, tm), 0).astype(F32)
    onehot = jnp.zeros((e, tm), F32)
    idxs, ws = [], []
    for _ in range(TOP_K):
        m = jnp.max(masked, axis=0, keepdims=True)
        f = jnp.min(jnp.where(masked == m, ei, float(e)), axis=0, keepdims=True)
        hit = ei == f
        idxs.append(f)
        ws.append(jnp.sum(jnp.where(hit, scores, 0.0), axis=0, keepdims=True))
        masked = jnp.where(hit, NEG_INF, masked)
        onehot = jnp.where(hit, 1.0, onehot)
    wsum = ws[0]
    for k in range(1, TOP_K):
        wsum = wsum + ws[k]

    r_i = lax.broadcasted_iota(jnp.int32, (tm, tm), 0)
    c_i = lax.broadcasted_iota(jnp.int32, (tm, tm), 1)
    upper = jnp.where(r_i < c_i, 1.0, 0.0).astype(BF16)
    rk = jnp.dot(onehot.astype(BF16), upper, preferred_element_type=F32) + carry_ref[...]
    carry_ref[...] += jnp.sum(onehot, axis=1, keepdims=True)

    for k in range(TOP_K):
        idx_ref[k:k + 1, :] = idxs[k].astype(jnp.int32)
        w_ref[k:k + 1, :] = ws[k] / wsum * ROUTED_SCALE
        rank_ref[k:k + 1, :] = jnp.sum(jnp.where(ei == idxs[k], rk, 0.0), axis=0, keepdims=True).astype(jnp.int32)
    cnt_ref[...] = jnp.broadcast_to(carry_ref[...], cnt_ref.shape).astype(jnp.int32)


def _route(x1, scf, shf, wr_t, rbias, tm):
    b, n, d = x1.shape
    t = b * n
    e = wr_t.shape[0]
    per_b = n // tm
    vec = pl.BlockSpec((1, 1, d), lambda i: (i // per_b, 0, 0))
    out_t = pl.BlockSpec((TOP_K, tm), lambda i: (0, i))
    return pl.pallas_call(
        _route_kernel,
        out_shape=(jax.ShapeDtypeStruct((TOP_K, t), jnp.int32), jax.ShapeDtypeStruct((TOP_K, t), F32),
                   jax.ShapeDtypeStruct((TOP_K, t), jnp.int32), jax.ShapeDtypeStruct((e, LANES), jnp.int32)),
        grid=(t // tm,),
        in_specs=[
            pl.BlockSpec((tm, d), lambda i: (i, 0)),
            vec, vec,
            pl.BlockSpec((e, d), lambda i: (0, 0)),
            pl.BlockSpec((e, 1), lambda i: (0, 0)),
        ],
        out_specs=(out_t, out_t, out_t, pl.BlockSpec((e, LANES), lambda i: (0, 0))),
        scratch_shapes=[pltpu.VMEM((e, 1), F32)],
        compiler_params=_cparams("arbitrary"),
        name="route",
    )(x1.reshape(t, d), scf, shf, wr_t, rbias.reshape(e, 1))


def _gather_rows(table, idx):
    m = idx.shape[0]
    w = table.shape[1]
    workers = SC_CORES * SC_SUBCORES
    n_ch = m // (workers * SC_CHUNK)
    assert m % (workers * SC_CHUNK) == 0 and n_ch % 2 == 0
    mesh = plsc.VectorSubcoreMesh(core_axis_name="c", subcore_axis_name="s",
                                  num_cores=SC_CORES, num_subcores=SC_SUBCORES)

    @functools.partial(
        pl.kernel, mesh=mesh,
        out_type=jax.ShapeDtypeStruct((m, w), table.dtype),
        scratch_types=[
            pltpu.VMEM((n_ch, SC_CHUNK), I32),
            pltpu.VMEM((SC_CHUNK, w), table.dtype), pltpu.VMEM((SC_CHUNK, w), table.dtype),
            pltpu.SemaphoreType.DMA, pltpu.SemaphoreType.DMA, pltpu.SemaphoreType.DMA, pltpu.SemaphoreType.DMA,
        ],
        name="sc_gather_rows",
    )
    def gather(table_hbm, idx_hbm, out_hbm, idx_all, buf0, buf1, gsem0, gsem1, wsem0, wsem1):
        first = (lax.axis_index("s") * SC_CORES + lax.axis_index("c")) * n_ch
        bufs, gsem, wsem = (buf0, buf1), (gsem0, gsem1), (wsem0, wsem1)
        pltpu.sync_copy(idx_hbm.at[pl.ds(first, n_ch)], idx_all)

        def gather_copy(j, s):
            return pltpu.make_async_copy(table_hbm.at[idx_all.at[j]], bufs[s], gsem[s])

        def write_copy(j, s):
            rows = pl.ds(pl.multiple_of((first + j) * SC_CHUNK, SC_CHUNK), SC_CHUNK)
            return pltpu.make_async_copy(bufs[s], out_hbm.at[rows], wsem[s])

        gather_copy(0, 0).start()

        @pl.loop(0, n_ch, step=2)
        def _(jj):
            for s in range(2):
                j = jj + s

                @pl.when(j >= 1)
                def _():
                    write_copy(j - 1, 1 - s).wait()

                @pl.when(j + 1 < n_ch)
                def _():
                    gather_copy(j + 1, 1 - s).start()

                gather_copy(j, s).wait()
                write_copy(j, s).start()

        write_copy(n_ch - 1, 1).wait()

    return gather(table, idx.reshape(m // SC_CHUNK, SC_CHUNK))


def _scatter_rows(src, pos3, p):
    w = src.shape[1]
    n_chunks, k, ch = pos3.shape
    workers = SC_CORES * SC_SUBCORES
    per_w = n_chunks // workers
    assert ch == SC_CHUNK and n_chunks % workers == 0 and src.shape[0] == n_chunks * ch
    mesh = plsc.VectorSubcoreMesh(core_axis_name="c", subcore_axis_name="s",
                                  num_cores=SC_CORES, num_subcores=SC_SUBCORES)

    @functools.partial(
        pl.kernel, mesh=mesh,
        out_type=jax.ShapeDtypeStruct((p, w), src.dtype),
        scratch_types=[
            pltpu.VMEM((k, ch), I32),
            pltpu.VMEM((ch, w), src.dtype),
            pltpu.SemaphoreType.DMA,
        ],
        name="sc_scatter_rows",
    )
    def scatter(src_hbm, pos_hbm, out_hbm, idx_v, rows_v, sem):
        first = (lax.axis_index("s") * SC_CORES + lax.axis_index("c")) * per_w

        @pl.loop(0, per_w)
        def _(j):
            c = first + j
            pltpu.sync_copy(pos_hbm.at[c], idx_v)
            pltpu.sync_copy(src_hbm.at[pl.ds(pl.multiple_of(c * ch, ch), ch)], rows_v)
            copies = [pltpu.async_copy(rows_v, out_hbm.at[idx_v.at[kk]], sem) for kk in range(k)]
            for cp in copies:
                cp.wait()

    return scatter(src, pos3)


def _experts_kernel(ps_ref, nb_ref, cnt_ref, nt_ref, xs_hbm, wg_ref, wu_ref, wd_ref, y_hbm,
                    xbuf, ybuf, in_sem, out_sem, wg_bf, wu_bf, wd_bf):
    e = pl.program_id(0)
    nb, cnt, n_total = nb_ref[e], cnt_ref[e], nt_ref[0]
    g0 = ps_ref[e] // MOE_ROWS

    def in_copy(g):
        rows = pl.ds(pl.multiple_of(g * MOE_ROWS, MOE_ROWS), MOE_ROWS)
        return pltpu.make_async_copy(xs_hbm.at[rows], xbuf.at[g % MOE_SLOTS], in_sem.at[g % MOE_SLOTS])

    def out_copy(g):
        rows = pl.ds(pl.multiple_of(g * MOE_ROWS, MOE_ROWS), MOE_ROWS)
        return pltpu.make_async_copy(ybuf.at[g % MOE_SLOTS], y_hbm.at[rows], out_sem.at[g % MOE_SLOTS])

    @pl.when(e == 0)
    def _():
        for j in range(MOE_LOOKAHEAD):
            @pl.when(j < n_total)
            def _():
                in_copy(j).start()

    @pl.when(nb > 0)
    def _():
        wg_bf[...] = wg_ref[0, 0].astype(BF16)
        wu_bf[...] = wu_ref[0, 0].astype(BF16)
        wd_bf[...] = wd_ref[0, 0].astype(BF16)

        def process(b, width):
            g = g0 + b
            for u in range(width):
                @pl.when(g + u + MOE_LOOKAHEAD < n_total)
                def _():
                    in_copy(g + u + MOE_LOOKAHEAD).start()

            for u in range(width):
                in_copy(g + u).wait()

                @pl.when(g + u >= MOE_SLOTS)
                def _():
                    out_copy(g + u - MOE_SLOTS).wait()

            packed = jnp.concatenate([xbuf[(g + u) % MOE_SLOTS] for u in range(width)], axis=0)
            row = lax.broadcasted_iota(I32, (width * MOE_ROWS, 1), 0) + b * MOE_ROWS
            x_lo, x_hi = (v.astype(BF16) for v in _unpack_bf16_pairs(jnp.where(row < cnt, packed, 0)))
            half = x_lo.shape[1]

            def up(w_bf):
                return (jnp.dot(x_lo, w_bf[:half, :], preferred_element_type=F32)
                        + jnp.dot(x_hi, w_bf[half:, :], preferred_element_type=F32))

            hb = (_silu(up(wg_bf)) * up(wu_bf)).astype(BF16)
            y = _pack_bf16_pairs(jnp.dot(hb, wd_bf[...], preferred_element_type=F32))
            for u in range(width):
                ybuf[(g + u) % MOE_SLOTS] = y[u * MOE_ROWS:(u + 1) * MOE_ROWS]
                out_copy(g + u).start()

        def body(i, c):
            process(MOE_GROUP * i, MOE_GROUP)
            return c

        lax.fori_loop(0, nb // MOE_GROUP, body, 0)

        def tail(i, c):
            process(i, 1)
            return c

        lax.fori_loop(nb // MOE_GROUP * MOE_GROUP, nb, tail, 0)

    @pl.when(e == pl.num_programs(0) - 1)
    def _():
        for j in range(MOE_SLOTS):
            @pl.when(n_total - 1 - j >= 0)
            def _():
                out_copy(n_total - 1 - j).wait()


def _routed_experts(xs, wg, wu, wd, layer, pstarts, nblk, counts, n_total):
    p, dp = xs.shape
    _, e, d, f = wg.shape
    wspec = lambda r, c: pl.BlockSpec((1, 1, r, c), lambda i, ps, nb, cnt, nt: (layer, i, 0, 0))
    grid_spec = pltpu.PrefetchScalarGridSpec(
        num_scalar_prefetch=4,
        grid=(e,),
        in_specs=[pl.BlockSpec(memory_space=pl.ANY), wspec(d, f), wspec(d, f), wspec(f, d)],
        out_specs=pl.BlockSpec(memory_space=pl.ANY),
        scratch_shapes=[
            pltpu.VMEM((MOE_SLOTS, MOE_ROWS, dp), I32), pltpu.VMEM((MOE_SLOTS, MOE_ROWS, dp), I32),
            pltpu.SemaphoreType.DMA((MOE_SLOTS,)), pltpu.SemaphoreType.DMA((MOE_SLOTS,)),
            pltpu.VMEM((d, f), BF16), pltpu.VMEM((d, f), BF16), pltpu.VMEM((f, d), BF16),
        ],
    )
    return pl.pallas_call(
        _experts_kernel,
        out_shape=jax.ShapeDtypeStruct((p, dp), I32),
        grid_spec=grid_spec,
        compiler_params=_cparams("arbitrary"),
        name="routed_experts",
    )(pstarts, nblk, counts, n_total, xs, wg, wu, wd)


def _combine_kernel(yg_ref, w_ref, hf_ref, sg_ref, su_ref, sd_ref, x_ref, gf_ref, lg_ref, lb_ref, o_ref, *, alpha):
    w = w_ref[...]
    r_lo, r_hi = _unpack_bf16_pairs(yg_ref[0])
    r_lo, r_hi = w[:, 0:1] * r_lo, w[:, 0:1] * r_hi
    for k in range(1, TOP_K):
        y_lo, y_hi = _unpack_bf16_pairs(yg_ref[k])
        r_lo, r_hi = r_lo + w[:, k:k + 1] * y_lo, r_hi + w[:, k:k + 1] * y_hi
    routed = jnp.concatenate([r_lo, r_hi], axis=1)
    hf = jnp.concatenate(_unpack_bf16_pairs(hf_ref[...]), axis=1).astype(BF16)
    g = jnp.dot(hf, sg_ref[...], preferred_element_type=F32)
    u = jnp.dot(hf, su_ref[...], preferred_element_type=F32)
    shared = jnp.dot((_silu(g) * u).astype(BF16), sd_ref[...], preferred_element_type=F32)
    o_ref[...] = _layer_norm(alpha * x_ref[...] + gf_ref[0] * (routed + shared), lg_ref[...], lb_ref[...])


def _combine(yg, w_tk, hf, sg_bf, su_bf, sd_bf, x1, gf, ln_g, ln_b, alpha, tm, per_b):
    t, d = x1.shape
    f = sg_bf.shape[1]
    row = pl.BlockSpec((1, d), lambda i: (0, 0))
    tile = pl.BlockSpec((tm, d), lambda i: (i, 0))
    return pl.pallas_call(
        functools.partial(_combine_kernel, alpha=alpha),
        out_shape=jax.ShapeDtypeStruct((t, d), F32),
        grid=(t // tm,),
        in_specs=[
            pl.BlockSpec((TOP_K, tm, d // 2), lambda i: (0, i, 0)),
            pl.BlockSpec((tm, TOP_K), lambda i: (i, 0)),
            pl.BlockSpec((tm, d // 2), lambda i: (i, 0)),
            pl.BlockSpec((d, f), lambda i: (0, 0)),
            pl.BlockSpec((d, f), lambda i: (0, 0)),
            pl.BlockSpec((f, d), lambda i: (0, 0)),
            tile,
            pl.BlockSpec((1, 1, d), lambda i: (i // per_b, 0, 0)),
            row, row,
        ],
        out_specs=tile,
        compiler_params=_cparams("parallel"),
        name="moe_combine",
    )(yg, w_tk, hf, sg_bf, su_bf, sd_bf, x1, gf, ln_g.reshape(1, d), ln_b.reshape(1, d))


def _moe_layer(x1, hf, scf, shf, gf, router_w, router_bias, wg, wu, wd, layer, sg, su, sd, ln_g, ln_b, alpha, tm):
    b, n, d = x1.shape
    t = b * n
    e = router_w.shape[1]
    idx_t, w_t, rank_t, cnt = _route(x1, scf, shf, router_w.T, router_bias, tm)

    counts = cnt[:, 0]
    padded = (counts + MOE_ROWS - 1) // MOE_ROWS * MOE_ROWS
    pends = jnp.cumsum(padded)
    pstarts = pends - padded
    sel = idx_t[:, :, None] == jnp.arange(e, dtype=I32)
    pos_t = jnp.sum(jnp.where(sel, pstarts, 0), axis=-1) + rank_t
    p = t * TOP_K + e * MOE_ROWS
    pos3 = pos_t.reshape(TOP_K, t // SC_CHUNK, SC_CHUNK).transpose(1, 0, 2)

    hf2 = hf.reshape(t, d // 2)
    xs = _scatter_rows(hf2, pos3, p)
    yb = _routed_experts(xs, wg, wu, wd, layer, pstarts.astype(I32), (padded // MOE_ROWS).astype(I32), counts,
                         (pends[-1:] // MOE_ROWS).astype(I32))
    yg = _gather_rows(yb, pos_t.reshape(-1)).reshape(TOP_K, t, d // 2)
    out = _combine(yg, w_t.T, hf2, sg.astype(BF16), su.astype(BF16), sd.astype(BF16), x1.reshape(t, d), gf,
                   ln_g, ln_b, alpha, tm, n // tm)
    return out.reshape(b, n, d)


def _rope_tables(n):
    rows = n // GRID_W
    row_pos = jnp.repeat(jnp.arange(rows, dtype=F32), GRID_W)
    col_pos = jnp.tile(jnp.arange(GRID_W, dtype=F32), rows)
    half = LANES // 4
    lane = jnp.arange(LANES)
    in_blk = lane % half
    freq = ROPE_THETA ** (-(2.0 * (in_blk % (half // 2)).astype(F32)) / half)
    use_col = (lane // half) % 2 == 1
    pos = jnp.where(use_col[None, :], col_pos[:, None], row_pos[:, None])
    ang = pos * freq[None, :]
    lo = (in_blk < half // 2)[None, :]
    sin = jnp.sin(ang)
    return jnp.cos(ang), jnp.where(lo, -sin, 0.0), jnp.where(lo, 0.0, sin)


def kernel(x, c, ctx, c_ctx, w_mod, b_mod, ln_g, ln_b, attn_w_in, attn_w_out, attn_lambda, attn_subln_g,
           sgu_w_in, sgu_b_in, sgu_norm_g, sgu_norm_b, sgu_w_s, sgu_b_s, sgu_w_out,
           router_w, router_bias, exp_w_gate, exp_w_up, exp_w_down, sh_w_gate, sh_w_up, sh_w_down):
    b, n, d = x.shape
    depth = w_mod.shape[0]
    assert b <= 7 and d == ATT_HEADS * LANES and n % GRID_W == 0
    alpha = (2 * depth) ** 0.25
    head_dim = d // ATT_HEADS // 2
    tm = 512 if n % 512 == 0 else 256

    cs = jnp.zeros((8, d), F32).at[:b].set(c).at[b].set(c_ctx)
    mods = _modulation(cs, w_mod, b_mod)

    def mod_vec(i, j):
        return mods[i, :, j * d:(j + 1) * d].reshape(8, 1, d)

    for i in range(depth):
        sh_m, sc_m, g_m, sh_f, sc_f, g_f = (mod_vec(i, j) for j in range(6))
        if i % N_MIXERS == 0:
            a = i // N_MIXERS
            lam_init = 0.8 - 0.6 * math.exp(-0.3 * i)
            w_in_bf = attn_w_in[a].astype(BF16)
            cos, slo, shi = _rope_tables(n)
            q_scale = head_dim ** -0.5 * math.log2(math.e)
            q, k, v = _qkv_proj(x, sc_m, sh_m, w_in_bf, cos, slo, shi, q_scale, tm)
            kc, vc = _kv_ctx_proj(ctx, sc_m[b:b + 1], sh_m[b:b + 1], w_in_bf[:, d:])
            tk = min(1024, n // 2)
            pre = _diff_attention(q, k, v, kc, vc, attn_lambda[a], attn_subln_g[a], lam_init, min(1024, n), tk)
            w_out_bf = attn_w_out[a].astype(BF16)
        else:
            s = i // N_MIXERS
            pre = _sgu_mixer(x, sc_m, sh_m, sgu_w_in[s].astype(BF16), sgu_b_in[s], sgu_norm_g[s], sgu_norm_b[s],
                             sgu_w_s[s].astype(BF16), sgu_b_s[s].T, 256)
            w_out_bf = sgu_w_out[s].astype(BF16)
        x1, hf = _post_mixer(pre, w_out_bf, x, g_m, ln_g[i, 0], ln_b[i, 0], sc_f, sh_f, alpha, tm)
        x = _moe_layer(x1, hf, sc_f, sh_f, g_f, router_w[i], router_bias[i], exp_w_gate, exp_w_up, exp_w_down, i,
                       sh_w_gate[i], sh_w_up[i], sh_w_down[i], ln_g[i, 1], ln_b[i, 1], alpha, tm)
    return x
```

```python
import functools
import math

import jax
import jax.numpy as jnp
from jax import lax
from jax.experimental import pallas as pl
from jax.experimental.pallas import tpu as pltpu
from jax.experimental.pallas import tpu_sc as plsc

F32 = jnp.float32
BF16 = jnp.bfloat16
I32 = jnp.int32

GRID_W = 64
ATT_HEADS = 8
ROPE_THETA = 10000.0
SGU_CHUNK = 128
SGU_GROUPS = 8
TOP_K = 8
N_GROUPS = 8
TOPK_GROUPS = 4
ROUTED_SCALE = 2.5
LN_EPS = 1e-5
N_MIXERS = 2

LANES = 128
MOE_ROWS = 256
ROUTE_SLAB = 512
MOE_GROUP = 2
MOE_LOOKAHEAD = 4
MOE_SLOTS = MOE_LOOKAHEAD + MOE_GROUP
ATT_STRIP = 16
SC_CORES = 2
SC_SUBCORES = 16
SC_CHUNK = 64
VMEM_LIMIT = 56 * 1024 * 1024
NEG_INF = float("-inf")


def _cparams(*sem):
    return pltpu.CompilerParams(dimension_semantics=sem, vmem_limit_bytes=VMEM_LIMIT)


def _layer_norm(z, g, b):
    mu = jnp.mean(z, axis=-1, keepdims=True)
    zc = z - mu
    var = jnp.mean(zc * zc, axis=-1, keepdims=True)
    return zc * lax.rsqrt(var + LN_EPS) * g + b


def _silu(x):
    return x * (1.0 / (1.0 + jnp.exp(-x)))


_HIGH_HALF = -65536


def _pack_bf16_pairs(y):
    w = y.shape[1] // 2
    bits = lax.bitcast_convert_type(y.astype(BF16).astype(F32), I32)
    return lax.shift_right_logical(bits[:, :w], 16) | (bits[:, w:] & _HIGH_HALF)


def _unpack_bf16_pairs(p):
    return (lax.bitcast_convert_type(lax.shift_left(p, 16), F32),
            lax.bitcast_convert_type(p & _HIGH_HALF, F32))


def _mod_kernel(cs_ref, w_ref, b_ref, o_ref):
    s = _silu(cs_ref[...])
    o_ref[0] = jnp.dot(s, w_ref[0], precision=lax.Precision.HIGHEST,
                       preferred_element_type=F32) + b_ref[0]


def _modulation(cs, w_mod, b_mod):
    depth, d, n6 = w_mod.shape
    tn = n6 // 4
    return pl.pallas_call(
        _mod_kernel,
        out_shape=jax.ShapeDtypeStruct((depth, 8, n6), F32),
        grid=(depth, n6 // tn),
        in_specs=[
            pl.BlockSpec((8, d), lambda l, j: (0, 0)),
            pl.BlockSpec((1, d, tn), lambda l, j: (l, 0, j)),
            pl.BlockSpec((1, 1, tn), lambda l, j: (l, 0, j)),
        ],
        out_specs=pl.BlockSpec((1, 8, tn), lambda l, j: (l, 0, j)),
        compiler_params=_cparams("parallel", "parallel"),
        name="modulation",
    )(cs, w_mod, b_mod.reshape(depth, 1, n6))


def _rope(xh, cos, sin_lo, sin_hi):
    return xh * cos + pltpu.roll(xh, LANES - 16, 1) * sin_lo + pltpu.roll(xh, 16, 1) * sin_hi


def _qkv_kernel(x_ref, sc_ref, sh_ref, w_ref, cos_ref, slo_ref, shi_ref, q_ref, k_ref, v_ref, *, q_scale):
    d = x_ref.shape[-1]
    h = (x_ref[0] * (1.0 + sc_ref[0]) + sh_ref[0]).astype(BF16)
    cos, slo, shi = cos_ref[...], slo_ref[...], shi_ref[...]
    q = jnp.dot(h, w_ref[:, 0:d], preferred_element_type=F32)
    for hd in range(ATT_HEADS):
        q_ref[0, hd] = (_rope(q[:, hd * LANES:(hd + 1) * LANES], cos, slo, shi) * q_scale).astype(BF16)
    k = jnp.dot(h, w_ref[:, d:2 * d], preferred_element_type=F32)
    for hd in range(ATT_HEADS):
        k_ref[0, hd] = _rope(k[:, hd * LANES:(hd + 1) * LANES], cos, slo, shi).astype(BF16)
    v = jnp.dot(h, w_ref[:, 2 * d:3 * d], preferred_element_type=F32)
    for hd in range(ATT_HEADS):
        v_ref[0, hd] = v[:, hd * LANES:(hd + 1) * LANES].T.astype(BF16)


def _qkv_proj(x, sc, sh, w_bf, cos, slo, shi, q_scale, tn):
    b, n, d = x.shape
    hd_shape = jax.ShapeDtypeStruct((b, ATT_HEADS, n, LANES), BF16)
    vec = pl.BlockSpec((1, 1, d), lambda bi, i: (bi, 0, 0))
    tab = pl.BlockSpec((tn, LANES), lambda bi, i: (i, 0))
    out = pl.BlockSpec((1, ATT_HEADS, tn, LANES), lambda bi, i: (bi, 0, i, 0))
    out_t = pl.BlockSpec((1, ATT_HEADS, LANES, tn), lambda bi, i: (bi, 0, 0, i))
    return pl.pallas_call(
        functools.partial(_qkv_kernel, q_scale=q_scale),
        out_shape=(hd_shape, hd_shape, jax.ShapeDtypeStruct((b, ATT_HEADS, LANES, n), BF16)),
        grid=(b, n // tn),
        in_specs=[
            pl.BlockSpec((1, tn, d), lambda bi, i: (bi, i, 0)),
            vec, vec,
            pl.BlockSpec((d, 3 * d), lambda bi, i: (0, 0)),
            tab, tab, tab,
        ],
        out_specs=(out, out, out_t),
        compiler_params=_cparams("parallel", "parallel"),
        name="qkv_proj",
    )(x, sc, sh, w_bf, cos, slo, shi)


def _kv_ctx_kernel(x_ref, sc_ref, sh_ref, w_ref, k_ref, v_ref):
    d = x_ref.shape[-1]
    h = (x_ref[0] * (1.0 + sc_ref[0]) + sh_ref[0]).astype(BF16)
    k = jnp.dot(h, w_ref[:, 0:d], preferred_element_type=F32)
    v = jnp.dot(h, w_ref[:, d:2 * d], preferred_element_type=F32)
    for hd in range(ATT_HEADS):
        k_ref[0, hd] = k[:, hd * LANES:(hd + 1) * LANES].astype(BF16)
        v_ref[0, hd] = v[:, hd * LANES:(hd + 1) * LANES].T.astype(BF16)


def _kv_ctx_proj(ctx, sc, sh, w_kv_bf):
    b, c, d = ctx.shape
    hd_shape = jax.ShapeDtypeStruct((b, ATT_HEADS, c, LANES), BF16)
    vec = pl.BlockSpec((1, 1, d), lambda bi: (0, 0, 0))
    out = pl.BlockSpec((1, ATT_HEADS, c, LANES), lambda bi: (bi, 0, 0, 0))
    out_t = pl.BlockSpec((1, ATT_HEADS, LANES, c), lambda bi: (bi, 0, 0, 0))
    return pl.pallas_call(
        _kv_ctx_kernel,
        out_shape=(hd_shape, jax.ShapeDtypeStruct((b, ATT_HEADS, LANES, c), BF16)),
        grid=(b,),
        in_specs=[
            pl.BlockSpec((1, c, d), lambda bi: (bi, 0, 0)),
            vec, vec,
            pl.BlockSpec((d, 2 * d), lambda bi: (0, 0)),
        ],
        out_specs=(out, out_t),
        compiler_params=_cparams("parallel"),
        name="kv_ctx_proj",
    )(ctx, sc, sh, w_kv_bf)


def _attn_finish(acc_a, l_a, acc_b, l_b, lamp_ref, g_ref, lam_init):
    lp = lamp_ref[...]
    lam = (jnp.exp(jnp.sum(lp[0:1] * lp[1:2], axis=-1, keepdims=True))
           - jnp.exp(jnp.sum(lp[2:3] * lp[3:4], axis=-1, keepdims=True)) + lam_init)
    o = acc_a / l_a - lam * (acc_b / l_b)
    o = o * lax.rsqrt(jnp.mean(o * o, axis=0, keepdims=True) + LN_EPS) * g_ref[...] * (1.0 - lam_init)
    return o.T.astype(BF16)


def _attn_kernel(q_ref, k_ref, vt_ref, kc_ref, vct_ref, lamp_ref, g_ref, o_ref,
                 s00, s01, s10, s11, p0, p1, acc0, acc1, *, tk, lam_init):
    q = q_ref[0, 0]
    tq = q.shape[0]
    half = LANES // 2
    lane = lax.broadcasted_iota(jnp.int32, q.shape, 1)
    zero = jnp.zeros_like(q)
    qs = (jnp.where(lane < half, q, zero), jnp.where(lane >= half, q, zero))
    n_chunks = k_ref.shape[2] // tk
    s_scr = ((s00, s01), (s10, s11))
    p_scr, acc_scr = (p0, p1), (acc0, acc1)
    nt = (((1,), (1,)), ((), ()))

    def scores(slot, kc):
        width = kc.shape[0]
        for mp in range(2):
            s_scr[slot][mp][0:width, :] = lax.dot_general(kc, qs[mp], nt, preferred_element_type=F32)

    def absorb(slot, vct, shift, sums):
        width = vct.shape[1]
        sums = list(sums)
        for mp in range(2):
            part = sums[mp]
            for r in range(width // ATT_STRIP):
                lo = r * ATT_STRIP
                tiles = [jnp.exp2(s_scr[slot][mp][lo + 8 * u:lo + 8 * (u + 1), :] - shift[mp])
                         for u in range(ATT_STRIP // 8)]
                p_scr[mp][lo:lo + ATT_STRIP, :] = jnp.concatenate(tiles, axis=0).astype(BF16)
                part = part + functools.reduce(lambda x, y: x + y, tiles)
            sums[mp] = part
            acc_scr[mp][...] += jnp.dot(vct, p_scr[mp][0:width, :], preferred_element_type=F32)
        return tuple(sums)

    def k_chunk(j):
        return k_ref[0, 0, pl.ds(pl.multiple_of(j * tk, tk), tk), :]

    def vt_chunk(j):
        return vt_ref[0, 0, :, pl.ds(pl.multiple_of(j * tk, tk), tk)]

    scores(0, k_chunk(0))
    shift = tuple(jnp.broadcast_to(jnp.max(s_scr[0][mp][...], axis=0, keepdims=True), (8, tq)) for mp in range(2))
    for mp in range(2):
        acc_scr[mp][...] = jnp.zeros((LANES, tq), F32)
    sums = (jnp.zeros((8, tq), F32), jnp.zeros((8, tq), F32))

    def pair(jj, sums):
        j0 = 2 * jj
        scores(1, k_chunk(j0 + 1))
        sums = absorb(0, vt_chunk(j0), shift, sums)
        scores(0, k_chunk(j0 + 2))
        return absorb(1, vt_chunk(j0 + 1), shift, sums)

    sums = lax.fori_loop(0, n_chunks // 2 - 1, pair, sums)
    scores(1, k_chunk(n_chunks - 1))
    sums = absorb(0, vt_chunk(n_chunks - 2), shift, sums)
    scores(0, kc_ref[0, 0])
    sums = absorb(1, vt_chunk(n_chunks - 1), shift, sums)
    sums = absorb(0, vct_ref[0, 0], shift, sums)

    tot = [jnp.sum(sums[mp], axis=0, keepdims=True) for mp in range(2)]
    bad = sum(jnp.sum(jnp.where(jnp.isfinite(x), 0.0, 1.0)) for x in (tot[0], tot[1], acc0[...], acc1[...]))

    @pl.when(bad == 0.0)
    def _():
        o_ref[0] = _attn_finish(acc0[...], tot[0], acc1[...], tot[1], lamp_ref, g_ref, lam_init)

    @pl.when(bad != 0.0)
    def _():
        def update(carry, kc, vct):
            new = []
            for mp in range(2):
                m, l, acc = carry[mp]
                s = lax.dot_general(kc, qs[mp], nt, preferred_element_type=F32)
                mn = jnp.maximum(m, jnp.max(s, axis=0, keepdims=True))
                a = jnp.exp2(m - mn)
                p = jnp.exp2(s - mn)
                new.append((mn, a * l + jnp.sum(p, axis=0, keepdims=True),
                            a * acc + jnp.dot(vct, p.astype(BF16), preferred_element_type=F32)))
            return tuple(new)

        init = tuple((jnp.full((1, tq), NEG_INF, F32), jnp.zeros((1, tq), F32), jnp.zeros((LANES, tq), F32))
                     for _ in range(2))
        carry = lax.fori_loop(0, n_chunks, lambda j, c: update(c, k_chunk(j), vt_chunk(j)), init)
        (_, l_a, acc_a), (_, l_b, acc_b) = update(carry, kc_ref[0, 0], vct_ref[0, 0])
        o_ref[0] = _attn_finish(acc_a, l_a, acc_b, l_b, lamp_ref, g_ref, lam_init)


def _diff_attention(q, k, vt, kc, vct, lam_p, subln_g, lam_init, tq, tk):
    b, h, n, _ = q.shape
    c = kc.shape[2]
    assert n % tk == 0 and (n // tk) % 2 == 0 and c <= tk and tk % ATT_STRIP == 0 and c % ATT_STRIP == 0
    spec = lambda r, cols: pl.BlockSpec((1, 1, r, cols), lambda bi, hi, i: (bi, hi, 0, 0))
    return pl.pallas_call(
        functools.partial(_attn_kernel, tk=tk, lam_init=lam_init),
        out_shape=jax.ShapeDtypeStruct((b, n, h * LANES), BF16),
        grid=(b, h, n // tq),
        in_specs=[
            pl.BlockSpec((1, 1, tq, LANES), lambda bi, hi, i: (bi, hi, i, 0)),
            spec(n, LANES), spec(LANES, n), spec(c, LANES), spec(LANES, c),
            pl.BlockSpec(lam_p.shape, lambda bi, hi, i: (0, 0)),
            pl.BlockSpec((LANES, 1), lambda bi, hi, i: (0, 0)),
        ],
        out_specs=pl.BlockSpec((1, tq, LANES), lambda bi, hi, i: (bi, i, hi)),
        scratch_shapes=[
            *[pltpu.VMEM((tk, tq), F32)] * 4,
            *[pltpu.VMEM((tk, tq), BF16)] * 2,
            *[pltpu.VMEM((LANES, tq), F32)] * 2,
        ],
        compiler_params=_cparams("parallel", "parallel", "parallel"),
        name="diff_attention",
    )(q, k, vt, kc, vct, lam_p, subln_g.reshape(LANES, 1))


def _sgu_kernel(x_ref, sc_ref, sh_ref, w_ref, b_ref, ng_ref, nb_ref, ws_ref, bs_ref, t_ref):
    f = t_ref.shape[-1]
    cg = f // SGU_GROUPS
    tm = x_ref.shape[1]
    h = (x_ref[0] * (1.0 + sc_ref[0]) + sh_ref[0]).astype(BF16)
    z = jnp.dot(h, w_ref[...], preferred_element_type=F32) + b_ref[...]
    z = 0.5 * z * (1.0 + lax.erf(z * (2.0 ** -0.5)))
    u = z[:, :f]
    v = _layer_norm(z[:, f:], ng_ref[...], nb_ref[...]).astype(BF16)
    for c in range(tm // SGU_CHUNK):
        rows = slice(c * SGU_CHUNK, (c + 1) * SGU_CHUNK)
        for g in range(SGU_GROUPS):
            cols = slice(g * cg, (g + 1) * cg)
            vm = jnp.dot(ws_ref[g], v[rows, cols], preferred_element_type=F32) + bs_ref[:, g:g + 1]
            t_ref[0, rows, cols] = (u[rows, cols] * vm).astype(BF16)


def _sgu_mixer(x, sc, sh, w_in_bf, b_in, norm_g, norm_b, w_s_bf, b_s_t, tm):
    b, n, d = x.shape
    f2 = w_in_bf.shape[1]
    f = f2 // 2
    vec = pl.BlockSpec((1, 1, d), lambda bi, i: (bi, 0, 0))
    full2 = lambda a: pl.BlockSpec(a.shape, lambda bi, i: (0,) * a.ndim)
    b_in2, ng2, nb2 = b_in.reshape(1, f2), norm_g.reshape(1, f), norm_b.reshape(1, f)
    return pl.pallas_call(
        _sgu_kernel,
        out_shape=jax.ShapeDtypeStruct((b, n, f), BF16),
        grid=(b, n // tm),
        in_specs=[
            pl.BlockSpec((1, tm, d), lambda bi, i: (bi, i, 0)),
            vec, vec,
            full2(w_in_bf), full2(b_in2), full2(ng2), full2(nb2), full2(w_s_bf), full2(b_s_t),
        ],
        out_specs=pl.BlockSpec((1, tm, f), lambda bi, i: (bi, i, 0)),
        compiler_params=_cparams("parallel", "parallel"),
        name="sgu_mixer",
    )(x, sc, sh, w_in_bf, b_in2, ng2, nb2, w_s_bf, b_s_t)


def _post_kernel(pre_ref, w_ref, x_ref, gm_ref, lg_ref, lb_ref, scf_ref, shf_ref, x1_ref, hf_ref, *, alpha):
    y = jnp.dot(pre_ref[0], w_ref[...], preferred_element_type=F32)
    x1 = _layer_norm(alpha * x_ref[0] + gm_ref[0] * y, lg_ref[...], lb_ref[...])
    x1_ref[0] = x1
    hf_ref[0] = _pack_bf16_pairs(x1 * (1.0 + scf_ref[0]) + shf_ref[0])


def _post_mixer(pre, w_bf, x, gm, ln_g, ln_b, scf, shf, alpha, tm):
    b, n, d = x.shape
    kd = pre.shape[-1]
    vec = pl.BlockSpec((1, 1, d), lambda bi, i: (bi, 0, 0))
    row = pl.BlockSpec((1, d), lambda bi, i: (0, 0))
    tile = pl.BlockSpec((1, tm, d), lambda bi, i: (bi, i, 0))
    return pl.pallas_call(
        functools.partial(_post_kernel, alpha=alpha),
        out_shape=(jax.ShapeDtypeStruct((b, n, d), F32), jax.ShapeDtypeStruct((b, n, d // 2), I32)),
        grid=(b, n // tm),
        in_specs=[
            pl.BlockSpec((1, tm, kd), lambda bi, i: (bi, i, 0)),
            pl.BlockSpec((kd, d), lambda bi, i: (0, 0)),
            tile, vec, row, row, vec, vec,
        ],
        out_specs=(tile, pl.BlockSpec((1, tm, d // 2), lambda bi, i: (bi, i, 0))),
        compiler_params=_cparams("parallel", "parallel"),
        name="post_mixer",
    )(pre, w_bf, x, gm, ln_g.reshape(1, d), ln_b.reshape(1, d), scf, shf)


def _route_select(scores, choice):
    e, w = scores.shape
    ge = e // N_GROUPS
    g3 = choice.reshape(N_GROUPS, ge, w)
    ri = lax.broadcasted_iota(jnp.int32, g3.shape, 1).astype(F32)
    m1 = jnp.max(g3, axis=1, keepdims=True)
    first = jnp.min(jnp.where(g3 == m1, ri, float(ge)), axis=1, keepdims=True)
    m2 = jnp.max(jnp.where(ri == first, NEG_INF, g3), axis=1, keepdims=True)
    gs = m1 + m2

    gi = lax.broadcasted_iota(jnp.int32, gs.shape, 0).astype(F32)
    gsel = jnp.zeros(gs.shape, F32)
    cur = gs
    for _ in range(TOPK_GROUPS):
        m = jnp.max(cur, axis=0, keepdims=True)
        f = jnp.min(jnp.where(cur == m, gi, float(N_GROUPS)), axis=0, keepdims=True)
        hit = gi == f
        gsel = jnp.where(hit, 1.0, gsel)
        cur = jnp.where(hit, NEG_INF, cur)
    emask = jnp.broadcast_to(gsel, g3.shape).reshape(e, w)
    masked = jnp.where(emask > 0.5, choice, NEG_INF)

    ei = lax.broadcasted_iota(jnp.int32, (e, w), 0).astype(F32)
    onehot = jnp.zeros((e, w), F32)
    idxs, ws = [], []
    for _ in range(TOP_K):
        m = jnp.max(masked, axis=0, keepdims=True)
        f = jnp.min(jnp.where(masked == m, ei, float(e)), axis=0, keepdims=True)
        hit = ei == f
        idxs.append(f)
        ws.append(jnp.sum(jnp.where(hit, scores, 0.0), axis=0, keepdims=True))
        masked = jnp.where(hit, NEG_INF, masked)
        onehot = jnp.where(hit, 1.0, onehot)
    return idxs, ws, onehot


def _route_kernel(x_ref, sc_ref, sh_ref, wr_ref, rb_ref, idx_ref, w_ref, rank_ref, cnt_ref, carry_ref):
    i = pl.program_id(0)
    e = wr_ref.shape[0]
    tm = x_ref.shape[0]

    @pl.when(i == 0)
    def _():
        carry_ref[...] = jnp.zeros_like(carry_ref)

    h = x_ref[...] * (1.0 + sc_ref[0]) + sh_ref[0]
    logits = lax.dot_general(wr_ref[...], h, (((1,), (1,)), ((), ())),
                             precision=lax.Precision.HIGHEST, preferred_element_type=F32)
    scores = 1.0 / (1.0 + jnp.exp(-logits))
    choice = scores + rb_ref[...]

    slab = min(ROUTE_SLAB, tm)
    slabs = [slice(j * slab, (j + 1) * slab) for j in range(tm // slab)]
    picks = [_route_select(scores[:, sl], choice[:, sl]) for sl in slabs]
    onehot = jnp.concatenate([pk[2] for pk in picks], axis=1)

    r_i = lax.broadcasted_iota(jnp.int32, (tm, tm), 0)
    c_i = lax.broadcasted_iota(jnp.int32, (tm, tm), 1)
    upper = jnp.where(r_i < c_i, 1.0, 0.0).astype(BF16)
    rk = jnp.dot(onehot.astype(BF16), upper, preferred_element_type=F32) + carry_ref[...]
    carry_ref[...] += jnp.sum(onehot, axis=1, keepdims=True)

    ei = lax.broadcasted_iota(jnp.int32, (e, slab), 0).astype(F32)
    for sl, (idxs, ws, _) in zip(slabs, picks):
        wsum = functools.reduce(lambda x, y: x + y, ws)
        for k in range(TOP_K):
            idx_ref[k:k + 1, sl] = idxs[k].astype(jnp.int32)
            w_ref[k:k + 1, sl] = ws[k] / wsum * ROUTED_SCALE
            rank_ref[k:k + 1, sl] = jnp.sum(jnp.where(ei == idxs[k], rk[:, sl], 0.0), axis=0,
                                            keepdims=True).astype(jnp.int32)
    cnt_ref[...] = jnp.broadcast_to(carry_ref[...], cnt_ref.shape).astype(jnp.int32)


def _route(x1, scf, shf, wr_t, rbias, tm):
    b, n, d = x1.shape
    t = b * n
    e = wr_t.shape[0]
    per_b = n // tm
    vec = pl.BlockSpec((1, 1, d), lambda i: (i // per_b, 0, 0))
    out_t = pl.BlockSpec((TOP_K, tm), lambda i: (0, i))
    return pl.pallas_call(
        _route_kernel,
        out_shape=(jax.ShapeDtypeStruct((TOP_K, t), jnp.int32), jax.ShapeDtypeStruct((TOP_K, t), F32),
                   jax.ShapeDtypeStruct((TOP_K, t), jnp.int32), jax.ShapeDtypeStruct((e, LANES), jnp.int32)),
        grid=(t // tm,),
        in_specs=[
            pl.BlockSpec((tm, d), lambda i: (i, 0)),
            vec, vec,
            pl.BlockSpec((e, d), lambda i: (0, 0)),
            pl.BlockSpec((e, 1), lambda i: (0, 0)),
        ],
        out_specs=(out_t, out_t, out_t, pl.BlockSpec((e, LANES), lambda i: (0, 0))),
        scratch_shapes=[pltpu.VMEM((e, 1), F32)],
        compiler_params=_cparams("arbitrary"),
        name="route",
    )(x1.reshape(t, d), scf, shf, wr_t, rbias.reshape(e, 1))


def _gather_rows(table, idx):
    m = idx.shape[0]
    w = table.shape[1]
    workers = SC_CORES * SC_SUBCORES
    n_ch = m // (workers * SC_CHUNK)
    assert m % (workers * SC_CHUNK) == 0 and n_ch % 2 == 0
    mesh = plsc.VectorSubcoreMesh(core_axis_name="c", subcore_axis_name="s",
                                  num_cores=SC_CORES, num_subcores=SC_SUBCORES)

    @functools.partial(
        pl.kernel, mesh=mesh,
        out_type=jax.ShapeDtypeStruct((m, w), table.dtype),
        scratch_types=[
            pltpu.VMEM((n_ch, SC_CHUNK), I32),
            pltpu.VMEM((SC_CHUNK, w), table.dtype), pltpu.VMEM((SC_CHUNK, w), table.dtype),
            pltpu.SemaphoreType.DMA, pltpu.SemaphoreType.DMA, pltpu.SemaphoreType.DMA, pltpu.SemaphoreType.DMA,
        ],
        name="sc_gather_rows",
    )
    def gather(table_hbm, idx_hbm, out_hbm, idx_all, buf0, buf1, gsem0, gsem1, wsem0, wsem1):
        first = (lax.axis_index("s") * SC_CORES + lax.axis_index("c")) * n_ch
        bufs, gsem, wsem = (buf0, buf1), (gsem0, gsem1), (wsem0, wsem1)
        pltpu.sync_copy(idx_hbm.at[pl.ds(first, n_ch)], idx_all)

        def gather_copy(j, s):
            return pltpu.make_async_copy(table_hbm.at[idx_all.at[j]], bufs[s], gsem[s])

        def write_copy(j, s):
            rows = pl.ds(pl.multiple_of((first + j) * SC_CHUNK, SC_CHUNK), SC_CHUNK)
            return pltpu.make_async_copy(bufs[s], out_hbm.at[rows], wsem[s])

        gather_copy(0, 0).start()

        @pl.loop(0, n_ch, step=2)
        def _(jj):
            for s in range(2):
                j = jj + s

                @pl.when(j >= 1)
                def _():
                    write_copy(j - 1, 1 - s).wait()

                @pl.when(j + 1 < n_ch)
                def _():
                    gather_copy(j + 1, 1 - s).start()

                gather_copy(j, s).wait()
                write_copy(j, s).start()

        write_copy(n_ch - 1, 1).wait()

    return gather(table, idx.reshape(m // SC_CHUNK, SC_CHUNK))


def _scatter_rows(src, pos3, p):
    w = src.shape[1]
    n_chunks, k, ch = pos3.shape
    workers = SC_CORES * SC_SUBCORES
    per_w = n_chunks // workers
    assert ch == SC_CHUNK and n_chunks % workers == 0 and src.shape[0] == n_chunks * ch
    mesh = plsc.VectorSubcoreMesh(core_axis_name="c", subcore_axis_name="s",
                                  num_cores=SC_CORES, num_subcores=SC_SUBCORES)

    @functools.partial(
        pl.kernel, mesh=mesh,
        out_type=jax.ShapeDtypeStruct((p, w), src.dtype),
        scratch_types=[
            pltpu.VMEM((k, ch), I32),
            pltpu.VMEM((ch, w), src.dtype),
            pltpu.SemaphoreType.DMA,
        ],
        name="sc_scatter_rows",
    )
    def scatter(src_hbm, pos_hbm, out_hbm, idx_v, rows_v, sem):
        first = (lax.axis_index("s") * SC_CORES + lax.axis_index("c")) * per_w

        @pl.loop(0, per_w)
        def _(j):
            c = first + j
            pltpu.sync_copy(pos_hbm.at[c], idx_v)
            pltpu.sync_copy(src_hbm.at[pl.ds(pl.multiple_of(c * ch, ch), ch)], rows_v)
            copies = [pltpu.async_copy(rows_v, out_hbm.at[idx_v.at[kk]], sem) for kk in range(k)]
            for cp in copies:
                cp.wait()

    return scatter(src, pos3)


def _experts_kernel(ps_ref, nb_ref, cnt_ref, nt_ref, xs_hbm, wg_ref, wu_ref, wd_ref, y_hbm,
                    xbuf, ybuf, in_sem, out_sem, wg_bf, wu_bf, wd_bf):
    e = pl.program_id(0)
    nb, cnt, n_total = nb_ref[e], cnt_ref[e], nt_ref[0]
    g0 = ps_ref[e] // MOE_ROWS

    def in_copy(g):
        rows = pl.ds(pl.multiple_of(g * MOE_ROWS, MOE_ROWS), MOE_ROWS)
        return pltpu.make_async_copy(xs_hbm.at[rows], xbuf.at[g % MOE_SLOTS], in_sem.at[g % MOE_SLOTS])

    def out_copy(g):
        rows = pl.ds(pl.multiple_of(g * MOE_ROWS, MOE_ROWS), MOE_ROWS)
        return pltpu.make_async_copy(ybuf.at[g % MOE_SLOTS], y_hbm.at[rows], out_sem.at[g % MOE_SLOTS])

    @pl.when(e == 0)
    def _():
        for j in range(MOE_LOOKAHEAD):
            @pl.when(j < n_total)
            def _():
                in_copy(j).start()

    @pl.when(nb > 0)
    def _():
        wg_bf[...] = wg_ref[0, 0].astype(BF16)
        wu_bf[...] = wu_ref[0, 0].astype(BF16)
        wd_bf[...] = wd_ref[0, 0].astype(BF16)

        def process(b, width):
            g = g0 + b
            for u in range(width):
                @pl.when(g + u + MOE_LOOKAHEAD < n_total)
                def _():
                    in_copy(g + u + MOE_LOOKAHEAD).start()

            for u in range(width):
                in_copy(g + u).wait()

                @pl.when(g + u >= MOE_SLOTS)
                def _():
                    out_copy(g + u - MOE_SLOTS).wait()

            packed = jnp.concatenate([xbuf[(g + u) % MOE_SLOTS] for u in range(width)], axis=0)
            row = lax.broadcasted_iota(I32, (width * MOE_ROWS, 1), 0) + b * MOE_ROWS
            x_lo, x_hi = (v.astype(BF16) for v in _unpack_bf16_pairs(jnp.where(row < cnt, packed, 0)))
            half = x_lo.shape[1]

            def up(w_bf):
                return (jnp.dot(x_lo, w_bf[:half, :], preferred_element_type=F32)
                        + jnp.dot(x_hi, w_bf[half:, :], preferred_element_type=F32))

            hb = (_silu(up(wg_bf)) * up(wu_bf)).astype(BF16)
            y = _pack_bf16_pairs(jnp.dot(hb, wd_bf[...], preferred_element_type=F32))
            for u in range(width):
                ybuf[(g + u) % MOE_SLOTS] = y[u * MOE_ROWS:(u + 1) * MOE_ROWS]
                out_copy(g + u).start()

        def body(i, c):
            process(MOE_GROUP * i, MOE_GROUP)
            return c

        lax.fori_loop(0, nb // MOE_GROUP, body, 0)

        def tail(i, c):
            process(i, 1)
            return c

        lax.fori_loop(nb // MOE_GROUP * MOE_GROUP, nb, tail, 0)

    @pl.when(e == pl.num_programs(0) - 1)
    def _():
        for j in range(MOE_SLOTS):
            @pl.when(n_total - 1 - j >= 0)
            def _():
                out_copy(n_total - 1 - j).wait()


def _routed_experts(xs, wg, wu, wd, layer, pstarts, nblk, counts, n_total):
    p, dp = xs.shape
    _, e, d, f = wg.shape
    wspec = lambda r, c: pl.BlockSpec((1, 1, r, c), lambda i, ps, nb, cnt, nt: (layer, i, 0, 0))
    grid_spec = pltpu.PrefetchScalarGridSpec(
        num_scalar_prefetch=4,
        grid=(e,),
        in_specs=[pl.BlockSpec(memory_space=pl.ANY), wspec(d, f), wspec(d, f), wspec(f, d)],
        out_specs=pl.BlockSpec(memory_space=pl.ANY),
        scratch_shapes=[
            pltpu.VMEM((MOE_SLOTS, MOE_ROWS, dp), I32), pltpu.VMEM((MOE_SLOTS, MOE_ROWS, dp), I32),
            pltpu.SemaphoreType.DMA((MOE_SLOTS,)), pltpu.SemaphoreType.DMA((MOE_SLOTS,)),
            pltpu.VMEM((d, f), BF16), pltpu.VMEM((d, f), BF16), pltpu.VMEM((f, d), BF16),
        ],
    )
    return pl.pallas_call(
        _experts_kernel,
        out_shape=jax.ShapeDtypeStruct((p, dp), I32),
        grid_spec=grid_spec,
        compiler_params=_cparams("arbitrary"),
        name="routed_experts",
    )(pstarts, nblk, counts, n_total, xs, wg, wu, wd)


def _combine_kernel(yg_ref, w_ref, hf_ref, sg_ref, su_ref, sd_ref, x_ref, gf_ref, lg_ref, lb_ref, o_ref, *, alpha):
    w = w_ref[...]
    r_lo, r_hi = _unpack_bf16_pairs(yg_ref[0])
    r_lo, r_hi = w[:, 0:1] * r_lo, w[:, 0:1] * r_hi
    for k in range(1, TOP_K):
        y_lo, y_hi = _unpack_bf16_pairs(yg_ref[k])
        r_lo, r_hi = r_lo + w[:, k:k + 1] * y_lo, r_hi + w[:, k:k + 1] * y_hi
    routed = jnp.concatenate([r_lo, r_hi], axis=1)
    hf = jnp.concatenate(_unpack_bf16_pairs(hf_ref[...]), axis=1).astype(BF16)
    g = jnp.dot(hf, sg_ref[...], preferred_element_type=F32)
    u = jnp.dot(hf, su_ref[...], preferred_element_type=F32)
    shared = jnp.dot((_silu(g) * u).astype(BF16), sd_ref[...], preferred_element_type=F32)
    o_ref[...] = _layer_norm(alpha * x_ref[...] + gf_ref[0] * (routed + shared), lg_ref[...], lb_ref[...])


def _combine(yg, w_tk, hf, sg_bf, su_bf, sd_bf, x1, gf, ln_g, ln_b, alpha, tm, per_b):
    t, d = x1.shape
    f = sg_bf.shape[1]
    row = pl.BlockSpec((1, d), lambda i: (0, 0))
    tile = pl.BlockSpec((tm, d), lambda i: (i, 0))
    return pl.pallas_call(
        functools.partial(_combine_kernel, alpha=alpha),
        out_shape=jax.ShapeDtypeStruct((t, d), F32),
        grid=(t // tm,),
        in_specs=[
            pl.BlockSpec((TOP_K, tm, d // 2), lambda i: (0, i, 0)),
            pl.BlockSpec((tm, TOP_K), lambda i: (i, 0)),
            pl.BlockSpec((tm, d // 2), lambda i: (i, 0)),
            pl.BlockSpec((d, f), lambda i: (0, 0)),
            pl.BlockSpec((d, f), lambda i: (0, 0)),
            pl.BlockSpec((f, d), lambda i: (0, 0)),
            tile,
            pl.BlockSpec((1, 1, d), lambda i: (i // per_b, 0, 0)),
            row, row,
        ],
        out_specs=tile,
        compiler_params=_cparams("parallel"),
        name="moe_combine",
    )(yg, w_tk, hf, sg_bf, su_bf, sd_bf, x1, gf, ln_g.reshape(1, d), ln_b.reshape(1, d))


def _moe_layer(x1, hf, scf, shf, gf, router_w, router_bias, wg, wu, wd, layer, sg, su, sd, ln_g, ln_b, alpha, tm):
    b, n, d = x1.shape
    t = b * n
    e = router_w.shape[1]
    idx_t, w_t, rank_t, cnt = _route(x1, scf, shf, router_w.T, router_bias, tm)

    counts = cnt[:, 0]
    padded = (counts + MOE_ROWS - 1) // MOE_ROWS * MOE_ROWS
    pends = jnp.cumsum(padded)
    pstarts = pends - padded
    sel = idx_t[:, :, None] == jnp.arange(e, dtype=I32)
    pos_t = jnp.sum(jnp.where(sel, pstarts, 0), axis=-1) + rank_t
    p = t * TOP_K + e * MOE_ROWS
    pos3 = pos_t.reshape(TOP_K, t // SC_CHUNK, SC_CHUNK).transpose(1, 0, 2)

    hf2 = hf.reshape(t, d // 2)
    xs = _scatter_rows(hf2, pos3, p)
    yb = _routed_experts(xs, wg, wu, wd, layer, pstarts.astype(I32), (padded // MOE_ROWS).astype(I32), counts,
                         (pends[-1:] // MOE_ROWS).astype(I32))
    yg = _gather_rows(yb, pos_t.reshape(-1)).reshape(TOP_K, t, d // 2)
    out = _combine(yg, w_t.T, hf2, sg.astype(BF16), su.astype(BF16), sd.astype(BF16), x1.reshape(t, d), gf,
                   ln_g, ln_b, alpha, tm, n // tm)
    return out.reshape(b, n, d)


def _rope_tables(n):
    rows = n // GRID_W
    row_pos = jnp.repeat(jnp.arange(rows, dtype=F32), GRID_W)
    col_pos = jnp.tile(jnp.arange(GRID_W, dtype=F32), rows)
    half = LANES // 4
    lane = jnp.arange(LANES)
    in_blk = lane % half
    freq = ROPE_THETA ** (-(2.0 * (in_blk % (half // 2)).astype(F32)) / half)
    use_col = (lane // half) % 2 == 1
    pos = jnp.where(use_col[None, :], col_pos[:, None], row_pos[:, None])
    ang = pos * freq[None, :]
    lo = (in_blk < half // 2)[None, :]
    sin = jnp.sin(ang)
    return jnp.cos(ang), jnp.where(lo, -sin, 0.0), jnp.where(lo, 0.0, sin)


def kernel(x, c, ctx, c_ctx, w_mod, b_mod, ln_g, ln_b, attn_w_in, attn_w_out, attn_lambda, attn_subln_g,
           sgu_w_in, sgu_b_in, sgu_norm_g, sgu_norm_b, sgu_w_s, sgu_b_s, sgu_w_out,
           router_w, router_bias, exp_w_gate, exp_w_up, exp_w_down, sh_w_gate, sh_w_up, sh_w_down):
    b, n, d = x.shape
    depth = w_mod.shape[0]
    assert b <= 7 and d == ATT_HEADS * LANES and n % GRID_W == 0
    alpha = (2 * depth) ** 0.25
    head_dim = d // ATT_HEADS // 2
    tm = 512 if n % 512 == 0 else 256

    cs = jnp.zeros((8, d), F32).at[:b].set(c).at[b].set(c_ctx)
    mods = _modulation(cs, w_mod, b_mod)

    def mod_vec(i, j):
        return mods[i, :, j * d:(j + 1) * d].reshape(8, 1, d)

    for i in range(depth):
        sh_m, sc_m, g_m, sh_f, sc_f, g_f = (mod_vec(i, j) for j in range(6))
        if i % N_MIXERS == 0:
            a = i // N_MIXERS
            lam_init = 0.8 - 0.6 * math.exp(-0.3 * i)
            w_in_bf = attn_w_in[a].astype(BF16)
            cos, slo, shi = _rope_tables(n)
            q_scale = head_dim ** -0.5 * math.log2(math.e)
            q, k, v = _qkv_proj(x, sc_m, sh_m, w_in_bf, cos, slo, shi, q_scale, tm)
            kc, vc = _kv_ctx_proj(ctx, sc_m[b:b + 1], sh_m[b:b + 1], w_in_bf[:, d:])
            tk = min(1024, n // 2)
            pre = _diff_attention(q, k, v, kc, vc, attn_lambda[a], attn_subln_g[a], lam_init, min(1024, n), tk)
            w_out_bf = attn_w_out[a].astype(BF16)
        else:
            s = i // N_MIXERS
            pre = _sgu_mixer(x, sc_m, sh_m, sgu_w_in[s].astype(BF16), sgu_b_in[s], sgu_norm_g[s], sgu_norm_b[s],
                             sgu_w_s[s].astype(BF16), sgu_b_s[s].T, 256)
            w_out_bf = sgu_w_out[s].astype(BF16)
        x1, hf = _post_mixer(pre, w_out_bf, x, g_m, ln_g[i, 0], ln_b[i, 0], sc_f, sh_f, alpha, tm)
        x = _moe_layer(x1, hf, sc_f, sh_f, g_f, router_w[i], router_bias[i], exp_w_gate, exp_w_up, exp_w_down, i,
                       sh_w_gate[i], sh_w_up[i], sh_w_down[i], ln_g[i, 1], ln_b[i, 1], alpha, tm)
    return x
```

```python
import functools
import math

import jax
import jax.numpy as jnp
from jax import lax
from jax.experimental import pallas as pl
from jax.experimental.pallas import tpu as pltpu
from jax.experimental.pallas import tpu_sc as plsc

F32 = jnp.float32
BF16 = jnp.bfloat16
I32 = jnp.int32

GRID_W = 64
ATT_HEADS = 8
ROPE_THETA = 10000.0
SGU_CHUNK = 128
SGU_GROUPS = 8
TOP_K = 8
N_GROUPS = 8
TOPK_GROUPS = 4
ROUTED_SCALE = 2.5
LN_EPS = 1e-5
N_MIXERS = 2

LANES = 128
MOE_ROWS = 256
MOE_COMBINE_RANGES = 2
ROUTE_SLAB = 512
MOE_GROUP = 2
MOE_LOOKAHEAD = 4
MOE_SLOTS = MOE_LOOKAHEAD + MOE_GROUP
ATT_STRIP = 16
SC_CORES = 2
SC_SUBCORES = 16
SC_CHUNK = 64
VMEM_LIMIT = 56 * 1024 * 1024
NEG_INF = float("-inf")


def _cparams(*sem):
    return pltpu.CompilerParams(dimension_semantics=sem, vmem_limit_bytes=VMEM_LIMIT)


def _layer_norm(z, g, b):
    mu = jnp.mean(z, axis=-1, keepdims=True)
    zc = z - mu
    var = jnp.mean(zc * zc, axis=-1, keepdims=True)
    return zc * lax.rsqrt(var + LN_EPS) * g + b


def _silu(x):
    return x * (1.0 / (1.0 + jnp.exp(-x)))


_HIGH_HALF = -65536


def _pack_bf16_pairs(y):
    w = y.shape[1] // 2
    bits = lax.bitcast_convert_type(y.astype(BF16).astype(F32), I32)
    return lax.shift_right_logical(bits[:, :w], 16) | (bits[:, w:] & _HIGH_HALF)


def _unpack_bf16_pairs(p):
    return (lax.bitcast_convert_type(lax.shift_left(p, 16), F32),
            lax.bitcast_convert_type(p & _HIGH_HALF, F32))


def _mod_kernel(cs_ref, w_ref, b_ref, o_ref):
    s = _silu(cs_ref[...])
    o_ref[0] = jnp.dot(s, w_ref[0], precision=lax.Precision.HIGHEST,
                       preferred_element_type=F32) + b_ref[0]


def _modulation(cs, w_mod, b_mod):
    depth, d, n6 = w_mod.shape
    tn = n6 // 4
    return pl.pallas_call(
        _mod_kernel,
        out_shape=jax.ShapeDtypeStruct((depth, 8, n6), F32),
        grid=(depth, n6 // tn),
        in_specs=[
            pl.BlockSpec((8, d), lambda l, j: (0, 0)),
            pl.BlockSpec((1, d, tn), lambda l, j: (l, 0, j)),
            pl.BlockSpec((1, 1, tn), lambda l, j: (l, 0, j)),
        ],
        out_specs=pl.BlockSpec((1, 8, tn), lambda l, j: (l, 0, j)),
        compiler_params=_cparams("parallel", "parallel"),
        name="modulation",
    )(cs, w_mod, b_mod.reshape(depth, 1, n6))


def _rope(xh, cos, sin_lo, sin_hi):
    return xh * cos + pltpu.roll(xh, LANES - 16, 1) * sin_lo + pltpu.roll(xh, 16, 1) * sin_hi


def _qkv_kernel(x_ref, sc_ref, sh_ref, w_ref, cos_ref, slo_ref, shi_ref, q_ref, k_ref, v_ref, *, q_scale):
    d = x_ref.shape[-1]
    h = (x_ref[0] * (1.0 + sc_ref[0]) + sh_ref[0]).astype(BF16)
    cos, slo, shi = cos_ref[...], slo_ref[...], shi_ref[...]
    q = jnp.dot(h, w_ref[:, 0:d], preferred_element_type=F32)
    for hd in range(ATT_HEADS):
        q_ref[0, hd] = (_rope(q[:, hd * LANES:(hd + 1) * LANES], cos, slo, shi) * q_scale).astype(BF16)
    k = jnp.dot(h, w_ref[:, d:2 * d], preferred_element_type=F32)
    for hd in range(ATT_HEADS):
        k_ref[0, hd] = _rope(k[:, hd * LANES:(hd + 1) * LANES], cos, slo, shi).astype(BF16)
    v = jnp.dot(h, w_ref[:, 2 * d:3 * d], preferred_element_type=F32)
    for hd in range(ATT_HEADS):
        v_ref[0, hd] = v[:, hd * LANES:(hd + 1) * LANES].T.astype(BF16)


def _qkv_proj(x, sc, sh, w_bf, cos, slo, shi, q_scale, tn):
    b, n, d = x.shape
    hd_shape = jax.ShapeDtypeStruct((b, ATT_HEADS, n, LANES), BF16)
    vec = pl.BlockSpec((1, 1, d), lambda bi, i: (bi, 0, 0))
    tab = pl.BlockSpec((tn, LANES), lambda bi, i: (i, 0))
    out = pl.BlockSpec((1, ATT_HEADS, tn, LANES), lambda bi, i: (bi, 0, i, 0))
    out_t = pl.BlockSpec((1, ATT_HEADS, LANES, tn), lambda bi, i: (bi, 0, 0, i))
    return pl.pallas_call(
        functools.partial(_qkv_kernel, q_scale=q_scale),
        out_shape=(hd_shape, hd_shape, jax.ShapeDtypeStruct((b, ATT_HEADS, LANES, n), BF16)),
        grid=(b, n // tn),
        in_specs=[
            pl.BlockSpec((1, tn, d), lambda bi, i: (bi, i, 0)),
            vec, vec,
            pl.BlockSpec((d, 3 * d), lambda bi, i: (0, 0)),
            tab, tab, tab,
        ],
        out_specs=(out, out, out_t),
        compiler_params=_cparams("parallel", "parallel"),
        name="qkv_proj",
    )(x, sc, sh, w_bf, cos, slo, shi)


def _kv_ctx_kernel(x_ref, sc_ref, sh_ref, w_ref, k_ref, v_ref):
    d = x_ref.shape[-1]
    h = (x_ref[0] * (1.0 + sc_ref[0]) + sh_ref[0]).astype(BF16)
    k = jnp.dot(h, w_ref[:, 0:d], preferred_element_type=F32)
    v = jnp.dot(h, w_ref[:, d:2 * d], preferred_element_type=F32)
    for hd in range(ATT_HEADS):
        k_ref[0, hd] = k[:, hd * LANES:(hd + 1) * LANES].astype(BF16)
        v_ref[0, hd] = v[:, hd * LANES:(hd + 1) * LANES].T.astype(BF16)


def _kv_ctx_proj(ctx, sc, sh, w_kv_bf):
    b, c, d = ctx.shape
    hd_shape = jax.ShapeDtypeStruct((b, ATT_HEADS, c, LANES), BF16)
    vec = pl.BlockSpec((1, 1, d), lambda bi: (0, 0, 0))
    out = pl.BlockSpec((1, ATT_HEADS, c, LANES), lambda bi: (bi, 0, 0, 0))
    out_t = pl.BlockSpec((1, ATT_HEADS, LANES, c), lambda bi: (bi, 0, 0, 0))
    return pl.pallas_call(
        _kv_ctx_kernel,
        out_shape=(hd_shape, jax.ShapeDtypeStruct((b, ATT_HEADS, LANES, c), BF16)),
        grid=(b,),
        in_specs=[
            pl.BlockSpec((1, c, d), lambda bi: (bi, 0, 0)),
            vec, vec,
            pl.BlockSpec((d, 2 * d), lambda bi: (0, 0)),
        ],
        out_specs=(out, out_t),
        compiler_params=_cparams("parallel"),
        name="kv_ctx_proj",
    )(ctx, sc, sh, w_kv_bf)


def _attn_finish(acc_a, l_a, acc_b, l_b, lamp_ref, g_ref, lam_init):
    lp = lamp_ref[...]
    lam = (jnp.exp(jnp.sum(lp[0:1] * lp[1:2], axis=-1, keepdims=True))
           - jnp.exp(jnp.sum(lp[2:3] * lp[3:4], axis=-1, keepdims=True)) + lam_init)
    o = acc_a / l_a - lam * (acc_b / l_b)
    o = o * lax.rsqrt(jnp.mean(o * o, axis=0, keepdims=True) + LN_EPS) * g_ref[...] * (1.0 - lam_init)
    return o.T.astype(BF16)


def _attn_kernel(q_ref, k_ref, vt_ref, kc_ref, vct_ref, lamp_ref, g_ref, o_ref,
                 s00, s01, s10, s11, p0, p1, acc0, acc1, *, tk, lam_init):
    q = q_ref[0, 0]
    tq = q.shape[0]
    half = LANES // 2
    lane = lax.broadcasted_iota(jnp.int32, q.shape, 1)
    zero = jnp.zeros_like(q)
    qs = (jnp.where(lane < half, q, zero), jnp.where(lane >= half, q, zero))
    n_chunks = k_ref.shape[2] // tk
    s_scr = ((s00, s01), (s10, s11))
    p_scr, acc_scr = (p0, p1), (acc0, acc1)
    nt = (((1,), (1,)), ((), ()))

    def scores(slot, kc):
        width = kc.shape[0]
        for mp in range(2):
            s_scr[slot][mp][0:width, :] = lax.dot_general(kc, qs[mp], nt, preferred_element_type=F32)

    def absorb(slot, vct, shift, sums):
        width = vct.shape[1]
        sums = list(sums)
        for mp in range(2):
            part = sums[mp]
            for r in range(width // ATT_STRIP):
                lo = r * ATT_STRIP
                tiles = [jnp.exp2(s_scr[slot][mp][lo + 8 * u:lo + 8 * (u + 1), :] - shift[mp])
                         for u in range(ATT_STRIP // 8)]
                p_scr[mp][lo:lo + ATT_STRIP, :] = jnp.concatenate(tiles, axis=0).astype(BF16)
                part = part + functools.reduce(lambda x, y: x + y, tiles)
            sums[mp] = part
            acc_scr[mp][...] += jnp.dot(vct, p_scr[mp][0:width, :], preferred_element_type=F32)
        return tuple(sums)

    def k_chunk(j):
        return k_ref[0, 0, pl.ds(pl.multiple_of(j * tk, tk), tk), :]

    def vt_chunk(j):
        return vt_ref[0, 0, :, pl.ds(pl.multiple_of(j * tk, tk), tk)]

    scores(0, k_chunk(0))
    shift = tuple(jnp.broadcast_to(jnp.max(s_scr[0][mp][...], axis=0, keepdims=True), (8, tq)) for mp in range(2))
    for mp in range(2):
        acc_scr[mp][...] = jnp.zeros((LANES, tq), F32)
    sums = (jnp.zeros((8, tq), F32), jnp.zeros((8, tq), F32))

    def pair(jj, sums):
        j0 = 2 * jj
        scores(1, k_chunk(j0 + 1))
        sums = absorb(0, vt_chunk(j0), shift, sums)
        scores(0, k_chunk(j0 + 2))
        return absorb(1, vt_chunk(j0 + 1), shift, sums)

    sums = lax.fori_loop(0, n_chunks // 2 - 1, pair, sums)
    scores(1, k_chunk(n_chunks - 1))
    sums = absorb(0, vt_chunk(n_chunks - 2), shift, sums)
    scores(0, kc_ref[0, 0])
    sums = absorb(1, vt_chunk(n_chunks - 1), shift, sums)
    sums = absorb(0, vct_ref[0, 0], shift, sums)

    tot = [jnp.sum(sums[mp], axis=0, keepdims=True) for mp in range(2)]
    bad = sum(jnp.sum(jnp.where(jnp.isfinite(x), 0.0, 1.0)) for x in (tot[0], tot[1], acc0[...], acc1[...]))

    @pl.when(bad == 0.0)
    def _():
        o_ref[0] = _attn_finish(acc0[...], tot[0], acc1[...], tot[1], lamp_ref, g_ref, lam_init)

    @pl.when(bad != 0.0)
    def _():
        def update(carry, kc, vct):
            new = []
            for mp in range(2):
                m, l, acc = carry[mp]
                s = lax.dot_general(kc, qs[mp], nt, preferred_element_type=F32)
                mn = jnp.maximum(m, jnp.max(s, axis=0, keepdims=True))
                a = jnp.exp2(m - mn)
                p = jnp.exp2(s - mn)
                new.append((mn, a * l + jnp.sum(p, axis=0, keepdims=True),
                            a * acc + jnp.dot(vct, p.astype(BF16), preferred_element_type=F32)))
            return tuple(new)

        init = tuple((jnp.full((1, tq), NEG_INF, F32), jnp.zeros((1, tq), F32), jnp.zeros((LANES, tq), F32))
                     for _ in range(2))
        carry = lax.fori_loop(0, n_chunks, lambda j, c: update(c, k_chunk(j), vt_chunk(j)), init)
        (_, l_a, acc_a), (_, l_b, acc_b) = update(carry, kc_ref[0, 0], vct_ref[0, 0])
        o_ref[0] = _attn_finish(acc_a, l_a, acc_b, l_b, lamp_ref, g_ref, lam_init)


def _diff_attention(q, k, vt, kc, vct, lam_p, subln_g, lam_init, tq, tk):
    b, h, n, _ = q.shape
    c = kc.shape[2]
    assert n % tk == 0 and (n // tk) % 2 == 0 and c <= tk and tk % ATT_STRIP == 0 and c % ATT_STRIP == 0
    spec = lambda r, cols: pl.BlockSpec((1, 1, r, cols), lambda bi, hi, i: (bi, hi, 0, 0))
    return pl.pallas_call(
        functools.partial(_attn_kernel, tk=tk, lam_init=lam_init),
        out_shape=jax.ShapeDtypeStruct((b, n, h * LANES), BF16),
        grid=(b, h, n // tq),
        in_specs=[
            pl.BlockSpec((1, 1, tq, LANES), lambda bi, hi, i: (bi, hi, i, 0)),
            spec(n, LANES), spec(LANES, n), spec(c, LANES), spec(LANES, c),
            pl.BlockSpec(lam_p.shape, lambda bi, hi, i: (0, 0)),
            pl.BlockSpec((LANES, 1), lambda bi, hi, i: (0, 0)),
        ],
        out_specs=pl.BlockSpec((1, tq, LANES), lambda bi, hi, i: (bi, i, hi)),
        scratch_shapes=[
            *[pltpu.VMEM((tk, tq), F32)] * 4,
            *[pltpu.VMEM((tk, tq), BF16)] * 2,
            *[pltpu.VMEM((LANES, tq), F32)] * 2,
        ],
        compiler_params=_cparams("parallel", "parallel", "parallel"),
        name="diff_attention",
    )(q, k, vt, kc, vct, lam_p, subln_g.reshape(LANES, 1))


def _sgu_kernel(x_ref, sc_ref, sh_ref, w_ref, b_ref, ng_ref, nb_ref, ws_ref, bs_ref, t_ref):
    f = t_ref.shape[-1]
    cg = f // SGU_GROUPS
    tm = x_ref.shape[1]
    h = (x_ref[0] * (1.0 + sc_ref[0]) + sh_ref[0]).astype(BF16)
    z = jnp.dot(h, w_ref[...], preferred_element_type=F32) + b_ref[...]
    z = 0.5 * z * (1.0 + lax.erf(z * (2.0 ** -0.5)))
    u = z[:, :f]
    v = _layer_norm(z[:, f:], ng_ref[...], nb_ref[...]).astype(BF16)
    for c in range(tm // SGU_CHUNK):
        rows = slice(c * SGU_CHUNK, (c + 1) * SGU_CHUNK)
        for g in range(SGU_GROUPS):
            cols = slice(g * cg, (g + 1) * cg)
            vm = jnp.dot(ws_ref[g], v[rows, cols], preferred_element_type=F32) + bs_ref[:, g:g + 1]
            t_ref[0, rows, cols] = (u[rows, cols] * vm).astype(BF16)


def _sgu_mixer(x, sc, sh, w_in_bf, b_in, norm_g, norm_b, w_s_bf, b_s_t, tm):
    b, n, d = x.shape
    f2 = w_in_bf.shape[1]
    f = f2 // 2
    vec = pl.BlockSpec((1, 1, d), lambda bi, i: (bi, 0, 0))
    full2 = lambda a: pl.BlockSpec(a.shape, lambda bi, i: (0,) * a.ndim)
    b_in2, ng2, nb2 = b_in.reshape(1, f2), norm_g.reshape(1, f), norm_b.reshape(1, f)
    return pl.pallas_call(
        _sgu_kernel,
        out_shape=jax.ShapeDtypeStruct((b, n, f), BF16),
        grid=(b, n // tm),
        in_specs=[
            pl.BlockSpec((1, tm, d), lambda bi, i: (bi, i, 0)),
            vec, vec,
            full2(w_in_bf), full2(b_in2), full2(ng2), full2(nb2), full2(w_s_bf), full2(b_s_t),
        ],
        out_specs=pl.BlockSpec((1, tm, f), lambda bi, i: (bi, i, 0)),
        compiler_params=_cparams("parallel", "parallel"),
        name="sgu_mixer",
    )(x, sc, sh, w_in_bf, b_in2, ng2, nb2, w_s_bf, b_s_t)


def _post_kernel(pre_ref, w_ref, x_ref, gm_ref, lg_ref, lb_ref, scf_ref, shf_ref, x1_ref, hf_ref, *, alpha):
    y = jnp.dot(pre_ref[0], w_ref[...], preferred_element_type=F32)
    x1 = _layer_norm(alpha * x_ref[0] + gm_ref[0] * y, lg_ref[...], lb_ref[...])
    x1_ref[0] = x1
    hf_ref[0] = _pack_bf16_pairs(x1 * (1.0 + scf_ref[0]) + shf_ref[0])


def _post_mixer(pre, w_bf, x, gm, ln_g, ln_b, scf, shf, alpha, tm):
    b, n, d = x.shape
    kd = pre.shape[-1]
    vec = pl.BlockSpec((1, 1, d), lambda bi, i: (bi, 0, 0))
    row = pl.BlockSpec((1, d), lambda bi, i: (0, 0))
    tile = pl.BlockSpec((1, tm, d), lambda bi, i: (bi, i, 0))
    return pl.pallas_call(
        functools.partial(_post_kernel, alpha=alpha),
        out_shape=(jax.ShapeDtypeStruct((b, n, d), F32), jax.ShapeDtypeStruct((b, n, d // 2), I32)),
        grid=(b, n // tm),
        in_specs=[
            pl.BlockSpec((1, tm, kd), lambda bi, i: (bi, i, 0)),
            pl.BlockSpec((kd, d), lambda bi, i: (0, 0)),
            tile, vec, row, row, vec, vec,
        ],
        out_specs=(tile, pl.BlockSpec((1, tm, d // 2), lambda bi, i: (bi, i, 0))),
        compiler_params=_cparams("parallel", "parallel"),
        name="post_mixer",
    )(pre, w_bf, x, gm, ln_g.reshape(1, d), ln_b.reshape(1, d), scf, shf)


def _route_select(scores, choice):
    e, w = scores.shape
    ge = e // N_GROUPS
    g3 = choice.reshape(N_GROUPS, ge, w)
    ri = lax.broadcasted_iota(jnp.int32, g3.shape, 1).astype(F32)
    m1 = jnp.max(g3, axis=1, keepdims=True)
    first = jnp.min(jnp.where(g3 == m1, ri, float(ge)), axis=1, keepdims=True)
    m2 = jnp.max(jnp.where(ri == first, NEG_INF, g3), axis=1, keepdims=True)
    gs = m1 + m2

    gi = lax.broadcasted_iota(jnp.int32, gs.shape, 0).astype(F32)
    gsel = jnp.zeros(gs.shape, F32)
    cur = gs
    for _ in range(TOPK_GROUPS):
        m = jnp.max(cur, axis=0, keepdims=True)
        f = jnp.min(jnp.where(cur == m, gi, float(N_GROUPS)), axis=0, keepdims=True)
        hit = gi == f
        gsel = jnp.where(hit, 1.0, gsel)
        cur = jnp.where(hit, NEG_INF, cur)
    emask = jnp.broadcast_to(gsel, g3.shape).reshape(e, w)
    masked = jnp.where(emask > 0.5, choice, NEG_INF)

    ei = lax.broadcasted_iota(jnp.int32, (e, w), 0).astype(F32)
    onehot = jnp.zeros((e, w), F32)
    idxs, ws = [], []
    for _ in range(TOP_K):
        m = jnp.max(masked, axis=0, keepdims=True)
        f = jnp.min(jnp.where(masked == m, ei, float(e)), axis=0, keepdims=True)
        hit = ei == f
        idxs.append(f)
        ws.append(jnp.sum(jnp.where(hit, scores, 0.0), axis=0, keepdims=True))
        masked = jnp.where(hit, NEG_INF, masked)
        onehot = jnp.where(hit, 1.0, onehot)
    return idxs, ws, onehot


def _route_kernel(x_ref, sc_ref, sh_ref, wr_ref, rb_ref, idx_ref, w_ref, rank_ref, cnt_ref, carry_ref):
    i = pl.program_id(0)
    e = wr_ref.shape[0]
    tm = x_ref.shape[0]

    @pl.when(i == 0)
    def _():
        carry_ref[...] = jnp.zeros_like(carry_ref)

    h = x_ref[...] * (1.0 + sc_ref[0]) + sh_ref[0]
    logits = lax.dot_general(wr_ref[...], h, (((1,), (1,)), ((), ())),
                             precision=lax.Precision.HIGHEST, preferred_element_type=F32)
    scores = 1.0 / (1.0 + jnp.exp(-logits))
    choice = scores + rb_ref[...]

    slab = min(ROUTE_SLAB, tm)
    slabs = [slice(j * slab, (j + 1) * slab) for j in range(tm // slab)]
    picks = [_route_select(scores[:, sl], choice[:, sl]) for sl in slabs]
    onehot = jnp.concatenate([pk[2] for pk in picks], axis=1)

    r_i = lax.broadcasted_iota(jnp.int32, (tm, tm), 0)
    c_i = lax.broadcasted_iota(jnp.int32, (tm, tm), 1)
    upper = jnp.where(r_i < c_i, 1.0, 0.0).astype(BF16)
    rk = jnp.dot(onehot.astype(BF16), upper, preferred_element_type=F32) + carry_ref[...]
    carry_ref[...] += jnp.sum(onehot, axis=1, keepdims=True)

    ei = lax.broadcasted_iota(jnp.int32, (e, slab), 0).astype(F32)
    for sl, (idxs, ws, _) in zip(slabs, picks):
        wsum = functools.reduce(lambda x, y: x + y, ws)
        for k in range(TOP_K):
            idx_ref[k:k + 1, sl] = idxs[k].astype(jnp.int32)
            w_ref[k:k + 1, sl] = ws[k] / wsum * ROUTED_SCALE
            rank_ref[k:k + 1, sl] = jnp.sum(jnp.where(ei == idxs[k], rk[:, sl], 0.0), axis=0,
                                            keepdims=True).astype(jnp.int32)
    cnt_ref[...] = jnp.broadcast_to(carry_ref[...], cnt_ref.shape).astype(jnp.int32)


def _route(x1, scf, shf, wr_t, rbias, tm):
    b, n, d = x1.shape
    t = b * n
    e = wr_t.shape[0]
    per_b = n // tm
    vec = pl.BlockSpec((1, 1, d), lambda i: (i // per_b, 0, 0))
    out_t = pl.BlockSpec((TOP_K, tm), lambda i: (0, i))
    return pl.pallas_call(
        _route_kernel,
        out_shape=(jax.ShapeDtypeStruct((TOP_K, t), jnp.int32), jax.ShapeDtypeStruct((TOP_K, t), F32),
                   jax.ShapeDtypeStruct((TOP_K, t), jnp.int32), jax.ShapeDtypeStruct((e, LANES), jnp.int32)),
        grid=(t // tm,),
        in_specs=[
            pl.BlockSpec((tm, d), lambda i: (i, 0)),
            vec, vec,
            pl.BlockSpec((e, d), lambda i: (0, 0)),
            pl.BlockSpec((e, 1), lambda i: (0, 0)),
        ],
        out_specs=(out_t, out_t, out_t, pl.BlockSpec((e, LANES), lambda i: (0, 0))),
        scratch_shapes=[pltpu.VMEM((e, 1), F32)],
        compiler_params=_cparams("arbitrary"),
        name="route",
    )(x1.reshape(t, d), scf, shf, wr_t, rbias.reshape(e, 1))


def _gather_rows(table, idx):
    m = idx.shape[0]
    w = table.shape[1]
    workers = SC_CORES * SC_SUBCORES
    n_ch = m // (workers * SC_CHUNK)
    assert m % (workers * SC_CHUNK) == 0 and n_ch % 2 == 0
    mesh = plsc.VectorSubcoreMesh(core_axis_name="c", subcore_axis_name="s",
                                  num_cores=SC_CORES, num_subcores=SC_SUBCORES)

    @functools.partial(
        pl.kernel, mesh=mesh,
        out_type=jax.ShapeDtypeStruct((m, w), table.dtype),
        scratch_types=[
            pltpu.VMEM((n_ch, SC_CHUNK), I32),
            pltpu.VMEM((SC_CHUNK, w), table.dtype), pltpu.VMEM((SC_CHUNK, w), table.dtype),
            pltpu.SemaphoreType.DMA, pltpu.SemaphoreType.DMA, pltpu.SemaphoreType.DMA, pltpu.SemaphoreType.DMA,
        ],
        name="sc_gather_rows",
    )
    def gather(table_hbm, idx_hbm, out_hbm, idx_all, buf0, buf1, gsem0, gsem1, wsem0, wsem1):
        first = (lax.axis_index("s") * SC_CORES + lax.axis_index("c")) * n_ch
        bufs, gsem, wsem = (buf0, buf1), (gsem0, gsem1), (wsem0, wsem1)
        pltpu.sync_copy(idx_hbm.at[pl.ds(first, n_ch)], idx_all)

        def gather_copy(j, s):
            return pltpu.make_async_copy(table_hbm.at[idx_all.at[j]], bufs[s], gsem[s])

        def write_copy(j, s):
            rows = pl.ds(pl.multiple_of((first + j) * SC_CHUNK, SC_CHUNK), SC_CHUNK)
            return pltpu.make_async_copy(bufs[s], out_hbm.at[rows], wsem[s])

        gather_copy(0, 0).start()

        @pl.loop(0, n_ch, step=2)
        def _(jj):
            for s in range(2):
                j = jj + s

                @pl.when(j >= 1)
                def _():
                    write_copy(j - 1, 1 - s).wait()

                @pl.when(j + 1 < n_ch)
                def _():
                    gather_copy(j + 1, 1 - s).start()

                gather_copy(j, s).wait()
                write_copy(j, s).start()

        write_copy(n_ch - 1, 1).wait()

    return gather(table, idx.reshape(m // SC_CHUNK, SC_CHUNK))


def _scatter_rows(src, pos3, p):
    w = src.shape[1]
    n_chunks, k, ch = pos3.shape
    workers = SC_CORES * SC_SUBCORES
    per_w = n_chunks // workers
    assert ch == SC_CHUNK and n_chunks % workers == 0 and src.shape[0] == n_chunks * ch
    mesh = plsc.VectorSubcoreMesh(core_axis_name="c", subcore_axis_name="s",
                                  num_cores=SC_CORES, num_subcores=SC_SUBCORES)

    @functools.partial(
        pl.kernel, mesh=mesh,
        out_type=jax.ShapeDtypeStruct((p, w), src.dtype),
        scratch_types=[
            pltpu.VMEM((k, ch), I32),
            pltpu.VMEM((ch, w), src.dtype),
            pltpu.SemaphoreType.DMA,
        ],
        name="sc_scatter_rows",
    )
    def scatter(src_hbm, pos_hbm, out_hbm, idx_v, rows_v, sem):
        first = (lax.axis_index("s") * SC_CORES + lax.axis_index("c")) * per_w

        @pl.loop(0, per_w)
        def _(j):
            c = first + j
            pltpu.sync_copy(pos_hbm.at[c], idx_v)
            pltpu.sync_copy(src_hbm.at[pl.ds(pl.multiple_of(c * ch, ch), ch)], rows_v)
            copies = [pltpu.async_copy(rows_v, out_hbm.at[idx_v.at[kk]], sem) for kk in range(k)]
            for cp in copies:
                cp.wait()

    return scatter(src, pos3)


def _experts_kernel(ps_ref, nb_ref, cnt_ref, nt_ref, xs_hbm, wg_ref, wu_ref, wd_ref, y_hbm,
                    xbuf, ybuf, in_sem, out_sem, wg_bf, wu_bf, wd_bf):
    e = pl.program_id(0)
    nb, cnt, n_total = nb_ref[e], cnt_ref[e], nt_ref[0]
    g0 = ps_ref[e] // MOE_ROWS

    def in_copy(g):
        rows = pl.ds(pl.multiple_of(g * MOE_ROWS, MOE_ROWS), MOE_ROWS)
        return pltpu.make_async_copy(xs_hbm.at[rows], xbuf.at[g % MOE_SLOTS], in_sem.at[g % MOE_SLOTS])

    def out_copy(g):
        rows = pl.ds(pl.multiple_of(g * MOE_ROWS, MOE_ROWS), MOE_ROWS)
        return pltpu.make_async_copy(ybuf.at[g % MOE_SLOTS], y_hbm.at[rows], out_sem.at[g % MOE_SLOTS])

    @pl.when(e == 0)
    def _():
        for j in range(MOE_LOOKAHEAD):
            @pl.when(j < n_total)
            def _():
                in_copy(j).start()

    @pl.when(nb > 0)
    def _():
        wg_bf[...] = wg_ref[0, 0].astype(BF16)
        wu_bf[...] = wu_ref[0, 0].astype(BF16)
        wd_bf[...] = wd_ref[0, 0].astype(BF16)

        def process(b, width):
            g = g0 + b
            for u in range(width):
                @pl.when(g + u + MOE_LOOKAHEAD < n_total)
                def _():
                    in_copy(g + u + MOE_LOOKAHEAD).start()

            for u in range(width):
                in_copy(g + u).wait()

                @pl.when(g + u >= MOE_SLOTS)
                def _():
                    out_copy(g + u - MOE_SLOTS).wait()

            packed = jnp.concatenate([xbuf[(g + u) % MOE_SLOTS] for u in range(width)], axis=0)
            row = lax.broadcasted_iota(I32, (width * MOE_ROWS, 1), 0) + b * MOE_ROWS
            x_lo, x_hi = (v.astype(BF16) for v in _unpack_bf16_pairs(jnp.where(row < cnt, packed, 0)))
            half = x_lo.shape[1]

            def up(w_bf):
                return (jnp.dot(x_lo, w_bf[:half, :], preferred_element_type=F32)
                        + jnp.dot(x_hi, w_bf[half:, :], preferred_element_type=F32))

            hb = (_silu(up(wg_bf)) * up(wu_bf)).astype(BF16)
            y = _pack_bf16_pairs(jnp.dot(hb, wd_bf[...], preferred_element_type=F32))
            for u in range(width):
                ybuf[(g + u) % MOE_SLOTS] = y[u * MOE_ROWS:(u + 1) * MOE_ROWS]
                out_copy(g + u).start()

        def body(i, c):
            process(MOE_GROUP * i, MOE_GROUP)
            return c

        lax.fori_loop(0, nb // MOE_GROUP, body, 0)

        def tail(i, c):
            process(i, 1)
            return c

        lax.fori_loop(nb // MOE_GROUP * MOE_GROUP, nb, tail, 0)

    @pl.when(e == pl.num_programs(0) - 1)
    def _():
        for j in range(MOE_SLOTS):
            @pl.when(n_total - 1 - j >= 0)
            def _():
                out_copy(n_total - 1 - j).wait()


def _routed_experts(xs, wg, wu, wd, layer, pstarts, nblk, counts, n_total):
    p, dp = xs.shape
    _, e, d, f = wg.shape
    wspec = lambda r, c: pl.BlockSpec((1, 1, r, c), lambda i, ps, nb, cnt, nt: (layer, i, 0, 0))
    grid_spec = pltpu.PrefetchScalarGridSpec(
        num_scalar_prefetch=4,
        grid=(e,),
        in_specs=[pl.BlockSpec(memory_space=pl.ANY), wspec(d, f), wspec(d, f), wspec(f, d)],
        out_specs=pl.BlockSpec(memory_space=pl.ANY),
        scratch_shapes=[
            pltpu.VMEM((MOE_SLOTS, MOE_ROWS, dp), I32), pltpu.VMEM((MOE_SLOTS, MOE_ROWS, dp), I32),
            pltpu.SemaphoreType.DMA((MOE_SLOTS,)), pltpu.SemaphoreType.DMA((MOE_SLOTS,)),
            pltpu.VMEM((d, f), BF16), pltpu.VMEM((d, f), BF16), pltpu.VMEM((f, d), BF16),
        ],
    )
    return pl.pallas_call(
        _experts_kernel,
        out_shape=jax.ShapeDtypeStruct((p, dp), I32),
        grid_spec=grid_spec,
        compiler_params=_cparams("arbitrary"),
        name="routed_experts",
    )(pstarts, nblk, counts, n_total, xs, wg, wu, wd)


def _combine_kernel(yg_ref, w_ref, hf_ref, sg_ref, su_ref, sd_ref, x_ref, gf_ref, lg_ref, lb_ref, prev_ref, o_ref,
                    *, alpha):
    del prev_ref
    w = w_ref[...]
    r_lo, r_hi = _unpack_bf16_pairs(yg_ref[0])
    r_lo, r_hi = w[:, 0:1] * r_lo, w[:, 0:1] * r_hi
    for k in range(1, TOP_K):
        y_lo, y_hi = _unpack_bf16_pairs(yg_ref[k])
        r_lo, r_hi = r_lo + w[:, k:k + 1] * y_lo, r_hi + w[:, k:k + 1] * y_hi
    routed = jnp.concatenate([r_lo, r_hi], axis=1)
    hf = jnp.concatenate(_unpack_bf16_pairs(hf_ref[...]), axis=1).astype(BF16)
    g = jnp.dot(hf, sg_ref[...], preferred_element_type=F32)
    u = jnp.dot(hf, su_ref[...], preferred_element_type=F32)
    shared = jnp.dot((_silu(g) * u).astype(BF16), sd_ref[...], preferred_element_type=F32)
    o_ref[...] = _layer_norm(alpha * x_ref[...] + gf_ref[0] * (routed + shared), lg_ref[...], lb_ref[...])


def _combine(yg, w_tk, hf, sg_bf, su_bf, sd_bf, x1, gf, ln_g, ln_b, prev, alpha, tm, per_b, first):
    t, d = x1.shape
    f = sg_bf.shape[1]
    row = pl.BlockSpec((1, d), lambda i: (0, 0))
    tile = pl.BlockSpec((tm, d), lambda i: (first + i, 0))
    return pl.pallas_call(
        functools.partial(_combine_kernel, alpha=alpha),
        out_shape=jax.ShapeDtypeStruct((t, d), F32),
        grid=(yg.shape[1] // tm,),
        in_specs=[
            pl.BlockSpec((TOP_K, tm, d // 2), lambda i: (0, i, 0)),
            pl.BlockSpec((tm, TOP_K), lambda i: (first + i, 0)),
            pl.BlockSpec((tm, d // 2), lambda i: (first + i, 0)),
            pl.BlockSpec((d, f), lambda i: (0, 0)),
            pl.BlockSpec((d, f), lambda i: (0, 0)),
            pl.BlockSpec((f, d), lambda i: (0, 0)),
            tile,
            pl.BlockSpec((1, 1, d), lambda i: ((first + i) // per_b, 0, 0)),
            row, row,
            pl.BlockSpec(memory_space=pl.ANY),
        ],
        out_specs=tile,
        input_output_aliases={} if prev is None else {10: 0},
        compiler_params=_cparams("parallel"),
        name="moe_combine",
    )(yg, w_tk, hf, sg_bf, su_bf, sd_bf, x1, gf, ln_g.reshape(1, d), ln_b.reshape(1, d),
      gf if prev is None else prev)


def _moe_layer(x1, hf, scf, shf, gf, router_w, router_bias, wg, wu, wd, layer, sg, su, sd, ln_g, ln_b, alpha, tm):
    b, n, d = x1.shape
    t = b * n
    e = router_w.shape[1]
    idx_t, w_t, rank_t, cnt = _route(x1, scf, shf, router_w.T, router_bias, tm)

    counts = cnt[:, 0]
    padded = (counts + MOE_ROWS - 1) // MOE_ROWS * MOE_ROWS
    pends = jnp.cumsum(padded)
    pstarts = pends - padded
    sel = idx_t[:, :, None] == jnp.arange(e, dtype=I32)
    pos_t = jnp.sum(jnp.where(sel, pstarts, 0), axis=-1) + rank_t
    p = t * TOP_K + e * MOE_ROWS
    pos3 = pos_t.reshape(TOP_K, t // SC_CHUNK, SC_CHUNK).transpose(1, 0, 2)

    hf2 = hf.reshape(t, d // 2)
    xs = _scatter_rows(hf2, pos3, p)
    yb = _routed_experts(xs, wg, wu, wd, layer, pstarts.astype(I32), (padded // MOE_ROWS).astype(I32), counts,
                         (pends[-1:] // MOE_ROWS).astype(I32))
    x1f, w_tk = x1.reshape(t, d), w_t.T
    shared_w = (sg.astype(BF16), su.astype(BF16), sd.astype(BF16))
    th = t // MOE_COMBINE_RANGES
    out = None
    for r in range(MOE_COMBINE_RANGES):
        yg = _gather_rows(yb, pos_t[:, r * th:(r + 1) * th].reshape(-1)).reshape(TOP_K, th, d // 2)
        out = _combine(yg, w_tk, hf2, *shared_w, x1f, gf, ln_g, ln_b, out, alpha, tm, n // tm, r * th // tm)
    return out.reshape(b, n, d)


def _rope_tables(n):
    rows = n // GRID_W
    row_pos = jnp.repeat(jnp.arange(rows, dtype=F32), GRID_W)
    col_pos = jnp.tile(jnp.arange(GRID_W, dtype=F32), rows)
    half = LANES // 4
    lane = jnp.arange(LANES)
    in_blk = lane % half
    freq = ROPE_THETA ** (-(2.0 * (in_blk % (half // 2)).astype(F32)) / half)
    use_col = (lane // half) % 2 == 1
    pos = jnp.where(use_col[None, :], col_pos[:, None], row_pos[:, None])
    ang = pos * freq[None, :]
    lo = (in_blk < half // 2)[None, :]
    sin = jnp.sin(ang)
    return jnp.cos(ang), jnp.where(lo, -sin, 0.0), jnp.where(lo, 0.0, sin)


def kernel(x, c, ctx, c_ctx, w_mod, b_mod, ln_g, ln_b, attn_w_in, attn_w_out, attn_lambda, attn_subln_g,
           sgu_w_in, sgu_b_in, sgu_norm_g, sgu_norm_b, sgu_w_s, sgu_b_s, sgu_w_out,
           router_w, router_bias, exp_w_gate, exp_w_up, exp_w_down, sh_w_gate, sh_w_up, sh_w_down):
    b, n, d = x.shape
    depth = w_mod.shape[0]
    assert b <= 7 and d == ATT_HEADS * LANES and n % GRID_W == 0
    alpha = (2 * depth) ** 0.25
    head_dim = d // ATT_HEADS // 2
    tm = 512 if n % 512 == 0 else 256

    cs = jnp.zeros((8, d), F32).at[:b].set(c).at[b].set(c_ctx)
    mods = _modulation(cs, w_mod, b_mod)

    def mod_vec(i, j):
        return mods[i, :, j * d:(j + 1) * d].reshape(8, 1, d)

    for i in range(depth):
        sh_m, sc_m, g_m, sh_f, sc_f, g_f = (mod_vec(i, j) for j in range(6))
        if i % N_MIXERS == 0:
            a = i // N_MIXERS
            lam_init = 0.8 - 0.6 * math.exp(-0.3 * i)
            w_in_bf = attn_w_in[a].astype(BF16)
            cos, slo, shi = _rope_tables(n)
            q_scale = head_dim ** -0.5 * math.log2(math.e)
            q, k, v = _qkv_proj(x, sc_m, sh_m, w_in_bf, cos, slo, shi, q_scale, tm)
            kc, vc = _kv_ctx_proj(ctx, sc_m[b:b + 1], sh_m[b:b + 1], w_in_bf[:, d:])
            tk = min(1024, n // 2)
            pre = _diff_attention(q, k, v, kc, vc, attn_lambda[a], attn_subln_g[a], lam_init, min(1024, n), tk)
            w_out_bf = attn_w_out[a].astype(BF16)
        else:
            s = i // N_MIXERS
            pre = _sgu_mixer(x, sc_m, sh_m, sgu_w_in[s].astype(BF16), sgu_b_in[s], sgu_norm_g[s], sgu_norm_b[s],
                             sgu_w_s[s].astype(BF16), sgu_b_s[s].T, tm)
            w_out_bf = sgu_w_out[s].astype(BF16)
        x1, hf = _post_mixer(pre, w_out_bf, x, g_m, ln_g[i, 0], ln_b[i, 0], sc_f, sh_f, alpha, tm)
        x = _moe_layer(x1, hf, sc_f, sh_f, g_f, router_w[i], router_bias[i], exp_w_gate, exp_w_up, exp_w_down, i,
                       sh_w_gate[i], sh_w_up[i], sh_w_down[i], ln_g[i, 1], ln_b[i, 1], alpha, tm)
    return x
```

```python
import functools
import math

import jax
import jax.numpy as jnp
from jax import lax
from jax.experimental import pallas as pl
from jax.experimental.pallas import tpu as pltpu
from jax.experimental.pallas import tpu_sc as plsc

F32 = jnp.float32
BF16 = jnp.bfloat16
I32 = jnp.int32

GRID_W = 64
ATT_HEADS = 8
ROPE_THETA = 10000.0
SGU_CHUNK = 128
SGU_GROUPS = 8
TOP_K = 8
N_GROUPS = 8
TOPK_GROUPS = 4
ROUTED_SCALE = 2.5
LN_EPS = 1e-5
N_MIXERS = 2

LANES = 128
MOE_ROWS = 256
ROUTE_SLAB = 512
MOE_GROUPS = (4, 2, 1)
MOE_LOOKAHEAD = 4
MOE_SLOTS = MOE_LOOKAHEAD + MOE_GROUPS[0]
TOKEN_TILE = 512
ATT_QUERY_TILE = 1024
ATT_KEY_CHUNK = 1024
ATT_STRIP = 16
SC_CORES = 2
SC_SUBCORES = 16
SC_CHUNK = 64
VMEM_LIMIT = 56 * 1024 * 1024
NEG_INF = float("-inf")


def _cparams(*sem):
    return pltpu.CompilerParams(dimension_semantics=sem, vmem_limit_bytes=VMEM_LIMIT)


def _layer_norm(z, g, b):
    mu = jnp.mean(z, axis=-1, keepdims=True)
    zc = z - mu
    var = jnp.mean(zc * zc, axis=-1, keepdims=True)
    return zc * lax.rsqrt(var + LN_EPS) * g + b


def _silu(x):
    return x * (1.0 / (1.0 + jnp.exp(-x)))


_HIGH_HALF = -65536


def _pack_bf16_pairs(y):
    w = y.shape[1] // 2
    bits = lax.bitcast_convert_type(y.astype(BF16).astype(F32), I32)
    return lax.shift_right_logical(bits[:, :w], 16) | (bits[:, w:] & _HIGH_HALF)


def _unpack_bf16_pairs(p):
    return (lax.bitcast_convert_type(lax.shift_left(p, 16), F32),
            lax.bitcast_convert_type(p & _HIGH_HALF, F32))


def _mod_kernel(cs_ref, w_ref, b_ref, o_ref):
    s = _silu(cs_ref[...])
    o_ref[0] = jnp.dot(s, w_ref[0], precision=lax.Precision.HIGHEST,
                       preferred_element_type=F32) + b_ref[0]


def _modulation(cs, w_mod, b_mod):
    depth, d, n6 = w_mod.shape
    tn = n6 // 4
    return pl.pallas_call(
        _mod_kernel,
        out_shape=jax.ShapeDtypeStruct((depth, 8, n6), F32),
        grid=(depth, n6 // tn),
        in_specs=[
            pl.BlockSpec((8, d), lambda l, j: (0, 0)),
            pl.BlockSpec((1, d, tn), lambda l, j: (l, 0, j)),
            pl.BlockSpec((1, 1, tn), lambda l, j: (l, 0, j)),
        ],
        out_specs=pl.BlockSpec((1, 8, tn), lambda l, j: (l, 0, j)),
        compiler_params=_cparams("parallel", "parallel"),
        name="modulation",
    )(cs, w_mod, b_mod.reshape(depth, 1, n6))


def _rope(xh, cos, sin_lo, sin_hi):
    return xh * cos + pltpu.roll(xh, LANES - 16, 1) * sin_lo + pltpu.roll(xh, 16, 1) * sin_hi


def _qkv_kernel(x_ref, sc_ref, sh_ref, w_ref, cos_ref, slo_ref, shi_ref, q_ref, k_ref, v_ref, *, q_scale):
    d = x_ref.shape[-1]
    h = (x_ref[0] * (1.0 + sc_ref[0]) + sh_ref[0]).astype(BF16)
    cos, slo, shi = cos_ref[...], slo_ref[...], shi_ref[...]
    q = jnp.dot(h, w_ref[:, 0:d], preferred_element_type=F32)
    for hd in range(ATT_HEADS):
        q_ref[0, hd] = (_rope(q[:, hd * LANES:(hd + 1) * LANES], cos, slo, shi) * q_scale).astype(BF16)
    k = jnp.dot(h, w_ref[:, d:2 * d], preferred_element_type=F32)
    for hd in range(ATT_HEADS):
        k_ref[0, hd] = _rope(k[:, hd * LANES:(hd + 1) * LANES], cos, slo, shi).astype(BF16)
    v = jnp.dot(h, w_ref[:, 2 * d:3 * d], preferred_element_type=F32)
    for hd in range(ATT_HEADS):
        v_ref[0, hd] = v[:, hd * LANES:(hd + 1) * LANES].T.astype(BF16)


def _qkv_proj(x, sc, sh, w_bf, cos, slo, shi, q_scale, tn):
    b, n, d = x.shape
    hd_shape = jax.ShapeDtypeStruct((b, ATT_HEADS, n, LANES), BF16)
    vec = pl.BlockSpec((1, 1, d), lambda bi, i: (bi, 0, 0))
    tab = pl.BlockSpec((tn, LANES), lambda bi, i: (i, 0))
    out = pl.BlockSpec((1, ATT_HEADS, tn, LANES), lambda bi, i: (bi, 0, i, 0))
    out_t = pl.BlockSpec((1, ATT_HEADS, LANES, tn), lambda bi, i: (bi, 0, 0, i))
    return pl.pallas_call(
        functools.partial(_qkv_kernel, q_scale=q_scale),
        out_shape=(hd_shape, hd_shape, jax.ShapeDtypeStruct((b, ATT_HEADS, LANES, n), BF16)),
        grid=(b, n // tn),
        in_specs=[
            pl.BlockSpec((1, tn, d), lambda bi, i: (bi, i, 0)),
            vec, vec,
            pl.BlockSpec((d, 3 * d), lambda bi, i: (0, 0)),
            tab, tab, tab,
        ],
        out_specs=(out, out, out_t),
        compiler_params=_cparams("parallel", "parallel"),
        name="qkv_proj",
    )(x, sc, sh, w_bf, cos, slo, shi)


def _kv_ctx_kernel(x_ref, sc_ref, sh_ref, w_ref, k_ref, v_ref):
    d = x_ref.shape[-1]
    h = (x_ref[0] * (1.0 + sc_ref[0]) + sh_ref[0]).astype(BF16)
    k = jnp.dot(h, w_ref[:, 0:d], preferred_element_type=F32)
    v = jnp.dot(h, w_ref[:, d:2 * d], preferred_element_type=F32)
    for hd in range(ATT_HEADS):
        k_ref[0, hd] = k[:, hd * LANES:(hd + 1) * LANES].astype(BF16)
        v_ref[0, hd] = v[:, hd * LANES:(hd + 1) * LANES].T.astype(BF16)


def _kv_ctx_proj(ctx, sc, sh, w_kv_bf):
    b, c, d = ctx.shape
    hd_shape = jax.ShapeDtypeStruct((b, ATT_HEADS, c, LANES), BF16)
    vec = pl.BlockSpec((1, 1, d), lambda bi: (0, 0, 0))
    out = pl.BlockSpec((1, ATT_HEADS, c, LANES), lambda bi: (bi, 0, 0, 0))
    out_t = pl.BlockSpec((1, ATT_HEADS, LANES, c), lambda bi: (bi, 0, 0, 0))
    return pl.pallas_call(
        _kv_ctx_kernel,
        out_shape=(hd_shape, jax.ShapeDtypeStruct((b, ATT_HEADS, LANES, c), BF16)),
        grid=(b,),
        in_specs=[
            pl.BlockSpec((1, c, d), lambda bi: (bi, 0, 0)),
            vec, vec,
            pl.BlockSpec((d, 2 * d), lambda bi: (0, 0)),
        ],
        out_specs=(out, out_t),
        compiler_params=_cparams("parallel"),
        name="kv_ctx_proj",
    )(ctx, sc, sh, w_kv_bf)


def _attn_finish(acc_a, l_a, acc_b, l_b, lamp_ref, g_ref, lam_init):
    lp = lamp_ref[...]
    lam = (jnp.exp(jnp.sum(lp[0:1] * lp[1:2], axis=-1, keepdims=True))
           - jnp.exp(jnp.sum(lp[2:3] * lp[3:4], axis=-1, keepdims=True)) + lam_init)
    o = acc_a / l_a - lam * (acc_b / l_b)
    o = o * lax.rsqrt(jnp.mean(o * o, axis=0, keepdims=True) + LN_EPS) * g_ref[...] * (1.0 - lam_init)
    return o.T.astype(BF16)


def _attn_kernel(q_ref, k_ref, vt_ref, kc_ref, vct_ref, lamp_ref, g_ref, o_ref,
                 s00, s01, s10, s11, p0, p1, acc0, acc1, *, tk, lam_init):
    q = q_ref[0, 0]
    tq = q.shape[0]
    half = LANES // 2
    lane = lax.broadcasted_iota(jnp.int32, q.shape, 1)
    zero = jnp.zeros_like(q)
    qs = (jnp.where(lane < half, q, zero), jnp.where(lane >= half, q, zero))
    n_chunks = k_ref.shape[2] // tk
    s_scr = ((s00, s01), (s10, s11))
    p_scr, acc_scr = (p0, p1), (acc0, acc1)
    nt = (((1,), (1,)), ((), ()))

    def scores(slot, kc):
        width = kc.shape[0]
        for mp in range(2):
            s_scr[slot][mp][0:width, :] = lax.dot_general(kc, qs[mp], nt, preferred_element_type=F32)

    def absorb(slot, vct, shift, sums):
        width = vct.shape[1]
        sums = list(sums)
        for mp in range(2):
            part = sums[mp]
            for r in range(width // ATT_STRIP):
                lo = r * ATT_STRIP
                tiles = [jnp.exp2(s_scr[slot][mp][lo + 8 * u:lo + 8 * (u + 1), :] - shift[mp])
                         for u in range(ATT_STRIP // 8)]
                p_scr[mp][lo:lo + ATT_STRIP, :] = jnp.concatenate(tiles, axis=0).astype(BF16)
                part = part + functools.reduce(lambda x, y: x + y, tiles)
            sums[mp] = part
            acc_scr[mp][...] += jnp.dot(vct, p_scr[mp][0:width, :], preferred_element_type=F32)
        return tuple(sums)

    def k_chunk(j):
        return k_ref[0, 0, pl.ds(pl.multiple_of(j * tk, tk), tk), :]

    def vt_chunk(j):
        return vt_ref[0, 0, :, pl.ds(pl.multiple_of(j * tk, tk), tk)]

    scores(0, k_chunk(0))
    shift = tuple(jnp.broadcast_to(jnp.max(s_scr[0][mp][...], axis=0, keepdims=True), (8, tq)) for mp in range(2))
    for mp in range(2):
        acc_scr[mp][...] = jnp.zeros((LANES, tq), F32)
    sums = (jnp.zeros((8, tq), F32), jnp.zeros((8, tq), F32))

    def pair(jj, sums):
        j0 = 2 * jj
        scores(1, k_chunk(j0 + 1))
        sums = absorb(0, vt_chunk(j0), shift, sums)
        scores(0, k_chunk(j0 + 2))
        return absorb(1, vt_chunk(j0 + 1), shift, sums)

    sums = lax.fori_loop(0, n_chunks // 2 - 1, pair, sums)
    scores(1, k_chunk(n_chunks - 1))
    sums = absorb(0, vt_chunk(n_chunks - 2), shift, sums)
    scores(0, kc_ref[0, 0])
    sums = absorb(1, vt_chunk(n_chunks - 1), shift, sums)
    sums = absorb(0, vct_ref[0, 0], shift, sums)

    tot = [jnp.sum(sums[mp], axis=0, keepdims=True) for mp in range(2)]
    bad = sum(jnp.sum(jnp.where(jnp.isfinite(x), 0.0, 1.0)) for x in (tot[0], tot[1], acc0[...], acc1[...]))

    @pl.when(bad == 0.0)
    def _():
        o_ref[0] = _attn_finish(acc0[...], tot[0], acc1[...], tot[1], lamp_ref, g_ref, lam_init)

    @pl.when(bad != 0.0)
    def _():
        def update(carry, kc, vct):
            new = []
            for mp in range(2):
                m, l, acc = carry[mp]
                s = lax.dot_general(kc, qs[mp], nt, preferred_element_type=F32)
                mn = jnp.maximum(m, jnp.max(s, axis=0, keepdims=True))
                a = jnp.exp2(m - mn)
                p = jnp.exp2(s - mn)
                new.append((mn, a * l + jnp.sum(p, axis=0, keepdims=True),
                            a * acc + jnp.dot(vct, p.astype(BF16), preferred_element_type=F32)))
            return tuple(new)

        init = tuple((jnp.full((1, tq), NEG_INF, F32), jnp.zeros((1, tq), F32), jnp.zeros((LANES, tq), F32))
                     for _ in range(2))
        carry = lax.fori_loop(0, n_chunks, lambda j, c: update(c, k_chunk(j), vt_chunk(j)), init)
        (_, l_a, acc_a), (_, l_b, acc_b) = update(carry, kc_ref[0, 0], vct_ref[0, 0])
        o_ref[0] = _attn_finish(acc_a, l_a, acc_b, l_b, lamp_ref, g_ref, lam_init)


def _diff_attention(q, k, vt, kc, vct, lam_p, subln_g, lam_init, tq, tk):
    b, h, n, _ = q.shape
    c = kc.shape[2]
    assert n % tk == 0 and (n // tk) % 2 == 0 and c <= tk and tk % ATT_STRIP == 0 and c % ATT_STRIP == 0
    spec = lambda r, cols: pl.BlockSpec((1, 1, r, cols), lambda bi, hi, i: (bi, hi, 0, 0))
    return pl.pallas_call(
        functools.partial(_attn_kernel, tk=tk, lam_init=lam_init),
        out_shape=jax.ShapeDtypeStruct((b, n, h * LANES), BF16),
        grid=(b, h, n // tq),
        in_specs=[
            pl.BlockSpec((1, 1, tq, LANES), lambda bi, hi, i: (bi, hi, i, 0)),
            spec(n, LANES), spec(LANES, n), spec(c, LANES), spec(LANES, c),
            pl.BlockSpec(lam_p.shape, lambda bi, hi, i: (0, 0)),
            pl.BlockSpec((LANES, 1), lambda bi, hi, i: (0, 0)),
        ],
        out_specs=pl.BlockSpec((1, tq, LANES), lambda bi, hi, i: (bi, i, hi)),
        scratch_shapes=[
            *[pltpu.VMEM((tk, tq), F32)] * 4,
            *[pltpu.VMEM((tk, tq), BF16)] * 2,
            *[pltpu.VMEM((LANES, tq), F32)] * 2,
        ],
        compiler_params=_cparams("parallel", "parallel", "parallel"),
        name="diff_attention",
    )(q, k, vt, kc, vct, lam_p, subln_g.reshape(LANES, 1))


def _sgu_kernel(x_ref, sc_ref, sh_ref, w_ref, b_ref, ng_ref, nb_ref, ws_ref, bs_ref, t_ref):
    f = t_ref.shape[-1]
    cg = f // SGU_GROUPS
    tm = x_ref.shape[1]
    h = (x_ref[0] * (1.0 + sc_ref[0]) + sh_ref[0]).astype(BF16)
    z = jnp.dot(h, w_ref[...], preferred_element_type=F32) + b_ref[...]
    z = 0.5 * z * (1.0 + lax.erf(z * (2.0 ** -0.5)))
    u = z[:, :f]
    v = _layer_norm(z[:, f:], ng_ref[...], nb_ref[...]).astype(BF16)
    for c in range(tm // SGU_CHUNK):
        rows = slice(c * SGU_CHUNK, (c + 1) * SGU_CHUNK)
        for g in range(SGU_GROUPS):
            cols = slice(g * cg, (g + 1) * cg)
            vm = jnp.dot(ws_ref[g], v[rows, cols], preferred_element_type=F32) + bs_ref[:, g:g + 1]
            t_ref[0, rows, cols] = (u[rows, cols] * vm).astype(BF16)


def _sgu_mixer(x, sc, sh, w_in_bf, b_in, norm_g, norm_b, w_s_bf, b_s_t, tm):
    b, n, d = x.shape
    f2 = w_in_bf.shape[1]
    f = f2 // 2
    vec = pl.BlockSpec((1, 1, d), lambda bi, i: (bi, 0, 0))
    full2 = lambda a: pl.BlockSpec(a.shape, lambda bi, i: (0,) * a.ndim)
    b_in2, ng2, nb2 = b_in.reshape(1, f2), norm_g.reshape(1, f), norm_b.reshape(1, f)
    return pl.pallas_call(
        _sgu_kernel,
        out_shape=jax.ShapeDtypeStruct((b, n, f), BF16),
        grid=(b, n // tm),
        in_specs=[
            pl.BlockSpec((1, tm, d), lambda bi, i: (bi, i, 0)),
            vec, vec,
            full2(w_in_bf), full2(b_in2), full2(ng2), full2(nb2), full2(w_s_bf), full2(b_s_t),
        ],
        out_specs=pl.BlockSpec((1, tm, f), lambda bi, i: (bi, i, 0)),
        compiler_params=_cparams("parallel", "parallel"),
        name="sgu_mixer",
    )(x, sc, sh, w_in_bf, b_in2, ng2, nb2, w_s_bf, b_s_t)


def _post_kernel(pre_ref, w_ref, x_ref, gm_ref, lg_ref, lb_ref, scf_ref, shf_ref, x1_ref, hf_ref, *, alpha):
    y = jnp.dot(pre_ref[0], w_ref[...], preferred_element_type=F32)
    x1 = _layer_norm(alpha * x_ref[0] + gm_ref[0] * y, lg_ref[...], lb_ref[...])
    x1_ref[0] = x1
    hf_ref[0] = _pack_bf16_pairs(x1 * (1.0 + scf_ref[0]) + shf_ref[0])


def _post_mixer(pre, w_bf, x, gm, ln_g, ln_b, scf, shf, alpha, tm):
    b, n, d = x.shape
    kd = pre.shape[-1]
    vec = pl.BlockSpec((1, 1, d), lambda bi, i: (bi, 0, 0))
    row = pl.BlockSpec((1, d), lambda bi, i: (0, 0))
    tile = pl.BlockSpec((1, tm, d), lambda bi, i: (bi, i, 0))
    return pl.pallas_call(
        functools.partial(_post_kernel, alpha=alpha),
        out_shape=(jax.ShapeDtypeStruct((b, n, d), F32), jax.ShapeDtypeStruct((b, n, d // 2), I32)),
        grid=(b, n // tm),
        in_specs=[
            pl.BlockSpec((1, tm, kd), lambda bi, i: (bi, i, 0)),
            pl.BlockSpec((kd, d), lambda bi, i: (0, 0)),
            tile, vec, row, row, vec, vec,
        ],
        out_specs=(tile, pl.BlockSpec((1, tm, d // 2), lambda bi, i: (bi, i, 0))),
        compiler_params=_cparams("parallel", "parallel"),
        name="post_mixer",
    )(pre, w_bf, x, gm, ln_g.reshape(1, d), ln_b.reshape(1, d), scf, shf)


def _route_select(scores, choice):
    e, w = scores.shape
    ge = e // N_GROUPS
    g3 = choice.reshape(N_GROUPS, ge, w)
    ri = lax.broadcasted_iota(jnp.int32, g3.shape, 1).astype(F32)
    m1 = jnp.max(g3, axis=1, keepdims=True)
    first = jnp.min(jnp.where(g3 == m1, ri, float(ge)), axis=1, keepdims=True)
    m2 = jnp.max(jnp.where(ri == first, NEG_INF, g3), axis=1, keepdims=True)
    gs = m1 + m2

    gi = lax.broadcasted_iota(jnp.int32, gs.shape, 0).astype(F32)
    gsel = jnp.zeros(gs.shape, F32)
    cur = gs
    for _ in range(TOPK_GROUPS):
        m = jnp.max(cur, axis=0, keepdims=True)
        f = jnp.min(jnp.where(cur == m, gi, float(N_GROUPS)), axis=0, keepdims=True)
        hit = gi == f
        gsel = jnp.where(hit, 1.0, gsel)
        cur = jnp.where(hit, NEG_INF, cur)
    emask = jnp.broadcast_to(gsel, g3.shape).reshape(e, w)
    masked = jnp.where(emask > 0.5, choice, NEG_INF)

    ei = lax.broadcasted_iota(jnp.int32, (e, w), 0).astype(F32)
    onehot = jnp.zeros((e, w), F32)
    idxs, ws = [], []
    for _ in range(TOP_K):
        m = jnp.max(masked, axis=0, keepdims=True)
        f = jnp.min(jnp.where(masked == m, ei, float(e)), axis=0, keepdims=True)
        hit = ei == f
        idxs.append(f)
        ws.append(jnp.sum(jnp.where(hit, scores, 0.0), axis=0, keepdims=True))
        masked = jnp.where(hit, NEG_INF, masked)
        onehot = jnp.where(hit, 1.0, onehot)
    return idxs, ws, onehot


def _route_kernel(x_ref, sc_ref, sh_ref, wr_ref, rb_ref, idx_ref, w_ref, rank_ref, cnt_ref, carry_ref):
    i = pl.program_id(0)
    e = wr_ref.shape[0]
    tm = x_ref.shape[0]

    @pl.when(i == 0)
    def _():
        carry_ref[...] = jnp.zeros_like(carry_ref)

    h = x_ref[...] * (1.0 + sc_ref[0]) + sh_ref[0]
    logits = lax.dot_general(wr_ref[...], h, (((1,), (1,)), ((), ())),
                             precision=lax.Precision.HIGHEST, preferred_element_type=F32)
    scores = 1.0 / (1.0 + jnp.exp(-logits))
    choice = scores + rb_ref[...]

    slab = min(ROUTE_SLAB, tm)
    slabs = [slice(j * slab, (j + 1) * slab) for j in range(tm // slab)]
    picks = [_route_select(scores[:, sl], choice[:, sl]) for sl in slabs]
    onehot = jnp.concatenate([pk[2] for pk in picks], axis=1)

    r_i = lax.broadcasted_iota(jnp.int32, (tm, tm), 0)
    c_i = lax.broadcasted_iota(jnp.int32, (tm, tm), 1)
    upper = jnp.where(r_i < c_i, 1.0, 0.0).astype(BF16)
    rk = jnp.dot(onehot.astype(BF16), upper, preferred_element_type=F32) + carry_ref[...]
    carry_ref[...] += jnp.sum(onehot, axis=1, keepdims=True)

    ei = lax.broadcasted_iota(jnp.int32, (e, slab), 0).astype(F32)
    for sl, (idxs, ws, _) in zip(slabs, picks):
        wsum = functools.reduce(lambda x, y: x + y, ws)
        for k in range(TOP_K):
            idx_ref[k:k + 1, sl] = idxs[k].astype(jnp.int32)
            w_ref[k:k + 1, sl] = ws[k] / wsum * ROUTED_SCALE
            rank_ref[k:k + 1, sl] = jnp.sum(jnp.where(ei == idxs[k], rk[:, sl], 0.0), axis=0,
                                            keepdims=True).astype(jnp.int32)
    cnt_ref[...] = jnp.broadcast_to(carry_ref[...], cnt_ref.shape).astype(jnp.int32)


def _route(x1, scf, shf, wr_t, rbias, tm):
    b, n, d = x1.shape
    t = b * n
    e = wr_t.shape[0]
    per_b = n // tm
    vec = pl.BlockSpec((1, 1, d), lambda i: (i // per_b, 0, 0))
    out_t = pl.BlockSpec((TOP_K, tm), lambda i: (0, i))
    return pl.pallas_call(
        _route_kernel,
        out_shape=(jax.ShapeDtypeStruct((TOP_K, t), jnp.int32), jax.ShapeDtypeStruct((TOP_K, t), F32),
                   jax.ShapeDtypeStruct((TOP_K, t), jnp.int32), jax.ShapeDtypeStruct((e, LANES), jnp.int32)),
        grid=(t // tm,),
        in_specs=[
            pl.BlockSpec((tm, d), lambda i: (i, 0)),
            vec, vec,
            pl.BlockSpec((e, d), lambda i: (0, 0)),
            pl.BlockSpec((e, 1), lambda i: (0, 0)),
        ],
        out_specs=(out_t, out_t, out_t, pl.BlockSpec((e, LANES), lambda i: (0, 0))),
        scratch_shapes=[pltpu.VMEM((e, 1), F32)],
        compiler_params=_cparams("arbitrary"),
        name="route",
    )(x1.reshape(t, d), scf, shf, wr_t, rbias.reshape(e, 1))


def _gather_rows(table, idx):
    m = idx.shape[0]
    w = table.shape[1]
    workers = SC_CORES * SC_SUBCORES
    n_ch = m // (workers * SC_CHUNK)
    assert m % (workers * SC_CHUNK) == 0 and n_ch % 2 == 0
    mesh = plsc.VectorSubcoreMesh(core_axis_name="c", subcore_axis_name="s",
                                  num_cores=SC_CORES, num_subcores=SC_SUBCORES)

    @functools.partial(
        pl.kernel, mesh=mesh,
        out_type=jax.ShapeDtypeStruct((m, w), table.dtype),
        scratch_types=[
            pltpu.VMEM((n_ch, SC_CHUNK), I32),
            pltpu.VMEM((SC_CHUNK, w), table.dtype), pltpu.VMEM((SC_CHUNK, w), table.dtype),
            pltpu.SemaphoreType.DMA, pltpu.SemaphoreType.DMA, pltpu.SemaphoreType.DMA, pltpu.SemaphoreType.DMA,
        ],
        name="sc_gather_rows",
    )
    def gather(table_hbm, idx_hbm, out_hbm, idx_all, buf0, buf1, gsem0, gsem1, wsem0, wsem1):
        first = (lax.axis_index("s") * SC_CORES + lax.axis_index("c")) * n_ch
        bufs, gsem, wsem = (buf0, buf1), (gsem0, gsem1), (wsem0, wsem1)
        pltpu.sync_copy(idx_hbm.at[pl.ds(first, n_ch)], idx_all)

        def gather_copy(j, s):
            return pltpu.make_async_copy(table_hbm.at[idx_all.at[j]], bufs[s], gsem[s])

        def write_copy(j, s):
            rows = pl.ds(pl.multiple_of((first + j) * SC_CHUNK, SC_CHUNK), SC_CHUNK)
            return pltpu.make_async_copy(bufs[s], out_hbm.at[rows], wsem[s])

        gather_copy(0, 0).start()

        @pl.loop(0, n_ch, step=2)
        def _(jj):
            for s in range(2):
                j = jj + s

                @pl.when(j >= 1)
                def _():
                    write_copy(j - 1, 1 - s).wait()

                @pl.when(j + 1 < n_ch)
                def _():
                    gather_copy(j + 1, 1 - s).start()

                gather_copy(j, s).wait()
                write_copy(j, s).start()

        write_copy(n_ch - 1, 1).wait()

    return gather(table, idx.reshape(m // SC_CHUNK, SC_CHUNK))


def _scatter_rows(src, pos3, p):
    w = src.shape[1]
    n_chunks, k, ch = pos3.shape
    workers = SC_CORES * SC_SUBCORES
    per_w = n_chunks // workers
    assert ch == SC_CHUNK and n_chunks % workers == 0 and src.shape[0] == n_chunks * ch
    mesh = plsc.VectorSubcoreMesh(core_axis_name="c", subcore_axis_name="s",
                                  num_cores=SC_CORES, num_subcores=SC_SUBCORES)

    @functools.partial(
        pl.kernel, mesh=mesh,
        out_type=jax.ShapeDtypeStruct((p, w), src.dtype),
        scratch_types=[
            pltpu.VMEM((k, ch), I32),
            pltpu.VMEM((ch, w), src.dtype),
            pltpu.SemaphoreType.DMA,
        ],
        name="sc_scatter_rows",
    )
    def scatter(src_hbm, pos_hbm, out_hbm, idx_v, rows_v, sem):
        first = (lax.axis_index("s") * SC_CORES + lax.axis_index("c")) * per_w

        @pl.loop(0, per_w)
        def _(j):
            c = first + j
            pltpu.sync_copy(pos_hbm.at[c], idx_v)
            pltpu.sync_copy(src_hbm.at[pl.ds(pl.multiple_of(c * ch, ch), ch)], rows_v)
            copies = [pltpu.async_copy(rows_v, out_hbm.at[idx_v.at[kk]], sem) for kk in range(k)]
            for cp in copies:
                cp.wait()

    return scatter(src, pos3)


def _experts_kernel(ps_ref, nb_ref, cnt_ref, nt_ref, xs_hbm, wg_ref, wu_ref, wd_ref, y_hbm,
                    xbuf, ybuf, in_sem, out_sem, wg_bf, wu_bf, wd_bf):
    e = pl.program_id(0)
    nb, cnt, n_total = nb_ref[e], cnt_ref[e], nt_ref[0]
    g0 = ps_ref[e] // MOE_ROWS

    def in_copy(g):
        rows = pl.ds(pl.multiple_of(g * MOE_ROWS, MOE_ROWS), MOE_ROWS)
        return pltpu.make_async_copy(xs_hbm.at[rows], xbuf.at[g % MOE_SLOTS], in_sem.at[g % MOE_SLOTS])

    def out_copy(g):
        rows = pl.ds(pl.multiple_of(g * MOE_ROWS, MOE_ROWS), MOE_ROWS)
        return pltpu.make_async_copy(ybuf.at[g % MOE_SLOTS], y_hbm.at[rows], out_sem.at[g % MOE_SLOTS])

    @pl.when(e == 0)
    def _():
        for j in range(MOE_LOOKAHEAD):
            @pl.when(j < n_total)
            def _():
                in_copy(j).start()

    @pl.when(nb > 0)
    def _():
        wg_bf[...] = wg_ref[0, 0].astype(BF16)
        wu_bf[...] = wu_ref[0, 0].astype(BF16)
        wd_bf[...] = wd_ref[0, 0].astype(BF16)

        def process(b, width):
            g = g0 + b
            for u in range(width):
                @pl.when(g + u + MOE_LOOKAHEAD < n_total)
                def _():
                    in_copy(g + u + MOE_LOOKAHEAD).start()

            for u in range(width):
                in_copy(g + u).wait()

                @pl.when(g + u >= MOE_SLOTS)
                def _():
                    out_copy(g + u - MOE_SLOTS).wait()

            packed = jnp.concatenate([xbuf[(g + u) % MOE_SLOTS] for u in range(width)], axis=0)
            row = lax.broadcasted_iota(I32, (width * MOE_ROWS, 1), 0) + b * MOE_ROWS
            x_lo, x_hi = (v.astype(BF16) for v in _unpack_bf16_pairs(jnp.where(row < cnt, packed, 0)))
            half = x_lo.shape[1]

            def up(w_bf):
                return (jnp.dot(x_lo, w_bf[:half, :], preferred_element_type=F32)
                        + jnp.dot(x_hi, w_bf[half:, :], preferred_element_type=F32))

            hb = (_silu(up(wg_bf)) * up(wu_bf)).astype(BF16)
            y = _pack_bf16_pairs(jnp.dot(hb, wd_bf[...], preferred_element_type=F32))
            for u in range(width):
                ybuf[(g + u) % MOE_SLOTS] = y[u * MOE_ROWS:(u + 1) * MOE_ROWS]
                out_copy(g + u).start()

        start = 0
        for width in MOE_GROUPS:
            count = (nb - start) // width

            def body(i, c, width=width, start=start):
                process(start + width * i, width)
                return c

            lax.fori_loop(0, count, body, 0)
            start = start + count * width

    @pl.when(e == pl.num_programs(0) - 1)
    def _():
        for j in range(MOE_SLOTS):
            @pl.when(n_total - 1 - j >= 0)
            def _():
                out_copy(n_total - 1 - j).wait()


def _routed_experts(xs, wg, wu, wd, layer, pstarts, nblk, counts, n_total):
    p, dp = xs.shape
    _, e, d, f = wg.shape
    wspec = lambda r, c: pl.BlockSpec((1, 1, r, c), lambda i, ps, nb, cnt, nt: (layer, i, 0, 0))
    grid_spec = pltpu.PrefetchScalarGridSpec(
        num_scalar_prefetch=4,
        grid=(e,),
        in_specs=[pl.BlockSpec(memory_space=pl.ANY), wspec(d, f), wspec(d, f), wspec(f, d)],
        out_specs=pl.BlockSpec(memory_space=pl.ANY),
        scratch_shapes=[
            pltpu.VMEM((MOE_SLOTS, MOE_ROWS, dp), I32), pltpu.VMEM((MOE_SLOTS, MOE_ROWS, dp), I32),
            pltpu.SemaphoreType.DMA((MOE_SLOTS,)), pltpu.SemaphoreType.DMA((MOE_SLOTS,)),
            pltpu.VMEM((d, f), BF16), pltpu.VMEM((d, f), BF16), pltpu.VMEM((f, d), BF16),
        ],
    )
    return pl.pallas_call(
        _experts_kernel,
        out_shape=jax.ShapeDtypeStruct((p, dp), I32),
        grid_spec=grid_spec,
        compiler_params=_cparams("arbitrary"),
        name="routed_experts",
    )(pstarts, nblk, counts, n_total, xs, wg, wu, wd)


def _combine_kernel(yg_ref, w_ref, hf_ref, sg_ref, su_ref, sd_ref, x_ref, gf_ref, lg_ref, lb_ref, o_ref, *, alpha):
    w = w_ref[...]
    r_lo, r_hi = _unpack_bf16_pairs(yg_ref[0])
    r_lo, r_hi = w[:, 0:1] * r_lo, w[:, 0:1] * r_hi
    for k in range(1, TOP_K):
        y_lo, y_hi = _unpack_bf16_pairs(yg_ref[k])
        r_lo, r_hi = r_lo + w[:, k:k + 1] * y_lo, r_hi + w[:, k:k + 1] * y_hi
    routed = jnp.concatenate([r_lo, r_hi], axis=1)
    hf = jnp.concatenate(_unpack_bf16_pairs(hf_ref[...]), axis=1).astype(BF16)
    g = jnp.dot(hf, sg_ref[...], preferred_element_type=F32)
    u = jnp.dot(hf, su_ref[...], preferred_element_type=F32)
    shared = jnp.dot((_silu(g) * u).astype(BF16), sd_ref[...], preferred_element_type=F32)
    o_ref[...] = _layer_norm(alpha * x_ref[...] + gf_ref[0] * (routed + shared), lg_ref[...], lb_ref[...])


def _combine(yg, w_tk, hf, sg_bf, su_bf, sd_bf, x1, gf, ln_g, ln_b, alpha, tm, per_b):
    t, d = x1.shape
    f = sg_bf.shape[1]
    row = pl.BlockSpec((1, d), lambda i: (0, 0))
    tile = pl.BlockSpec((tm, d), lambda i: (i, 0))
    return pl.pallas_call(
        functools.partial(_combine_kernel, alpha=alpha),
        out_shape=jax.ShapeDtypeStruct((t, d), F32),
        grid=(t // tm,),
        in_specs=[
            pl.BlockSpec((TOP_K, tm, d // 2), lambda i: (0, i, 0)),
            pl.BlockSpec((tm, TOP_K), lambda i: (i, 0)),
            pl.BlockSpec((tm, d // 2), lambda i: (i, 0)),
            pl.BlockSpec((d, f), lambda i: (0, 0)),
            pl.BlockSpec((d, f), lambda i: (0, 0)),
            pl.BlockSpec((f, d), lambda i: (0, 0)),
            tile,
            pl.BlockSpec((1, 1, d), lambda i: (i // per_b, 0, 0)),
            row, row,
        ],
        out_specs=tile,
        compiler_params=_cparams("parallel"),
        name="moe_combine",
    )(yg, w_tk, hf, sg_bf, su_bf, sd_bf, x1, gf, ln_g.reshape(1, d), ln_b.reshape(1, d))


def _moe_layer(x1, hf, scf, shf, gf, router_w, router_bias, wg, wu, wd, layer, sg, su, sd, ln_g, ln_b, alpha, tm):
    b, n, d = x1.shape
    t = b * n
    e = router_w.shape[1]
    idx_t, w_t, rank_t, cnt = _route(x1, scf, shf, router_w.T, router_bias, tm)

    counts = cnt[:, 0]
    padded = (counts + MOE_ROWS - 1) // MOE_ROWS * MOE_ROWS
    pends = jnp.cumsum(padded)
    pstarts = pends - padded
    sel = idx_t[:, :, None] == jnp.arange(e, dtype=I32)
    pos_t = jnp.sum(jnp.where(sel, pstarts, 0), axis=-1) + rank_t
    p = t * TOP_K + e * MOE_ROWS
    pos3 = pos_t.reshape(TOP_K, t // SC_CHUNK, SC_CHUNK).transpose(1, 0, 2)

    hf2 = hf.reshape(t, d // 2)
    xs = _scatter_rows(hf2, pos3, p)
    yb = _routed_experts(xs, wg, wu, wd, layer, pstarts.astype(I32), (padded // MOE_ROWS).astype(I32), counts,
                         (pends[-1:] // MOE_ROWS).astype(I32))
    yg = _gather_rows(yb, pos_t.reshape(-1)).reshape(TOP_K, t, d // 2)
    out = _combine(yg, w_t.T, hf2, sg.astype(BF16), su.astype(BF16), sd.astype(BF16), x1.reshape(t, d), gf,
                   ln_g, ln_b, alpha, tm, n // tm)
    return out.reshape(b, n, d)


def _rope_tables(n):
    rows = n // GRID_W
    row_pos = jnp.repeat(jnp.arange(rows, dtype=F32), GRID_W)
    col_pos = jnp.tile(jnp.arange(GRID_W, dtype=F32), rows)
    half = LANES // 4
    lane = jnp.arange(LANES)
    in_blk = lane % half
    freq = ROPE_THETA ** (-(2.0 * (in_blk % (half // 2)).astype(F32)) / half)
    use_col = (lane // half) % 2 == 1
    pos = jnp.where(use_col[None, :], col_pos[:, None], row_pos[:, None])
    ang = pos * freq[None, :]
    lo = (in_blk < half // 2)[None, :]
    sin = jnp.sin(ang)
    return jnp.cos(ang), jnp.where(lo, -sin, 0.0), jnp.where(lo, 0.0, sin)


def kernel(x, c, ctx, c_ctx, w_mod, b_mod, ln_g, ln_b, attn_w_in, attn_w_out, attn_lambda, attn_subln_g,
           sgu_w_in, sgu_b_in, sgu_norm_g, sgu_norm_b, sgu_w_s, sgu_b_s, sgu_w_out,
           router_w, router_bias, exp_w_gate, exp_w_up, exp_w_down, sh_w_gate, sh_w_up, sh_w_down):
    b, n, d = x.shape
    depth = w_mod.shape[0]
    assert b <= 7 and d == ATT_HEADS * LANES and n % GRID_W == 0
    alpha = (2 * depth) ** 0.25
    head_dim = d // ATT_HEADS // 2
    tm = TOKEN_TILE if n % TOKEN_TILE == 0 else TOKEN_TILE // 2

    cs = jnp.zeros((8, d), F32).at[:b].set(c).at[b].set(c_ctx)
    mods = _modulation(cs, w_mod, b_mod)

    def mod_vec(i, j):
        return mods[i, :, j * d:(j + 1) * d].reshape(8, 1, d)

    for i in range(depth):
        sh_m, sc_m, g_m, sh_f, sc_f, g_f = (mod_vec(i, j) for j in range(6))
        if i % N_MIXERS == 0:
            a = i // N_MIXERS
            lam_init = 0.8 - 0.6 * math.exp(-0.3 * i)
            w_in_bf = attn_w_in[a].astype(BF16)
            cos, slo, shi = _rope_tables(n)
            q_scale = head_dim ** -0.5 * math.log2(math.e)
            q, k, vt = _qkv_proj(x, sc_m, sh_m, w_in_bf, cos, slo, shi, q_scale, tm)
            kc, vct = _kv_ctx_proj(ctx, sc_m[b:b + 1], sh_m[b:b + 1], w_in_bf[:, d:])
            pre = _diff_attention(q, k, vt, kc, vct, attn_lambda[a], attn_subln_g[a], lam_init,
                                  min(ATT_QUERY_TILE, n), min(ATT_KEY_CHUNK, n // 2))
            w_out_bf = attn_w_out[a].astype(BF16)
        else:
            s = i // N_MIXERS
            pre = _sgu_mixer(x, sc_m, sh_m, sgu_w_in[s].astype(BF16), sgu_b_in[s], sgu_norm_g[s], sgu_norm_b[s],
                             sgu_w_s[s].astype(BF16), sgu_b_s[s].T, tm)
            w_out_bf = sgu_w_out[s].astype(BF16)
        x1, hf = _post_mixer(pre, w_out_bf, x, g_m, ln_g[i, 0], ln_b[i, 0], sc_f, sh_f, alpha, tm)
        x = _moe_layer(x1, hf, sc_f, sh_f, g_f, router_w[i], router_bias[i], exp_w_gate, exp_w_up, exp_w_down, i,
                       sh_w_gate[i], sh_w_up[i], sh_w_down[i], ln_g[i, 1], ln_b[i, 1], alpha, tm)
    return x
```

```python
import functools
import math

import jax
import jax.numpy as jnp
from jax import lax
from jax.experimental import pallas as pl
from jax.experimental.pallas import tpu as pltpu
from jax.experimental.pallas import tpu_sc as plsc

F32 = jnp.float32
BF16 = jnp.bfloat16
I32 = jnp.int32

GRID_W = 64
ATT_HEADS = 8
ROPE_THETA = 10000.0
SGU_CHUNK = 128
SGU_GROUPS = 8
TOP_K = 8
N_GROUPS = 8
TOPK_GROUPS = 4
ROUTED_SCALE = 2.5
LN_EPS = 1e-5
N_MIXERS = 2

LANES = 128
MOE_ROWS = 256
ROUTE_SLAB = 512
MOE_GROUPS = (2, 1)
MOE_LOOKAHEAD = 6
MOE_SLOTS = MOE_LOOKAHEAD + MOE_GROUPS[0]
TOKEN_TILE = 512
ATT_QUERY_TILE = 1024
ATT_KEY_CHUNK = 1024
ATT_STRIP = 16
SC_CORES = 2
SC_SUBCORES = 16
SC_CHUNK = 64
VMEM_LIMIT = 56 * 1024 * 1024
NEG_INF = float("-inf")


def _cparams(*sem):
    return pltpu.CompilerParams(dimension_semantics=sem, vmem_limit_bytes=VMEM_LIMIT)


def _layer_norm(z, g, b):
    mu = jnp.mean(z, axis=-1, keepdims=True)
    zc = z - mu
    var = jnp.mean(zc * zc, axis=-1, keepdims=True)
    return zc * lax.rsqrt(var + LN_EPS) * g + b


def _silu(x):
    return x * (1.0 / (1.0 + jnp.exp(-x)))


_HIGH_HALF = -65536


def _pack_bf16_pairs(y):
    w = y.shape[1] // 2
    bits = lax.bitcast_convert_type(y.astype(BF16).astype(F32), I32)
    return lax.shift_right_logical(bits[:, :w], 16) | (bits[:, w:] & _HIGH_HALF)


def _unpack_bf16_pairs(p):
    return (lax.bitcast_convert_type(lax.shift_left(p, 16), F32),
            lax.bitcast_convert_type(p & _HIGH_HALF, F32))


def _mod_kernel(cs_ref, w_ref, b_ref, o_ref):
    s = _silu(cs_ref[...])
    o_ref[0] = jnp.dot(s, w_ref[0], precision=lax.Precision.HIGHEST,
                       preferred_element_type=F32) + b_ref[0]


def _modulation(cs, w_mod, b_mod):
    depth, d, n6 = w_mod.shape
    tn = n6 // 4
    return pl.pallas_call(
        _mod_kernel,
        out_shape=jax.ShapeDtypeStruct((depth, 8, n6), F32),
        grid=(depth, n6 // tn),
        in_specs=[
            pl.BlockSpec((8, d), lambda l, j: (0, 0)),
            pl.BlockSpec((1, d, tn), lambda l, j: (l, 0, j)),
            pl.BlockSpec((1, 1, tn), lambda l, j: (l, 0, j)),
        ],
        out_specs=pl.BlockSpec((1, 8, tn), lambda l, j: (l, 0, j)),
        compiler_params=_cparams("parallel", "parallel"),
        name="modulation",
    )(cs, w_mod, b_mod.reshape(depth, 1, n6))


def _rope(xh, cos, sin_lo, sin_hi):
    return xh * cos + pltpu.roll(xh, LANES - 16, 1) * sin_lo + pltpu.roll(xh, 16, 1) * sin_hi


def _qkv_kernel(x_ref, sc_ref, sh_ref, w_ref, cos_ref, slo_ref, shi_ref, q_ref, k_ref, v_ref, *, q_scale):
    d = x_ref.shape[-1]
    h = (x_ref[0] * (1.0 + sc_ref[0]) + sh_ref[0]).astype(BF16)
    cos, slo, shi = cos_ref[...], slo_ref[...], shi_ref[...]
    q = jnp.dot(h, w_ref[:, 0:d], preferred_element_type=F32)
    for hd in range(ATT_HEADS):
        q_ref[0, hd] = (_rope(q[:, hd * LANES:(hd + 1) * LANES], cos, slo, shi) * q_scale).astype(BF16)
    k = jnp.dot(h, w_ref[:, d:2 * d], preferred_element_type=F32)
    for hd in range(ATT_HEADS):
        k_ref[0, hd] = _rope(k[:, hd * LANES:(hd + 1) * LANES], cos, slo, shi).astype(BF16)
    v = jnp.dot(h, w_ref[:, 2 * d:3 * d], preferred_element_type=F32)
    for hd in range(ATT_HEADS):
        v_ref[0, hd] = v[:, hd * LANES:(hd + 1) * LANES].T.astype(BF16)


def _qkv_proj(x, sc, sh, w_bf, cos, slo, shi, q_scale, tn):
    b, n, d = x.shape
    hd_shape = jax.ShapeDtypeStruct((b, ATT_HEADS, n, LANES), BF16)
    vec = pl.BlockSpec((1, 1, d), lambda bi, i: (bi, 0, 0))
    tab = pl.BlockSpec((tn, LANES), lambda bi, i: (i, 0))
    out = pl.BlockSpec((1, ATT_HEADS, tn, LANES), lambda bi, i: (bi, 0, i, 0))
    out_t = pl.BlockSpec((1, ATT_HEADS, LANES, tn), lambda bi, i: (bi, 0, 0, i))
    return pl.pallas_call(
        functools.partial(_qkv_kernel, q_scale=q_scale),
        out_shape=(hd_shape, hd_shape, jax.ShapeDtypeStruct((b, ATT_HEADS, LANES, n), BF16)),
        grid=(b, n // tn),
        in_specs=[
            pl.BlockSpec((1, tn, d), lambda bi, i: (bi, i, 0)),
            vec, vec,
            pl.BlockSpec((d, 3 * d), lambda bi, i: (0, 0)),
            tab, tab, tab,
        ],
        out_specs=(out, out, out_t),
        compiler_params=_cparams("parallel", "parallel"),
        name="qkv_proj",
    )(x, sc, sh, w_bf, cos, slo, shi)


def _kv_ctx_kernel(x_ref, sc_ref, sh_ref, w_ref, k_ref, v_ref):
    d = x_ref.shape[-1]
    h = (x_ref[0] * (1.0 + sc_ref[0]) + sh_ref[0]).astype(BF16)
    k = jnp.dot(h, w_ref[:, 0:d], preferred_element_type=F32)
    v = jnp.dot(h, w_ref[:, d:2 * d], preferred_element_type=F32)
    for hd in range(ATT_HEADS):
        k_ref[0, hd] = k[:, hd * LANES:(hd + 1) * LANES].astype(BF16)
        v_ref[0, hd] = v[:, hd * LANES:(hd + 1) * LANES].T.astype(BF16)


def _kv_ctx_proj(ctx, sc, sh, w_kv_bf):
    b, c, d = ctx.shape
    hd_shape = jax.ShapeDtypeStruct((b, ATT_HEADS, c, LANES), BF16)
    vec = pl.BlockSpec((1, 1, d), lambda bi: (0, 0, 0))
    out = pl.BlockSpec((1, ATT_HEADS, c, LANES), lambda bi: (bi, 0, 0, 0))
    out_t = pl.BlockSpec((1, ATT_HEADS, LANES, c), lambda bi: (bi, 0, 0, 0))
    return pl.pallas_call(
        _kv_ctx_kernel,
        out_shape=(hd_shape, jax.ShapeDtypeStruct((b, ATT_HEADS, LANES, c), BF16)),
        grid=(b,),
        in_specs=[
            pl.BlockSpec((1, c, d), lambda bi: (bi, 0, 0)),
            vec, vec,
            pl.BlockSpec((d, 2 * d), lambda bi: (0, 0)),
        ],
        out_specs=(out, out_t),
        compiler_params=_cparams("parallel"),
        name="kv_ctx_proj",
    )(ctx, sc, sh, w_kv_bf)


def _attn_finish(acc_a, l_a, acc_b, l_b, lamp_ref, g_ref, lam_init):
    lp = lamp_ref[...]
    lam = (jnp.exp(jnp.sum(lp[0:1] * lp[1:2], axis=-1, keepdims=True))
           - jnp.exp(jnp.sum(lp[2:3] * lp[3:4], axis=-1, keepdims=True)) + lam_init)
    o = acc_a / l_a - lam * (acc_b / l_b)
    o = o * lax.rsqrt(jnp.mean(o * o, axis=0, keepdims=True) + LN_EPS) * g_ref[...] * (1.0 - lam_init)
    return o.T.astype(BF16)


def _attn_kernel(q_ref, k_ref, vt_ref, kc_ref, vct_ref, lamp_ref, g_ref, o_ref,
                 s00, s01, s10, s11, p0, p1, acc0, acc1, *, tk, lam_init):
    q = q_ref[0, 0]
    tq = q.shape[0]
    half = LANES // 2
    lane = lax.broadcasted_iota(jnp.int32, q.shape, 1)
    zero = jnp.zeros_like(q)
    qs = (jnp.where(lane < half, q, zero), jnp.where(lane >= half, q, zero))
    n_chunks = k_ref.shape[2] // tk
    s_scr = ((s00, s01), (s10, s11))
    p_scr, acc_scr = (p0, p1), (acc0, acc1)
    nt = (((1,), (1,)), ((), ()))

    def scores(slot, kc):
        width = kc.shape[0]
        for mp in range(2):
            s_scr[slot][mp][0:width, :] = lax.dot_general(kc, qs[mp], nt, preferred_element_type=F32)

    def absorb(slot, vct, shift, sums):
        width = vct.shape[1]
        sums = list(sums)
        for mp in range(2):
            part = sums[mp]
            for r in range(width // ATT_STRIP):
                lo = r * ATT_STRIP
                tiles = [jnp.exp2(s_scr[slot][mp][lo + 8 * u:lo + 8 * (u + 1), :] - shift[mp])
                         for u in range(ATT_STRIP // 8)]
                p_scr[mp][lo:lo + ATT_STRIP, :] = jnp.concatenate(tiles, axis=0).astype(BF16)
                part = part + functools.reduce(lambda x, y: x + y, tiles)
            sums[mp] = part
            acc_scr[mp][...] += jnp.dot(vct, p_scr[mp][0:width, :], preferred_element_type=F32)
        return tuple(sums)

    def k_chunk(j):
        return k_ref[0, 0, pl.ds(pl.multiple_of(j * tk, tk), tk), :]

    def vt_chunk(j):
        return vt_ref[0, 0, :, pl.ds(pl.multiple_of(j * tk, tk), tk)]

    scores(0, k_chunk(0))
    shift = tuple(jnp.broadcast_to(jnp.max(s_scr[0][mp][...], axis=0, keepdims=True), (8, tq)) for mp in range(2))
    for mp in range(2):
        acc_scr[mp][...] = jnp.zeros((LANES, tq), F32)
    sums = (jnp.zeros((8, tq), F32), jnp.zeros((8, tq), F32))

    def pair(jj, sums):
        j0 = 2 * jj
        scores(1, k_chunk(j0 + 1))
        sums = absorb(0, vt_chunk(j0), shift, sums)
        scores(0, k_chunk(j0 + 2))
        return absorb(1, vt_chunk(j0 + 1), shift, sums)

    sums = lax.fori_loop(0, n_chunks // 2 - 1, pair, sums)
    scores(1, k_chunk(n_chunks - 1))
    sums = absorb(0, vt_chunk(n_chunks - 2), shift, sums)
    scores(0, kc_ref[0, 0])
    sums = absorb(1, vt_chunk(n_chunks - 1), shift, sums)
    sums = absorb(0, vct_ref[0, 0], shift, sums)

    tot = [jnp.sum(sums[mp], axis=0, keepdims=True) for mp in range(2)]
    bad = sum(jnp.sum(jnp.where(jnp.isfinite(x), 0.0, 1.0)) for x in (tot[0], tot[1], acc0[...], acc1[...]))

    @pl.when(bad == 0.0)
    def _():
        o_ref[0] = _attn_finish(acc0[...], tot[0], acc1[...], tot[1], lamp_ref, g_ref, lam_init)

    @pl.when(bad != 0.0)
    def _():
        def update(carry, kc, vct):
            new = []
            for mp in range(2):
                m, l, acc = carry[mp]
                s = lax.dot_general(kc, qs[mp], nt, preferred_element_type=F32)
                mn = jnp.maximum(m, jnp.max(s, axis=0, keepdims=True))
                a = jnp.exp2(m - mn)
                p = jnp.exp2(s - mn)
                new.append((mn, a * l + jnp.sum(p, axis=0, keepdims=True),
                            a * acc + jnp.dot(vct, p.astype(BF16), preferred_element_type=F32)))
            return tuple(new)

        init = tuple((jnp.full((1, tq), NEG_INF, F32), jnp.zeros((1, tq), F32), jnp.zeros((LANES, tq), F32))
                     for _ in range(2))
        carry = lax.fori_loop(0, n_chunks, lambda j, c: update(c, k_chunk(j), vt_chunk(j)), init)
        (_, l_a, acc_a), (_, l_b, acc_b) = update(carry, kc_ref[0, 0], vct_ref[0, 0])
        o_ref[0] = _attn_finish(acc_a, l_a, acc_b, l_b, lamp_ref, g_ref, lam_init)


def _diff_attention(q, k, vt, kc, vct, lam_p, subln_g, lam_init, tq, tk):
    b, h, n, _ = q.shape
    c = kc.shape[2]
    assert n % tk == 0 and (n // tk) % 2 == 0 and c <= tk and tk % ATT_STRIP == 0 and c % ATT_STRIP == 0
    spec = lambda r, cols: pl.BlockSpec((1, 1, r, cols), lambda bi, hi, i: (bi, hi, 0, 0))
    return pl.pallas_call(
        functools.partial(_attn_kernel, tk=tk, lam_init=lam_init),
        out_shape=jax.ShapeDtypeStruct((b, n, h * LANES), BF16),
        grid=(b, h, n // tq),
        in_specs=[
            pl.BlockSpec((1, 1, tq, LANES), lambda bi, hi, i: (bi, hi, i, 0)),
            spec(n, LANES), spec(LANES, n), spec(c, LANES), spec(LANES, c),
            pl.BlockSpec(lam_p.shape, lambda bi, hi, i: (0, 0)),
            pl.BlockSpec((LANES, 1), lambda bi, hi, i: (0, 0)),
        ],
        out_specs=pl.BlockSpec((1, tq, LANES), lambda bi, hi, i: (bi, i, hi)),
        scratch_shapes=[
            *[pltpu.VMEM((tk, tq), F32)] * 4,
            *[pltpu.VMEM((tk, tq), BF16)] * 2,
            *[pltpu.VMEM((LANES, tq), F32)] * 2,
        ],
        compiler_params=_cparams("parallel", "parallel", "parallel"),
        name="diff_attention",
    )(q, k, vt, kc, vct, lam_p, subln_g.reshape(LANES, 1))


def _sgu_kernel(x_ref, sc_ref, sh_ref, w_ref, b_ref, ng_ref, nb_ref, ws_ref, bs_ref, t_ref):
    f = t_ref.shape[-1]
    cg = f // SGU_GROUPS
    tm = x_ref.shape[1]
    h = (x_ref[0] * (1.0 + sc_ref[0]) + sh_ref[0]).astype(BF16)
    z = jnp.dot(h, w_ref[...], preferred_element_type=F32) + b_ref[...]
    z = 0.5 * z * (1.0 + lax.erf(z * (2.0 ** -0.5)))
    u = z[:, :f]
    v = _layer_norm(z[:, f:], ng_ref[...], nb_ref[...]).astype(BF16)
    for c in range(tm // SGU_CHUNK):
        rows = slice(c * SGU_CHUNK, (c + 1) * SGU_CHUNK)
        for g in range(SGU_GROUPS):
            cols = slice(g * cg, (g + 1) * cg)
            vm = jnp.dot(ws_ref[g], v[rows, cols], preferred_element_type=F32) + bs_ref[:, g:g + 1]
            t_ref[0, rows, cols] = (u[rows, cols] * vm).astype(BF16)


def _sgu_mixer(x, sc, sh, w_in_bf, b_in, norm_g, norm_b, w_s_bf, b_s_t, tm):
    b, n, d = x.shape
    f2 = w_in_bf.shape[1]
    f = f2 // 2
    vec = pl.BlockSpec((1, 1, d), lambda bi, i: (bi, 0, 0))
    full2 = lambda a: pl.BlockSpec(a.shape, lambda bi, i: (0,) * a.ndim)
    b_in2, ng2, nb2 = b_in.reshape(1, f2), norm_g.reshape(1, f), norm_b.reshape(1, f)
    return pl.pallas_call(
        _sgu_kernel,
        out_shape=jax.ShapeDtypeStruct((b, n, f), BF16),
        grid=(b, n // tm),
        in_specs=[
            pl.BlockSpec((1, tm, d), lambda bi, i: (bi, i, 0)),
            vec, vec,
            full2(w_in_bf), full2(b_in2), full2(ng2), full2(nb2), full2(w_s_bf), full2(b_s_t),
        ],
        out_specs=pl.BlockSpec((1, tm, f), lambda bi, i: (bi, i, 0)),
        compiler_params=_cparams("parallel", "parallel"),
        name="sgu_mixer",
    )(x, sc, sh, w_in_bf, b_in2, ng2, nb2, w_s_bf, b_s_t)


def _post_kernel(pre_ref, w_ref, x_ref, gm_ref, lg_ref, lb_ref, scf_ref, shf_ref, x1_ref, hf_ref, *, alpha):
    y = jnp.dot(pre_ref[0], w_ref[...], preferred_element_type=F32)
    x1 = _layer_norm(alpha * x_ref[0] + gm_ref[0] * y, lg_ref[...], lb_ref[...])
    x1_ref[0] = x1
    hf_ref[0] = _pack_bf16_pairs(x1 * (1.0 + scf_ref[0]) + shf_ref[0])


def _post_mixer(pre, w_bf, x, gm, ln_g, ln_b, scf, shf, alpha, tm):
    b, n, d = x.shape
    kd = pre.shape[-1]
    vec = pl.BlockSpec((1, 1, d), lambda bi, i: (bi, 0, 0))
    row = pl.BlockSpec((1, d), lambda bi, i: (0, 0))
    tile = pl.BlockSpec((1, tm, d), lambda bi, i: (bi, i, 0))
    return pl.pallas_call(
        functools.partial(_post_kernel, alpha=alpha),
        out_shape=(jax.ShapeDtypeStruct((b, n, d), F32), jax.ShapeDtypeStruct((b, n, d // 2), I32)),
        grid=(b, n // tm),
        in_specs=[
            pl.BlockSpec((1, tm, kd), lambda bi, i: (bi, i, 0)),
            pl.BlockSpec((kd, d), lambda bi, i: (0, 0)),
            tile, vec, row, row, vec, vec,
        ],
        out_specs=(tile, pl.BlockSpec((1, tm, d // 2), lambda bi, i: (bi, i, 0))),
        compiler_params=_cparams("parallel", "parallel"),
        name="post_mixer",
    )(pre, w_bf, x, gm, ln_g.reshape(1, d), ln_b.reshape(1, d), scf, shf)


def _route_select(scores, choice):
    e, w = scores.shape
    ge = e // N_GROUPS
    g3 = choice.reshape(N_GROUPS, ge, w)
    ri = lax.broadcasted_iota(jnp.int32, g3.shape, 1).astype(F32)
    m1 = jnp.max(g3, axis=1, keepdims=True)
    first = jnp.min(jnp.where(g3 == m1, ri, float(ge)), axis=1, keepdims=True)
    m2 = jnp.max(jnp.where(ri == first, NEG_INF, g3), axis=1, keepdims=True)
    gs = m1 + m2

    gi = lax.broadcasted_iota(jnp.int32, gs.shape, 0).astype(F32)
    gsel = jnp.zeros(gs.shape, F32)
    cur = gs
    for _ in range(TOPK_GROUPS):
        m = jnp.max(cur, axis=0, keepdims=True)
        f = jnp.min(jnp.where(cur == m, gi, float(N_GROUPS)), axis=0, keepdims=True)
        hit = gi == f
        gsel = jnp.where(hit, 1.0, gsel)
        cur = jnp.where(hit, NEG_INF, cur)
    emask = jnp.broadcast_to(gsel, g3.shape).reshape(e, w)
    masked = jnp.where(emask > 0.5, choice, NEG_INF)

    ei = lax.broadcasted_iota(jnp.int32, (e, w), 0).astype(F32)
    onehot = jnp.zeros((e, w), F32)
    idxs, ws = [], []
    for _ in range(TOP_K):
        m = jnp.max(masked, axis=0, keepdims=True)
        f = jnp.min(jnp.where(masked == m, ei, float(e)), axis=0, keepdims=True)
        hit = ei == f
        idxs.append(f)
        ws.append(jnp.sum(jnp.where(hit, scores, 0.0), axis=0, keepdims=True))
        masked = jnp.where(hit, NEG_INF, masked)
        onehot = jnp.where(hit, 1.0, onehot)
    return idxs, ws, onehot


def _route_kernel(x_ref, sc_ref, sh_ref, wr_ref, rb_ref, idx_ref, w_ref, rank_ref, cnt_ref, carry_ref):
    i = pl.program_id(0)
    e = wr_ref.shape[0]
    tm = x_ref.shape[0]

    @pl.when(i == 0)
    def _():
        carry_ref[...] = jnp.zeros_like(carry_ref)

    h = x_ref[...] * (1.0 + sc_ref[0]) + sh_ref[0]
    logits = lax.dot_general(wr_ref[...], h, (((1,), (1,)), ((), ())),
                             precision=lax.Precision.HIGHEST, preferred_element_type=F32)
    scores = 1.0 / (1.0 + jnp.exp(-logits))
    choice = scores + rb_ref[...]

    slab = min(ROUTE_SLAB, tm)
    slabs = [slice(j * slab, (j + 1) * slab) for j in range(tm // slab)]
    picks = [_route_select(scores[:, sl], choice[:, sl]) for sl in slabs]
    onehot = jnp.concatenate([pk[2] for pk in picks], axis=1)

    r_i = lax.broadcasted_iota(jnp.int32, (tm, tm), 0)
    c_i = lax.broadcasted_iota(jnp.int32, (tm, tm), 1)
    upper = jnp.where(r_i < c_i, 1.0, 0.0).astype(BF16)
    rk = jnp.dot(onehot.astype(BF16), upper, preferred_element_type=F32) + carry_ref[...]
    carry_ref[...] += jnp.sum(onehot, axis=1, keepdims=True)

    ei = lax.broadcasted_iota(jnp.int32, (e, slab), 0).astype(F32)
    for sl, (idxs, ws, _) in zip(slabs, picks):
        wsum = functools.reduce(lambda x, y: x + y, ws)
        for k in range(TOP_K):
            idx_ref[k:k + 1, sl] = idxs[k].astype(jnp.int32)
            w_ref[k:k + 1, sl] = ws[k] / wsum * ROUTED_SCALE
            rank_ref[k:k + 1, sl] = jnp.sum(jnp.where(ei == idxs[k], rk[:, sl], 0.0), axis=0,
                                            keepdims=True).astype(jnp.int32)
    cnt_ref[...] = jnp.broadcast_to(carry_ref[...], cnt_ref.shape).astype(jnp.int32)


def _route(x1, scf, shf, wr_t, rbias, tm):
    b, n, d = x1.shape
    t = b * n
    e = wr_t.shape[0]
    per_b = n // tm
    vec = pl.BlockSpec((1, 1, d), lambda i: (i // per_b, 0, 0))
    out_t = pl.BlockSpec((TOP_K, tm), lambda i: (0, i))
    return pl.pallas_call(
        _route_kernel,
        out_shape=(jax.ShapeDtypeStruct((TOP_K, t), jnp.int32), jax.ShapeDtypeStruct((TOP_K, t), F32),
                   jax.ShapeDtypeStruct((TOP_K, t), jnp.int32), jax.ShapeDtypeStruct((e, LANES), jnp.int32)),
        grid=(t // tm,),
        in_specs=[
            pl.BlockSpec((tm, d), lambda i: (i, 0)),
            vec, vec,
            pl.BlockSpec((e, d), lambda i: (0, 0)),
            pl.BlockSpec((e, 1), lambda i: (0, 0)),
        ],
        out_specs=(out_t, out_t, out_t, pl.BlockSpec((e, LANES), lambda i: (0, 0))),
        scratch_shapes=[pltpu.VMEM((e, 1), F32)],
        compiler_params=_cparams("arbitrary"),
        name="route",
    )(x1.reshape(t, d), scf, shf, wr_t, rbias.reshape(e, 1))


def _gather_rows(table, idx):
    m = idx.shape[0]
    w = table.shape[1]
    workers = SC_CORES * SC_SUBCORES
    n_ch = m // (workers * SC_CHUNK)
    assert m % (workers * SC_CHUNK) == 0 and n_ch % 2 == 0
    mesh = plsc.VectorSubcoreMesh(core_axis_name="c", subcore_axis_name="s",
                                  num_cores=SC_CORES, num_subcores=SC_SUBCORES)

    @functools.partial(
        pl.kernel, mesh=mesh,
        out_type=jax.ShapeDtypeStruct((m, w), table.dtype),
        scratch_types=[
            pltpu.VMEM((n_ch, SC_CHUNK), I32),
            pltpu.VMEM((SC_CHUNK, w), table.dtype), pltpu.VMEM((SC_CHUNK, w), table.dtype),
            pltpu.SemaphoreType.DMA, pltpu.SemaphoreType.DMA, pltpu.SemaphoreType.DMA, pltpu.SemaphoreType.DMA,
        ],
        name="sc_gather_rows",
    )
    def gather(table_hbm, idx_hbm, out_hbm, idx_all, buf0, buf1, gsem0, gsem1, wsem0, wsem1):
        first = (lax.axis_index("s") * SC_CORES + lax.axis_index("c")) * n_ch
        bufs, gsem, wsem = (buf0, buf1), (gsem0, gsem1), (wsem0, wsem1)
        pltpu.sync_copy(idx_hbm.at[pl.ds(first, n_ch)], idx_all)

        def gather_copy(j, s):
            return pltpu.make_async_copy(table_hbm.at[idx_all.at[j]], bufs[s], gsem[s])

        def write_copy(j, s):
            rows = pl.ds(pl.multiple_of((first + j) * SC_CHUNK, SC_CHUNK), SC_CHUNK)
            return pltpu.make_async_copy(bufs[s], out_hbm.at[rows], wsem[s])

        gather_copy(0, 0).start()

        @pl.loop(0, n_ch, step=2)
        def _(jj):
            for s in range(2):
                j = jj + s

                @pl.when(j >= 1)
                def _():
                    write_copy(j - 1, 1 - s).wait()

                @pl.when(j + 1 < n_ch)
                def _():
                    gather_copy(j + 1, 1 - s).start()

                gather_copy(j, s).wait()
                write_copy(j, s).start()

        write_copy(n_ch - 1, 1).wait()

    return gather(table, idx.reshape(m // SC_CHUNK, SC_CHUNK))


def _scatter_rows(src, pos3, p):
    w = src.shape[1]
    n_chunks, k, ch = pos3.shape
    workers = SC_CORES * SC_SUBCORES
    per_w = n_chunks // workers
    assert ch == SC_CHUNK and n_chunks % workers == 0 and src.shape[0] == n_chunks * ch
    mesh = plsc.VectorSubcoreMesh(core_axis_name="c", subcore_axis_name="s",
                                  num_cores=SC_CORES, num_subcores=SC_SUBCORES)

    @functools.partial(
        pl.kernel, mesh=mesh,
        out_type=jax.ShapeDtypeStruct((p, w), src.dtype),
        scratch_types=[
            pltpu.VMEM((k, ch), I32),
            pltpu.VMEM((ch, w), src.dtype),
            pltpu.SemaphoreType.DMA,
        ],
        name="sc_scatter_rows",
    )
    def scatter(src_hbm, pos_hbm, out_hbm, idx_v, rows_v, sem):
        first = (lax.axis_index("s") * SC_CORES + lax.axis_index("c")) * per_w

        @pl.loop(0, per_w)
        def _(j):
            c = first + j
            pltpu.sync_copy(pos_hbm.at[c], idx_v)
            pltpu.sync_copy(src_hbm.at[pl.ds(pl.multiple_of(c * ch, ch), ch)], rows_v)
            copies = [pltpu.async_copy(rows_v, out_hbm.at[idx_v.at[kk]], sem) for kk in range(k)]
            for cp in copies:
                cp.wait()

    return scatter(src, pos3)


def _experts_kernel(ps_ref, nb_ref, cnt_ref, nt_ref, xs_hbm, wg_ref, wu_ref, wd_ref, y_hbm,
                    xbuf, ybuf, in_sem, out_sem, wg_bf, wu_bf, wd_bf):
    e = pl.program_id(0)
    nb, cnt, n_total = nb_ref[e], cnt_ref[e], nt_ref[0]
    g0 = ps_ref[e] // MOE_ROWS

    def in_copy(g):
        rows = pl.ds(pl.multiple_of(g * MOE_ROWS, MOE_ROWS), MOE_ROWS)
        return pltpu.make_async_copy(xs_hbm.at[rows], xbuf.at[g % MOE_SLOTS], in_sem.at[g % MOE_SLOTS])

    def out_copy(g):
        rows = pl.ds(pl.multiple_of(g * MOE_ROWS, MOE_ROWS), MOE_ROWS)
        return pltpu.make_async_copy(ybuf.at[g % MOE_SLOTS], y_hbm.at[rows], out_sem.at[g % MOE_SLOTS])

    @pl.when(e == 0)
    def _():
        for j in range(MOE_LOOKAHEAD):
            @pl.when(j < n_total)
            def _():
                in_copy(j).start()

    @pl.when(nb > 0)
    def _():
        wg_bf[...] = wg_ref[0, 0].astype(BF16)
        wu_bf[...] = wu_ref[0, 0].astype(BF16)
        wd_bf[...] = wd_ref[0, 0].astype(BF16)

        def process(b, width):
            g = g0 + b
            for u in range(width):
                @pl.when(g + u + MOE_LOOKAHEAD < n_total)
                def _():
                    in_copy(g + u + MOE_LOOKAHEAD).start()

            for u in range(width):
                in_copy(g + u).wait()

                @pl.when(g + u >= MOE_SLOTS)
                def _():
                    out_copy(g + u - MOE_SLOTS).wait()

            packed = jnp.concatenate([xbuf[(g + u) % MOE_SLOTS] for u in range(width)], axis=0)
            row = lax.broadcasted_iota(I32, (width * MOE_ROWS, 1), 0) + b * MOE_ROWS
            x_lo, x_hi = (v.astype(BF16) for v in _unpack_bf16_pairs(jnp.where(row < cnt, packed, 0)))
            half = x_lo.shape[1]

            def up(w_bf):
                return (jnp.dot(x_lo, w_bf[:half, :], preferred_element_type=F32)
                        + jnp.dot(x_hi, w_bf[half:, :], preferred_element_type=F32))

            hb = (_silu(up(wg_bf)) * up(wu_bf)).astype(BF16)
            y = _pack_bf16_pairs(jnp.dot(hb, wd_bf[...], preferred_element_type=F32))
            for u in range(width):
                ybuf[(g + u) % MOE_SLOTS] = y[u * MOE_ROWS:(u + 1) * MOE_ROWS]
                out_copy(g + u).start()

        start = 0
        for width in MOE_GROUPS:
            count = (nb - start) // width

            def body(i, c, width=width, start=start):
                process(start + width * i, width)
                return c

            lax.fori_loop(0, count, body, 0)
            start = start + count * width

    @pl.when(e == pl.num_programs(0) - 1)
    def _():
        for j in range(MOE_SLOTS):
            @pl.when(n_total - 1 - j >= 0)
            def _():
                out_copy(n_total - 1 - j).wait()


def _routed_experts(xs, wg, wu, wd, layer, pstarts, nblk, counts, n_total):
    p, dp = xs.shape
    _, e, d, f = wg.shape
    wspec = lambda r, c: pl.BlockSpec((1, 1, r, c), lambda i, ps, nb, cnt, nt: (layer, i, 0, 0))
    grid_spec = pltpu.PrefetchScalarGridSpec(
        num_scalar_prefetch=4,
        grid=(e,),
        in_specs=[pl.BlockSpec(memory_space=pl.ANY), wspec(d, f), wspec(d, f), wspec(f, d)],
        out_specs=pl.BlockSpec(memory_space=pl.ANY),
        scratch_shapes=[
            pltpu.VMEM((MOE_SLOTS, MOE_ROWS, dp), I32), pltpu.VMEM((MOE_SLOTS, MOE_ROWS, dp), I32),
            pltpu.SemaphoreType.DMA((MOE_SLOTS,)), pltpu.SemaphoreType.DMA((MOE_SLOTS,)),
            pltpu.VMEM((d, f), BF16), pltpu.VMEM((d, f), BF16), pltpu.VMEM((f, d), BF16),
        ],
    )
    return pl.pallas_call(
        _experts_kernel,
        out_shape=jax.ShapeDtypeStruct((p, dp), I32),
        grid_spec=grid_spec,
        compiler_params=_cparams("arbitrary"),
        name="routed_experts",
    )(pstarts, nblk, counts, n_total, xs, wg, wu, wd)


def _combine_kernel(yg_ref, w_ref, hf_ref, sg_ref, su_ref, sd_ref, x_ref, gf_ref, lg_ref, lb_ref, o_ref, *, alpha):
    w = w_ref[...]
    r_lo, r_hi = _unpack_bf16_pairs(yg_ref[0])
    r_lo, r_hi = w[:, 0:1] * r_lo, w[:, 0:1] * r_hi
    for k in range(1, TOP_K):
        y_lo, y_hi = _unpack_bf16_pairs(yg_ref[k])
        r_lo, r_hi = r_lo + w[:, k:k + 1] * y_lo, r_hi + w[:, k:k + 1] * y_hi
    routed = jnp.concatenate([r_lo, r_hi], axis=1)
    hf = jnp.concatenate(_unpack_bf16_pairs(hf_ref[...]), axis=1).astype(BF16)
    g = jnp.dot(hf, sg_ref[...], preferred_element_type=F32)
    u = jnp.dot(hf, su_ref[...], preferred_element_type=F32)
    shared = jnp.dot((_silu(g) * u).astype(BF16), sd_ref[...], preferred_element_type=F32)
    o_ref[...] = _layer_norm(alpha * x_ref[...] + gf_ref[0] * (routed + shared), lg_ref[...], lb_ref[...])


def _combine(yg, w_tk, hf, sg_bf, su_bf, sd_bf, x1, gf, ln_g, ln_b, alpha, tm, per_b):
    t, d = x1.shape
    f = sg_bf.shape[1]
    row = pl.BlockSpec((1, d), lambda i: (0, 0))
    tile = pl.BlockSpec((tm, d), lambda i: (i, 0))
    return pl.pallas_call(
        functools.partial(_combine_kernel, alpha=alpha),
        out_shape=jax.ShapeDtypeStruct((t, d), F32),
        grid=(t // tm,),
        in_specs=[
            pl.BlockSpec((TOP_K, tm, d // 2), lambda i: (0, i, 0)),
            pl.BlockSpec((tm, TOP_K), lambda i: (i, 0)),
            pl.BlockSpec((tm, d // 2), lambda i: (i, 0)),
            pl.BlockSpec((d, f), lambda i: (0, 0)),
            pl.BlockSpec((d, f), lambda i: (0, 0)),
            pl.BlockSpec((f, d), lambda i: (0, 0)),
            tile,
            pl.BlockSpec((1, 1, d), lambda i: (i // per_b, 0, 0)),
            row, row,
        ],
        out_specs=tile,
        compiler_params=_cparams("parallel"),
        name="moe_combine",
    )(yg, w_tk, hf, sg_bf, su_bf, sd_bf, x1, gf, ln_g.reshape(1, d), ln_b.reshape(1, d))


def _moe_layer(x1, hf, scf, shf, gf, router_w, router_bias, wg, wu, wd, layer, sg, su, sd, ln_g, ln_b, alpha, tm):
    b, n, d = x1.shape
    t = b * n
    e = router_w.shape[1]
    idx_t, w_t, rank_t, cnt = _route(x1, scf, shf, router_w.T, router_bias, tm)

    counts = cnt[:, 0]
    padded = (counts + MOE_ROWS - 1) // MOE_ROWS * MOE_ROWS
    pends = jnp.cumsum(padded)
    pstarts = pends - padded
    sel = idx_t[:, :, None] == jnp.arange(e, dtype=I32)
    pos_t = jnp.sum(jnp.where(sel, pstarts, 0), axis=-1) + rank_t
    p = t * TOP_K + e * MOE_ROWS
    pos3 = pos_t.reshape(TOP_K, t // SC_CHUNK, SC_CHUNK).transpose(1, 0, 2)

    hf2 = hf.reshape(t, d // 2)
    xs = _scatter_rows(hf2, pos3, p)
    yb = _routed_experts(xs, wg, wu, wd, layer, pstarts.astype(I32), (padded // MOE_ROWS).astype(I32), counts,
                         (pends[-1:] // MOE_ROWS).astype(I32))
    yg = _gather_rows(yb, pos_t.reshape(-1)).reshape(TOP_K, t, d // 2)
    out = _combine(yg, w_t.T, hf2, sg.astype(BF16), su.astype(BF16), sd.astype(BF16), x1.reshape(t, d), gf,
                   ln_g, ln_b, alpha, tm, n // tm)
    return out.reshape(b, n, d)


def _rope_tables(n):
    rows = n // GRID_W
    row_pos = jnp.repeat(jnp.arange(rows, dtype=F32), GRID_W)
    col_pos = jnp.tile(jnp.arange(GRID_W, dtype=F32), rows)
    half = LANES // 4
    lane = jnp.arange(LANES)
    in_blk = lane % half
    freq = ROPE_THETA ** (-(2.0 * (in_blk % (half // 2)).astype(F32)) / half)
    use_col = (lane // half) % 2 == 1
    pos = jnp.where(use_col[None, :], col_pos[:, None], row_pos[:, None])
    ang = pos * freq[None, :]
    lo = (in_blk < half // 2)[None, :]
    sin = jnp.sin(ang)
    return jnp.cos(ang), jnp.where(lo, -sin, 0.0), jnp.where(lo, 0.0, sin)


def kernel(x, c, ctx, c_ctx, w_mod, b_mod, ln_g, ln_b, attn_w_in, attn_w_out, attn_lambda, attn_subln_g,
           sgu_w_in, sgu_b_in, sgu_norm_g, sgu_norm_b, sgu_w_s, sgu_b_s, sgu_w_out,
           router_w, router_bias, exp_w_gate, exp_w_up, exp_w_down, sh_w_gate, sh_w_up, sh_w_down):
    b, n, d = x.shape
    depth = w_mod.shape[0]
    assert b <= 7 and d == ATT_HEADS * LANES and n % GRID_W == 0
    alpha = (2 * depth) ** 0.25
    head_dim = d // ATT_HEADS // 2
    tm = TOKEN_TILE if n % TOKEN_TILE == 0 else TOKEN_TILE // 2

    cs = jnp.zeros((8, d), F32).at[:b].set(c).at[b].set(c_ctx)
    mods = _modulation(cs, w_mod, b_mod)

    def mod_vec(i, j):
        return mods[i, :, j * d:(j + 1) * d].reshape(8, 1, d)

    for i in range(depth):
        sh_m, sc_m, g_m, sh_f, sc_f, g_f = (mod_vec(i, j) for j in range(6))
        if i % N_MIXERS == 0:
            a = i // N_MIXERS
            lam_init = 0.8 - 0.6 * math.exp(-0.3 * i)
            w_in_bf = attn_w_in[a].astype(BF16)
            cos, slo, shi = _rope_tables(n)
            q_scale = head_dim ** -0.5 * math.log2(math.e)
            q, k, vt = _qkv_proj(x, sc_m, sh_m, w_in_bf, cos, slo, shi, q_scale, tm)
            kc, vct = _kv_ctx_proj(ctx, sc_m[b:b + 1], sh_m[b:b + 1], w_in_bf[:, d:])
            pre = _diff_attention(q, k, vt, kc, vct, attn_lambda[a], attn_subln_g[a], lam_init,
                                  min(ATT_QUERY_TILE, n), min(ATT_KEY_CHUNK, n // 2))
            w_out_bf = attn_w_out[a].astype(BF16)
        else:
            s = i // N_MIXERS
            pre = _sgu_mixer(x, sc_m, sh_m, sgu_w_in[s].astype(BF16), sgu_b_in[s], sgu_norm_g[s], sgu_norm_b[s],
                             sgu_w_s[s].astype(BF16), sgu_b_s[s].T, tm)
            w_out_bf = sgu_w_out[s].astype(BF16)
        x1, hf = _post_mixer(pre, w_out_bf, x, g_m, ln_g[i, 0], ln_b[i, 0], sc_f, sh_f, alpha, tm)
        x = _moe_layer(x1, hf, sc_f, sh_f, g_f, router_w[i], router_bias[i], exp_w_gate, exp_w_up, exp_w_down, i,
                       sh_w_gate[i], sh_w_up[i], sh_w_down[i], ln_g[i, 1], ln_b[i, 1], alpha, tm)
    return x
```

```python
import functools
import math

import jax
import jax.numpy as jnp
from jax import lax
from jax.experimental import pallas as pl
from jax.experimental.pallas import tpu as pltpu
from jax.experimental.pallas import tpu_sc as plsc

F32 = jnp.float32
BF16 = jnp.bfloat16
I32 = jnp.int32

GRID_W = 64
ATT_HEADS = 8
ROPE_THETA = 10000.0
SGU_CHUNK = 128
SGU_GROUPS = 8
TOP_K = 8
N_GROUPS = 8
TOPK_GROUPS = 4
ROUTED_SCALE = 2.5
LN_EPS = 1e-5
N_MIXERS = 2

LANES = 128
MOE_ROWS = 256
ROUTE_SLAB = 512
MOE_GROUPS = (2, 1)
MOE_LOOKAHEAD = 8
MOE_SLOTS = MOE_LOOKAHEAD + MOE_GROUPS[0]
TOKEN_TILE = 512
ATT_QUERY_TILE = 1024
ATT_KEY_CHUNK = 1024
ATT_STRIP = 16
SC_CORES = 2
SC_SUBCORES = 16
SC_CHUNK = 64
VMEM_LIMIT = 56 * 1024 * 1024
NEG_INF = float("-inf")


def _cparams(*sem):
    return pltpu.CompilerParams(dimension_semantics=sem, vmem_limit_bytes=VMEM_LIMIT)


def _layer_norm(z, g, b):
    mu = jnp.mean(z, axis=-1, keepdims=True)
    zc = z - mu
    var = jnp.mean(zc * zc, axis=-1, keepdims=True)
    return zc * lax.rsqrt(var + LN_EPS) * g + b


def _silu(x):
    return x * (1.0 / (1.0 + jnp.exp(-x)))


_HIGH_HALF = -65536


def _pack_bf16_pairs(y):
    w = y.shape[1] // 2
    bits = lax.bitcast_convert_type(y.astype(BF16).astype(F32), I32)
    return lax.shift_right_logical(bits[:, :w], 16) | (bits[:, w:] & _HIGH_HALF)


def _unpack_bf16_pairs(p):
    return (lax.bitcast_convert_type(lax.shift_left(p, 16), F32),
            lax.bitcast_convert_type(p & _HIGH_HALF, F32))


def _mod_kernel(cs_ref, w_ref, b_ref, o_ref):
    s = _silu(cs_ref[...])
    o_ref[0] = jnp.dot(s, w_ref[0], precision=lax.Precision.HIGHEST,
                       preferred_element_type=F32) + b_ref[0]


def _modulation(cs, w_mod, b_mod):
    depth, d, n6 = w_mod.shape
    tn = n6 // 4
    return pl.pallas_call(
        _mod_kernel,
        out_shape=jax.ShapeDtypeStruct((depth, 8, n6), F32),
        grid=(depth, n6 // tn),
        in_specs=[
            pl.BlockSpec((8, d), lambda l, j: (0, 0)),
            pl.BlockSpec((1, d, tn), lambda l, j: (l, 0, j)),
            pl.BlockSpec((1, 1, tn), lambda l, j: (l, 0, j)),
        ],
        out_specs=pl.BlockSpec((1, 8, tn), lambda l, j: (l, 0, j)),
        compiler_params=_cparams("parallel", "parallel"),
        name="modulation",
    )(cs, w_mod, b_mod.reshape(depth, 1, n6))


def _rope(xh, cos, sin_lo, sin_hi):
    return xh * cos + pltpu.roll(xh, LANES - 16, 1) * sin_lo + pltpu.roll(xh, 16, 1) * sin_hi


def _qkv_kernel(x_ref, sc_ref, sh_ref, w_ref, cos_ref, slo_ref, shi_ref, q_ref, k_ref, v_ref, *, q_scale):
    d = x_ref.shape[-1]
    h = (x_ref[0] * (1.0 + sc_ref[0]) + sh_ref[0]).astype(BF16)
    cos, slo, shi = cos_ref[...], slo_ref[...], shi_ref[...]
    q = jnp.dot(h, w_ref[:, 0:d], preferred_element_type=F32)
    for hd in range(ATT_HEADS):
        q_ref[0, hd] = (_rope(q[:, hd * LANES:(hd + 1) * LANES], cos, slo, shi) * q_scale).astype(BF16)
    k = jnp.dot(h, w_ref[:, d:2 * d], preferred_element_type=F32)
    for hd in range(ATT_HEADS):
        k_ref[0, hd] = _rope(k[:, hd * LANES:(hd + 1) * LANES], cos, slo, shi).astype(BF16)
    v = jnp.dot(h, w_ref[:, 2 * d:3 * d], preferred_element_type=F32)
    for hd in range(ATT_HEADS):
        v_ref[0, hd] = v[:, hd * LANES:(hd + 1) * LANES].T.astype(BF16)


def _qkv_proj(x, sc, sh, w_bf, cos, slo, shi, q_scale, tn):
    b, n, d = x.shape
    hd_shape = jax.ShapeDtypeStruct((b, ATT_HEADS, n, LANES), BF16)
    vec = pl.BlockSpec((1, 1, d), lambda bi, i: (bi, 0, 0))
    tab = pl.BlockSpec((tn, LANES), lambda bi, i: (i, 0))
    out = pl.BlockSpec((1, ATT_HEADS, tn, LANES), lambda bi, i: (bi, 0, i, 0))
    out_t = pl.BlockSpec((1, ATT_HEADS, LANES, tn), lambda bi, i: (bi, 0, 0, i))
    return pl.pallas_call(
        functools.partial(_qkv_kernel, q_scale=q_scale),
        out_shape=(hd_shape, hd_shape, jax.ShapeDtypeStruct((b, ATT_HEADS, LANES, n), BF16)),
        grid=(b, n // tn),
        in_specs=[
            pl.BlockSpec((1, tn, d), lambda bi, i: (bi, i, 0)),
            vec, vec,
            pl.BlockSpec((d, 3 * d), lambda bi, i: (0, 0)),
            tab, tab, tab,
        ],
        out_specs=(out, out, out_t),
        compiler_params=_cparams("parallel", "parallel"),
        name="qkv_proj",
    )(x, sc, sh, w_bf, cos, slo, shi)


def _kv_ctx_kernel(x_ref, sc_ref, sh_ref, w_ref, k_ref, v_ref):
    d = x_ref.shape[-1]
    h = (x_ref[0] * (1.0 + sc_ref[0]) + sh_ref[0]).astype(BF16)
    k = jnp.dot(h, w_ref[:, 0:d], preferred_element_type=F32)
    v = jnp.dot(h, w_ref[:, d:2 * d], preferred_element_type=F32)
    for hd in range(ATT_HEADS):
        k_ref[0, hd] = k[:, hd * LANES:(hd + 1) * LANES].astype(BF16)
        v_ref[0, hd] = v[:, hd * LANES:(hd + 1) * LANES].T.astype(BF16)


def _kv_ctx_proj(ctx, sc, sh, w_kv_bf):
    b, c, d = ctx.shape
    hd_shape = jax.ShapeDtypeStruct((b, ATT_HEADS, c, LANES), BF16)
    vec = pl.BlockSpec((1, 1, d), lambda bi: (0, 0, 0))
    out = pl.BlockSpec((1, ATT_HEADS, c, LANES), lambda bi: (bi, 0, 0, 0))
    out_t = pl.BlockSpec((1, ATT_HEADS, LANES, c), lambda bi: (bi, 0, 0, 0))
    return pl.pallas_call(
        _kv_ctx_kernel,
        out_shape=(hd_shape, jax.ShapeDtypeStruct((b, ATT_HEADS, LANES, c), BF16)),
        grid=(b,),
        in_specs=[
            pl.BlockSpec((1, c, d), lambda bi: (bi, 0, 0)),
            vec, vec,
            pl.BlockSpec((d, 2 * d), lambda bi: (0, 0)),
        ],
        out_specs=(out, out_t),
        compiler_params=_cparams("parallel"),
        name="kv_ctx_proj",
    )(ctx, sc, sh, w_kv_bf)


def _attn_finish(acc_a, l_a, acc_b, l_b, lamp_ref, g_ref, lam_init):
    lp = lamp_ref[...]
    lam = (jnp.exp(jnp.sum(lp[0:1] * lp[1:2], axis=-1, keepdims=True))
           - jnp.exp(jnp.sum(lp[2:3] * lp[3:4], axis=-1, keepdims=True)) + lam_init)
    o = acc_a / l_a - lam * (acc_b / l_b)
    o = o * lax.rsqrt(jnp.mean(o * o, axis=0, keepdims=True) + LN_EPS) * g_ref[...] * (1.0 - lam_init)
    return o.T.astype(BF16)


def _attn_kernel(q_ref, k_ref, vt_ref, kc_ref, vct_ref, lamp_ref, g_ref, o_ref,
                 s00, s01, s10, s11, p0, p1, acc0, acc1, *, tk, lam_init):
    q = q_ref[0, 0]
    tq = q.shape[0]
    half = LANES // 2
    lane = lax.broadcasted_iota(jnp.int32, q.shape, 1)
    zero = jnp.zeros_like(q)
    qs = (jnp.where(lane < half, q, zero), jnp.where(lane >= half, q, zero))
    n_chunks = k_ref.shape[2] // tk
    s_scr = ((s00, s01), (s10, s11))
    p_scr, acc_scr = (p0, p1), (acc0, acc1)
    nt = (((1,), (1,)), ((), ()))

    def scores(slot, kc):
        width = kc.shape[0]
        for mp in range(2):
            s_scr[slot][mp][0:width, :] = lax.dot_general(kc, qs[mp], nt, preferred_element_type=F32)

    def absorb(slot, vct, shift, sums):
        width = vct.shape[1]
        sums = list(sums)
        for mp in range(2):
            part = sums[mp]
            for r in range(width // ATT_STRIP):
                lo = r * ATT_STRIP
                tiles = [jnp.exp2(s_scr[slot][mp][lo + 8 * u:lo + 8 * (u + 1), :] - shift[mp])
                         for u in range(ATT_STRIP // 8)]
                p_scr[mp][lo:lo + ATT_STRIP, :] = jnp.concatenate(tiles, axis=0).astype(BF16)
                part = part + functools.reduce(lambda x, y: x + y, tiles)
            sums[mp] = part
            acc_scr[mp][...] += jnp.dot(vct, p_scr[mp][0:width, :], preferred_element_type=F32)
        return tuple(sums)

    def k_chunk(j):
        return k_ref[0, 0, pl.ds(pl.multiple_of(j * tk, tk), tk), :]

    def vt_chunk(j):
        return vt_ref[0, 0, :, pl.ds(pl.multiple_of(j * tk, tk), tk)]

    scores(0, k_chunk(0))
    shift = tuple(jnp.broadcast_to(jnp.max(s_scr[0][mp][...], axis=0, keepdims=True), (8, tq)) for mp in range(2))
    for mp in range(2):
        acc_scr[mp][...] = jnp.zeros((LANES, tq), F32)
    sums = (jnp.zeros((8, tq), F32), jnp.zeros((8, tq), F32))

    def pair(jj, sums):
        j0 = 2 * jj
        scores(1, k_chunk(j0 + 1))
        sums = absorb(0, vt_chunk(j0), shift, sums)
        scores(0, k_chunk(j0 + 2))
        return absorb(1, vt_chunk(j0 + 1), shift, sums)

    sums = lax.fori_loop(0, n_chunks // 2 - 1, pair, sums)
    scores(1, k_chunk(n_chunks - 1))
    sums = absorb(0, vt_chunk(n_chunks - 2), shift, sums)
    scores(0, kc_ref[0, 0])
    sums = absorb(1, vt_chunk(n_chunks - 1), shift, sums)
    sums = absorb(0, vct_ref[0, 0], shift, sums)

    tot = [jnp.sum(sums[mp], axis=0, keepdims=True) for mp in range(2)]
    bad = sum(jnp.sum(jnp.where(jnp.isfinite(x), 0.0, 1.0)) for x in (tot[0], tot[1], acc0[...], acc1[...]))

    @pl.when(bad == 0.0)
    def _():
        o_ref[0] = _attn_finish(acc0[...], tot[0], acc1[...], tot[1], lamp_ref, g_ref, lam_init)

    @pl.when(bad != 0.0)
    def _():
        def update(carry, kc, vct):
            new = []
            for mp in range(2):
                m, l, acc = carry[mp]
                s = lax.dot_general(kc, qs[mp], nt, preferred_element_type=F32)
                mn = jnp.maximum(m, jnp.max(s, axis=0, keepdims=True))
                a = jnp.exp2(m - mn)
                p = jnp.exp2(s - mn)
                new.append((mn, a * l + jnp.sum(p, axis=0, keepdims=True),
                            a * acc + jnp.dot(vct, p.astype(BF16), preferred_element_type=F32)))
            return tuple(new)

        init = tuple((jnp.full((1, tq), NEG_INF, F32), jnp.zeros((1, tq), F32), jnp.zeros((LANES, tq), F32))
                     for _ in range(2))
        carry = lax.fori_loop(0, n_chunks, lambda j, c: update(c, k_chunk(j), vt_chunk(j)), init)
        (_, l_a, acc_a), (_, l_b, acc_b) = update(carry, kc_ref[0, 0], vct_ref[0, 0])
        o_ref[0] = _attn_finish(acc_a, l_a, acc_b, l_b, lamp_ref, g_ref, lam_init)


def _diff_attention(q, k, vt, kc, vct, lam_p, subln_g, lam_init, tq, tk):
    b, h, n, _ = q.shape
    c = kc.shape[2]
    assert n % tk == 0 and (n // tk) % 2 == 0 and c <= tk and tk % ATT_STRIP == 0 and c % ATT_STRIP == 0
    spec = lambda r, cols: pl.BlockSpec((1, 1, r, cols), lambda bi, hi, i: (bi, hi, 0, 0))
    return pl.pallas_call(
        functools.partial(_attn_kernel, tk=tk, lam_init=lam_init),
        out_shape=jax.ShapeDtypeStruct((b, n, h * LANES), BF16),
        grid=(b, h, n // tq),
        in_specs=[
            pl.BlockSpec((1, 1, tq, LANES), lambda bi, hi, i: (bi, hi, i, 0)),
            spec(n, LANES), spec(LANES, n), spec(c, LANES), spec(LANES, c),
            pl.BlockSpec(lam_p.shape, lambda bi, hi, i: (0, 0)),
            pl.BlockSpec((LANES, 1), lambda bi, hi, i: (0, 0)),
        ],
        out_specs=pl.BlockSpec((1, tq, LANES), lambda bi, hi, i: (bi, i, hi)),
        scratch_shapes=[
            *[pltpu.VMEM((tk, tq), F32)] * 4,
            *[pltpu.VMEM((tk, tq), BF16)] * 2,
            *[pltpu.VMEM((LANES, tq), F32)] * 2,
        ],
        compiler_params=_cparams("parallel", "parallel", "parallel"),
        name="diff_attention",
    )(q, k, vt, kc, vct, lam_p, subln_g.reshape(LANES, 1))


def _sgu_kernel(x_ref, sc_ref, sh_ref, w_ref, b_ref, ng_ref, nb_ref, ws_ref, bs_ref, t_ref):
    f = t_ref.shape[-1]
    cg = f // SGU_GROUPS
    tm = x_ref.shape[1]
    h = (x_ref[0] * (1.0 + sc_ref[0]) + sh_ref[0]).astype(BF16)
    z = jnp.dot(h, w_ref[...], preferred_element_type=F32) + b_ref[...]
    z = 0.5 * z * (1.0 + lax.erf(z * (2.0 ** -0.5)))
    u = z[:, :f]
    v = _layer_norm(z[:, f:], ng_ref[...], nb_ref[...]).astype(BF16)
    for c in range(tm // SGU_CHUNK):
        rows = slice(c * SGU_CHUNK, (c + 1) * SGU_CHUNK)
        for g in range(SGU_GROUPS):
            cols = slice(g * cg, (g + 1) * cg)
            vm = jnp.dot(ws_ref[g], v[rows, cols], preferred_element_type=F32) + bs_ref[:, g:g + 1]
            t_ref[0, rows, cols] = (u[rows, cols] * vm).astype(BF16)


def _sgu_mixer(x, sc, sh, w_in_bf, b_in, norm_g, norm_b, w_s_bf, b_s_t, tm):
    b, n, d = x.shape
    f2 = w_in_bf.shape[1]
    f = f2 // 2
    vec = pl.BlockSpec((1, 1, d), lambda bi, i: (bi, 0, 0))
    full2 = lambda a: pl.BlockSpec(a.shape, lambda bi, i: (0,) * a.ndim)
    b_in2, ng2, nb2 = b_in.reshape(1, f2), norm_g.reshape(1, f), norm_b.reshape(1, f)
    return pl.pallas_call(
        _sgu_kernel,
        out_shape=jax.ShapeDtypeStruct((b, n, f), BF16),
        grid=(b, n // tm),
        in_specs=[
            pl.BlockSpec((1, tm, d), lambda bi, i: (bi, i, 0)),
            vec, vec,
            full2(w_in_bf), full2(b_in2), full2(ng2), full2(nb2), full2(w_s_bf), full2(b_s_t),
        ],
        out_specs=pl.BlockSpec((1, tm, f), lambda bi, i: (bi, i, 0)),
        compiler_params=_cparams("parallel", "parallel"),
        name="sgu_mixer",
    )(x, sc, sh, w_in_bf, b_in2, ng2, nb2, w_s_bf, b_s_t)


def _post_kernel(pre_ref, w_ref, x_ref, gm_ref, lg_ref, lb_ref, scf_ref, shf_ref, x1_ref, hf_ref, *, alpha):
    y = jnp.dot(pre_ref[0], w_ref[...], preferred_element_type=F32)
    x1 = _layer_norm(alpha * x_ref[0] + gm_ref[0] * y, lg_ref[...], lb_ref[...])
    x1_ref[0] = x1
    hf_ref[0] = _pack_bf16_pairs(x1 * (1.0 + scf_ref[0]) + shf_ref[0])


def _post_mixer(pre, w_bf, x, gm, ln_g, ln_b, scf, shf, alpha, tm):
    b, n, d = x.shape
    kd = pre.shape[-1]
    vec = pl.BlockSpec((1, 1, d), lambda bi, i: (bi, 0, 0))
    row = pl.BlockSpec((1, d), lambda bi, i: (0, 0))
    tile = pl.BlockSpec((1, tm, d), lambda bi, i: (bi, i, 0))
    return pl.pallas_call(
        functools.partial(_post_kernel, alpha=alpha),
        out_shape=(jax.ShapeDtypeStruct((b, n, d), F32), jax.ShapeDtypeStruct((b, n, d // 2), I32)),
        grid=(b, n // tm),
        in_specs=[
            pl.BlockSpec((1, tm, kd), lambda bi, i: (bi, i, 0)),
            pl.BlockSpec((kd, d), lambda bi, i: (0, 0)),
            tile, vec, row, row, vec, vec,
        ],
        out_specs=(tile, pl.BlockSpec((1, tm, d // 2), lambda bi, i: (bi, i, 0))),
        compiler_params=_cparams("parallel", "parallel"),
        name="post_mixer",
    )(pre, w_bf, x, gm, ln_g.reshape(1, d), ln_b.reshape(1, d), scf, shf)


def _route_select(scores, choice):
    e, w = scores.shape
    ge = e // N_GROUPS
    g3 = choice.reshape(N_GROUPS, ge, w)
    ri = lax.broadcasted_iota(jnp.int32, g3.shape, 1).astype(F32)
    m1 = jnp.max(g3, axis=1, keepdims=True)
    first = jnp.min(jnp.where(g3 == m1, ri, float(ge)), axis=1, keepdims=True)
    m2 = jnp.max(jnp.where(ri == first, NEG_INF, g3), axis=1, keepdims=True)
    gs = m1 + m2

    gi = lax.broadcasted_iota(jnp.int32, gs.shape, 0).astype(F32)
    gsel = jnp.zeros(gs.shape, F32)
    cur = gs
    for _ in range(TOPK_GROUPS):
        m = jnp.max(cur, axis=0, keepdims=True)
        f = jnp.min(jnp.where(cur == m, gi, float(N_GROUPS)), axis=0, keepdims=True)
        hit = gi == f
        gsel = jnp.where(hit, 1.0, gsel)
        cur = jnp.where(hit, NEG_INF, cur)
    emask = jnp.broadcast_to(gsel, g3.shape).reshape(e, w)
    masked = jnp.where(emask > 0.5, choice, NEG_INF)

    ei = lax.broadcasted_iota(jnp.int32, (e, w), 0).astype(F32)
    onehot = jnp.zeros((e, w), F32)
    idxs, ws = [], []
    for _ in range(TOP_K):
        m = jnp.max(masked, axis=0, keepdims=True)
        f = jnp.min(jnp.where(masked == m, ei, float(e)), axis=0, keepdims=True)
        hit = ei == f
        idxs.append(f)
        ws.append(jnp.sum(jnp.where(hit, scores, 0.0), axis=0, keepdims=True))
        masked = jnp.where(hit, NEG_INF, masked)
        onehot = jnp.where(hit, 1.0, onehot)
    return idxs, ws, onehot


def _route_kernel(x_ref, sc_ref, sh_ref, wr_ref, rb_ref, idx_ref, w_ref, rank_ref, cnt_ref, carry_ref):
    i = pl.program_id(0)
    e = wr_ref.shape[0]
    tm = x_ref.shape[0]

    @pl.when(i == 0)
    def _():
        carry_ref[...] = jnp.zeros_like(carry_ref)

    h = x_ref[...] * (1.0 + sc_ref[0]) + sh_ref[0]
    logits = lax.dot_general(wr_ref[...], h, (((1,), (1,)), ((), ())),
                             precision=lax.Precision.HIGHEST, preferred_element_type=F32)
    scores = 1.0 / (1.0 + jnp.exp(-logits))
    choice = scores + rb_ref[...]

    slab = min(ROUTE_SLAB, tm)
    slabs = [slice(j * slab, (j + 1) * slab) for j in range(tm // slab)]
    picks = [_route_select(scores[:, sl], choice[:, sl]) for sl in slabs]
    onehot = jnp.concatenate([pk[2] for pk in picks], axis=1)

    r_i = lax.broadcasted_iota(jnp.int32, (tm, tm), 0)
    c_i = lax.broadcasted_iota(jnp.int32, (tm, tm), 1)
    upper = jnp.where(r_i < c_i, 1.0, 0.0).astype(BF16)
    rk = jnp.dot(onehot.astype(BF16), upper, preferred_element_type=F32) + carry_ref[...]
    carry_ref[...] += jnp.sum(onehot, axis=1, keepdims=True)

    ei = lax.broadcasted_iota(jnp.int32, (e, slab), 0).astype(F32)
    for sl, (idxs, ws, _) in zip(slabs, picks):
        wsum = functools.reduce(lambda x, y: x + y, ws)
        for k in range(TOP_K):
            idx_ref[k:k + 1, sl] = idxs[k].astype(jnp.int32)
            w_ref[k:k + 1, sl] = ws[k] / wsum * ROUTED_SCALE
            rank_ref[k:k + 1, sl] = jnp.sum(jnp.where(ei == idxs[k], rk[:, sl], 0.0), axis=0,
                                            keepdims=True).astype(jnp.int32)
    cnt_ref[...] = jnp.broadcast_to(carry_ref[...], cnt_ref.shape).astype(jnp.int32)


def _route(x1, scf, shf, wr_t, rbias, tm):
    b, n, d = x1.shape
    t = b * n
    e = wr_t.shape[0]
    per_b = n // tm
    vec = pl.BlockSpec((1, 1, d), lambda i: (i // per_b, 0, 0))
    out_t = pl.BlockSpec((TOP_K, tm), lambda i: (0, i))
    return pl.pallas_call(
        _route_kernel,
        out_shape=(jax.ShapeDtypeStruct((TOP_K, t), jnp.int32), jax.ShapeDtypeStruct((TOP_K, t), F32),
                   jax.ShapeDtypeStruct((TOP_K, t), jnp.int32), jax.ShapeDtypeStruct((e, LANES), jnp.int32)),
        grid=(t // tm,),
        in_specs=[
            pl.BlockSpec((tm, d), lambda i: (i, 0)),
            vec, vec,
            pl.BlockSpec((e, d), lambda i: (0, 0)),
            pl.BlockSpec((e, 1), lambda i: (0, 0)),
        ],
        out_specs=(out_t, out_t, out_t, pl.BlockSpec((e, LANES), lambda i: (0, 0))),
        scratch_shapes=[pltpu.VMEM((e, 1), F32)],
        compiler_params=_cparams("arbitrary"),
        name="route",
    )(x1.reshape(t, d), scf, shf, wr_t, rbias.reshape(e, 1))


def _gather_rows(table, idx):
    m = idx.shape[0]
    w = table.shape[1]
    workers = SC_CORES * SC_SUBCORES
    n_ch = m // (workers * SC_CHUNK)
    assert m % (workers * SC_CHUNK) == 0 and n_ch % 2 == 0
    mesh = plsc.VectorSubcoreMesh(core_axis_name="c", subcore_axis_name="s",
                                  num_cores=SC_CORES, num_subcores=SC_SUBCORES)

    @functools.partial(
        pl.kernel, mesh=mesh,
        out_type=jax.ShapeDtypeStruct((m, w), table.dtype),
        scratch_types=[
            pltpu.VMEM((n_ch, SC_CHUNK), I32),
            pltpu.VMEM((SC_CHUNK, w), table.dtype), pltpu.VMEM((SC_CHUNK, w), table.dtype),
            pltpu.SemaphoreType.DMA, pltpu.SemaphoreType.DMA, pltpu.SemaphoreType.DMA, pltpu.SemaphoreType.DMA,
        ],
        name="sc_gather_rows",
    )
    def gather(table_hbm, idx_hbm, out_hbm, idx_all, buf0, buf1, gsem0, gsem1, wsem0, wsem1):
        first = (lax.axis_index("s") * SC_CORES + lax.axis_index("c")) * n_ch
        bufs, gsem, wsem = (buf0, buf1), (gsem0, gsem1), (wsem0, wsem1)
        pltpu.sync_copy(idx_hbm.at[pl.ds(first, n_ch)], idx_all)

        def gather_copy(j, s):
            return pltpu.make_async_copy(table_hbm.at[idx_all.at[j]], bufs[s], gsem[s])

        def write_copy(j, s):
            rows = pl.ds(pl.multiple_of((first + j) * SC_CHUNK, SC_CHUNK), SC_CHUNK)
            return pltpu.make_async_copy(bufs[s], out_hbm.at[rows], wsem[s])

        gather_copy(0, 0).start()

        @pl.loop(0, n_ch, step=2)
        def _(jj):
            for s in range(2):
                j = jj + s

                @pl.when(j >= 1)
                def _():
                    write_copy(j - 1, 1 - s).wait()

                @pl.when(j + 1 < n_ch)
                def _():
                    gather_copy(j + 1, 1 - s).start()

                gather_copy(j, s).wait()
                write_copy(j, s).start()

        write_copy(n_ch - 1, 1).wait()

    return gather(table, idx.reshape(m // SC_CHUNK, SC_CHUNK))


def _scatter_rows(src, pos3, p):
    w = src.shape[1]
    n_chunks, k, ch = pos3.shape
    workers = SC_CORES * SC_SUBCORES
    per_w = n_chunks // workers
    assert ch == SC_CHUNK and n_chunks % workers == 0 and src.shape[0] == n_chunks * ch
    mesh = plsc.VectorSubcoreMesh(core_axis_name="c", subcore_axis_name="s",
                                  num_cores=SC_CORES, num_subcores=SC_SUBCORES)

    @functools.partial(
        pl.kernel, mesh=mesh,
        out_type=jax.ShapeDtypeStruct((p, w), src.dtype),
        scratch_types=[
            pltpu.VMEM((k, ch), I32),
            pltpu.VMEM((ch, w), src.dtype),
            pltpu.SemaphoreType.DMA,
        ],
        name="sc_scatter_rows",
    )
    def scatter(src_hbm, pos_hbm, out_hbm, idx_v, rows_v, sem):
        first = (lax.axis_index("s") * SC_CORES + lax.axis_index("c")) * per_w

        @pl.loop(0, per_w)
        def _(j):
            c = first + j
            pltpu.sync_copy(pos_hbm.at[c], idx_v)
            pltpu.sync_copy(src_hbm.at[pl.ds(pl.multiple_of(c * ch, ch), ch)], rows_v)
            copies = [pltpu.async_copy(rows_v, out_hbm.at[idx_v.at[kk]], sem) for kk in range(k)]
            for cp in copies:
                cp.wait()

    return scatter(src, pos3)


def _experts_kernel(ps_ref, nb_ref, cnt_ref, nt_ref, xs_hbm, wg_ref, wu_ref, wd_ref, y_hbm,
                    xbuf, ybuf, in_sem, out_sem, wg_bf, wu_bf, wd_bf):
    e = pl.program_id(0)
    nb, cnt, n_total = nb_ref[e], cnt_ref[e], nt_ref[0]
    g0 = ps_ref[e] // MOE_ROWS

    def in_copy(g):
        rows = pl.ds(pl.multiple_of(g * MOE_ROWS, MOE_ROWS), MOE_ROWS)
        return pltpu.make_async_copy(xs_hbm.at[rows], xbuf.at[g % MOE_SLOTS], in_sem.at[g % MOE_SLOTS])

    def out_copy(g):
        rows = pl.ds(pl.multiple_of(g * MOE_ROWS, MOE_ROWS), MOE_ROWS)
        return pltpu.make_async_copy(ybuf.at[g % MOE_SLOTS], y_hbm.at[rows], out_sem.at[g % MOE_SLOTS])

    @pl.when(e == 0)
    def _():
        for j in range(MOE_LOOKAHEAD):
            @pl.when(j < n_total)
            def _():
                in_copy(j).start()

    @pl.when(nb > 0)
    def _():
        wg_bf[...] = wg_ref[0, 0].astype(BF16)
        wu_bf[...] = wu_ref[0, 0].astype(BF16)
        wd_bf[...] = wd_ref[0, 0].astype(BF16)

        def process(b, width):
            g = g0 + b
            for u in range(width):
                @pl.when(g + u + MOE_LOOKAHEAD < n_total)
                def _():
                    in_copy(g + u + MOE_LOOKAHEAD).start()

            for u in range(width):
                in_copy(g + u).wait()

                @pl.when(g + u >= MOE_SLOTS)
                def _():
                    out_copy(g + u - MOE_SLOTS).wait()

            packed = jnp.concatenate([xbuf[(g + u) % MOE_SLOTS] for u in range(width)], axis=0)
            row = lax.broadcasted_iota(I32, (width * MOE_ROWS, 1), 0) + b * MOE_ROWS
            x_lo, x_hi = (v.astype(BF16) for v in _unpack_bf16_pairs(jnp.where(row < cnt, packed, 0)))
            half = x_lo.shape[1]

            def up(w_bf):
                return (jnp.dot(x_lo, w_bf[:half, :], preferred_element_type=F32)
                        + jnp.dot(x_hi, w_bf[half:, :], preferred_element_type=F32))

            hb = (_silu(up(wg_bf)) * up(wu_bf)).astype(BF16)
            y = _pack_bf16_pairs(jnp.dot(hb, wd_bf[...], preferred_element_type=F32))
            for u in range(width):
                ybuf[(g + u) % MOE_SLOTS] = y[u * MOE_ROWS:(u + 1) * MOE_ROWS]
                out_copy(g + u).start()

        start = 0
        for width in MOE_GROUPS:
            count = (nb - start) // width

            def body(i, c, width=width, start=start):
                process(start + width * i, width)
                return c

            lax.fori_loop(0, count, body, 0)
            start = start + count * width

    @pl.when(e == pl.num_programs(0) - 1)
    def _():
        for j in range(MOE_SLOTS):
            @pl.when(n_total - 1 - j >= 0)
            def _():
                out_copy(n_total - 1 - j).wait()


def _routed_experts(xs, wg, wu, wd, layer, pstarts, nblk, counts, n_total):
    p, dp = xs.shape
    _, e, d, f = wg.shape
    wspec = lambda r, c: pl.BlockSpec((1, 1, r, c), lambda i, ps, nb, cnt, nt: (layer, i, 0, 0))
    grid_spec = pltpu.PrefetchScalarGridSpec(
        num_scalar_prefetch=4,
        grid=(e,),
        in_specs=[pl.BlockSpec(memory_space=pl.ANY), wspec(d, f), wspec(d, f), wspec(f, d)],
        out_specs=pl.BlockSpec(memory_space=pl.ANY),
        scratch_shapes=[
            pltpu.VMEM((MOE_SLOTS, MOE_ROWS, dp), I32), pltpu.VMEM((MOE_SLOTS, MOE_ROWS, dp), I32),
            pltpu.SemaphoreType.DMA((MOE_SLOTS,)), pltpu.SemaphoreType.DMA((MOE_SLOTS,)),
            pltpu.VMEM((d, f), BF16), pltpu.VMEM((d, f), BF16), pltpu.VMEM((f, d), BF16),
        ],
    )
    return pl.pallas_call(
        _experts_kernel,
        out_shape=jax.ShapeDtypeStruct((p, dp), I32),
        grid_spec=grid_spec,
        compiler_params=_cparams("arbitrary"),
        name="routed_experts",
    )(pstarts, nblk, counts, n_total, xs, wg, wu, wd)


def _combine_kernel(yg_ref, w_ref, hf_ref, sg_ref, su_ref, sd_ref, x_ref, gf_ref, lg_ref, lb_ref, o_ref, *, alpha):
    w = w_ref[...]
    r_lo, r_hi = _unpack_bf16_pairs(yg_ref[0])
    r_lo, r_hi = w[:, 0:1] * r_lo, w[:, 0:1] * r_hi
    for k in range(1, TOP_K):
        y_lo, y_hi = _unpack_bf16_pairs(yg_ref[k])
        r_lo, r_hi = r_lo + w[:, k:k + 1] * y_lo, r_hi + w[:, k:k + 1] * y_hi
    routed = jnp.concatenate([r_lo, r_hi], axis=1)
    hf = jnp.concatenate(_unpack_bf16_pairs(hf_ref[...]), axis=1).astype(BF16)
    g = jnp.dot(hf, sg_ref[...], preferred_element_type=F32)
    u = jnp.dot(hf, su_ref[...], preferred_element_type=F32)
    shared = jnp.dot((_silu(g) * u).astype(BF16), sd_ref[...], preferred_element_type=F32)
    o_ref[...] = _layer_norm(alpha * x_ref[...] + gf_ref[0] * (routed + shared), lg_ref[...], lb_ref[...])


def _combine(yg, w_tk, hf, sg_bf, su_bf, sd_bf, x1, gf, ln_g, ln_b, alpha, tm, per_b):
    t, d = x1.shape
    f = sg_bf.shape[1]
    row = pl.BlockSpec((1, d), lambda i: (0, 0))
    tile = pl.BlockSpec((tm, d), lambda i: (i, 0))
    return pl.pallas_call(
        functools.partial(_combine_kernel, alpha=alpha),
        out_shape=jax.ShapeDtypeStruct((t, d), F32),
        grid=(t // tm,),
        in_specs=[
            pl.BlockSpec((TOP_K, tm, d // 2), lambda i: (0, i, 0)),
            pl.BlockSpec((tm, TOP_K), lambda i: (i, 0)),
            pl.BlockSpec((tm, d // 2), lambda i: (i, 0)),
            pl.BlockSpec((d, f), lambda i: (0, 0)),
            pl.BlockSpec((d, f), lambda i: (0, 0)),
            pl.BlockSpec((f, d), lambda i: (0, 0)),
            tile,
            pl.BlockSpec((1, 1, d), lambda i: (i // per_b, 0, 0)),
            row, row,
        ],
        out_specs=tile,
        compiler_params=_cparams("parallel"),
        name="moe_combine",
    )(yg, w_tk, hf, sg_bf, su_bf, sd_bf, x1, gf, ln_g.reshape(1, d), ln_b.reshape(1, d))


def _moe_layer(x1, hf, scf, shf, gf, router_w, router_bias, wg, wu, wd, layer, sg, su, sd, ln_g, ln_b, alpha, tm):
    b, n, d = x1.shape
    t = b * n
    e = router_w.shape[1]
    idx_t, w_t, rank_t, cnt = _route(x1, scf, shf, router_w.T, router_bias, tm)

    counts = cnt[:, 0]
    padded = (counts + MOE_ROWS - 1) // MOE_ROWS * MOE_ROWS
    pends = jnp.cumsum(padded)
    pstarts = pends - padded
    sel = idx_t[:, :, None] == jnp.arange(e, dtype=I32)
    pos_t = jnp.sum(jnp.where(sel, pstarts, 0), axis=-1) + rank_t
    p = t * TOP_K + e * MOE_ROWS
    pos3 = pos_t.reshape(TOP_K, t // SC_CHUNK, SC_CHUNK).transpose(1, 0, 2)

    hf2 = hf.reshape(t, d // 2)
    xs = _scatter_rows(hf2, pos3, p)
    yb = _routed_experts(xs, wg, wu, wd, layer, pstarts.astype(I32), (padded // MOE_ROWS).astype(I32), counts,
                         (pends[-1:] // MOE_ROWS).astype(I32))
    yg = _gather_rows(yb, pos_t.reshape(-1)).reshape(TOP_K, t, d // 2)
    out = _combine(yg, w_t.T, hf2, sg.astype(BF16), su.astype(BF16), sd.astype(BF16), x1.reshape(t, d), gf,
                   ln_g, ln_b, alpha, tm, n // tm)
    return out.reshape(b, n, d)


def _rope_tables(n):
    rows = n // GRID_W
    row_pos = jnp.repeat(jnp.arange(rows, dtype=F32), GRID_W)
    col_pos = jnp.tile(jnp.arange(GRID_W, dtype=F32), rows)
    half = LANES // 4
    lane = jnp.arange(LANES)
    in_blk = lane % half
    freq = ROPE_THETA ** (-(2.0 * (in_blk % (half // 2)).astype(F32)) / half)
    use_col = (lane // half) % 2 == 1
    pos = jnp.where(use_col[None, :], col_pos[:, None], row_pos[:, None])
    ang = pos * freq[None, :]
    lo = (in_blk < half // 2)[None, :]
    sin = jnp.sin(ang)
    return jnp.cos(ang), jnp.where(lo, -sin, 0.0), jnp.where(lo, 0.0, sin)


def kernel(x, c, ctx, c_ctx, w_mod, b_mod, ln_g, ln_b, attn_w_in, attn_w_out, attn_lambda, attn_subln_g,
           sgu_w_in, sgu_b_in, sgu_norm_g, sgu_norm_b, sgu_w_s, sgu_b_s, sgu_w_out,
           router_w, router_bias, exp_w_gate, exp_w_up, exp_w_down, sh_w_gate, sh_w_up, sh_w_down):
    b, n, d = x.shape
    depth = w_mod.shape[0]
    assert b <= 7 and d == ATT_HEADS * LANES and n % GRID_W == 0
    alpha = (2 * depth) ** 0.25
    head_dim = d // ATT_HEADS // 2
    tm = TOKEN_TILE if n % TOKEN_TILE == 0 else TOKEN_TILE // 2

    cs = jnp.zeros((8, d), F32).at[:b].set(c).at[b].set(c_ctx)
    mods = _modulation(cs, w_mod, b_mod)

    def mod_vec(i, j):
        return mods[i, :, j * d:(j + 1) * d].reshape(8, 1, d)

    for i in range(depth):
        sh_m, sc_m, g_m, sh_f, sc_f, g_f = (mod_vec(i, j) for j in range(6))
        if i % N_MIXERS == 0:
            a = i // N_MIXERS
            lam_init = 0.8 - 0.6 * math.exp(-0.3 * i)
            w_in_bf = attn_w_in[a].astype(BF16)
            cos, slo, shi = _rope_tables(n)
            q_scale = head_dim ** -0.5 * math.log2(math.e)
            q, k, vt = _qkv_proj(x, sc_m, sh_m, w_in_bf, cos, slo, shi, q_scale, tm)
            kc, vct = _kv_ctx_proj(ctx, sc_m[b:b + 1], sh_m[b:b + 1], w_in_bf[:, d:])
            pre = _diff_attention(q, k, vt, kc, vct, attn_lambda[a], attn_subln_g[a], lam_init,
                                  min(ATT_QUERY_TILE, n), min(ATT_KEY_CHUNK, n // 2))
            w_out_bf = attn_w_out[a].astype(BF16)
        else:
            s = i // N_MIXERS
            pre = _sgu_mixer(x, sc_m, sh_m, sgu_w_in[s].astype(BF16), sgu_b_in[s], sgu_norm_g[s], sgu_norm_b[s],
                             sgu_w_s[s].astype(BF16), sgu_b_s[s].T, tm)
            w_out_bf = sgu_w_out[s].astype(BF16)
        x1, hf = _post_mixer(pre, w_out_bf, x, g_m, ln_g[i, 0], ln_b[i, 0], sc_f, sh_f, alpha, tm)
        x = _moe_layer(x1, hf, sc_f, sh_f, g_f, router_w[i], router_bias[i], exp_w_gate, exp_w_up, exp_w_down, i,
                       sh_w_gate[i], sh_w_up[i], sh_w_down[i], ln_g[i, 1], ln_b[i, 1], alpha, tm)
    return x
```

```python
import functools
import math

import jax
import jax.numpy as jnp
from jax import lax
from jax.experimental import pallas as pl
from jax.experimental.pallas import tpu as pltpu
from jax.experimental.pallas import tpu_sc as plsc

F32 = jnp.float32
BF16 = jnp.bfloat16
I32 = jnp.int32

GRID_W = 64
ATT_HEADS = 8
ROPE_THETA = 10000.0
SGU_CHUNK = 128
SGU_GROUPS = 8
TOP_K = 8
N_GROUPS = 8
TOPK_GROUPS = 4
ROUTED_SCALE = 2.5
LN_EPS = 1e-5
N_MIXERS = 2

LANES = 128
MOE_ROWS = 256
ROUTE_SLAB = 512
MOE_GROUPS = (2, 1)
MOE_LOOKAHEAD = 6
MOE_SLOTS = MOE_LOOKAHEAD + MOE_GROUPS[0]
TOKEN_TILE = 512
ATT_QUERY_TILE = 1024
ATT_KEY_CHUNK = 1024
ATT_STRIP = 16
SC_CORES = 2
SC_SUBCORES = 16
SC_CHUNK = 64
VMEM_LIMIT = 56 * 1024 * 1024
NEG_INF = float("-inf")


def _cparams(*sem):
    return pltpu.CompilerParams(dimension_semantics=sem, vmem_limit_bytes=VMEM_LIMIT)


def _layer_norm(z, g, b):
    mu = jnp.mean(z, axis=-1, keepdims=True)
    zc = z - mu
    var = jnp.mean(zc * zc, axis=-1, keepdims=True)
    return zc * lax.rsqrt(var + LN_EPS) * g + b


def _silu(x):
    return x * (1.0 / (1.0 + jnp.exp(-x)))


_HIGH_HALF = -65536


def _pack_bf16_pairs(y):
    w = y.shape[1] // 2
    bits = lax.bitcast_convert_type(y.astype(BF16).astype(F32), I32)
    return lax.shift_right_logical(bits[:, :w], 16) | (bits[:, w:] & _HIGH_HALF)


def _unpack_bf16_pairs(p):
    return (lax.bitcast_convert_type(lax.shift_left(p, 16), F32),
            lax.bitcast_convert_type(p & _HIGH_HALF, F32))


def _mod_kernel(cs_ref, w_ref, b_ref, o_ref):
    s = _silu(cs_ref[...])
    o_ref[0] = jnp.dot(s, w_ref[0], precision=lax.Precision.HIGHEST,
                       preferred_element_type=F32) + b_ref[0]


def _modulation(cs, w_mod, b_mod):
    depth, d, n6 = w_mod.shape
    tn = n6 // 4
    return pl.pallas_call(
        _mod_kernel,
        out_shape=jax.ShapeDtypeStruct((depth, 8, n6), F32),
        grid=(depth, n6 // tn),
        in_specs=[
            pl.BlockSpec((8, d), lambda l, j: (0, 0)),
            pl.BlockSpec((1, d, tn), lambda l, j: (l, 0, j)),
            pl.BlockSpec((1, 1, tn), lambda l, j: (l, 0, j)),
        ],
        out_specs=pl.BlockSpec((1, 8, tn), lambda l, j: (l, 0, j)),
        compiler_params=_cparams("parallel", "parallel"),
        name="modulation",
    )(cs, w_mod, b_mod.reshape(depth, 1, n6))


def _rope(xh, cos, sin_lo, sin_hi):
    return xh * cos + pltpu.roll(xh, LANES - 16, 1) * sin_lo + pltpu.roll(xh, 16, 1) * sin_hi


def _qkv_kernel(x_ref, sc_ref, sh_ref, w_ref, cos_ref, slo_ref, shi_ref, q_ref, k_ref, v_ref, *, q_scale):
    d = x_ref.shape[-1]
    h = (x_ref[0] * (1.0 + sc_ref[0]) + sh_ref[0]).astype(BF16)
    cos, slo, shi = cos_ref[...], slo_ref[...], shi_ref[...]
    q = jnp.dot(h, w_ref[:, 0:d], preferred_element_type=F32)
    for hd in range(ATT_HEADS):
        q_ref[0, hd] = (_rope(q[:, hd * LANES:(hd + 1) * LANES], cos, slo, shi) * q_scale).astype(BF16)
    k = jnp.dot(h, w_ref[:, d:2 * d], preferred_element_type=F32)
    for hd in range(ATT_HEADS):
        k_ref[0, hd] = _rope(k[:, hd * LANES:(hd + 1) * LANES], cos, slo, shi).astype(BF16)
    v = jnp.dot(h, w_ref[:, 2 * d:3 * d], preferred_element_type=F32)
    for hd in range(ATT_HEADS):
        v_ref[0, hd] = v[:, hd * LANES:(hd + 1) * LANES].T.astype(BF16)


def _qkv_proj(x, sc, sh, w_bf, cos, slo, shi, q_scale, tn):
    b, n, d = x.shape
    hd_shape = jax.ShapeDtypeStruct((b, ATT_HEADS, n, LANES), BF16)
    vec = pl.BlockSpec((1, 1, d), lambda bi, i: (bi, 0, 0))
    tab = pl.BlockSpec((tn, LANES), lambda bi, i: (i, 0))
    out = pl.BlockSpec((1, ATT_HEADS, tn, LANES), lambda bi, i: (bi, 0, i, 0))
    out_t = pl.BlockSpec((1, ATT_HEADS, LANES, tn), lambda bi, i: (bi, 0, 0, i))
    return pl.pallas_call(
        functools.partial(_qkv_kernel, q_scale=q_scale),
        out_shape=(hd_shape, hd_shape, jax.ShapeDtypeStruct((b, ATT_HEADS, LANES, n), BF16)),
        grid=(b, n // tn),
        in_specs=[
            pl.BlockSpec((1, tn, d), lambda bi, i: (bi, i, 0)),
            vec, vec,
            pl.BlockSpec((d, 3 * d), lambda bi, i: (0, 0)),
            tab, tab, tab,
        ],
        out_specs=(out, out, out_t),
        compiler_params=_cparams("parallel", "parallel"),
        name="qkv_proj",
    )(x, sc, sh, w_bf, cos, slo, shi)


def _kv_ctx_kernel(x_ref, sc_ref, sh_ref, w_ref, k_ref, v_ref):
    d = x_ref.shape[-1]
    h = (x_ref[0] * (1.0 + sc_ref[0]) + sh_ref[0]).astype(BF16)
    k = jnp.dot(h, w_ref[:, 0:d], preferred_element_type=F32)
    v = jnp.dot(h, w_ref[:, d:2 * d], preferred_element_type=F32)
    for hd in range(ATT_HEADS):
        k_ref[0, hd] = k[:, hd * LANES:(hd + 1) * LANES].astype(BF16)
        v_ref[0, hd] = v[:, hd * LANES:(hd + 1) * LANES].T.astype(BF16)


def _kv_ctx_proj(ctx, sc, sh, w_kv_bf):
    b, c, d = ctx.shape
    hd_shape = jax.ShapeDtypeStruct((b, ATT_HEADS, c, LANES), BF16)
    vec = pl.BlockSpec((1, 1, d), lambda bi: (0, 0, 0))
    out = pl.BlockSpec((1, ATT_HEADS, c, LANES), lambda bi: (bi, 0, 0, 0))
    out_t = pl.BlockSpec((1, ATT_HEADS, LANES, c), lambda bi: (bi, 0, 0, 0))
    return pl.pallas_call(
        _kv_ctx_kernel,
        out_shape=(hd_shape, jax.ShapeDtypeStruct((b, ATT_HEADS, LANES, c), BF16)),
        grid=(b,),
        in_specs=[
            pl.BlockSpec((1, c, d), lambda bi: (bi, 0, 0)),
            vec, vec,
            pl.BlockSpec((d, 2 * d), lambda bi: (0, 0)),
        ],
        out_specs=(out, out_t),
        compiler_params=_cparams("parallel"),
        name="kv_ctx_proj",
    )(ctx, sc, sh, w_kv_bf)


def _attn_finish(acc_a, l_a, acc_b, l_b, lamp_ref, g_ref, lam_init):
    lp = lamp_ref[...]
    lam = (jnp.exp(jnp.sum(lp[0:1] * lp[1:2], axis=-1, keepdims=True))
           - jnp.exp(jnp.sum(lp[2:3] * lp[3:4], axis=-1, keepdims=True)) + lam_init)
    o = acc_a / l_a - lam * (acc_b / l_b)
    o = o * lax.rsqrt(jnp.mean(o * o, axis=0, keepdims=True) + LN_EPS) * g_ref[...] * (1.0 - lam_init)
    return o.T.astype(BF16)


def _attn_kernel(q_ref, k_ref, vt_ref, kc_ref, vct_ref, lamp_ref, g_ref, o_ref,
                 s00, s01, s10, s11, p0, p1, acc0, acc1, *, tk, lam_init):
    q = q_ref[0, 0]
    tq = q.shape[0]
    half = LANES // 2
    lane = lax.broadcasted_iota(jnp.int32, q.shape, 1)
    zero = jnp.zeros_like(q)
    qs = (jnp.where(lane < half, q, zero), jnp.where(lane >= half, q, zero))
    n_chunks = k_ref.shape[2] // tk
    s_scr = ((s00, s01), (s10, s11))
    p_scr, acc_scr = (p0, p1), (acc0, acc1)
    nt = (((1,), (1,)), ((), ()))

    def scores(slot, kc):
        width = kc.shape[0]
        for mp in range(2):
            s_scr[slot][mp][0:width, :] = lax.dot_general(kc, qs[mp], nt, preferred_element_type=F32)

    def absorb(slot, vct, shift, sums):
        width = vct.shape[1]
        sums = list(sums)
        for mp in range(2):
            part = sums[mp]
            for r in range(width // ATT_STRIP):
                lo = r * ATT_STRIP
                tiles = [jnp.exp2(s_scr[slot][mp][lo + 8 * u:lo + 8 * (u + 1), :] - shift[mp])
                         for u in range(ATT_STRIP // 8)]
                p_scr[mp][lo:lo + ATT_STRIP, :] = jnp.concatenate(tiles, axis=0).astype(BF16)
                part = part + functools.reduce(lambda x, y: x + y, tiles)
            sums[mp] = part
            acc_scr[mp][...] += jnp.dot(vct, p_scr[mp][0:width, :], preferred_element_type=F32)
        return tuple(sums)

    def k_chunk(j):
        return k_ref[0, 0, pl.ds(pl.multiple_of(j * tk, tk), tk), :]

    def vt_chunk(j):
        return vt_ref[0, 0, :, pl.ds(pl.multiple_of(j * tk, tk), tk)]

    scores(0, k_chunk(0))
    shift = tuple(jnp.broadcast_to(jnp.max(s_scr[0][mp][...], axis=0, keepdims=True), (8, tq)) for mp in range(2))
    for mp in range(2):
        acc_scr[mp][...] = jnp.zeros((LANES, tq), F32)
    sums = (jnp.zeros((8, tq), F32), jnp.zeros((8, tq), F32))

    def pair(jj, sums):
        j0 = 2 * jj
        scores(1, k_chunk(j0 + 1))
        sums = absorb(0, vt_chunk(j0), shift, sums)
        scores(0, k_chunk(j0 + 2))
        return absorb(1, vt_chunk(j0 + 1), shift, sums)

    sums = lax.fori_loop(0, n_chunks // 2 - 1, pair, sums)
    scores(1, k_chunk(n_chunks - 1))
    sums = absorb(0, vt_chunk(n_chunks - 2), shift, sums)
    scores(0, kc_ref[0, 0])
    sums = absorb(1, vt_chunk(n_chunks - 1), shift, sums)
    sums = absorb(0, vct_ref[0, 0], shift, sums)

    tot = [jnp.sum(sums[mp], axis=0, keepdims=True) for mp in range(2)]
    bad = sum(jnp.sum(jnp.where(jnp.isfinite(x), 0.0, 1.0)) for x in (tot[0], tot[1], acc0[...], acc1[...]))

    @pl.when(bad == 0.0)
    def _():
        o_ref[0] = _attn_finish(acc0[...], tot[0], acc1[...], tot[1], lamp_ref, g_ref, lam_init)

    @pl.when(bad != 0.0)
    def _():
        def update(carry, kc, vct):
            new = []
            for mp in range(2):
                m, l, acc = carry[mp]
                s = lax.dot_general(kc, qs[mp], nt, preferred_element_type=F32)
                mn = jnp.maximum(m, jnp.max(s, axis=0, keepdims=True))
                a = jnp.exp2(m - mn)
                p = jnp.exp2(s - mn)
                new.append((mn, a * l + jnp.sum(p, axis=0, keepdims=True),
                            a * acc + jnp.dot(vct, p.astype(BF16), preferred_element_type=F32)))
            return tuple(new)

        init = tuple((jnp.full((1, tq), NEG_INF, F32), jnp.zeros((1, tq), F32), jnp.zeros((LANES, tq), F32))
                     for _ in range(2))
        carry = lax.fori_loop(0, n_chunks, lambda j, c: update(c, k_chunk(j), vt_chunk(j)), init)
        (_, l_a, acc_a), (_, l_b, acc_b) = update(carry, kc_ref[0, 0], vct_ref[0, 0])
        o_ref[0] = _attn_finish(acc_a, l_a, acc_b, l_b, lamp_ref, g_ref, lam_init)


def _diff_attention(q, k, vt, kc, vct, lam_p, subln_g, lam_init, tq, tk):
    b, h, n, _ = q.shape
    c = kc.shape[2]
    assert n % tk == 0 and (n // tk) % 2 == 0 and c <= tk and tk % ATT_STRIP == 0 and c % ATT_STRIP == 0
    spec = lambda r, cols: pl.BlockSpec((1, 1, r, cols), lambda bi, hi, i: (bi, hi, 0, 0))
    return pl.pallas_call(
        functools.partial(_attn_kernel, tk=tk, lam_init=lam_init),
        out_shape=jax.ShapeDtypeStruct((b, n, h * LANES), BF16),
        grid=(b, h, n // tq),
        in_specs=[
            pl.BlockSpec((1, 1, tq, LANES), lambda bi, hi, i: (bi, hi, i, 0)),
            spec(n, LANES), spec(LANES, n), spec(c, LANES), spec(LANES, c),
            pl.BlockSpec(lam_p.shape, lambda bi, hi, i: (0, 0)),
            pl.BlockSpec((LANES, 1), lambda bi, hi, i: (0, 0)),
        ],
        out_specs=pl.BlockSpec((1, tq, LANES), lambda bi, hi, i: (bi, i, hi)),
        scratch_shapes=[
            *[pltpu.VMEM((tk, tq), F32)] * 4,
            *[pltpu.VMEM((tk, tq), BF16)] * 2,
            *[pltpu.VMEM((LANES, tq), F32)] * 2,
        ],
        compiler_params=_cparams("parallel", "parallel", "parallel"),
        name="diff_attention",
    )(q, k, vt, kc, vct, lam_p, subln_g.reshape(LANES, 1))


def _sgu_kernel(x_ref, sc_ref, sh_ref, w_ref, b_ref, ng_ref, nb_ref, ws_ref, bs_ref, t_ref):
    f = t_ref.shape[-1]
    cg = f // SGU_GROUPS
    tm = x_ref.shape[1]
    h = (x_ref[0] * (1.0 + sc_ref[0]) + sh_ref[0]).astype(BF16)
    z = jnp.dot(h, w_ref[...], preferred_element_type=F32) + b_ref[...]
    z = 0.5 * z * (1.0 + lax.erf(z * (2.0 ** -0.5)))
    u = z[:, :f]
    v = _layer_norm(z[:, f:], ng_ref[...], nb_ref[...]).astype(BF16)
    for c in range(tm // SGU_CHUNK):
        rows = slice(c * SGU_CHUNK, (c + 1) * SGU_CHUNK)
        for g in range(SGU_GROUPS):
            cols = slice(g * cg, (g + 1) * cg)
            vm = jnp.dot(ws_ref[g], v[rows, cols], preferred_element_type=F32) + bs_ref[:, g:g + 1]
            t_ref[0, rows, cols] = (u[rows, cols] * vm).astype(BF16)


def _sgu_mixer(x, sc, sh, w_in_bf, b_in, norm_g, norm_b, w_s_bf, b_s_t, tm):
    b, n, d = x.shape
    f2 = w_in_bf.shape[1]
    f = f2 // 2
    vec = pl.BlockSpec((1, 1, d), lambda bi, i: (bi, 0, 0))
    full2 = lambda a: pl.BlockSpec(a.shape, lambda bi, i: (0,) * a.ndim)
    b_in2, ng2, nb2 = b_in.reshape(1, f2), norm_g.reshape(1, f), norm_b.reshape(1, f)
    return pl.pallas_call(
        _sgu_kernel,
        out_shape=jax.ShapeDtypeStruct((b, n, f), BF16),
        grid=(b, n // tm),
        in_specs=[
            pl.BlockSpec((1, tm, d), lambda bi, i: (bi, i, 0)),
            vec, vec,
            full2(w_in_bf), full2(b_in2), full2(ng2), full2(nb2), full2(w_s_bf), full2(b_s_t),
        ],
        out_specs=pl.BlockSpec((1, tm, f), lambda bi, i: (bi, i, 0)),
        compiler_params=_cparams("parallel", "parallel"),
        name="sgu_mixer",
    )(x, sc, sh, w_in_bf, b_in2, ng2, nb2, w_s_bf, b_s_t)


def _post_kernel(pre_ref, w_ref, x_ref, gm_ref, lg_ref, lb_ref, scf_ref, shf_ref, x1_ref, hf_ref, *, alpha):
    y = jnp.dot(pre_ref[0], w_ref[...], preferred_element_type=F32)
    x1 = _layer_norm(alpha * x_ref[0] + gm_ref[0] * y, lg_ref[...], lb_ref[...])
    x1_ref[0] = x1
    hf_ref[0] = _pack_bf16_pairs(x1 * (1.0 + scf_ref[0]) + shf_ref[0])


def _post_mixer(pre, w_bf, x, gm, ln_g, ln_b, scf, shf, alpha, tm):
    b, n, d = x.shape
    kd = pre.shape[-1]
    vec = pl.BlockSpec((1, 1, d), lambda bi, i: (bi, 0, 0))
    row = pl.BlockSpec((1, d), lambda bi, i: (0, 0))
    tile = pl.BlockSpec((1, tm, d), lambda bi, i: (bi, i, 0))
    return pl.pallas_call(
        functools.partial(_post_kernel, alpha=alpha),
        out_shape=(jax.ShapeDtypeStruct((b, n, d), F32), jax.ShapeDtypeStruct((b, n, d // 2), I32)),
        grid=(b, n // tm),
        in_specs=[
            pl.BlockSpec((1, tm, kd), lambda bi, i: (bi, i, 0)),
            pl.BlockSpec((kd, d), lambda bi, i: (0, 0)),
            tile, vec, row, row, vec, vec,
        ],
        out_specs=(tile, pl.BlockSpec((1, tm, d // 2), lambda bi, i: (bi, i, 0))),
        compiler_params=_cparams("parallel", "parallel"),
        name="post_mixer",
    )(pre, w_bf, x, gm, ln_g.reshape(1, d), ln_b.reshape(1, d), scf, shf)


def _route_select(scores, choice):
    e, w = scores.shape
    ge = e // N_GROUPS
    g3 = choice.reshape(N_GROUPS, ge, w)
    ri = lax.broadcasted_iota(jnp.int32, g3.shape, 1).astype(F32)
    m1 = jnp.max(g3, axis=1, keepdims=True)
    first = jnp.min(jnp.where(g3 == m1, ri, float(ge)), axis=1, keepdims=True)
    m2 = jnp.max(jnp.where(ri == first, NEG_INF, g3), axis=1, keepdims=True)
    gs = m1 + m2

    gi = lax.broadcasted_iota(jnp.int32, gs.shape, 0).astype(F32)
    gsel = jnp.zeros(gs.shape, F32)
    cur = gs
    for _ in range(TOPK_GROUPS):
        m = jnp.max(cur, axis=0, keepdims=True)
        f = jnp.min(jnp.where(cur == m, gi, float(N_GROUPS)), axis=0, keepdims=True)
        hit = gi == f
        gsel = jnp.where(hit, 1.0, gsel)
        cur = jnp.where(hit, NEG_INF, cur)
    emask = jnp.broadcast_to(gsel, g3.shape).reshape(e, w)
    masked = jnp.where(emask > 0.5, choice, NEG_INF)

    ei = lax.broadcasted_iota(jnp.int32, (e, w), 0).astype(F32)
    onehot = jnp.zeros((e, w), F32)
    idxs, ws = [], []
    for _ in range(TOP_K):
        m = jnp.max(masked, axis=0, keepdims=True)
        f = jnp.min(jnp.where(masked == m, ei, float(e)), axis=0, keepdims=True)
        hit = ei == f
        idxs.append(f)
        ws.append(jnp.sum(jnp.where(hit, scores, 0.0), axis=0, keepdims=True))
        masked = jnp.where(hit, NEG_INF, masked)
        onehot = jnp.where(hit, 1.0, onehot)
    return idxs, ws, onehot


def _route_kernel(x_ref, sc_ref, sh_ref, wr_ref, rb_ref, idx_ref, w_ref, rank_ref, cnt_ref, carry_ref):
    i = pl.program_id(0)
    e = wr_ref.shape[1]
    tm = x_ref.shape[0]

    @pl.when(i == 0)
    def _():
        carry_ref[...] = jnp.zeros_like(carry_ref)

    h = x_ref[...] * (1.0 + sc_ref[0]) + sh_ref[0]
    h_hi = h.astype(BF16)
    h_lo = (h - h_hi.astype(F32)).astype(BF16)
    nt = (((1,), (1,)), ((), ()))
    logits = (lax.dot_general(wr_ref[0], h_hi, nt, preferred_element_type=F32)
              + (lax.dot_general(wr_ref[0], h_lo, nt, preferred_element_type=F32)
                 + lax.dot_general(wr_ref[1], h_hi, nt, preferred_element_type=F32)))
    scores = 1.0 / (1.0 + jnp.exp(-logits))
    choice = scores + rb_ref[...]

    slab = min(ROUTE_SLAB, tm)
    slabs = [slice(j * slab, (j + 1) * slab) for j in range(tm // slab)]
    picks = [_route_select(scores[:, sl], choice[:, sl]) for sl in slabs]
    onehot = jnp.concatenate([pk[2] for pk in picks], axis=1)

    r_i = lax.broadcasted_iota(jnp.int32, (tm, tm), 0)
    c_i = lax.broadcasted_iota(jnp.int32, (tm, tm), 1)
    upper = jnp.where(r_i < c_i, 1.0, 0.0).astype(BF16)
    rk = jnp.dot(onehot.astype(BF16), upper, preferred_element_type=F32) + carry_ref[...]
    carry_ref[...] += jnp.sum(onehot, axis=1, keepdims=True)

    ei = lax.broadcasted_iota(jnp.int32, (e, slab), 0).astype(F32)
    for sl, (idxs, ws, _) in zip(slabs, picks):
        wsum = functools.reduce(lambda x, y: x + y, ws)
        for k in range(TOP_K):
            idx_ref[k:k + 1, sl] = idxs[k].astype(jnp.int32)
            w_ref[k:k + 1, sl] = ws[k] / wsum * ROUTED_SCALE
            rank_ref[k:k + 1, sl] = jnp.sum(jnp.where(ei == idxs[k], rk[:, sl], 0.0), axis=0,
                                            keepdims=True).astype(jnp.int32)
    cnt_ref[...] = jnp.broadcast_to(carry_ref[...], cnt_ref.shape).astype(jnp.int32)


def _route(x1, scf, shf, wr_t, rbias, tm):
    b, n, d = x1.shape
    t = b * n
    e = wr_t.shape[0]
    per_b = n // tm
    wr_hi = wr_t.astype(BF16)
    wr_split = jnp.stack([wr_hi, (wr_t - wr_hi.astype(F32)).astype(BF16)])
    vec = pl.BlockSpec((1, 1, d), lambda i: (i // per_b, 0, 0))
    out_t = pl.BlockSpec((TOP_K, tm), lambda i: (0, i))
    return pl.pallas_call(
        _route_kernel,
        out_shape=(jax.ShapeDtypeStruct((TOP_K, t), jnp.int32), jax.ShapeDtypeStruct((TOP_K, t), F32),
                   jax.ShapeDtypeStruct((TOP_K, t), jnp.int32), jax.ShapeDtypeStruct((e, LANES), jnp.int32)),
        grid=(t // tm,),
        in_specs=[
            pl.BlockSpec((tm, d), lambda i: (i, 0)),
            vec, vec,
            pl.BlockSpec((2, e, d), lambda i: (0, 0, 0)),
            pl.BlockSpec((e, 1), lambda i: (0, 0)),
        ],
        out_specs=(out_t, out_t, out_t, pl.BlockSpec((e, LANES), lambda i: (0, 0))),
        scratch_shapes=[pltpu.VMEM((e, 1), F32)],
        compiler_params=_cparams("arbitrary"),
        name="route",
    )(x1.reshape(t, d), scf, shf, wr_split, rbias.reshape(e, 1))


def _gather_rows(table, idx):
    m = idx.shape[0]
    w = table.shape[1]
    workers = SC_CORES * SC_SUBCORES
    n_ch = m // (workers * SC_CHUNK)
    assert m % (workers * SC_CHUNK) == 0 and n_ch % 2 == 0
    mesh = plsc.VectorSubcoreMesh(core_axis_name="c", subcore_axis_name="s",
                                  num_cores=SC_CORES, num_subcores=SC_SUBCORES)

    @functools.partial(
        pl.kernel, mesh=mesh,
        out_type=jax.ShapeDtypeStruct((m, w), table.dtype),
        scratch_types=[
            pltpu.VMEM((n_ch, SC_CHUNK), I32),
            pltpu.VMEM((SC_CHUNK, w), table.dtype), pltpu.VMEM((SC_CHUNK, w), table.dtype),
            pltpu.SemaphoreType.DMA, pltpu.SemaphoreType.DMA, pltpu.SemaphoreType.DMA, pltpu.SemaphoreType.DMA,
        ],
        name="sc_gather_rows",
    )
    def gather(table_hbm, idx_hbm, out_hbm, idx_all, buf0, buf1, gsem0, gsem1, wsem0, wsem1):
        first = (lax.axis_index("s") * SC_CORES + lax.axis_index("c")) * n_ch
        bufs, gsem, wsem = (buf0, buf1), (gsem0, gsem1), (wsem0, wsem1)
        pltpu.sync_copy(idx_hbm.at[pl.ds(first, n_ch)], idx_all)

        def gather_copy(j, s):
            return pltpu.make_async_copy(table_hbm.at[idx_all.at[j]], bufs[s], gsem[s])

        def write_copy(j, s):
            rows = pl.ds(pl.multiple_of((first + j) * SC_CHUNK, SC_CHUNK), SC_CHUNK)
            return pltpu.make_async_copy(bufs[s], out_hbm.at[rows], wsem[s])

        gather_copy(0, 0).start()

        @pl.loop(0, n_ch, step=2)
        def _(jj):
            for s in range(2):
                j = jj + s

                @pl.when(j >= 1)
                def _():
                    write_copy(j - 1, 1 - s).wait()

                @pl.when(j + 1 < n_ch)
                def _():
                    gather_copy(j + 1, 1 - s).start()

                gather_copy(j, s).wait()
                write_copy(j, s).start()

        write_copy(n_ch - 1, 1).wait()

    return gather(table, idx.reshape(m // SC_CHUNK, SC_CHUNK))


def _scatter_rows(src, pos3, p):
    w = src.shape[1]
    n_chunks, k, ch = pos3.shape
    workers = SC_CORES * SC_SUBCORES
    per_w = n_chunks // workers
    assert ch == SC_CHUNK and n_chunks % workers == 0 and src.shape[0] == n_chunks * ch
    mesh = plsc.VectorSubcoreMesh(core_axis_name="c", subcore_axis_name="s",
                                  num_cores=SC_CORES, num_subcores=SC_SUBCORES)

    @functools.partial(
        pl.kernel, mesh=mesh,
        out_type=jax.ShapeDtypeStruct((p, w), src.dtype),
        scratch_types=[
            pltpu.VMEM((k, ch), I32),
            pltpu.VMEM((ch, w), src.dtype),
            pltpu.SemaphoreType.DMA,
        ],
        name="sc_scatter_rows",
    )
    def scatter(src_hbm, pos_hbm, out_hbm, idx_v, rows_v, sem):
        first = (lax.axis_index("s") * SC_CORES + lax.axis_index("c")) * per_w

        @pl.loop(0, per_w)
        def _(j):
            c = first + j
            pltpu.sync_copy(pos_hbm.at[c], idx_v)
            pltpu.sync_copy(src_hbm.at[pl.ds(pl.multiple_of(c * ch, ch), ch)], rows_v)
            copies = [pltpu.async_copy(rows_v, out_hbm.at[idx_v.at[kk]], sem) for kk in range(k)]
            for cp in copies:
                cp.wait()

    return scatter(src, pos3)


def _experts_kernel(ps_ref, nb_ref, cnt_ref, nt_ref, xs_hbm, wg_ref, wu_ref, wd_ref, y_hbm,
                    xbuf, ybuf, in_sem, out_sem, wg_bf, wu_bf, wd_bf):
    e = pl.program_id(0)
    nb, cnt, n_total = nb_ref[e], cnt_ref[e], nt_ref[0]
    g0 = ps_ref[e] // MOE_ROWS

    def in_copy(g):
        rows = pl.ds(pl.multiple_of(g * MOE_ROWS, MOE_ROWS), MOE_ROWS)
        return pltpu.make_async_copy(xs_hbm.at[rows], xbuf.at[g % MOE_SLOTS], in_sem.at[g % MOE_SLOTS])

    def out_copy(g):
        rows = pl.ds(pl.multiple_of(g * MOE_ROWS, MOE_ROWS), MOE_ROWS)
        return pltpu.make_async_copy(ybuf.at[g % MOE_SLOTS], y_hbm.at[rows], out_sem.at[g % MOE_SLOTS])

    @pl.when(e == 0)
    def _():
        for j in range(MOE_LOOKAHEAD):
            @pl.when(j < n_total)
            def _():
                in_copy(j).start()

    @pl.when(nb > 0)
    def _():
        wg_bf[...] = wg_ref[0, 0].astype(BF16)
        wu_bf[...] = wu_ref[0, 0].astype(BF16)
        wd_bf[...] = wd_ref[0, 0].astype(BF16)

        def process(b, width):
            g = g0 + b
            for u in range(width):
                @pl.when(g + u + MOE_LOOKAHEAD < n_total)
                def _():
                    in_copy(g + u + MOE_LOOKAHEAD).start()

            for u in range(width):
                in_copy(g + u).wait()

                @pl.when(g + u >= MOE_SLOTS)
                def _():
                    out_copy(g + u - MOE_SLOTS).wait()

            packed = jnp.concatenate([xbuf[(g + u) % MOE_SLOTS] for u in range(width)], axis=0)
            row = lax.broadcasted_iota(I32, (width * MOE_ROWS, 1), 0) + b * MOE_ROWS
            x_lo, x_hi = (v.astype(BF16) for v in _unpack_bf16_pairs(jnp.where(row < cnt, packed, 0)))
            half = x_lo.shape[1]

            def up(w_bf):
                return (jnp.dot(x_lo, w_bf[:half, :], preferred_element_type=F32)
                        + jnp.dot(x_hi, w_bf[half:, :], preferred_element_type=F32))

            hb = (_silu(up(wg_bf)) * up(wu_bf)).astype(BF16)
            y = _pack_bf16_pairs(jnp.dot(hb, wd_bf[...], preferred_element_type=F32))
            for u in range(width):
                ybuf[(g + u) % MOE_SLOTS] = y[u * MOE_ROWS:(u + 1) * MOE_ROWS]
                out_copy(g + u).start()

        start = 0
        for width in MOE_GROUPS:
            count = (nb - start) // width

            def body(i, c, width=width, start=start):
                process(start + width * i, width)
                return c

            lax.fori_loop(0, count, body, 0)
            start = start + count * width

    @pl.when(e == pl.num_programs(0) - 1)
    def _():
        for j in range(MOE_SLOTS):
            @pl.when(n_total - 1 - j >= 0)
            def _():
                out_copy(n_total - 1 - j).wait()


def _routed_experts(xs, wg, wu, wd, layer, pstarts, nblk, counts, n_total):
    p, dp = xs.shape
    _, e, d, f = wg.shape
    wspec = lambda r, c: pl.BlockSpec((1, 1, r, c), lambda i, ps, nb, cnt, nt: (layer, i, 0, 0))
    grid_spec = pltpu.PrefetchScalarGridSpec(
        num_scalar_prefetch=4,
        grid=(e,),
        in_specs=[pl.BlockSpec(memory_space=pl.ANY), wspec(d, f), wspec(d, f), wspec(f, d)],
        out_specs=pl.BlockSpec(memory_space=pl.ANY),
        scratch_shapes=[
            pltpu.VMEM((MOE_SLOTS, MOE_ROWS, dp), I32), pltpu.VMEM((MOE_SLOTS, MOE_ROWS, dp), I32),
            pltpu.SemaphoreType.DMA((MOE_SLOTS,)), pltpu.SemaphoreType.DMA((MOE_SLOTS,)),
            pltpu.VMEM((d, f), BF16), pltpu.VMEM((d, f), BF16), pltpu.VMEM((f, d), BF16),
        ],
    )
    return pl.pallas_call(
        _experts_kernel,
        out_shape=jax.ShapeDtypeStruct((p, dp), I32),
        grid_spec=grid_spec,
        compiler_params=_cparams("arbitrary"),
        name="routed_experts",
    )(pstarts, nblk, counts, n_total, xs, wg, wu, wd)


def _combine_kernel(yg_ref, w_ref, hf_ref, sg_ref, su_ref, sd_ref, x_ref, gf_ref, lg_ref, lb_ref, o_ref, *, alpha):
    w = w_ref[...]
    r_lo, r_hi = _unpack_bf16_pairs(yg_ref[0])
    r_lo, r_hi = w[:, 0:1] * r_lo, w[:, 0:1] * r_hi
    for k in range(1, TOP_K):
        y_lo, y_hi = _unpack_bf16_pairs(yg_ref[k])
        r_lo, r_hi = r_lo + w[:, k:k + 1] * y_lo, r_hi + w[:, k:k + 1] * y_hi
    routed = jnp.concatenate([r_lo, r_hi], axis=1)
    hf = jnp.concatenate(_unpack_bf16_pairs(hf_ref[...]), axis=1).astype(BF16)
    g = jnp.dot(hf, sg_ref[...], preferred_element_type=F32)
    u = jnp.dot(hf, su_ref[...], preferred_element_type=F32)
    shared = jnp.dot((_silu(g) * u).astype(BF16), sd_ref[...], preferred_element_type=F32)
    o_ref[...] = _layer_norm(alpha * x_ref[...] + gf_ref[0] * (routed + shared), lg_ref[...], lb_ref[...])


def _combine(yg, w_tk, hf, sg_bf, su_bf, sd_bf, x1, gf, ln_g, ln_b, alpha, tm, per_b):
    t, d = x1.shape
    f = sg_bf.shape[1]
    row = pl.BlockSpec((1, d), lambda i: (0, 0))
    tile = pl.BlockSpec((tm, d), lambda i: (i, 0))
    return pl.pallas_call(
        functools.partial(_combine_kernel, alpha=alpha),
        out_shape=jax.ShapeDtypeStruct((t, d), F32),
        grid=(t // tm,),
        in_specs=[
            pl.BlockSpec((TOP_K, tm, d // 2), lambda i: (0, i, 0)),
            pl.BlockSpec((tm, TOP_K), lambda i: (i, 0)),
            pl.BlockSpec((tm, d // 2), lambda i: (i, 0)),
            pl.BlockSpec((d, f), lambda i: (0, 0)),
            pl.BlockSpec((d, f), lambda i: (0, 0)),
            pl.BlockSpec((f, d), lambda i: (0, 0)),
            tile,
            pl.BlockSpec((1, 1, d), lambda i: (i // per_b, 0, 0)),
            row, row,
        ],
        out_specs=tile,
        compiler_params=_cparams("parallel"),
        name="moe_combine",
    )(yg, w_tk, hf, sg_bf, su_bf, sd_bf, x1, gf, ln_g.reshape(1, d), ln_b.reshape(1, d))


def _moe_layer(x1, hf, scf, shf, gf, router_w, router_bias, wg, wu, wd, layer, sg, su, sd, ln_g, ln_b, alpha, tm):
    b, n, d = x1.shape
    t = b * n
    e = router_w.shape[1]
    idx_t, w_t, rank_t, cnt = _route(x1, scf, shf, router_w.T, router_bias, tm)

    counts = cnt[:, 0]
    padded = (counts + MOE_ROWS - 1) // MOE_ROWS * MOE_ROWS
    pends = jnp.cumsum(padded)
    pstarts = pends - padded
    sel = idx_t[:, :, None] == jnp.arange(e, dtype=I32)
    pos_t = jnp.sum(jnp.where(sel, pstarts, 0), axis=-1) + rank_t
    p = t * TOP_K + e * MOE_ROWS
    pos3 = pos_t.reshape(TOP_K, t // SC_CHUNK, SC_CHUNK).transpose(1, 0, 2)

    hf2 = hf.reshape(t, d // 2)
    xs = _scatter_rows(hf2, pos3, p)
    yb = _routed_experts(xs, wg, wu, wd, layer, pstarts.astype(I32), (padded // MOE_ROWS).astype(I32), counts,
                         (pends[-1:] // MOE_ROWS).astype(I32))
    yg = _gather_rows(yb, pos_t.reshape(-1)).reshape(TOP_K, t, d // 2)
    out = _combine(yg, w_t.T, hf2, sg.astype(BF16), su.astype(BF16), sd.astype(BF16), x1.reshape(t, d), gf,
                   ln_g, ln_b, alpha, tm, n // tm)
    return out.reshape(b, n, d)


def _rope_tables(n):
    rows = n // GRID_W
    row_pos = jnp.repeat(jnp.arange(rows, dtype=F32), GRID_W)
    col_pos = jnp.tile(jnp.arange(GRID_W, dtype=F32), rows)
    half = LANES // 4
    lane = jnp.arange(LANES)
    in_blk = lane % half
    freq = ROPE_THETA ** (-(2.0 * (in_blk % (half // 2)).astype(F32)) / half)
    use_col = (lane // half) % 2 == 1
    pos = jnp.where(use_col[None, :], col_pos[:, None], row_pos[:, None])
    ang = pos * freq[None, :]
    lo = (in_blk < half // 2)[None, :]
    sin = jnp.sin(ang)
    return jnp.cos(ang), jnp.where(lo, -sin, 0.0), jnp.where(lo, 0.0, sin)


def kernel(x, c, ctx, c_ctx, w_mod, b_mod, ln_g, ln_b, attn_w_in, attn_w_out, attn_lambda, attn_subln_g,
           sgu_w_in, sgu_b_in, sgu_norm_g, sgu_norm_b, sgu_w_s, sgu_b_s, sgu_w_out,
           router_w, router_bias, exp_w_gate, exp_w_up, exp_w_down, sh_w_gate, sh_w_up, sh_w_down):
    b, n, d = x.shape
    depth = w_mod.shape[0]
    assert b <= 7 and d == ATT_HEADS * LANES and n % GRID_W == 0
    alpha = (2 * depth) ** 0.25
    head_dim = d // ATT_HEADS // 2
    tm = TOKEN_TILE if n % TOKEN_TILE == 0 else TOKEN_TILE // 2

    cs = jnp.zeros((8, d), F32).at[:b].set(c).at[b].set(c_ctx)
    mods = _modulation(cs, w_mod, b_mod)

    def mod_vec(i, j):
        return mods[i, :, j * d:(j + 1) * d].reshape(8, 1, d)

    for i in range(depth):
        sh_m, sc_m, g_m, sh_f, sc_f, g_f = (mod_vec(i, j) for j in range(6))
        if i % N_MIXERS == 0:
            a = i // N_MIXERS
            lam_init = 0.8 - 0.6 * math.exp(-0.3 * i)
            w_in_bf = attn_w_in[a].astype(BF16)
            cos, slo, shi = _rope_tables(n)
            q_scale = head_dim ** -0.5 * math.log2(math.e)
            q, k, vt = _qkv_proj(x, sc_m, sh_m, w_in_bf, cos, slo, shi, q_scale, tm)
            kc, vct = _kv_ctx_proj(ctx, sc_m[b:b + 1], sh_m[b:b + 1], w_in_bf[:, d:])
            pre = _diff_attention(q, k, vt, kc, vct, attn_lambda[a], attn_subln_g[a], lam_init,
                                  min(ATT_QUERY_TILE, n), min(ATT_KEY_CHUNK, n // 2))
            w_out_bf = attn_w_out[a].astype(BF16)
        else:
            s = i // N_MIXERS
            pre = _sgu_mixer(x, sc_m, sh_m, sgu_w_in[s].astype(BF16), sgu_b_in[s], sgu_norm_g[s], sgu_norm_b[s],
                             sgu_w_s[s].astype(BF16), sgu_b_s[s].T, tm)
            w_out_bf = sgu_w_out[s].astype(BF16)
        x1, hf = _post_mixer(pre, w_out_bf, x, g_m, ln_g[i, 0], ln_b[i, 0], sc_f, sh_f, alpha, tm)
        x = _moe_layer(x1, hf, sc_f, sh_f, g_f, router_w[i], router_bias[i], exp_w_gate, exp_w_up, exp_w_down, i,
                       sh_w_gate[i], sh_w_up[i], sh_w_down[i], ln_g[i, 1], ln_b[i, 1], alpha, tm)
    return x
```

```python
import functools
import math

import jax
import jax.numpy as jnp
from jax import lax
from jax.experimental import pallas as pl
from jax.experimental.pallas import tpu as pltpu
from jax.experimental.pallas import tpu_sc as plsc

F32 = jnp.float32
BF16 = jnp.bfloat16
I32 = jnp.int32

GRID_W = 64
ATT_HEADS = 8
ROPE_THETA = 10000.0
SGU_CHUNK = 128
SGU_GROUPS = 8
TOP_K = 8
N_GROUPS = 8
TOPK_GROUPS = 4
ROUTED_SCALE = 2.5
LN_EPS = 1e-5
N_MIXERS = 2

LANES = 128
MOE_ROWS = 256
ROUTE_SLAB = 512
MOE_GROUPS = (2, 1)
MOE_LOOKAHEAD = 6
MOE_SLOTS = MOE_LOOKAHEAD + MOE_GROUPS[0]
TOKEN_TILE = 512
ATT_QUERY_TILE = 1024
ATT_KEY_CHUNK = 1024
ATT_STRIP = 16
SC_CORES = 2
SC_SUBCORES = 16
SC_CHUNK = 64
SC_SCATTER_CHUNK = 128
VMEM_LIMIT = 56 * 1024 * 1024
NEG_INF = float("-inf")


def _cparams(*sem):
    return pltpu.CompilerParams(dimension_semantics=sem, vmem_limit_bytes=VMEM_LIMIT)


def _layer_norm(z, g, b):
    mu = jnp.mean(z, axis=-1, keepdims=True)
    zc = z - mu
    var = jnp.mean(zc * zc, axis=-1, keepdims=True)
    return zc * lax.rsqrt(var + LN_EPS) * g + b


def _silu(x):
    return x * (1.0 / (1.0 + jnp.exp(-x)))


_HIGH_HALF = -65536


def _pack_bf16_pairs(y):
    w = y.shape[1] // 2
    bits = lax.bitcast_convert_type(y.astype(BF16).astype(F32), I32)
    return lax.shift_right_logical(bits[:, :w], 16) | (bits[:, w:] & _HIGH_HALF)


def _unpack_bf16_pairs(p):
    return (lax.bitcast_convert_type(lax.shift_left(p, 16), F32),
            lax.bitcast_convert_type(p & _HIGH_HALF, F32))


def _mod_kernel(cs_ref, w_ref, b_ref, o_ref):
    s = _silu(cs_ref[...])
    o_ref[0] = jnp.dot(s, w_ref[0], precision=lax.Precision.HIGHEST,
                       preferred_element_type=F32) + b_ref[0]


def _modulation(cs, w_mod, b_mod):
    depth, d, n6 = w_mod.shape
    tn = n6 // 4
    return pl.pallas_call(
        _mod_kernel,
        out_shape=jax.ShapeDtypeStruct((depth, 8, n6), F32),
        grid=(depth, n6 // tn),
        in_specs=[
            pl.BlockSpec((8, d), lambda l, j: (0, 0)),
            pl.BlockSpec((1, d, tn), lambda l, j: (l, 0, j)),
            pl.BlockSpec((1, 1, tn), lambda l, j: (l, 0, j)),
        ],
        out_specs=pl.BlockSpec((1, 8, tn), lambda l, j: (l, 0, j)),
        compiler_params=_cparams("parallel", "parallel"),
        name="modulation",
    )(cs, w_mod, b_mod.reshape(depth, 1, n6))


def _rope(xh, cos, sin_lo, sin_hi):
    return xh * cos + pltpu.roll(xh, LANES - 16, 1) * sin_lo + pltpu.roll(xh, 16, 1) * sin_hi


def _qkv_kernel(x_ref, sc_ref, sh_ref, w_ref, cos_ref, slo_ref, shi_ref, q_ref, k_ref, v_ref, *, q_scale):
    d = x_ref.shape[-1]
    h = (x_ref[0] * (1.0 + sc_ref[0]) + sh_ref[0]).astype(BF16)
    cos, slo, shi = cos_ref[...], slo_ref[...], shi_ref[...]
    q = jnp.dot(h, w_ref[:, 0:d], preferred_element_type=F32)
    for hd in range(ATT_HEADS):
        q_ref[0, hd] = (_rope(q[:, hd * LANES:(hd + 1) * LANES], cos, slo, shi) * q_scale).astype(BF16)
    k = jnp.dot(h, w_ref[:, d:2 * d], preferred_element_type=F32)
    for hd in range(ATT_HEADS):
        k_ref[0, hd] = _rope(k[:, hd * LANES:(hd + 1) * LANES], cos, slo, shi).astype(BF16)
    v = jnp.dot(h, w_ref[:, 2 * d:3 * d], preferred_element_type=F32)
    for hd in range(ATT_HEADS):
        v_ref[0, hd] = v[:, hd * LANES:(hd + 1) * LANES].T.astype(BF16)


def _qkv_proj(x, sc, sh, w_bf, cos, slo, shi, q_scale, tn):
    b, n, d = x.shape
    hd_shape = jax.ShapeDtypeStruct((b, ATT_HEADS, n, LANES), BF16)
    vec = pl.BlockSpec((1, 1, d), lambda bi, i: (bi, 0, 0))
    tab = pl.BlockSpec((tn, LANES), lambda bi, i: (i, 0))
    out = pl.BlockSpec((1, ATT_HEADS, tn, LANES), lambda bi, i: (bi, 0, i, 0))
    out_t = pl.BlockSpec((1, ATT_HEADS, LANES, tn), lambda bi, i: (bi, 0, 0, i))
    return pl.pallas_call(
        functools.partial(_qkv_kernel, q_scale=q_scale),
        out_shape=(hd_shape, hd_shape, jax.ShapeDtypeStruct((b, ATT_HEADS, LANES, n), BF16)),
        grid=(b, n // tn),
        in_specs=[
            pl.BlockSpec((1, tn, d), lambda bi, i: (bi, i, 0)),
            vec, vec,
            pl.BlockSpec((d, 3 * d), lambda bi, i: (0, 0)),
            tab, tab, tab,
        ],
        out_specs=(out, out, out_t),
        compiler_params=_cparams("parallel", "parallel"),
        name="qkv_proj",
    )(x, sc, sh, w_bf, cos, slo, shi)


def _kv_ctx_kernel(x_ref, sc_ref, sh_ref, w_ref, k_ref, v_ref):
    d = x_ref.shape[-1]
    h = (x_ref[0] * (1.0 + sc_ref[0]) + sh_ref[0]).astype(BF16)
    k = jnp.dot(h, w_ref[:, 0:d], preferred_element_type=F32)
    v = jnp.dot(h, w_ref[:, d:2 * d], preferred_element_type=F32)
    for hd in range(ATT_HEADS):
        k_ref[0, hd] = k[:, hd * LANES:(hd + 1) * LANES].astype(BF16)
        v_ref[0, hd] = v[:, hd * LANES:(hd + 1) * LANES].T.astype(BF16)


def _kv_ctx_proj(ctx, sc, sh, w_kv_bf):
    b, c, d = ctx.shape
    hd_shape = jax.ShapeDtypeStruct((b, ATT_HEADS, c, LANES), BF16)
    vec = pl.BlockSpec((1, 1, d), lambda bi: (0, 0, 0))
    out = pl.BlockSpec((1, ATT_HEADS, c, LANES), lambda bi: (bi, 0, 0, 0))
    out_t = pl.BlockSpec((1, ATT_HEADS, LANES, c), lambda bi: (bi, 0, 0, 0))
    return pl.pallas_call(
        _kv_ctx_kernel,
        out_shape=(hd_shape, jax.ShapeDtypeStruct((b, ATT_HEADS, LANES, c), BF16)),
        grid=(b,),
        in_specs=[
            pl.BlockSpec((1, c, d), lambda bi: (bi, 0, 0)),
            vec, vec,
            pl.BlockSpec((d, 2 * d), lambda bi: (0, 0)),
        ],
        out_specs=(out, out_t),
        compiler_params=_cparams("parallel"),
        name="kv_ctx_proj",
    )(ctx, sc, sh, w_kv_bf)


def _attn_finish(acc_a, l_a, acc_b, l_b, lamp_ref, g_ref, lam_init):
    lp = lamp_ref[...]
    lam = (jnp.exp(jnp.sum(lp[0:1] * lp[1:2], axis=-1, keepdims=True))
           - jnp.exp(jnp.sum(lp[2:3] * lp[3:4], axis=-1, keepdims=True)) + lam_init)
    o = acc_a / l_a - lam * (acc_b / l_b)
    o = o * lax.rsqrt(jnp.mean(o * o, axis=0, keepdims=True) + LN_EPS) * g_ref[...] * (1.0 - lam_init)
    return o.T.astype(BF16)


def _attn_kernel(q_ref, k_ref, vt_ref, kc_ref, vct_ref, lamp_ref, g_ref, o_ref,
                 s00, s01, s10, s11, p0, p1, acc0, acc1, *, tk, lam_init):
    q = q_ref[0, 0]
    tq = q.shape[0]
    half = LANES // 2
    lane = lax.broadcasted_iota(jnp.int32, q.shape, 1)
    zero = jnp.zeros_like(q)
    qs = (jnp.where(lane < half, q, zero), jnp.where(lane >= half, q, zero))
    n_chunks = k_ref.shape[2] // tk
    s_scr = ((s00, s01), (s10, s11))
    p_scr, acc_scr = (p0, p1), (acc0, acc1)
    nt = (((1,), (1,)), ((), ()))

    def scores(slot, kc):
        width = kc.shape[0]
        for mp in range(2):
            s_scr[slot][mp][0:width, :] = lax.dot_general(kc, qs[mp], nt, preferred_element_type=F32)

    def absorb(slot, vct, shift, sums):
        width = vct.shape[1]
        sums = list(sums)
        for mp in range(2):
            part = sums[mp]
            for r in range(width // ATT_STRIP):
                lo = r * ATT_STRIP
                tiles = [jnp.exp2(s_scr[slot][mp][lo + 8 * u:lo + 8 * (u + 1), :] - shift[mp])
                         for u in range(ATT_STRIP // 8)]
                p_scr[mp][lo:lo + ATT_STRIP, :] = jnp.concatenate(tiles, axis=0).astype(BF16)
                part = part + functools.reduce(lambda x, y: x + y, tiles)
            sums[mp] = part
            acc_scr[mp][...] += jnp.dot(vct, p_scr[mp][0:width, :], preferred_element_type=F32)
        return tuple(sums)

    def k_chunk(j):
        return k_ref[0, 0, pl.ds(pl.multiple_of(j * tk, tk), tk), :]

    def vt_chunk(j):
        return vt_ref[0, 0, :, pl.ds(pl.multiple_of(j * tk, tk), tk)]

    scores(0, k_chunk(0))
    shift = tuple(jnp.broadcast_to(jnp.max(s_scr[0][mp][...], axis=0, keepdims=True), (8, tq)) for mp in range(2))
    for mp in range(2):
        acc_scr[mp][...] = jnp.zeros((LANES, tq), F32)
    sums = (jnp.zeros((8, tq), F32), jnp.zeros((8, tq), F32))

    def pair(jj, sums):
        j0 = 2 * jj
        scores(1, k_chunk(j0 + 1))
        sums = absorb(0, vt_chunk(j0), shift, sums)
        scores(0, k_chunk(j0 + 2))
        return absorb(1, vt_chunk(j0 + 1), shift, sums)

    sums = lax.fori_loop(0, n_chunks // 2 - 1, pair, sums)
    scores(1, k_chunk(n_chunks - 1))
    sums = absorb(0, vt_chunk(n_chunks - 2), shift, sums)
    scores(0, kc_ref[0, 0])
    sums = absorb(1, vt_chunk(n_chunks - 1), shift, sums)
    sums = absorb(0, vct_ref[0, 0], shift, sums)

    tot = [jnp.sum(sums[mp], axis=0, keepdims=True) for mp in range(2)]
    bad = sum(jnp.sum(jnp.where(jnp.isfinite(x), 0.0, 1.0)) for x in (tot[0], tot[1], acc0[...], acc1[...]))

    @pl.when(bad == 0.0)
    def _():
        o_ref[0] = _attn_finish(acc0[...], tot[0], acc1[...], tot[1], lamp_ref, g_ref, lam_init)

    @pl.when(bad != 0.0)
    def _():
        def update(carry, kc, vct):
            new = []
            for mp in range(2):
                m, l, acc = carry[mp]
                s = lax.dot_general(kc, qs[mp], nt, preferred_element_type=F32)
                mn = jnp.maximum(m, jnp.max(s, axis=0, keepdims=True))
                a = jnp.exp2(m - mn)
                p = jnp.exp2(s - mn)
                new.append((mn, a * l + jnp.sum(p, axis=0, keepdims=True),
                            a * acc + jnp.dot(vct, p.astype(BF16), preferred_element_type=F32)))
            return tuple(new)

        init = tuple((jnp.full((1, tq), NEG_INF, F32), jnp.zeros((1, tq), F32), jnp.zeros((LANES, tq), F32))
                     for _ in range(2))
        carry = lax.fori_loop(0, n_chunks, lambda j, c: update(c, k_chunk(j), vt_chunk(j)), init)
        (_, l_a, acc_a), (_, l_b, acc_b) = update(carry, kc_ref[0, 0], vct_ref[0, 0])
        o_ref[0] = _attn_finish(acc_a, l_a, acc_b, l_b, lamp_ref, g_ref, lam_init)


def _diff_attention(q, k, vt, kc, vct, lam_p, subln_g, lam_init, tq, tk):
    b, h, n, _ = q.shape
    c = kc.shape[2]
    assert n % tk == 0 and (n // tk) % 2 == 0 and c <= tk and tk % ATT_STRIP == 0 and c % ATT_STRIP == 0
    spec = lambda r, cols: pl.BlockSpec((1, 1, r, cols), lambda bi, hi, i: (bi, hi, 0, 0))
    return pl.pallas_call(
        functools.partial(_attn_kernel, tk=tk, lam_init=lam_init),
        out_shape=jax.ShapeDtypeStruct((b, n, h * LANES), BF16),
        grid=(b, h, n // tq),
        in_specs=[
            pl.BlockSpec((1, 1, tq, LANES), lambda bi, hi, i: (bi, hi, i, 0)),
            spec(n, LANES), spec(LANES, n), spec(c, LANES), spec(LANES, c),
            pl.BlockSpec(lam_p.shape, lambda bi, hi, i: (0, 0)),
            pl.BlockSpec((LANES, 1), lambda bi, hi, i: (0, 0)),
        ],
        out_specs=pl.BlockSpec((1, tq, LANES), lambda bi, hi, i: (bi, i, hi)),
        scratch_shapes=[
            *[pltpu.VMEM((tk, tq), F32)] * 4,
            *[pltpu.VMEM((tk, tq), BF16)] * 2,
            *[pltpu.VMEM((LANES, tq), F32)] * 2,
        ],
        compiler_params=_cparams("parallel", "parallel", "parallel"),
        name="diff_attention",
    )(q, k, vt, kc, vct, lam_p, subln_g.reshape(LANES, 1))


def _sgu_kernel(x_ref, sc_ref, sh_ref, w_ref, b_ref, ng_ref, nb_ref, ws_ref, bs_ref, t_ref):
    f = t_ref.shape[-1]
    cg = f // SGU_GROUPS
    tm = x_ref.shape[1]
    h = (x_ref[0] * (1.0 + sc_ref[0]) + sh_ref[0]).astype(BF16)
    z = jnp.dot(h, w_ref[...], preferred_element_type=F32) + b_ref[...]
    z = 0.5 * z * (1.0 + lax.erf(z * (2.0 ** -0.5)))
    u = z[:, :f]
    v = _layer_norm(z[:, f:], ng_ref[...], nb_ref[...]).astype(BF16)
    for c in range(tm // SGU_CHUNK):
        rows = slice(c * SGU_CHUNK, (c + 1) * SGU_CHUNK)
        for g in range(SGU_GROUPS):
            cols = slice(g * cg, (g + 1) * cg)
            vm = jnp.dot(ws_ref[g], v[rows, cols], preferred_element_type=F32) + bs_ref[:, g:g + 1]
            t_ref[0, rows, cols] = (u[rows, cols] * vm).astype(BF16)


def _sgu_mixer(x, sc, sh, w_in_bf, b_in, norm_g, norm_b, w_s_bf, b_s_t, tm):
    b, n, d = x.shape
    f2 = w_in_bf.shape[1]
    f = f2 // 2
    vec = pl.BlockSpec((1, 1, d), lambda bi, i: (bi, 0, 0))
    full2 = lambda a: pl.BlockSpec(a.shape, lambda bi, i: (0,) * a.ndim)
    b_in2, ng2, nb2 = b_in.reshape(1, f2), norm_g.reshape(1, f), norm_b.reshape(1, f)
    return pl.pallas_call(
        _sgu_kernel,
        out_shape=jax.ShapeDtypeStruct((b, n, f), BF16),
        grid=(b, n // tm),
        in_specs=[
            pl.BlockSpec((1, tm, d), lambda bi, i: (bi, i, 0)),
            vec, vec,
            full2(w_in_bf), full2(b_in2), full2(ng2), full2(nb2), full2(w_s_bf), full2(b_s_t),
        ],
        out_specs=pl.BlockSpec((1, tm, f), lambda bi, i: (bi, i, 0)),
        compiler_params=_cparams("parallel", "parallel"),
        name="sgu_mixer",
    )(x, sc, sh, w_in_bf, b_in2, ng2, nb2, w_s_bf, b_s_t)


def _post_kernel(pre_ref, w_ref, x_ref, gm_ref, lg_ref, lb_ref, scf_ref, shf_ref, x1_ref, hf_ref, *, alpha):
    y = jnp.dot(pre_ref[0], w_ref[...], preferred_element_type=F32)
    x1 = _layer_norm(alpha * x_ref[0] + gm_ref[0] * y, lg_ref[...], lb_ref[...])
    x1_ref[0] = x1
    hf_ref[0] = _pack_bf16_pairs(x1 * (1.0 + scf_ref[0]) + shf_ref[0])


def _post_mixer(pre, w_bf, x, gm, ln_g, ln_b, scf, shf, alpha, tm):
    b, n, d = x.shape
    kd = pre.shape[-1]
    vec = pl.BlockSpec((1, 1, d), lambda bi, i: (bi, 0, 0))
    row = pl.BlockSpec((1, d), lambda bi, i: (0, 0))
    tile = pl.BlockSpec((1, tm, d), lambda bi, i: (bi, i, 0))
    return pl.pallas_call(
        functools.partial(_post_kernel, alpha=alpha),
        out_shape=(jax.ShapeDtypeStruct((b, n, d), F32), jax.ShapeDtypeStruct((b, n, d // 2), I32)),
        grid=(b, n // tm),
        in_specs=[
            pl.BlockSpec((1, tm, kd), lambda bi, i: (bi, i, 0)),
            pl.BlockSpec((kd, d), lambda bi, i: (0, 0)),
            tile, vec, row, row, vec, vec,
        ],
        out_specs=(tile, pl.BlockSpec((1, tm, d // 2), lambda bi, i: (bi, i, 0))),
        compiler_params=_cparams("parallel", "parallel"),
        name="post_mixer",
    )(pre, w_bf, x, gm, ln_g.reshape(1, d), ln_b.reshape(1, d), scf, shf)


def _route_select(scores, choice):
    e, w = scores.shape
    ge = e // N_GROUPS
    g3 = choice.reshape(N_GROUPS, ge, w)
    ri = lax.broadcasted_iota(jnp.int32, g3.shape, 1).astype(F32)
    m1 = jnp.max(g3, axis=1, keepdims=True)
    first = jnp.min(jnp.where(g3 == m1, ri, float(ge)), axis=1, keepdims=True)
    m2 = jnp.max(jnp.where(ri == first, NEG_INF, g3), axis=1, keepdims=True)
    gs = m1 + m2

    gi = lax.broadcasted_iota(jnp.int32, gs.shape, 0).astype(F32)
    gsel = jnp.zeros(gs.shape, F32)
    cur = gs
    for _ in range(TOPK_GROUPS):
        m = jnp.max(cur, axis=0, keepdims=True)
        f = jnp.min(jnp.where(cur == m, gi, float(N_GROUPS)), axis=0, keepdims=True)
        hit = gi == f
        gsel = jnp.where(hit, 1.0, gsel)
        cur = jnp.where(hit, NEG_INF, cur)
    emask = jnp.broadcast_to(gsel, g3.shape).reshape(e, w)
    masked = jnp.where(emask > 0.5, choice, NEG_INF)

    ei = lax.broadcasted_iota(jnp.int32, (e, w), 0).astype(F32)
    onehot = jnp.zeros((e, w), F32)
    idxs, ws = [], []
    for _ in range(TOP_K):
        m = jnp.max(masked, axis=0, keepdims=True)
        f = jnp.min(jnp.where(masked == m, ei, float(e)), axis=0, keepdims=True)
        hit = ei == f
        idxs.append(f)
        ws.append(jnp.sum(jnp.where(hit, scores, 0.0), axis=0, keepdims=True))
        masked = jnp.where(hit, NEG_INF, masked)
        onehot = jnp.where(hit, 1.0, onehot)
    return idxs, ws, onehot


def _route_kernel(x_ref, sc_ref, sh_ref, wr_ref, rb_ref, idx_ref, w_ref, rank_ref, cnt_ref, carry_ref):
    i = pl.program_id(0)
    e = wr_ref.shape[1]
    tm = x_ref.shape[0]

    @pl.when(i == 0)
    def _():
        carry_ref[...] = jnp.zeros_like(carry_ref)

    h = x_ref[...] * (1.0 + sc_ref[0]) + sh_ref[0]
    h_hi = h.astype(BF16)
    h_lo = (h - h_hi.astype(F32)).astype(BF16)
    nt = (((1,), (1,)), ((), ()))
    logits = (lax.dot_general(wr_ref[0], h_hi, nt, preferred_element_type=F32)
              + (lax.dot_general(wr_ref[0], h_lo, nt, preferred_element_type=F32)
                 + lax.dot_general(wr_ref[1], h_hi, nt, preferred_element_type=F32)))
    scores = 1.0 / (1.0 + jnp.exp(-logits))
    choice = scores + rb_ref[...]

    slab = min(ROUTE_SLAB, tm)
    slabs = [slice(j * slab, (j + 1) * slab) for j in range(tm // slab)]
    picks = [_route_select(scores[:, sl], choice[:, sl]) for sl in slabs]
    onehot = jnp.concatenate([pk[2] for pk in picks], axis=1)

    r_i = lax.broadcasted_iota(jnp.int32, (tm, tm), 0)
    c_i = lax.broadcasted_iota(jnp.int32, (tm, tm), 1)
    upper = jnp.where(r_i < c_i, 1.0, 0.0).astype(BF16)
    rk = jnp.dot(onehot.astype(BF16), upper, preferred_element_type=F32) + carry_ref[...]
    carry_ref[...] += jnp.sum(onehot, axis=1, keepdims=True)

    ei = lax.broadcasted_iota(jnp.int32, (e, slab), 0).astype(F32)
    for sl, (idxs, ws, _) in zip(slabs, picks):
        wsum = functools.reduce(lambda x, y: x + y, ws)
        for k in range(TOP_K):
            idx_ref[k:k + 1, sl] = idxs[k].astype(jnp.int32)
            w_ref[k:k + 1, sl] = ws[k] / wsum * ROUTED_SCALE
            rank_ref[k:k + 1, sl] = jnp.sum(jnp.where(ei == idxs[k], rk[:, sl], 0.0), axis=0,
                                            keepdims=True).astype(jnp.int32)
    cnt_ref[...] = jnp.broadcast_to(carry_ref[...], cnt_ref.shape).astype(jnp.int32)


def _route(x1, scf, shf, wr_t, rbias, tm):
    b, n, d = x1.shape
    t = b * n
    e = wr_t.shape[0]
    per_b = n // tm
    wr_hi = wr_t.astype(BF16)
    wr_split = jnp.stack([wr_hi, (wr_t - wr_hi.astype(F32)).astype(BF16)])
    vec = pl.BlockSpec((1, 1, d), lambda i: (i // per_b, 0, 0))
    out_t = pl.BlockSpec((TOP_K, tm), lambda i: (0, i))
    return pl.pallas_call(
        _route_kernel,
        out_shape=(jax.ShapeDtypeStruct((TOP_K, t), jnp.int32), jax.ShapeDtypeStruct((TOP_K, t), F32),
                   jax.ShapeDtypeStruct((TOP_K, t), jnp.int32), jax.ShapeDtypeStruct((e, LANES), jnp.int32)),
        grid=(t // tm,),
        in_specs=[
            pl.BlockSpec((tm, d), lambda i: (i, 0)),
            vec, vec,
            pl.BlockSpec((2, e, d), lambda i: (0, 0, 0)),
            pl.BlockSpec((e, 1), lambda i: (0, 0)),
        ],
        out_specs=(out_t, out_t, out_t, pl.BlockSpec((e, LANES), lambda i: (0, 0))),
        scratch_shapes=[pltpu.VMEM((e, 1), F32)],
        compiler_params=_cparams("arbitrary"),
        name="route",
    )(x1.reshape(t, d), scf, shf, wr_split, rbias.reshape(e, 1))


def _gather_rows(table, idx):
    m = idx.shape[0]
    w = table.shape[1]
    workers = SC_CORES * SC_SUBCORES
    n_ch = m // (workers * SC_CHUNK)
    assert m % (workers * SC_CHUNK) == 0 and n_ch % 2 == 0
    mesh = plsc.VectorSubcoreMesh(core_axis_name="c", subcore_axis_name="s",
                                  num_cores=SC_CORES, num_subcores=SC_SUBCORES)

    @functools.partial(
        pl.kernel, mesh=mesh,
        out_type=jax.ShapeDtypeStruct((m, w), table.dtype),
        scratch_types=[
            pltpu.VMEM((n_ch, SC_CHUNK), I32),
            pltpu.VMEM((SC_CHUNK, w), table.dtype), pltpu.VMEM((SC_CHUNK, w), table.dtype),
            pltpu.SemaphoreType.DMA, pltpu.SemaphoreType.DMA, pltpu.SemaphoreType.DMA, pltpu.SemaphoreType.DMA,
        ],
        name="sc_gather_rows",
    )
    def gather(table_hbm, idx_hbm, out_hbm, idx_all, buf0, buf1, gsem0, gsem1, wsem0, wsem1):
        first = (lax.axis_index("s") * SC_CORES + lax.axis_index("c")) * n_ch
        bufs, gsem, wsem = (buf0, buf1), (gsem0, gsem1), (wsem0, wsem1)
        pltpu.sync_copy(idx_hbm.at[pl.ds(first, n_ch)], idx_all)

        def gather_copy(j, s):
            return pltpu.make_async_copy(table_hbm.at[idx_all.at[j]], bufs[s], gsem[s])

        def write_copy(j, s):
            rows = pl.ds(pl.multiple_of((first + j) * SC_CHUNK, SC_CHUNK), SC_CHUNK)
            return pltpu.make_async_copy(bufs[s], out_hbm.at[rows], wsem[s])

        gather_copy(0, 0).start()

        @pl.loop(0, n_ch, step=2)
        def _(jj):
            for s in range(2):
                j = jj + s

                @pl.when(j >= 1)
                def _():
                    write_copy(j - 1, 1 - s).wait()

                @pl.when(j + 1 < n_ch)
                def _():
                    gather_copy(j + 1, 1 - s).start()

                gather_copy(j, s).wait()
                write_copy(j, s).start()

        write_copy(n_ch - 1, 1).wait()

    return gather(table, idx.reshape(m // SC_CHUNK, SC_CHUNK))


def _scatter_rows(src, pos3, p):
    w = src.shape[1]
    n_chunks, k, ch = pos3.shape
    workers = SC_CORES * SC_SUBCORES
    per_w = n_chunks // workers
    assert ch == SC_SCATTER_CHUNK and n_chunks % workers == 0 and src.shape[0] == n_chunks * ch
    mesh = plsc.VectorSubcoreMesh(core_axis_name="c", subcore_axis_name="s",
                                  num_cores=SC_CORES, num_subcores=SC_SUBCORES)

    @functools.partial(
        pl.kernel, mesh=mesh,
        out_type=jax.ShapeDtypeStruct((p, w), src.dtype),
        scratch_types=[
            pltpu.VMEM((k, ch), I32),
            pltpu.VMEM((ch, w), src.dtype),
            pltpu.SemaphoreType.DMA,
        ],
        name="sc_scatter_rows",
    )
    def scatter(src_hbm, pos_hbm, out_hbm, idx_v, rows_v, sem):
        first = (lax.axis_index("s") * SC_CORES + lax.axis_index("c")) * per_w

        @pl.loop(0, per_w)
        def _(j):
            c = first + j
            pltpu.sync_copy(pos_hbm.at[c], idx_v)
            pltpu.sync_copy(src_hbm.at[pl.ds(pl.multiple_of(c * ch, ch), ch)], rows_v)
            copies = [pltpu.async_copy(rows_v, out_hbm.at[idx_v.at[kk]], sem) for kk in range(k)]
            for cp in copies:
                cp.wait()

    return scatter(src, pos3)


def _experts_kernel(ps_ref, nb_ref, cnt_ref, nt_ref, xs_hbm, wg_ref, wu_ref, wd_ref, y_hbm,
                    xbuf, ybuf, in_sem, out_sem, wg_bf, wu_bf, wd_bf):
    e = pl.program_id(0)
    nb, cnt, n_total = nb_ref[e], cnt_ref[e], nt_ref[0]
    g0 = ps_ref[e] // MOE_ROWS

    def in_copy(g):
        rows = pl.ds(pl.multiple_of(g * MOE_ROWS, MOE_ROWS), MOE_ROWS)
        return pltpu.make_async_copy(xs_hbm.at[rows], xbuf.at[g % MOE_SLOTS], in_sem.at[g % MOE_SLOTS])

    def out_copy(g):
        rows = pl.ds(pl.multiple_of(g * MOE_ROWS, MOE_ROWS), MOE_ROWS)
        return pltpu.make_async_copy(ybuf.at[g % MOE_SLOTS], y_hbm.at[rows], out_sem.at[g % MOE_SLOTS])

    @pl.when(e == 0)
    def _():
        for j in range(MOE_LOOKAHEAD):
            @pl.when(j < n_total)
            def _():
                in_copy(j).start()

    @pl.when(nb > 0)
    def _():
        wg_bf[...] = wg_ref[0, 0].astype(BF16)
        wu_bf[...] = wu_ref[0, 0].astype(BF16)
        wd_bf[...] = wd_ref[0, 0].astype(BF16)

        def process(b, width):
            g = g0 + b
            for u in range(width):
                @pl.when(g + u + MOE_LOOKAHEAD < n_total)
                def _():
                    in_copy(g + u + MOE_LOOKAHEAD).start()

            for u in range(width):
                in_copy(g + u).wait()

                @pl.when(g + u >= MOE_SLOTS)
                def _():
                    out_copy(g + u - MOE_SLOTS).wait()

            packed = jnp.concatenate([xbuf[(g + u) % MOE_SLOTS] for u in range(width)], axis=0)
            row = lax.broadcasted_iota(I32, (width * MOE_ROWS, 1), 0) + b * MOE_ROWS
            x_lo, x_hi = (v.astype(BF16) for v in _unpack_bf16_pairs(jnp.where(row < cnt, packed, 0)))
            half = x_lo.shape[1]

            def up(w_bf):
                return (jnp.dot(x_lo, w_bf[:half, :], preferred_element_type=F32)
                        + jnp.dot(x_hi, w_bf[half:, :], preferred_element_type=F32))

            hb = (_silu(up(wg_bf)) * up(wu_bf)).astype(BF16)
            y = _pack_bf16_pairs(jnp.dot(hb, wd_bf[...], preferred_element_type=F32))
            for u in range(width):
                ybuf[(g + u) % MOE_SLOTS] = y[u * MOE_ROWS:(u + 1) * MOE_ROWS]
                out_copy(g + u).start()

        start = 0
        for width in MOE_GROUPS:
            count = (nb - start) // width

            def body(i, c, width=width, start=start):
                process(start + width * i, width)
                return c

            lax.fori_loop(0, count, body, 0)
            start = start + count * width

    @pl.when(e == pl.num_programs(0) - 1)
    def _():
        for j in range(MOE_SLOTS):
            @pl.when(n_total - 1 - j >= 0)
            def _():
                out_copy(n_total - 1 - j).wait()


def _routed_experts(xs, wg, wu, wd, layer, pstarts, nblk, counts, n_total):
    p, dp = xs.shape
    _, e, d, f = wg.shape
    wspec = lambda r, c: pl.BlockSpec((1, 1, r, c), lambda i, ps, nb, cnt, nt: (layer, i, 0, 0))
    grid_spec = pltpu.PrefetchScalarGridSpec(
        num_scalar_prefetch=4,
        grid=(e,),
        in_specs=[pl.BlockSpec(memory_space=pl.ANY), wspec(d, f), wspec(d, f), wspec(f, d)],
        out_specs=pl.BlockSpec(memory_space=pl.ANY),
        scratch_shapes=[
            pltpu.VMEM((MOE_SLOTS, MOE_ROWS, dp), I32), pltpu.VMEM((MOE_SLOTS, MOE_ROWS, dp), I32),
            pltpu.SemaphoreType.DMA((MOE_SLOTS,)), pltpu.SemaphoreType.DMA((MOE_SLOTS,)),
            pltpu.VMEM((d, f), BF16), pltpu.VMEM((d, f), BF16), pltpu.VMEM((f, d), BF16),
        ],
    )
    return pl.pallas_call(
        _experts_kernel,
        out_shape=jax.ShapeDtypeStruct((p, dp), I32),
        grid_spec=grid_spec,
        compiler_params=_cparams("arbitrary"),
        name="routed_experts",
    )(pstarts, nblk, counts, n_total, xs, wg, wu, wd)


def _combine_kernel(yg_ref, w_ref, hf_ref, sg_ref, su_ref, sd_ref, x_ref, gf_ref, lg_ref, lb_ref, o_ref, *, alpha):
    w = w_ref[...]
    r_lo, r_hi = _unpack_bf16_pairs(yg_ref[0])
    r_lo, r_hi = w[:, 0:1] * r_lo, w[:, 0:1] * r_hi
    for k in range(1, TOP_K):
        y_lo, y_hi = _unpack_bf16_pairs(yg_ref[k])
        r_lo, r_hi = r_lo + w[:, k:k + 1] * y_lo, r_hi + w[:, k:k + 1] * y_hi
    routed = jnp.concatenate([r_lo, r_hi], axis=1)
    hf = jnp.concatenate(_unpack_bf16_pairs(hf_ref[...]), axis=1).astype(BF16)
    g = jnp.dot(hf, sg_ref[...], preferred_element_type=F32)
    u = jnp.dot(hf, su_ref[...], preferred_element_type=F32)
    shared = jnp.dot((_silu(g) * u).astype(BF16), sd_ref[...], preferred_element_type=F32)
    o_ref[...] = _layer_norm(alpha * x_ref[...] + gf_ref[0] * (routed + shared), lg_ref[...], lb_ref[...])


def _combine(yg, w_tk, hf, sg_bf, su_bf, sd_bf, x1, gf, ln_g, ln_b, alpha, tm, per_b):
    t, d = x1.shape
    f = sg_bf.shape[1]
    row = pl.BlockSpec((1, d), lambda i: (0, 0))
    tile = pl.BlockSpec((tm, d), lambda i: (i, 0))
    return pl.pallas_call(
        functools.partial(_combine_kernel, alpha=alpha),
        out_shape=jax.ShapeDtypeStruct((t, d), F32),
        grid=(t // tm,),
        in_specs=[
            pl.BlockSpec((TOP_K, tm, d // 2), lambda i: (0, i, 0)),
            pl.BlockSpec((tm, TOP_K), lambda i: (i, 0)),
            pl.BlockSpec((tm, d // 2), lambda i: (i, 0)),
            pl.BlockSpec((d, f), lambda i: (0, 0)),
            pl.BlockSpec((d, f), lambda i: (0, 0)),
            pl.BlockSpec((f, d), lambda i: (0, 0)),
            tile,
            pl.BlockSpec((1, 1, d), lambda i: (i // per_b, 0, 0)),
            row, row,
        ],
        out_specs=tile,
        compiler_params=_cparams("parallel"),
        name="moe_combine",
    )(yg, w_tk, hf, sg_bf, su_bf, sd_bf, x1, gf, ln_g.reshape(1, d), ln_b.reshape(1, d))


def _moe_layer(x1, hf, scf, shf, gf, router_w, router_bias, wg, wu, wd, layer, sg, su, sd, ln_g, ln_b, alpha, tm):
    b, n, d = x1.shape
    t = b * n
    e = router_w.shape[1]
    idx_t, w_t, rank_t, cnt = _route(x1, scf, shf, router_w.T, router_bias, tm)

    counts = cnt[:, 0]
    padded = (counts + MOE_ROWS - 1) // MOE_ROWS * MOE_ROWS
    pends = jnp.cumsum(padded)
    pstarts = pends - padded
    sel = idx_t[:, :, None] == jnp.arange(e, dtype=I32)
    pos_t = jnp.sum(jnp.where(sel, pstarts, 0), axis=-1) + rank_t
    p = t * TOP_K + e * MOE_ROWS
    pos3 = pos_t.reshape(TOP_K, t // SC_SCATTER_CHUNK, SC_SCATTER_CHUNK).transpose(1, 0, 2)

    hf2 = hf.reshape(t, d // 2)
    xs = _scatter_rows(hf2, pos3, p)
    yb = _routed_experts(xs, wg, wu, wd, layer, pstarts.astype(I32), (padded // MOE_ROWS).astype(I32), counts,
                         (pends[-1:] // MOE_ROWS).astype(I32))
    yg = _gather_rows(yb, pos_t.reshape(-1)).reshape(TOP_K, t, d // 2)
    out = _combine(yg, w_t.T, hf2, sg.astype(BF16), su.astype(BF16), sd.astype(BF16), x1.reshape(t, d), gf,
                   ln_g, ln_b, alpha, tm, n // tm)
    return out.reshape(b, n, d)


def _rope_tables(n):
    rows = n // GRID_W
    row_pos = jnp.repeat(jnp.arange(rows, dtype=F32), GRID_W)
    col_pos = jnp.tile(jnp.arange(GRID_W, dtype=F32), rows)
    half = LANES // 4
    lane = jnp.arange(LANES)
    in_blk = lane % half
    freq = ROPE_THETA ** (-(2.0 * (in_blk % (half // 2)).astype(F32)) / half)
    use_col = (lane // half) % 2 == 1
    pos = jnp.where(use_col[None, :], col_pos[:, None], row_pos[:, None])
    ang = pos * freq[None, :]
    lo = (in_blk < half // 2)[None, :]
    sin = jnp.sin(ang)
    return jnp.cos(ang), jnp.where(lo, -sin, 0.0), jnp.where(lo, 0.0, sin)


def kernel(x, c, ctx, c_ctx, w_mod, b_mod, ln_g, ln_b, attn_w_in, attn_w_out, attn_lambda, attn_subln_g,
           sgu_w_in, sgu_b_in, sgu_norm_g, sgu_norm_b, sgu_w_s, sgu_b_s, sgu_w_out,
           router_w, router_bias, exp_w_gate, exp_w_up, exp_w_down, sh_w_gate, sh_w_up, sh_w_down):
    b, n, d = x.shape
    depth = w_mod.shape[0]
    assert b <= 7 and d == ATT_HEADS * LANES and n % GRID_W == 0
    alpha = (2 * depth) ** 0.25
    head_dim = d // ATT_HEADS // 2
    tm = TOKEN_TILE if n % TOKEN_TILE == 0 else TOKEN_TILE // 2

    cs = jnp.zeros((8, d), F32).at[:b].set(c).at[b].set(c_ctx)
    mods = _modulation(cs, w_mod, b_mod)

    def mod_vec(i, j):
        return mods[i, :, j * d:(j + 1) * d].reshape(8, 1, d)

    for i in range(depth):
        sh_m, sc_m, g_m, sh_f, sc_f, g_f = (mod_vec(i, j) for j in range(6))
        if i % N_MIXERS == 0:
            a = i // N_MIXERS
            lam_init = 0.8 - 0.6 * math.exp(-0.3 * i)
            w_in_bf = attn_w_in[a].astype(BF16)
            cos, slo, shi = _rope_tables(n)
            q_scale = head_dim ** -0.5 * math.log2(math.e)
            q, k, vt = _qkv_proj(x, sc_m, sh_m, w_in_bf, cos, slo, shi, q_scale, tm)
            kc, vct = _kv_ctx_proj(ctx, sc_m[b:b + 1], sh_m[b:b + 1], w_in_bf[:, d:])
            pre = _diff_attention(q, k, vt, kc, vct, attn_lambda[a], attn_subln_g[a], lam_init,
                                  min(ATT_QUERY_TILE, n), min(ATT_KEY_CHUNK, n // 2))
            w_out_bf = attn_w_out[a].astype(BF16)
        else:
            s = i // N_MIXERS
            pre = _sgu_mixer(x, sc_m, sh_m, sgu_w_in[s].astype(BF16), sgu_b_in[s], sgu_norm_g[s], sgu_norm_b[s],
                             sgu_w_s[s].astype(BF16), sgu_b_s[s].T, tm)
            w_out_bf = sgu_w_out[s].astype(BF16)
        x1, hf = _post_mixer(pre, w_out_bf, x, g_m, ln_g[i, 0], ln_b[i, 0], sc_f, sh_f, alpha, tm)
        x = _moe_layer(x1, hf, sc_f, sh_f, g_f, router_w[i], router_bias[i], exp_w_gate, exp_w_up, exp_w_down, i,
                       sh_w_gate[i], sh_w_up[i], sh_w_down[i], ln_g[i, 1], ln_b[i, 1], alpha, tm)
    return x
```

```python
import functools
import math

import jax
import jax.numpy as jnp
from jax import lax
from jax.experimental import pallas as pl
from jax.experimental.pallas import tpu as pltpu
from jax.experimental.pallas import tpu_sc as plsc

F32 = jnp.float32
BF16 = jnp.bfloat16
I32 = jnp.int32

GRID_W = 64
ATT_HEADS = 8
ROPE_THETA = 10000.0
SGU_CHUNK = 128
SGU_GROUPS = 8
TOP_K = 8
N_GROUPS = 8
TOPK_GROUPS = 4
ROUTED_SCALE = 2.5
LN_EPS = 1e-5
N_MIXERS = 2

LANES = 128
MOE_ROWS = 256
SLOT_TILE = 2048
ROUTE_SLAB = 512
MOE_GROUPS = (2, 1)
MOE_LOOKAHEAD = 6
MOE_SLOTS = MOE_LOOKAHEAD + MOE_GROUPS[0]
TOKEN_TILE = 512
ATT_QUERY_TILE = 1024
ATT_KEY_CHUNK = 1024
ATT_STRIP = 16
SC_CORES = 2
SC_SUBCORES = 16
SC_CHUNK = 64
SC_SCATTER_CHUNK = 128
VMEM_LIMIT = 56 * 1024 * 1024
NEG_INF = float("-inf")


def _cparams(*sem):
    return pltpu.CompilerParams(dimension_semantics=sem, vmem_limit_bytes=VMEM_LIMIT)


def _layer_norm(z, g, b):
    mu = jnp.mean(z, axis=-1, keepdims=True)
    zc = z - mu
    var = jnp.mean(zc * zc, axis=-1, keepdims=True)
    return zc * lax.rsqrt(var + LN_EPS) * g + b


def _silu(x):
    return x * (1.0 / (1.0 + jnp.exp(-x)))


_HIGH_HALF = -65536


def _pack_bf16_pairs(y):
    w = y.shape[1] // 2
    bits = lax.bitcast_convert_type(y.astype(BF16).astype(F32), I32)
    return lax.shift_right_logical(bits[:, :w], 16) | (bits[:, w:] & _HIGH_HALF)


def _unpack_bf16_pairs(p):
    return (lax.bitcast_convert_type(lax.shift_left(p, 16), F32),
            lax.bitcast_convert_type(p & _HIGH_HALF, F32))


def _mod_kernel(cs_ref, w_ref, b_ref, o_ref):
    s = _silu(cs_ref[...])
    o_ref[0] = jnp.dot(s, w_ref[0], precision=lax.Precision.HIGHEST,
                       preferred_element_type=F32) + b_ref[0]


def _modulation(cs, w_mod, b_mod):
    depth, d, n6 = w_mod.shape
    tn = n6 // 4
    return pl.pallas_call(
        _mod_kernel,
        out_shape=jax.ShapeDtypeStruct((depth, 8, n6), F32),
        grid=(depth, n6 // tn),
        in_specs=[
            pl.BlockSpec((8, d), lambda l, j: (0, 0)),
            pl.BlockSpec((1, d, tn), lambda l, j: (l, 0, j)),
            pl.BlockSpec((1, 1, tn), lambda l, j: (l, 0, j)),
        ],
        out_specs=pl.BlockSpec((1, 8, tn), lambda l, j: (l, 0, j)),
        compiler_params=_cparams("parallel", "parallel"),
        name="modulation",
    )(cs, w_mod, b_mod.reshape(depth, 1, n6))


def _rope(xh, cos, sin_lo, sin_hi):
    return xh * cos + pltpu.roll(xh, LANES - 16, 1) * sin_lo + pltpu.roll(xh, 16, 1) * sin_hi


def _qkv_kernel(x_ref, sc_ref, sh_ref, w_ref, cos_ref, slo_ref, shi_ref, q_ref, k_ref, v_ref, *, q_scale):
    d = x_ref.shape[-1]
    h = (x_ref[0] * (1.0 + sc_ref[0]) + sh_ref[0]).astype(BF16)
    cos, slo, shi = cos_ref[...], slo_ref[...], shi_ref[...]
    q = jnp.dot(h, w_ref[:, 0:d], preferred_element_type=F32)
    for hd in range(ATT_HEADS):
        q_ref[0, hd] = (_rope(q[:, hd * LANES:(hd + 1) * LANES], cos, slo, shi) * q_scale).astype(BF16)
    k = jnp.dot(h, w_ref[:, d:2 * d], preferred_element_type=F32)
    for hd in range(ATT_HEADS):
        k_ref[0, hd] = _rope(k[:, hd * LANES:(hd + 1) * LANES], cos, slo, shi).astype(BF16)
    v = jnp.dot(h, w_ref[:, 2 * d:3 * d], preferred_element_type=F32)
    for hd in range(ATT_HEADS):
        v_ref[0, hd] = v[:, hd * LANES:(hd + 1) * LANES].T.astype(BF16)


def _qkv_proj(x, sc, sh, w_bf, cos, slo, shi, q_scale, tn):
    b, n, d = x.shape
    hd_shape = jax.ShapeDtypeStruct((b, ATT_HEADS, n, LANES), BF16)
    vec = pl.BlockSpec((1, 1, d), lambda bi, i: (bi, 0, 0))
    tab = pl.BlockSpec((tn, LANES), lambda bi, i: (i, 0))
    out = pl.BlockSpec((1, ATT_HEADS, tn, LANES), lambda bi, i: (bi, 0, i, 0))
    out_t = pl.BlockSpec((1, ATT_HEADS, LANES, tn), lambda bi, i: (bi, 0, 0, i))
    return pl.pallas_call(
        functools.partial(_qkv_kernel, q_scale=q_scale),
        out_shape=(hd_shape, hd_shape, jax.ShapeDtypeStruct((b, ATT_HEADS, LANES, n), BF16)),
        grid=(b, n // tn),
        in_specs=[
            pl.BlockSpec((1, tn, d), lambda bi, i: (bi, i, 0)),
            vec, vec,
            pl.BlockSpec((d, 3 * d), lambda bi, i: (0, 0)),
            tab, tab, tab,
        ],
        out_specs=(out, out, out_t),
        compiler_params=_cparams("parallel", "parallel"),
        name="qkv_proj",
    )(x, sc, sh, w_bf, cos, slo, shi)


def _kv_ctx_kernel(x_ref, sc_ref, sh_ref, w_ref, k_ref, v_ref):
    d = x_ref.shape[-1]
    h = (x_ref[0] * (1.0 + sc_ref[0]) + sh_ref[0]).astype(BF16)
    k = jnp.dot(h, w_ref[:, 0:d], preferred_element_type=F32)
    v = jnp.dot(h, w_ref[:, d:2 * d], preferred_element_type=F32)
    for hd in range(ATT_HEADS):
        k_ref[0, hd] = k[:, hd * LANES:(hd + 1) * LANES].astype(BF16)
        v_ref[0, hd] = v[:, hd * LANES:(hd + 1) * LANES].T.astype(BF16)


def _kv_ctx_proj(ctx, sc, sh, w_kv_bf):
    b, c, d = ctx.shape
    hd_shape = jax.ShapeDtypeStruct((b, ATT_HEADS, c, LANES), BF16)
    vec = pl.BlockSpec((1, 1, d), lambda bi: (0, 0, 0))
    out = pl.BlockSpec((1, ATT_HEADS, c, LANES), lambda bi: (bi, 0, 0, 0))
    out_t = pl.BlockSpec((1, ATT_HEADS, LANES, c), lambda bi: (bi, 0, 0, 0))
    return pl.pallas_call(
        _kv_ctx_kernel,
        out_shape=(hd_shape, jax.ShapeDtypeStruct((b, ATT_HEADS, LANES, c), BF16)),
        grid=(b,),
        in_specs=[
            pl.BlockSpec((1, c, d), lambda bi: (bi, 0, 0)),
            vec, vec,
            pl.BlockSpec((d, 2 * d), lambda bi: (0, 0)),
        ],
        out_specs=(out, out_t),
        compiler_params=_cparams("parallel"),
        name="kv_ctx_proj",
    )(ctx, sc, sh, w_kv_bf)


def _attn_finish(acc_a, l_a, acc_b, l_b, lamp_ref, g_ref, lam_init):
    lp = lamp_ref[...]
    lam = (jnp.exp(jnp.sum(lp[0:1] * lp[1:2], axis=-1, keepdims=True))
           - jnp.exp(jnp.sum(lp[2:3] * lp[3:4], axis=-1, keepdims=True)) + lam_init)
    o = acc_a / l_a - lam * (acc_b / l_b)
    o = o * lax.rsqrt(jnp.mean(o * o, axis=0, keepdims=True) + LN_EPS) * g_ref[...] * (1.0 - lam_init)
    return o.T.astype(BF16)


def _attn_kernel(q_ref, k_ref, vt_ref, kc_ref, vct_ref, lamp_ref, g_ref, o_ref,
                 s00, s01, s10, s11, p0, p1, acc0, acc1, *, tk, lam_init):
    q = q_ref[0, 0]
    tq = q.shape[0]
    half = LANES // 2
    lane = lax.broadcasted_iota(jnp.int32, q.shape, 1)
    zero = jnp.zeros_like(q)
    qs = (jnp.where(lane < half, q, zero), jnp.where(lane >= half, q, zero))
    n_chunks = k_ref.shape[2] // tk
    s_scr = ((s00, s01), (s10, s11))
    p_scr, acc_scr = (p0, p1), (acc0, acc1)
    nt = (((1,), (1,)), ((), ()))

    def scores(slot, kc):
        width = kc.shape[0]
        for mp in range(2):
            s_scr[slot][mp][0:width, :] = lax.dot_general(kc, qs[mp], nt, preferred_element_type=F32)

    def absorb(slot, vct, shift, sums):
        width = vct.shape[1]
        sums = list(sums)
        for mp in range(2):
            part = sums[mp]
            for r in range(width // ATT_STRIP):
                lo = r * ATT_STRIP
                tiles = [jnp.exp2(s_scr[slot][mp][lo + 8 * u:lo + 8 * (u + 1), :] - shift[mp])
                         for u in range(ATT_STRIP // 8)]
                p_scr[mp][lo:lo + ATT_STRIP, :] = jnp.concatenate(tiles, axis=0).astype(BF16)
                part = part + functools.reduce(lambda x, y: x + y, tiles)
            sums[mp] = part
            acc_scr[mp][...] += jnp.dot(vct, p_scr[mp][0:width, :], preferred_element_type=F32)
        return tuple(sums)

    def k_chunk(j):
        return k_ref[0, 0, pl.ds(pl.multiple_of(j * tk, tk), tk), :]

    def vt_chunk(j):
        return vt_ref[0, 0, :, pl.ds(pl.multiple_of(j * tk, tk), tk)]

    scores(0, k_chunk(0))
    shift = tuple(jnp.broadcast_to(jnp.max(s_scr[0][mp][...], axis=0, keepdims=True), (8, tq)) for mp in range(2))
    for mp in range(2):
        acc_scr[mp][...] = jnp.zeros((LANES, tq), F32)
    sums = (jnp.zeros((8, tq), F32), jnp.zeros((8, tq), F32))

    def pair(jj, sums):
        j0 = 2 * jj
        scores(1, k_chunk(j0 + 1))
        sums = absorb(0, vt_chunk(j0), shift, sums)
        scores(0, k_chunk(j0 + 2))
        return absorb(1, vt_chunk(j0 + 1), shift, sums)

    sums = lax.fori_loop(0, n_chunks // 2 - 1, pair, sums)
    scores(1, k_chunk(n_chunks - 1))
    sums = absorb(0, vt_chunk(n_chunks - 2), shift, sums)
    scores(0, kc_ref[0, 0])
    sums = absorb(1, vt_chunk(n_chunks - 1), shift, sums)
    sums = absorb(0, vct_ref[0, 0], shift, sums)

    tot = [jnp.sum(sums[mp], axis=0, keepdims=True) for mp in range(2)]
    bad = sum(jnp.sum(jnp.where(jnp.isfinite(x), 0.0, 1.0)) for x in (tot[0], tot[1], acc0[...], acc1[...]))

    @pl.when(bad == 0.0)
    def _():
        o_ref[0] = _attn_finish(acc0[...], tot[0], acc1[...], tot[1], lamp_ref, g_ref, lam_init)

    @pl.when(bad != 0.0)
    def _():
        def update(carry, kc, vct):
            new = []
            for mp in range(2):
                m, l, acc = carry[mp]
                s = lax.dot_general(kc, qs[mp], nt, preferred_element_type=F32)
                mn = jnp.maximum(m, jnp.max(s, axis=0, keepdims=True))
                a = jnp.exp2(m - mn)
                p = jnp.exp2(s - mn)
                new.append((mn, a * l + jnp.sum(p, axis=0, keepdims=True),
                            a * acc + jnp.dot(vct, p.astype(BF16), preferred_element_type=F32)))
            return tuple(new)

        init = tuple((jnp.full((1, tq), NEG_INF, F32), jnp.zeros((1, tq), F32), jnp.zeros((LANES, tq), F32))
                     for _ in range(2))
        carry = lax.fori_loop(0, n_chunks, lambda j, c: update(c, k_chunk(j), vt_chunk(j)), init)
        (_, l_a, acc_a), (_, l_b, acc_b) = update(carry, kc_ref[0, 0], vct_ref[0, 0])
        o_ref[0] = _attn_finish(acc_a, l_a, acc_b, l_b, lamp_ref, g_ref, lam_init)


def _diff_attention(q, k, vt, kc, vct, lam_p, subln_g, lam_init, tq, tk):
    b, h, n, _ = q.shape
    c = kc.shape[2]
    assert n % tk == 0 and (n // tk) % 2 == 0 and c <= tk and tk % ATT_STRIP == 0 and c % ATT_STRIP == 0
    spec = lambda r, cols: pl.BlockSpec((1, 1, r, cols), lambda bi, hi, i: (bi, hi, 0, 0))
    return pl.pallas_call(
        functools.partial(_attn_kernel, tk=tk, lam_init=lam_init),
        out_shape=jax.ShapeDtypeStruct((b, n, h * LANES), BF16),
        grid=(b, h, n // tq),
        in_specs=[
            pl.BlockSpec((1, 1, tq, LANES), lambda bi, hi, i: (bi, hi, i, 0)),
            spec(n, LANES), spec(LANES, n), spec(c, LANES), spec(LANES, c),
            pl.BlockSpec(lam_p.shape, lambda bi, hi, i: (0, 0)),
            pl.BlockSpec((LANES, 1), lambda bi, hi, i: (0, 0)),
        ],
        out_specs=pl.BlockSpec((1, tq, LANES), lambda bi, hi, i: (bi, i, hi)),
        scratch_shapes=[
            *[pltpu.VMEM((tk, tq), F32)] * 4,
            *[pltpu.VMEM((tk, tq), BF16)] * 2,
            *[pltpu.VMEM((LANES, tq), F32)] * 2,
        ],
        compiler_params=_cparams("parallel", "parallel", "parallel"),
        name="diff_attention",
    )(q, k, vt, kc, vct, lam_p, subln_g.reshape(LANES, 1))


def _sgu_kernel(x_ref, sc_ref, sh_ref, w_ref, b_ref, ng_ref, nb_ref, ws_ref, bs_ref, t_ref):
    f = t_ref.shape[-1]
    cg = f // SGU_GROUPS
    tm = x_ref.shape[1]
    h = (x_ref[0] * (1.0 + sc_ref[0]) + sh_ref[0]).astype(BF16)
    z = jnp.dot(h, w_ref[...], preferred_element_type=F32) + b_ref[...]
    z = 0.5 * z * (1.0 + lax.erf(z * (2.0 ** -0.5)))
    u = z[:, :f]
    v = _layer_norm(z[:, f:], ng_ref[...], nb_ref[...]).astype(BF16)
    for c in range(tm // SGU_CHUNK):
        rows = slice(c * SGU_CHUNK, (c + 1) * SGU_CHUNK)
        for g in range(SGU_GROUPS):
            cols = slice(g * cg, (g + 1) * cg)
            vm = jnp.dot(ws_ref[g], v[rows, cols], preferred_element_type=F32) + bs_ref[:, g:g + 1]
            t_ref[0, rows, cols] = (u[rows, cols] * vm).astype(BF16)


def _sgu_mixer(x, sc, sh, w_in_bf, b_in, norm_g, norm_b, w_s_bf, b_s_t, tm):
    b, n, d = x.shape
    f2 = w_in_bf.shape[1]
    f = f2 // 2
    vec = pl.BlockSpec((1, 1, d), lambda bi, i: (bi, 0, 0))
    full2 = lambda a: pl.BlockSpec(a.shape, lambda bi, i: (0,) * a.ndim)
    b_in2, ng2, nb2 = b_in.reshape(1, f2), norm_g.reshape(1, f), norm_b.reshape(1, f)
    return pl.pallas_call(
        _sgu_kernel,
        out_shape=jax.ShapeDtypeStruct((b, n, f), BF16),
        grid=(b, n // tm),
        in_specs=[
            pl.BlockSpec((1, tm, d), lambda bi, i: (bi, i, 0)),
            vec, vec,
            full2(w_in_bf), full2(b_in2), full2(ng2), full2(nb2), full2(w_s_bf), full2(b_s_t),
        ],
        out_specs=pl.BlockSpec((1, tm, f), lambda bi, i: (bi, i, 0)),
        compiler_params=_cparams("parallel", "parallel"),
        name="sgu_mixer",
    )(x, sc, sh, w_in_bf, b_in2, ng2, nb2, w_s_bf, b_s_t)


def _post_kernel(pre_ref, w_ref, x_ref, gm_ref, lg_ref, lb_ref, scf_ref, shf_ref, x1_ref, hf_ref, *, alpha):
    y = jnp.dot(pre_ref[0], w_ref[...], preferred_element_type=F32)
    x1 = _layer_norm(alpha * x_ref[0] + gm_ref[0] * y, lg_ref[...], lb_ref[...])
    x1_ref[0] = x1
    hf_ref[0] = _pack_bf16_pairs(x1 * (1.0 + scf_ref[0]) + shf_ref[0])


def _post_mixer(pre, w_bf, x, gm, ln_g, ln_b, scf, shf, alpha, tm):
    b, n, d = x.shape
    kd = pre.shape[-1]
    vec = pl.BlockSpec((1, 1, d), lambda bi, i: (bi, 0, 0))
    row = pl.BlockSpec((1, d), lambda bi, i: (0, 0))
    tile = pl.BlockSpec((1, tm, d), lambda bi, i: (bi, i, 0))
    return pl.pallas_call(
        functools.partial(_post_kernel, alpha=alpha),
        out_shape=(jax.ShapeDtypeStruct((b, n, d), F32), jax.ShapeDtypeStruct((b, n, d // 2), I32)),
        grid=(b, n // tm),
        in_specs=[
            pl.BlockSpec((1, tm, kd), lambda bi, i: (bi, i, 0)),
            pl.BlockSpec((kd, d), lambda bi, i: (0, 0)),
            tile, vec, row, row, vec, vec,
        ],
        out_specs=(tile, pl.BlockSpec((1, tm, d // 2), lambda bi, i: (bi, i, 0))),
        compiler_params=_cparams("parallel", "parallel"),
        name="post_mixer",
    )(pre, w_bf, x, gm, ln_g.reshape(1, d), ln_b.reshape(1, d), scf, shf)


def _route_select(scores, choice):
    e, w = scores.shape
    ge = e // N_GROUPS
    g3 = choice.reshape(N_GROUPS, ge, w)
    ri = lax.broadcasted_iota(jnp.int32, g3.shape, 1).astype(F32)
    m1 = jnp.max(g3, axis=1, keepdims=True)
    first = jnp.min(jnp.where(g3 == m1, ri, float(ge)), axis=1, keepdims=True)
    m2 = jnp.max(jnp.where(ri == first, NEG_INF, g3), axis=1, keepdims=True)
    gs = m1 + m2

    gi = lax.broadcasted_iota(jnp.int32, gs.shape, 0).astype(F32)
    gsel = jnp.zeros(gs.shape, F32)
    cur = gs
    for _ in range(TOPK_GROUPS):
        m = jnp.max(cur, axis=0, keepdims=True)
        f = jnp.min(jnp.where(cur == m, gi, float(N_GROUPS)), axis=0, keepdims=True)
        hit = gi == f
        gsel = jnp.where(hit, 1.0, gsel)
        cur = jnp.where(hit, NEG_INF, cur)
    emask = jnp.broadcast_to(gsel, g3.shape).reshape(e, w)
    masked = jnp.where(emask > 0.5, choice, NEG_INF)

    ei = lax.broadcasted_iota(jnp.int32, (e, w), 0).astype(F32)
    onehot = jnp.zeros((e, w), F32)
    idxs, ws = [], []
    for _ in range(TOP_K):
        m = jnp.max(masked, axis=0, keepdims=True)
        f = jnp.min(jnp.where(masked == m, ei, float(e)), axis=0, keepdims=True)
        hit = ei == f
        idxs.append(f)
        ws.append(jnp.sum(jnp.where(hit, scores, 0.0), axis=0, keepdims=True))
        masked = jnp.where(hit, NEG_INF, masked)
        onehot = jnp.where(hit, 1.0, onehot)
    return idxs, ws, onehot


def _route_kernel(x_ref, sc_ref, sh_ref, wr_ref, rb_ref, idx_ref, w_ref, rank_ref, cnt_ref, carry_ref):
    i = pl.program_id(0)
    e = wr_ref.shape[1]
    tm = x_ref.shape[0]

    @pl.when(i == 0)
    def _():
        carry_ref[...] = jnp.zeros_like(carry_ref)

    h = x_ref[...] * (1.0 + sc_ref[0]) + sh_ref[0]
    h_hi = h.astype(BF16)
    h_lo = (h - h_hi.astype(F32)).astype(BF16)
    nt = (((1,), (1,)), ((), ()))
    logits = (lax.dot_general(wr_ref[0], h_hi, nt, preferred_element_type=F32)
              + (lax.dot_general(wr_ref[0], h_lo, nt, preferred_element_type=F32)
                 + lax.dot_general(wr_ref[1], h_hi, nt, preferred_element_type=F32)))
    scores = 1.0 / (1.0 + jnp.exp(-logits))
    choice = scores + rb_ref[...]

    slab = min(ROUTE_SLAB, tm)
    slabs = [slice(j * slab, (j + 1) * slab) for j in range(tm // slab)]
    picks = [_route_select(scores[:, sl], choice[:, sl]) for sl in slabs]
    onehot = jnp.concatenate([pk[2] for pk in picks], axis=1)

    r_i = lax.broadcasted_iota(jnp.int32, (tm, tm), 0)
    c_i = lax.broadcasted_iota(jnp.int32, (tm, tm), 1)
    upper = jnp.where(r_i < c_i, 1.0, 0.0).astype(BF16)
    rk = jnp.dot(onehot.astype(BF16), upper, preferred_element_type=F32) + carry_ref[...]
    carry_ref[...] += jnp.sum(onehot, axis=1, keepdims=True)

    ei = lax.broadcasted_iota(jnp.int32, (e, slab), 0).astype(F32)
    for sl, (idxs, ws, _) in zip(slabs, picks):
        wsum = functools.reduce(lambda x, y: x + y, ws)
        for k in range(TOP_K):
            idx_ref[k:k + 1, sl] = idxs[k].astype(jnp.int32)
            w_ref[k:k + 1, sl] = ws[k] / wsum * ROUTED_SCALE
            rank_ref[k:k + 1, sl] = jnp.sum(jnp.where(ei == idxs[k], rk[:, sl], 0.0), axis=0,
                                            keepdims=True).astype(jnp.int32)
    cnt_ref[...] = jnp.broadcast_to(carry_ref[...], cnt_ref.shape).astype(jnp.int32)


def _route(x1, scf, shf, wr_t, rbias, tm):
    b, n, d = x1.shape
    t = b * n
    e = wr_t.shape[0]
    per_b = n // tm
    wr_hi = wr_t.astype(BF16)
    wr_split = jnp.stack([wr_hi, (wr_t - wr_hi.astype(F32)).astype(BF16)])
    vec = pl.BlockSpec((1, 1, d), lambda i: (i // per_b, 0, 0))
    out_t = pl.BlockSpec((TOP_K, tm), lambda i: (0, i))
    return pl.pallas_call(
        _route_kernel,
        out_shape=(jax.ShapeDtypeStruct((TOP_K, t), jnp.int32), jax.ShapeDtypeStruct((TOP_K, t), F32),
                   jax.ShapeDtypeStruct((TOP_K, t), jnp.int32), jax.ShapeDtypeStruct((e, LANES), jnp.int32)),
        grid=(t // tm,),
        in_specs=[
            pl.BlockSpec((tm, d), lambda i: (i, 0)),
            vec, vec,
            pl.BlockSpec((2, e, d), lambda i: (0, 0, 0)),
            pl.BlockSpec((e, 1), lambda i: (0, 0)),
        ],
        out_specs=(out_t, out_t, out_t, pl.BlockSpec((e, LANES), lambda i: (0, 0))),
        scratch_shapes=[pltpu.VMEM((e, 1), F32)],
        compiler_params=_cparams("arbitrary"),
        name="route",
    )(x1.reshape(t, d), scf, shf, wr_split, rbias.reshape(e, 1))


def _slot_kernel(idx_ref, rank_ref, ps_ref, pos_ref):
    e, w = ps_ref.shape[0], idx_ref.shape[1]
    ei = lax.broadcasted_iota(jnp.int32, (e, w), 0)
    starts = ps_ref[...]
    for k in range(TOP_K):
        base = jnp.sum(jnp.where(ei == idx_ref[k:k + 1, :], starts, 0.0), axis=0, keepdims=True)
        pos_ref[k:k + 1, :] = base.astype(jnp.int32) + rank_ref[k:k + 1, :]


def _slot_positions(idx_t, rank_t, pstarts):
    k, t = idx_t.shape
    e = pstarts.shape[0]
    w = min(SLOT_TILE, t)
    tile = pl.BlockSpec((k, w), lambda i: (0, i))
    return pl.pallas_call(
        _slot_kernel,
        out_shape=jax.ShapeDtypeStruct((k, t), I32),
        grid=(t // w,),
        in_specs=[tile, tile, pl.BlockSpec((e, 1), lambda i: (0, 0))],
        out_specs=tile,
        compiler_params=_cparams("parallel"),
        name="slot_positions",
    )(idx_t, rank_t, pstarts.astype(F32).reshape(e, 1))


def _gather_rows(table, idx):
    m = idx.shape[0]
    w = table.shape[1]
    workers = SC_CORES * SC_SUBCORES
    n_ch = m // (workers * SC_CHUNK)
    assert m % (workers * SC_CHUNK) == 0 and n_ch % 2 == 0
    mesh = plsc.VectorSubcoreMesh(core_axis_name="c", subcore_axis_name="s",
                                  num_cores=SC_CORES, num_subcores=SC_SUBCORES)

    @functools.partial(
        pl.kernel, mesh=mesh,
        out_type=jax.ShapeDtypeStruct((m, w), table.dtype),
        scratch_types=[
            pltpu.VMEM((n_ch, SC_CHUNK), I32),
            pltpu.VMEM((SC_CHUNK, w), table.dtype), pltpu.VMEM((SC_CHUNK, w), table.dtype),
            pltpu.SemaphoreType.DMA, pltpu.SemaphoreType.DMA, pltpu.SemaphoreType.DMA, pltpu.SemaphoreType.DMA,
        ],
        name="sc_gather_rows",
    )
    def gather(table_hbm, idx_hbm, out_hbm, idx_all, buf0, buf1, gsem0, gsem1, wsem0, wsem1):
        first = (lax.axis_index("s") * SC_CORES + lax.axis_index("c")) * n_ch
        bufs, gsem, wsem = (buf0, buf1), (gsem0, gsem1), (wsem0, wsem1)
        pltpu.sync_copy(idx_hbm.at[pl.ds(first, n_ch)], idx_all)

        def gather_copy(j, s):
            return pltpu.make_async_copy(table_hbm.at[idx_all.at[j]], bufs[s], gsem[s])

        def write_copy(j, s):
            rows = pl.ds(pl.multiple_of((first + j) * SC_CHUNK, SC_CHUNK), SC_CHUNK)
            return pltpu.make_async_copy(bufs[s], out_hbm.at[rows], wsem[s])

        gather_copy(0, 0).start()

        @pl.loop(0, n_ch, step=2)
        def _(jj):
            for s in range(2):
                j = jj + s

                @pl.when(j >= 1)
                def _():
                    write_copy(j - 1, 1 - s).wait()

                @pl.when(j + 1 < n_ch)
                def _():
                    gather_copy(j + 1, 1 - s).start()

                gather_copy(j, s).wait()
                write_copy(j, s).start()

        write_copy(n_ch - 1, 1).wait()

    return gather(table, idx.reshape(m // SC_CHUNK, SC_CHUNK))


def _scatter_rows(src, pos3, p):
    w = src.shape[1]
    n_chunks, k, ch = pos3.shape
    workers = SC_CORES * SC_SUBCORES
    per_w = n_chunks // workers
    assert ch == SC_SCATTER_CHUNK and n_chunks % workers == 0 and src.shape[0] == n_chunks * ch
    mesh = plsc.VectorSubcoreMesh(core_axis_name="c", subcore_axis_name="s",
                                  num_cores=SC_CORES, num_subcores=SC_SUBCORES)

    @functools.partial(
        pl.kernel, mesh=mesh,
        out_type=jax.ShapeDtypeStruct((p, w), src.dtype),
        scratch_types=[
            pltpu.VMEM((k, ch), I32),
            pltpu.VMEM((ch, w), src.dtype),
            pltpu.SemaphoreType.DMA,
        ],
        name="sc_scatter_rows",
    )
    def scatter(src_hbm, pos_hbm, out_hbm, idx_v, rows_v, sem):
        first = (lax.axis_index("s") * SC_CORES + lax.axis_index("c")) * per_w

        @pl.loop(0, per_w)
        def _(j):
            c = first + j
            pltpu.sync_copy(pos_hbm.at[c], idx_v)
            pltpu.sync_copy(src_hbm.at[pl.ds(pl.multiple_of(c * ch, ch), ch)], rows_v)
            copies = [pltpu.async_copy(rows_v, out_hbm.at[idx_v.at[kk]], sem) for kk in range(k)]
            for cp in copies:
                cp.wait()

    return scatter(src, pos3)


def _experts_kernel(ps_ref, nb_ref, cnt_ref, nt_ref, xs_hbm, wg_ref, wu_ref, wd_ref, y_hbm,
                    xbuf, ybuf, in_sem, out_sem, wg_bf, wu_bf, wd_bf):
    e = pl.program_id(0)
    nb, cnt, n_total = nb_ref[e], cnt_ref[e], nt_ref[0]
    g0 = ps_ref[e] // MOE_ROWS

    def in_copy(g):
        rows = pl.ds(pl.multiple_of(g * MOE_ROWS, MOE_ROWS), MOE_ROWS)
        return pltpu.make_async_copy(xs_hbm.at[rows], xbuf.at[g % MOE_SLOTS], in_sem.at[g % MOE_SLOTS])

    def out_copy(g):
        rows = pl.ds(pl.multiple_of(g * MOE_ROWS, MOE_ROWS), MOE_ROWS)
        return pltpu.make_async_copy(ybuf.at[g % MOE_SLOTS], y_hbm.at[rows], out_sem.at[g % MOE_SLOTS])

    @pl.when(e == 0)
    def _():
        for j in range(MOE_LOOKAHEAD):
            @pl.when(j < n_total)
            def _():
                in_copy(j).start()

    @pl.when(nb > 0)
    def _():
        wg_bf[...] = wg_ref[0, 0].astype(BF16)
        wu_bf[...] = wu_ref[0, 0].astype(BF16)
        wd_bf[...] = wd_ref[0, 0].astype(BF16)

        def process(b, width):
            g = g0 + b
            for u in range(width):
                @pl.when(g + u + MOE_LOOKAHEAD < n_total)
                def _():
                    in_copy(g + u + MOE_LOOKAHEAD).start()

            for u in range(width):
                in_copy(g + u).wait()

                @pl.when(g + u >= MOE_SLOTS)
                def _():
                    out_copy(g + u - MOE_SLOTS).wait()

            packed = jnp.concatenate([xbuf[(g + u) % MOE_SLOTS] for u in range(width)], axis=0)
            row = lax.broadcasted_iota(I32, (width * MOE_ROWS, 1), 0) + b * MOE_ROWS
            x_lo, x_hi = (v.astype(BF16) for v in _unpack_bf16_pairs(jnp.where(row < cnt, packed, 0)))
            half = x_lo.shape[1]

            def up(w_bf):
                return (jnp.dot(x_lo, w_bf[:half, :], preferred_element_type=F32)
                        + jnp.dot(x_hi, w_bf[half:, :], preferred_element_type=F32))

            hb = (_silu(up(wg_bf)) * up(wu_bf)).astype(BF16)
            y = _pack_bf16_pairs(jnp.dot(hb, wd_bf[...], preferred_element_type=F32))
            for u in range(width):
                ybuf[(g + u) % MOE_SLOTS] = y[u * MOE_ROWS:(u + 1) * MOE_ROWS]
                out_copy(g + u).start()

        start = 0
        for width in MOE_GROUPS:
            count = (nb - start) // width

            def body(i, c, width=width, start=start):
                process(start + width * i, width)
                return c

            lax.fori_loop(0, count, body, 0)
            start = start + count * width

    @pl.when(e == pl.num_programs(0) - 1)
    def _():
        for j in range(MOE_SLOTS):
            @pl.when(n_total - 1 - j >= 0)
            def _():
                out_copy(n_total - 1 - j).wait()


def _routed_experts(xs, wg, wu, wd, layer, pstarts, nblk, counts, n_total):
    p, dp = xs.shape
    _, e, d, f = wg.shape
    wspec = lambda r, c: pl.BlockSpec((1, 1, r, c), lambda i, ps, nb, cnt, nt: (layer, i, 0, 0))
    grid_spec = pltpu.PrefetchScalarGridSpec(
        num_scalar_prefetch=4,
        grid=(e,),
        in_specs=[pl.BlockSpec(memory_space=pl.ANY), wspec(d, f), wspec(d, f), wspec(f, d)],
        out_specs=pl.BlockSpec(memory_space=pl.ANY),
        scratch_shapes=[
            pltpu.VMEM((MOE_SLOTS, MOE_ROWS, dp), I32), pltpu.VMEM((MOE_SLOTS, MOE_ROWS, dp), I32),
            pltpu.SemaphoreType.DMA((MOE_SLOTS,)), pltpu.SemaphoreType.DMA((MOE_SLOTS,)),
            pltpu.VMEM((d, f), BF16), pltpu.VMEM((d, f), BF16), pltpu.VMEM((f, d), BF16),
        ],
    )
    return pl.pallas_call(
        _experts_kernel,
        out_shape=jax.ShapeDtypeStruct((p, dp), I32),
        grid_spec=grid_spec,
        compiler_params=_cparams("arbitrary"),
        name="routed_experts",
    )(pstarts, nblk, counts, n_total, xs, wg, wu, wd)


def _combine_kernel(yg_ref, w_ref, hf_ref, sg_ref, su_ref, sd_ref, x_ref, gf_ref, lg_ref, lb_ref, o_ref, *, alpha):
    w = w_ref[...]
    r_lo, r_hi = _unpack_bf16_pairs(yg_ref[0])
    r_lo, r_hi = w[:, 0:1] * r_lo, w[:, 0:1] * r_hi
    for k in range(1, TOP_K):
        y_lo, y_hi = _unpack_bf16_pairs(yg_ref[k])
        r_lo, r_hi = r_lo + w[:, k:k + 1] * y_lo, r_hi + w[:, k:k + 1] * y_hi
    routed = jnp.concatenate([r_lo, r_hi], axis=1)
    hf = jnp.concatenate(_unpack_bf16_pairs(hf_ref[...]), axis=1).astype(BF16)
    g = jnp.dot(hf, sg_ref[...], preferred_element_type=F32)
    u = jnp.dot(hf, su_ref[...], preferred_element_type=F32)
    shared = jnp.dot((_silu(g) * u).astype(BF16), sd_ref[...], preferred_element_type=F32)
    o_ref[...] = _layer_norm(alpha * x_ref[...] + gf_ref[0] * (routed + shared), lg_ref[...], lb_ref[...])


def _combine(yg, w_tk, hf, sg_bf, su_bf, sd_bf, x1, gf, ln_g, ln_b, alpha, tm, per_b):
    t, d = x1.shape
    f = sg_bf.shape[1]
    row = pl.BlockSpec((1, d), lambda i: (0, 0))
    tile = pl.BlockSpec((tm, d), lambda i: (i, 0))
    return pl.pallas_call(
        functools.partial(_combine_kernel, alpha=alpha),
        out_shape=jax.ShapeDtypeStruct((t, d), F32),
        grid=(t // tm,),
        in_specs=[
            pl.BlockSpec((TOP_K, tm, d // 2), lambda i: (0, i, 0)),
            pl.BlockSpec((tm, TOP_K), lambda i: (i, 0)),
            pl.BlockSpec((tm, d // 2), lambda i: (i, 0)),
            pl.BlockSpec((d, f), lambda i: (0, 0)),
            pl.BlockSpec((d, f), lambda i: (0, 0)),
            pl.BlockSpec((f, d), lambda i: (0, 0)),
            tile,
            pl.BlockSpec((1, 1, d), lambda i: (i // per_b, 0, 0)),
            row, row,
        ],
        out_specs=tile,
        compiler_params=_cparams("parallel"),
        name="moe_combine",
    )(yg, w_tk, hf, sg_bf, su_bf, sd_bf, x1, gf, ln_g.reshape(1, d), ln_b.reshape(1, d))


def _moe_layer(x1, hf, scf, shf, gf, router_w, router_bias, wg, wu, wd, layer, sg, su, sd, ln_g, ln_b, alpha, tm):
    b, n, d = x1.shape
    t = b * n
    e = router_w.shape[1]
    idx_t, w_t, rank_t, cnt = _route(x1, scf, shf, router_w.T, router_bias, tm)

    counts = cnt[:, 0]
    padded = (counts + MOE_ROWS - 1) // MOE_ROWS * MOE_ROWS
    pends = jnp.cumsum(padded)
    pstarts = pends - padded
    pos_t = _slot_positions(idx_t, rank_t, pstarts)
    p = t * TOP_K + e * MOE_ROWS
    pos3 = pos_t.reshape(TOP_K, t // SC_SCATTER_CHUNK, SC_SCATTER_CHUNK).transpose(1, 0, 2)

    hf2 = hf.reshape(t, d // 2)
    xs = _scatter_rows(hf2, pos3, p)
    yb = _routed_experts(xs, wg, wu, wd, layer, pstarts.astype(I32), (padded // MOE_ROWS).astype(I32), counts,
                         (pends[-1:] // MOE_ROWS).astype(I32))
    yg = _gather_rows(yb, pos_t.reshape(-1)).reshape(TOP_K, t, d // 2)
    out = _combine(yg, w_t.T, hf2, sg.astype(BF16), su.astype(BF16), sd.astype(BF16), x1.reshape(t, d), gf,
                   ln_g, ln_b, alpha, tm, n // tm)
    return out.reshape(b, n, d)


def _rope_tables(n):
    rows = n // GRID_W
    row_pos = jnp.repeat(jnp.arange(rows, dtype=F32), GRID_W)
    col_pos = jnp.tile(jnp.arange(GRID_W, dtype=F32), rows)
    half = LANES // 4
    lane = jnp.arange(LANES)
    in_blk = lane % half
    freq = ROPE_THETA ** (-(2.0 * (in_blk % (half // 2)).astype(F32)) / half)
    use_col = (lane // half) % 2 == 1
    pos = jnp.where(use_col[None, :], col_pos[:, None], row_pos[:, None])
    ang = pos * freq[None, :]
    lo = (in_blk < half // 2)[None, :]
    sin = jnp.sin(ang)
    return jnp.cos(ang), jnp.where(lo, -sin, 0.0), jnp.where(lo, 0.0, sin)


def kernel(x, c, ctx, c_ctx, w_mod, b_mod, ln_g, ln_b, attn_w_in, attn_w_out, attn_lambda, attn_subln_g,
           sgu_w_in, sgu_b_in, sgu_norm_g, sgu_norm_b, sgu_w_s, sgu_b_s, sgu_w_out,
           router_w, router_bias, exp_w_gate, exp_w_up, exp_w_down, sh_w_gate, sh_w_up, sh_w_down):
    b, n, d = x.shape
    depth = w_mod.shape[0]
    assert b <= 7 and d == ATT_HEADS * LANES and n % GRID_W == 0
    alpha = (2 * depth) ** 0.25
    head_dim = d // ATT_HEADS // 2
    tm = TOKEN_TILE if n % TOKEN_TILE == 0 else TOKEN_TILE // 2

    cs = jnp.zeros((8, d), F32).at[:b].set(c).at[b].set(c_ctx)
    mods = _modulation(cs, w_mod, b_mod)

    def mod_vec(i, j):
        return mods[i, :, j * d:(j + 1) * d].reshape(8, 1, d)

    for i in range(depth):
        sh_m, sc_m, g_m, sh_f, sc_f, g_f = (mod_vec(i, j) for j in range(6))
        if i % N_MIXERS == 0:
            a = i // N_MIXERS
            lam_init = 0.8 - 0.6 * math.exp(-0.3 * i)
            w_in_bf = attn_w_in[a].astype(BF16)
            cos, slo, shi = _rope_tables(n)
            q_scale = head_dim ** -0.5 * math.log2(math.e)
            q, k, vt = _qkv_proj(x, sc_m, sh_m, w_in_bf, cos, slo, shi, q_scale, tm)
            kc, vct = _kv_ctx_proj(ctx, sc_m[b:b + 1], sh_m[b:b + 1], w_in_bf[:, d:])
            pre = _diff_attention(q, k, vt, kc, vct, attn_lambda[a], attn_subln_g[a], lam_init,
                                  min(ATT_QUERY_TILE, n), min(ATT_KEY_CHUNK, n // 2))
            w_out_bf = attn_w_out[a].astype(BF16)
        else:
            s = i // N_MIXERS
            pre = _sgu_mixer(x, sc_m, sh_m, sgu_w_in[s].astype(BF16), sgu_b_in[s], sgu_norm_g[s], sgu_norm_b[s],
                             sgu_w_s[s].astype(BF16), sgu_b_s[s].T, tm)
            w_out_bf = sgu_w_out[s].astype(BF16)
        x1, hf = _post_mixer(pre, w_out_bf, x, g_m, ln_g[i, 0], ln_b[i, 0], sc_f, sh_f, alpha, tm)
        x = _moe_layer(x1, hf, sc_f, sh_f, g_f, router_w[i], router_bias[i], exp_w_gate, exp_w_up, exp_w_down, i,
                       sh_w_gate[i], sh_w_up[i], sh_w_down[i], ln_g[i, 1], ln_b[i, 1], alpha, tm)
    return x
```

```python
import functools
import math

import jax
import jax.numpy as jnp
from jax import lax
from jax.experimental import pallas as pl
from jax.experimental.pallas import tpu as pltpu
from jax.experimental.pallas import tpu_sc as plsc

F32 = jnp.float32
BF16 = jnp.bfloat16
I32 = jnp.int32

GRID_W = 64
ATT_HEADS = 8
ROPE_THETA = 10000.0
SGU_CHUNK = 128
SGU_GROUPS = 8
TOP_K = 8
N_GROUPS = 8
TOPK_GROUPS = 4
ROUTED_SCALE = 2.5
LN_EPS = 1e-5
N_MIXERS = 2

LANES = 128
MOE_ROWS = 256
SLOT_TILE = 2048
ROUTE_SLAB = 512
MOE_GROUPS = (2, 1)
MOE_LOOKAHEAD = 6
MOE_SLOTS = MOE_LOOKAHEAD + MOE_GROUPS[0]
TOKEN_TILE = 512
ATT_QUERY_TILE = 2048
ATT_KEY_CHUNK = 512
ATT_STRIP = 16
SC_CORES = 2
SC_SUBCORES = 16
SC_CHUNK = 64
SC_SCATTER_CHUNK = 128
VMEM_LIMIT = 56 * 1024 * 1024
NEG_INF = float("-inf")


def _cparams(*sem):
    return pltpu.CompilerParams(dimension_semantics=sem, vmem_limit_bytes=VMEM_LIMIT)


def _layer_norm(z, g, b):
    mu = jnp.mean(z, axis=-1, keepdims=True)
    zc = z - mu
    var = jnp.mean(zc * zc, axis=-1, keepdims=True)
    return zc * lax.rsqrt(var + LN_EPS) * g + b


def _silu(x):
    return x * (1.0 / (1.0 + jnp.exp(-x)))


_HIGH_HALF = -65536


def _pack_bf16_pairs(y):
    w = y.shape[1] // 2
    bits = lax.bitcast_convert_type(y.astype(BF16).astype(F32), I32)
    return lax.shift_right_logical(bits[:, :w], 16) | (bits[:, w:] & _HIGH_HALF)


def _unpack_bf16_pairs(p):
    return (lax.bitcast_convert_type(lax.shift_left(p, 16), F32),
            lax.bitcast_convert_type(p & _HIGH_HALF, F32))


def _mod_kernel(cs_ref, w_ref, b_ref, o_ref):
    s = _silu(cs_ref[...])
    o_ref[0] = jnp.dot(s, w_ref[0], precision=lax.Precision.HIGHEST,
                       preferred_element_type=F32) + b_ref[0]


def _modulation(cs, w_mod, b_mod):
    depth, d, n6 = w_mod.shape
    tn = n6 // 4
    return pl.pallas_call(
        _mod_kernel,
        out_shape=jax.ShapeDtypeStruct((depth, 8, n6), F32),
        grid=(depth, n6 // tn),
        in_specs=[
            pl.BlockSpec((8, d), lambda l, j: (0, 0)),
            pl.BlockSpec((1, d, tn), lambda l, j: (l, 0, j)),
            pl.BlockSpec((1, 1, tn), lambda l, j: (l, 0, j)),
        ],
        out_specs=pl.BlockSpec((1, 8, tn), lambda l, j: (l, 0, j)),
        compiler_params=_cparams("parallel", "parallel"),
        name="modulation",
    )(cs, w_mod, b_mod.reshape(depth, 1, n6))


def _rope(xh, cos, sin_lo, sin_hi):
    return xh * cos + pltpu.roll(xh, LANES - 16, 1) * sin_lo + pltpu.roll(xh, 16, 1) * sin_hi


def _qkv_kernel(x_ref, sc_ref, sh_ref, w_ref, cos_ref, slo_ref, shi_ref, q_ref, k_ref, v_ref, *, q_scale):
    d = x_ref.shape[-1]
    h = (x_ref[0] * (1.0 + sc_ref[0]) + sh_ref[0]).astype(BF16)
    cos, slo, shi = cos_ref[...], slo_ref[...], shi_ref[...]
    q = jnp.dot(h, w_ref[:, 0:d], preferred_element_type=F32)
    for hd in range(ATT_HEADS):
        q_ref[0, hd] = (_rope(q[:, hd * LANES:(hd + 1) * LANES], cos, slo, shi) * q_scale).astype(BF16)
    k = jnp.dot(h, w_ref[:, d:2 * d], preferred_element_type=F32)
    for hd in range(ATT_HEADS):
        k_ref[0, hd] = _rope(k[:, hd * LANES:(hd + 1) * LANES], cos, slo, shi).astype(BF16)
    v = jnp.dot(h, w_ref[:, 2 * d:3 * d], preferred_element_type=F32)
    for hd in range(ATT_HEADS):
        v_ref[0, hd] = v[:, hd * LANES:(hd + 1) * LANES].T.astype(BF16)


def _qkv_proj(x, sc, sh, w_bf, cos, slo, shi, q_scale, tn):
    b, n, d = x.shape
    hd_shape = jax.ShapeDtypeStruct((b, ATT_HEADS, n, LANES), BF16)
    vec = pl.BlockSpec((1, 1, d), lambda bi, i: (bi, 0, 0))
    tab = pl.BlockSpec((tn, LANES), lambda bi, i: (i, 0))
    out = pl.BlockSpec((1, ATT_HEADS, tn, LANES), lambda bi, i: (bi, 0, i, 0))
    out_t = pl.BlockSpec((1, ATT_HEADS, LANES, tn), lambda bi, i: (bi, 0, 0, i))
    return pl.pallas_call(
        functools.partial(_qkv_kernel, q_scale=q_scale),
        out_shape=(hd_shape, hd_shape, jax.ShapeDtypeStruct((b, ATT_HEADS, LANES, n), BF16)),
        grid=(b, n // tn),
        in_specs=[
            pl.BlockSpec((1, tn, d), lambda bi, i: (bi, i, 0)),
            vec, vec,
            pl.BlockSpec((d, 3 * d), lambda bi, i: (0, 0)),
            tab, tab, tab,
        ],
        out_specs=(out, out, out_t),
        compiler_params=_cparams("parallel", "parallel"),
        name="qkv_proj",
    )(x, sc, sh, w_bf, cos, slo, shi)


def _kv_ctx_kernel(x_ref, sc_ref, sh_ref, w_ref, k_ref, v_ref):
    d = x_ref.shape[-1]
    h = (x_ref[0] * (1.0 + sc_ref[0]) + sh_ref[0]).astype(BF16)
    k = jnp.dot(h, w_ref[:, 0:d], preferred_element_type=F32)
    v = jnp.dot(h, w_ref[:, d:2 * d], preferred_element_type=F32)
    for hd in range(ATT_HEADS):
        k_ref[0, hd] = k[:, hd * LANES:(hd + 1) * LANES].astype(BF16)
        v_ref[0, hd] = v[:, hd * LANES:(hd + 1) * LANES].T.astype(BF16)


def _kv_ctx_proj(ctx, sc, sh, w_kv_bf):
    b, c, d = ctx.shape
    hd_shape = jax.ShapeDtypeStruct((b, ATT_HEADS, c, LANES), BF16)
    vec = pl.BlockSpec((1, 1, d), lambda bi: (0, 0, 0))
    out = pl.BlockSpec((1, ATT_HEADS, c, LANES), lambda bi: (bi, 0, 0, 0))
    out_t = pl.BlockSpec((1, ATT_HEADS, LANES, c), lambda bi: (bi, 0, 0, 0))
    return pl.pallas_call(
        _kv_ctx_kernel,
        out_shape=(hd_shape, jax.ShapeDtypeStruct((b, ATT_HEADS, LANES, c), BF16)),
        grid=(b,),
        in_specs=[
            pl.BlockSpec((1, c, d), lambda bi: (bi, 0, 0)),
            vec, vec,
            pl.BlockSpec((d, 2 * d), lambda bi: (0, 0)),
        ],
        out_specs=(out, out_t),
        compiler_params=_cparams("parallel"),
        name="kv_ctx_proj",
    )(ctx, sc, sh, w_kv_bf)


def _attn_finish(acc_a, l_a, acc_b, l_b, lamp_ref, g_ref, lam_init):
    lp = lamp_ref[...]
    lam = (jnp.exp(jnp.sum(lp[0:1] * lp[1:2], axis=-1, keepdims=True))
           - jnp.exp(jnp.sum(lp[2:3] * lp[3:4], axis=-1, keepdims=True)) + lam_init)
    o = acc_a / l_a - lam * (acc_b / l_b)
    o = o * lax.rsqrt(jnp.mean(o * o, axis=0, keepdims=True) + LN_EPS) * g_ref[...] * (1.0 - lam_init)
    return o.T.astype(BF16)


def _attn_kernel(q_ref, k_ref, vt_ref, kc_ref, vct_ref, lamp_ref, g_ref, o_ref,
                 s00, s01, s10, s11, p0, p1, acc0, acc1, *, tk, lam_init):
    q = q_ref[0, 0]
    tq = q.shape[0]
    half = LANES // 2
    lane = lax.broadcasted_iota(jnp.int32, q.shape, 1)
    zero = jnp.zeros_like(q)
    qs = (jnp.where(lane < half, q, zero), jnp.where(lane >= half, q, zero))
    n_chunks = k_ref.shape[2] // tk
    s_scr = ((s00, s01), (s10, s11))
    p_scr, acc_scr = (p0, p1), (acc0, acc1)
    nt = (((1,), (1,)), ((), ()))

    def scores(slot, kc):
        width = kc.shape[0]
        for mp in range(2):
            s_scr[slot][mp][0:width, :] = lax.dot_general(kc, qs[mp], nt, preferred_element_type=F32)

    def absorb(slot, vct, shift, sums):
        width = vct.shape[1]
        sums = list(sums)
        for mp in range(2):
            part = sums[mp]
            for r in range(width // ATT_STRIP):
                lo = r * ATT_STRIP
                tiles = [jnp.exp2(s_scr[slot][mp][lo + 8 * u:lo + 8 * (u + 1), :] - shift[mp])
                         for u in range(ATT_STRIP // 8)]
                p_scr[mp][lo:lo + ATT_STRIP, :] = jnp.concatenate(tiles, axis=0).astype(BF16)
                part = part + functools.reduce(lambda x, y: x + y, tiles)
            sums[mp] = part
            acc_scr[mp][...] += jnp.dot(vct, p_scr[mp][0:width, :], preferred_element_type=F32)
        return tuple(sums)

    def k_chunk(j):
        return k_ref[0, 0, pl.ds(pl.multiple_of(j * tk, tk), tk), :]

    def vt_chunk(j):
        return vt_ref[0, 0, :, pl.ds(pl.multiple_of(j * tk, tk), tk)]

    scores(0, k_chunk(0))
    shift = tuple(jnp.broadcast_to(jnp.max(s_scr[0][mp][...], axis=0, keepdims=True), (8, tq)) for mp in range(2))
    for mp in range(2):
        acc_scr[mp][...] = jnp.zeros((LANES, tq), F32)
    sums = (jnp.zeros((8, tq), F32), jnp.zeros((8, tq), F32))

    def pair(jj, sums):
        j0 = 2 * jj
        scores(1, k_chunk(j0 + 1))
        sums = absorb(0, vt_chunk(j0), shift, sums)
        scores(0, k_chunk(j0 + 2))
        return absorb(1, vt_chunk(j0 + 1), shift, sums)

    sums = lax.fori_loop(0, n_chunks // 2 - 1, pair, sums)
    scores(1, k_chunk(n_chunks - 1))
    sums = absorb(0, vt_chunk(n_chunks - 2), shift, sums)
    scores(0, kc_ref[0, 0])
    sums = absorb(1, vt_chunk(n_chunks - 1), shift, sums)
    sums = absorb(0, vct_ref[0, 0], shift, sums)

    tot = [jnp.sum(sums[mp], axis=0, keepdims=True) for mp in range(2)]
    bad = sum(jnp.sum(jnp.where(jnp.isfinite(x), 0.0, 1.0)) for x in (tot[0], tot[1], acc0[...], acc1[...]))

    @pl.when(bad == 0.0)
    def _():
        o_ref[0] = _attn_finish(acc0[...], tot[0], acc1[...], tot[1], lamp_ref, g_ref, lam_init)

    @pl.when(bad != 0.0)
    def _():
        def update(carry, kc, vct):
            new = []
            for mp in range(2):
                m, l, acc = carry[mp]
                s = lax.dot_general(kc, qs[mp], nt, preferred_element_type=F32)
                mn = jnp.maximum(m, jnp.max(s, axis=0, keepdims=True))
                a = jnp.exp2(m - mn)
                p = jnp.exp2(s - mn)
                new.append((mn, a * l + jnp.sum(p, axis=0, keepdims=True),
                            a * acc + jnp.dot(vct, p.astype(BF16), preferred_element_type=F32)))
            return tuple(new)

        init = tuple((jnp.full((1, tq), NEG_INF, F32), jnp.zeros((1, tq), F32), jnp.zeros((LANES, tq), F32))
                     for _ in range(2))
        carry = lax.fori_loop(0, n_chunks, lambda j, c: update(c, k_chunk(j), vt_chunk(j)), init)
        (_, l_a, acc_a), (_, l_b, acc_b) = update(carry, kc_ref[0, 0], vct_ref[0, 0])
        o_ref[0] = _attn_finish(acc_a, l_a, acc_b, l_b, lamp_ref, g_ref, lam_init)


def _diff_attention(q, k, vt, kc, vct, lam_p, subln_g, lam_init, tq, tk):
    b, h, n, _ = q.shape
    c = kc.shape[2]
    assert n % tk == 0 and (n // tk) % 2 == 0 and c <= tk and tk % ATT_STRIP == 0 and c % ATT_STRIP == 0
    spec = lambda r, cols: pl.BlockSpec((1, 1, r, cols), lambda bi, hi, i: (bi, hi, 0, 0))
    return pl.pallas_call(
        functools.partial(_attn_kernel, tk=tk, lam_init=lam_init),
        out_shape=jax.ShapeDtypeStruct((b, n, h * LANES), BF16),
        grid=(b, h, n // tq),
        in_specs=[
            pl.BlockSpec((1, 1, tq, LANES), lambda bi, hi, i: (bi, hi, i, 0)),
            spec(n, LANES), spec(LANES, n), spec(c, LANES), spec(LANES, c),
            pl.BlockSpec(lam_p.shape, lambda bi, hi, i: (0, 0)),
            pl.BlockSpec((LANES, 1), lambda bi, hi, i: (0, 0)),
        ],
        out_specs=pl.BlockSpec((1, tq, LANES), lambda bi, hi, i: (bi, i, hi)),
        scratch_shapes=[
            *[pltpu.VMEM((tk, tq), F32)] * 4,
            *[pltpu.VMEM((tk, tq), BF16)] * 2,
            *[pltpu.VMEM((LANES, tq), F32)] * 2,
        ],
        compiler_params=_cparams("parallel", "parallel", "parallel"),
        name="diff_attention",
    )(q, k, vt, kc, vct, lam_p, subln_g.reshape(LANES, 1))


def _sgu_kernel(x_ref, sc_ref, sh_ref, w_ref, b_ref, ng_ref, nb_ref, ws_ref, bs_ref, t_ref):
    f = t_ref.shape[-1]
    cg = f // SGU_GROUPS
    tm = x_ref.shape[1]
    h = (x_ref[0] * (1.0 + sc_ref[0]) + sh_ref[0]).astype(BF16)
    z = jnp.dot(h, w_ref[...], preferred_element_type=F32) + b_ref[...]
    z = 0.5 * z * (1.0 + lax.erf(z * (2.0 ** -0.5)))
    u = z[:, :f]
    v = _layer_norm(z[:, f:], ng_ref[...], nb_ref[...]).astype(BF16)
    for c in range(tm // SGU_CHUNK):
        rows = slice(c * SGU_CHUNK, (c + 1) * SGU_CHUNK)
        for g in range(SGU_GROUPS):
            cols = slice(g * cg, (g + 1) * cg)
            vm = jnp.dot(ws_ref[g], v[rows, cols], preferred_element_type=F32) + bs_ref[:, g:g + 1]
            t_ref[0, rows, cols] = (u[rows, cols] * vm).astype(BF16)


def _sgu_mixer(x, sc, sh, w_in_bf, b_in, norm_g, norm_b, w_s_bf, b_s_t, tm):
    b, n, d = x.shape
    f2 = w_in_bf.shape[1]
    f = f2 // 2
    vec = pl.BlockSpec((1, 1, d), lambda bi, i: (bi, 0, 0))
    full2 = lambda a: pl.BlockSpec(a.shape, lambda bi, i: (0,) * a.ndim)
    b_in2, ng2, nb2 = b_in.reshape(1, f2), norm_g.reshape(1, f), norm_b.reshape(1, f)
    return pl.pallas_call(
        _sgu_kernel,
        out_shape=jax.ShapeDtypeStruct((b, n, f), BF16),
        grid=(b, n // tm),
        in_specs=[
            pl.BlockSpec((1, tm, d), lambda bi, i: (bi, i, 0)),
            vec, vec,
            full2(w_in_bf), full2(b_in2), full2(ng2), full2(nb2), full2(w_s_bf), full2(b_s_t),
        ],
        out_specs=pl.BlockSpec((1, tm, f), lambda bi, i: (bi, i, 0)),
        compiler_params=_cparams("parallel", "parallel"),
        name="sgu_mixer",
    )(x, sc, sh, w_in_bf, b_in2, ng2, nb2, w_s_bf, b_s_t)


def _post_kernel(pre_ref, w_ref, x_ref, gm_ref, lg_ref, lb_ref, scf_ref, shf_ref, x1_ref, hf_ref, *, alpha):
    y = jnp.dot(pre_ref[0], w_ref[...], preferred_element_type=F32)
    x1 = _layer_norm(alpha * x_ref[0] + gm_ref[0] * y, lg_ref[...], lb_ref[...])
    x1_ref[0] = x1
    hf_ref[0] = _pack_bf16_pairs(x1 * (1.0 + scf_ref[0]) + shf_ref[0])


def _post_mixer(pre, w_bf, x, gm, ln_g, ln_b, scf, shf, alpha, tm):
    b, n, d = x.shape
    kd = pre.shape[-1]
    vec = pl.BlockSpec((1, 1, d), lambda bi, i: (bi, 0, 0))
    row = pl.BlockSpec((1, d), lambda bi, i: (0, 0))
    tile = pl.BlockSpec((1, tm, d), lambda bi, i: (bi, i, 0))
    return pl.pallas_call(
        functools.partial(_post_kernel, alpha=alpha),
        out_shape=(jax.ShapeDtypeStruct((b, n, d), F32), jax.ShapeDtypeStruct((b, n, d // 2), I32)),
        grid=(b, n // tm),
        in_specs=[
            pl.BlockSpec((1, tm, kd), lambda bi, i: (bi, i, 0)),
            pl.BlockSpec((kd, d), lambda bi, i: (0, 0)),
            tile, vec, row, row, vec, vec,
        ],
        out_specs=(tile, pl.BlockSpec((1, tm, d // 2), lambda bi, i: (bi, i, 0))),
        compiler_params=_cparams("parallel", "parallel"),
        name="post_mixer",
    )(pre, w_bf, x, gm, ln_g.reshape(1, d), ln_b.reshape(1, d), scf, shf)


def _route_select(scores, choice):
    e, w = scores.shape
    ge = e // N_GROUPS
    g3 = choice.reshape(N_GROUPS, ge, w)
    ri = lax.broadcasted_iota(jnp.int32, g3.shape, 1).astype(F32)
    m1 = jnp.max(g3, axis=1, keepdims=True)
    first = jnp.min(jnp.where(g3 == m1, ri, float(ge)), axis=1, keepdims=True)
    m2 = jnp.max(jnp.where(ri == first, NEG_INF, g3), axis=1, keepdims=True)
    gs = m1 + m2

    gi = lax.broadcasted_iota(jnp.int32, gs.shape, 0).astype(F32)
    gsel = jnp.zeros(gs.shape, F32)
    cur = gs
    for _ in range(TOPK_GROUPS):
        m = jnp.max(cur, axis=0, keepdims=True)
        f = jnp.min(jnp.where(cur == m, gi, float(N_GROUPS)), axis=0, keepdims=True)
        hit = gi == f
        gsel = jnp.where(hit, 1.0, gsel)
        cur = jnp.where(hit, NEG_INF, cur)
    emask = jnp.broadcast_to(gsel, g3.shape).reshape(e, w)
    masked = jnp.where(emask > 0.5, choice, NEG_INF)

    ei = lax.broadcasted_iota(jnp.int32, (e, w), 0).astype(F32)
    onehot = jnp.zeros((e, w), F32)
    idxs, ws = [], []
    for _ in range(TOP_K):
        m = jnp.max(masked, axis=0, keepdims=True)
        f = jnp.min(jnp.where(masked == m, ei, float(e)), axis=0, keepdims=True)
        hit = ei == f
        idxs.append(f)
        ws.append(jnp.sum(jnp.where(hit, scores, 0.0), axis=0, keepdims=True))
        masked = jnp.where(hit, NEG_INF, masked)
        onehot = jnp.where(hit, 1.0, onehot)
    return idxs, ws, onehot


def _route_kernel(x_ref, sc_ref, sh_ref, wr_ref, rb_ref, idx_ref, w_ref, rank_ref, cnt_ref, carry_ref):
    i = pl.program_id(0)
    e = wr_ref.shape[1]
    tm = x_ref.shape[0]

    @pl.when(i == 0)
    def _():
        carry_ref[...] = jnp.zeros_like(carry_ref)

    h = x_ref[...] * (1.0 + sc_ref[0]) + sh_ref[0]
    h_hi = h.astype(BF16)
    h_lo = (h - h_hi.astype(F32)).astype(BF16)
    nt = (((1,), (1,)), ((), ()))
    logits = (lax.dot_general(wr_ref[0], h_hi, nt, preferred_element_type=F32)
              + (lax.dot_general(wr_ref[0], h_lo, nt, preferred_element_type=F32)
                 + lax.dot_general(wr_ref[1], h_hi, nt, preferred_element_type=F32)))
    scores = 1.0 / (1.0 + jnp.exp(-logits))
    choice = scores + rb_ref[...]

    slab = min(ROUTE_SLAB, tm)
    slabs = [slice(j * slab, (j + 1) * slab) for j in range(tm // slab)]
    picks = [_route_select(scores[:, sl], choice[:, sl]) for sl in slabs]
    onehot = jnp.concatenate([pk[2] for pk in picks], axis=1)

    r_i = lax.broadcasted_iota(jnp.int32, (tm, tm), 0)
    c_i = lax.broadcasted_iota(jnp.int32, (tm, tm), 1)
    upper = jnp.where(r_i < c_i, 1.0, 0.0).astype(BF16)
    rk = jnp.dot(onehot.astype(BF16), upper, preferred_element_type=F32) + carry_ref[...]
    carry_ref[...] += jnp.sum(onehot, axis=1, keepdims=True)

    ei = lax.broadcasted_iota(jnp.int32, (e, slab), 0).astype(F32)
    for sl, (idxs, ws, _) in zip(slabs, picks):
        wsum = functools.reduce(lambda x, y: x + y, ws)
        for k in range(TOP_K):
            idx_ref[k:k + 1, sl] = idxs[k].astype(jnp.int32)
            w_ref[k:k + 1, sl] = ws[k] / wsum * ROUTED_SCALE
            rank_ref[k:k + 1, sl] = jnp.sum(jnp.where(ei == idxs[k], rk[:, sl], 0.0), axis=0,
                                            keepdims=True).astype(jnp.int32)
    cnt_ref[...] = jnp.broadcast_to(carry_ref[...], cnt_ref.shape).astype(jnp.int32)


def _route(x1, scf, shf, wr_t, rbias, tm):
    b, n, d = x1.shape
    t = b * n
    e = wr_t.shape[0]
    per_b = n // tm
    wr_hi = wr_t.astype(BF16)
    wr_split = jnp.stack([wr_hi, (wr_t - wr_hi.astype(F32)).astype(BF16)])
    vec = pl.BlockSpec((1, 1, d), lambda i: (i // per_b, 0, 0))
    out_t = pl.BlockSpec((TOP_K, tm), lambda i: (0, i))
    return pl.pallas_call(
        _route_kernel,
        out_shape=(jax.ShapeDtypeStruct((TOP_K, t), jnp.int32), jax.ShapeDtypeStruct((TOP_K, t), F32),
                   jax.ShapeDtypeStruct((TOP_K, t), jnp.int32), jax.ShapeDtypeStruct((e, LANES), jnp.int32)),
        grid=(t // tm,),
        in_specs=[
            pl.BlockSpec((tm, d), lambda i: (i, 0)),
            vec, vec,
            pl.BlockSpec((2, e, d), lambda i: (0, 0, 0)),
            pl.BlockSpec((e, 1), lambda i: (0, 0)),
        ],
        out_specs=(out_t, out_t, out_t, pl.BlockSpec((e, LANES), lambda i: (0, 0))),
        scratch_shapes=[pltpu.VMEM((e, 1), F32)],
        compiler_params=_cparams("arbitrary"),
        name="route",
    )(x1.reshape(t, d), scf, shf, wr_split, rbias.reshape(e, 1))


def _slot_kernel(idx_ref, rank_ref, ps_ref, pos_ref):
    e, w = ps_ref.shape[0], idx_ref.shape[1]
    ei = lax.broadcasted_iota(jnp.int32, (e, w), 0)
    starts = ps_ref[...]
    for k in range(TOP_K):
        base = jnp.sum(jnp.where(ei == idx_ref[k:k + 1, :], starts, 0.0), axis=0, keepdims=True)
        pos_ref[k:k + 1, :] = base.astype(jnp.int32) + rank_ref[k:k + 1, :]


def _slot_positions(idx_t, rank_t, pstarts):
    k, t = idx_t.shape
    e = pstarts.shape[0]
    w = min(SLOT_TILE, t)
    tile = pl.BlockSpec((k, w), lambda i: (0, i))
    return pl.pallas_call(
        _slot_kernel,
        out_shape=jax.ShapeDtypeStruct((k, t), I32),
        grid=(t // w,),
        in_specs=[tile, tile, pl.BlockSpec((e, 1), lambda i: (0, 0))],
        out_specs=tile,
        compiler_params=_cparams("parallel"),
        name="slot_positions",
    )(idx_t, rank_t, pstarts.astype(F32).reshape(e, 1))


def _gather_rows(table, idx):
    m = idx.shape[0]
    w = table.shape[1]
    workers = SC_CORES * SC_SUBCORES
    n_ch = m // (workers * SC_CHUNK)
    assert m % (workers * SC_CHUNK) == 0 and n_ch % 2 == 0
    mesh = plsc.VectorSubcoreMesh(core_axis_name="c", subcore_axis_name="s",
                                  num_cores=SC_CORES, num_subcores=SC_SUBCORES)

    @functools.partial(
        pl.kernel, mesh=mesh,
        out_type=jax.ShapeDtypeStruct((m, w), table.dtype),
        scratch_types=[
            pltpu.VMEM((n_ch, SC_CHUNK), I32),
            pltpu.VMEM((SC_CHUNK, w), table.dtype), pltpu.VMEM((SC_CHUNK, w), table.dtype),
            pltpu.SemaphoreType.DMA, pltpu.SemaphoreType.DMA, pltpu.SemaphoreType.DMA, pltpu.SemaphoreType.DMA,
        ],
        name="sc_gather_rows",
    )
    def gather(table_hbm, idx_hbm, out_hbm, idx_all, buf0, buf1, gsem0, gsem1, wsem0, wsem1):
        first = (lax.axis_index("s") * SC_CORES + lax.axis_index("c")) * n_ch
        bufs, gsem, wsem = (buf0, buf1), (gsem0, gsem1), (wsem0, wsem1)
        pltpu.sync_copy(idx_hbm.at[pl.ds(first, n_ch)], idx_all)

        def gather_copy(j, s):
            return pltpu.make_async_copy(table_hbm.at[idx_all.at[j]], bufs[s], gsem[s])

        def write_copy(j, s):
            rows = pl.ds(pl.multiple_of((first + j) * SC_CHUNK, SC_CHUNK), SC_CHUNK)
            return pltpu.make_async_copy(bufs[s], out_hbm.at[rows], wsem[s])

        gather_copy(0, 0).start()

        @pl.loop(0, n_ch, step=2)
        def _(jj):
            for s in range(2):
                j = jj + s

                @pl.when(j >= 1)
                def _():
                    write_copy(j - 1, 1 - s).wait()

                @pl.when(j + 1 < n_ch)
                def _():
                    gather_copy(j + 1, 1 - s).start()

                gather_copy(j, s).wait()
                write_copy(j, s).start()

        write_copy(n_ch - 1, 1).wait()

    return gather(table, idx.reshape(m // SC_CHUNK, SC_CHUNK))


def _scatter_rows(src, pos3, p):
    w = src.shape[1]
    n_chunks, k, ch = pos3.shape
    workers = SC_CORES * SC_SUBCORES
    per_w = n_chunks // workers
    assert ch == SC_SCATTER_CHUNK and n_chunks % workers == 0 and src.shape[0] == n_chunks * ch
    mesh = plsc.VectorSubcoreMesh(core_axis_name="c", subcore_axis_name="s",
                                  num_cores=SC_CORES, num_subcores=SC_SUBCORES)

    @functools.partial(
        pl.kernel, mesh=mesh,
        out_type=jax.ShapeDtypeStruct((p, w), src.dtype),
        scratch_types=[
            pltpu.VMEM((k, ch), I32),
            pltpu.VMEM((ch, w), src.dtype),
            pltpu.SemaphoreType.DMA,
        ],
        name="sc_scatter_rows",
    )
    def scatter(src_hbm, pos_hbm, out_hbm, idx_v, rows_v, sem):
        first = (lax.axis_index("s") * SC_CORES + lax.axis_index("c")) * per_w

        @pl.loop(0, per_w)
        def _(j):
            c = first + j
            pltpu.sync_copy(pos_hbm.at[c], idx_v)
            pltpu.sync_copy(src_hbm.at[pl.ds(pl.multiple_of(c * ch, ch), ch)], rows_v)
            copies = [pltpu.async_copy(rows_v, out_hbm.at[idx_v.at[kk]], sem) for kk in range(k)]
            for cp in copies:
                cp.wait()

    return scatter(src, pos3)


def _experts_kernel(ps_ref, nb_ref, cnt_ref, nt_ref, xs_hbm, wg_ref, wu_ref, wd_ref, y_hbm,
                    xbuf, ybuf, in_sem, out_sem, wg_bf, wu_bf, wd_bf):
    e = pl.program_id(0)
    nb, cnt, n_total = nb_ref[e], cnt_ref[e], nt_ref[0]
    g0 = ps_ref[e] // MOE_ROWS

    def in_copy(g):
        rows = pl.ds(pl.multiple_of(g * MOE_ROWS, MOE_ROWS), MOE_ROWS)
        return pltpu.make_async_copy(xs_hbm.at[rows], xbuf.at[g % MOE_SLOTS], in_sem.at[g % MOE_SLOTS])

    def out_copy(g):
        rows = pl.ds(pl.multiple_of(g * MOE_ROWS, MOE_ROWS), MOE_ROWS)
        return pltpu.make_async_copy(ybuf.at[g % MOE_SLOTS], y_hbm.at[rows], out_sem.at[g % MOE_SLOTS])

    @pl.when(e == 0)
    def _():
        for j in range(MOE_LOOKAHEAD):
            @pl.when(j < n_total)
            def _():
                in_copy(j).start()

    @pl.when(nb > 0)
    def _():
        wg_bf[...] = wg_ref[0, 0].astype(BF16)
        wu_bf[...] = wu_ref[0, 0].astype(BF16)
        wd_bf[...] = wd_ref[0, 0].astype(BF16)

        def process(b, width):
            g = g0 + b
            for u in range(width):
                @pl.when(g + u + MOE_LOOKAHEAD < n_total)
                def _():
                    in_copy(g + u + MOE_LOOKAHEAD).start()

            for u in range(width):
                in_copy(g + u).wait()

                @pl.when(g + u >= MOE_SLOTS)
                def _():
                    out_copy(g + u - MOE_SLOTS).wait()

            packed = jnp.concatenate([xbuf[(g + u) % MOE_SLOTS] for u in range(width)], axis=0)
            row = lax.broadcasted_iota(I32, (width * MOE_ROWS, 1), 0) + b * MOE_ROWS
            x_lo, x_hi = (v.astype(BF16) for v in _unpack_bf16_pairs(jnp.where(row < cnt, packed, 0)))
            half = x_lo.shape[1]

            def up(w_bf):
                return (jnp.dot(x_lo, w_bf[:half, :], preferred_element_type=F32)
                        + jnp.dot(x_hi, w_bf[half:, :], preferred_element_type=F32))

            hb = (_silu(up(wg_bf)) * up(wu_bf)).astype(BF16)
            y = _pack_bf16_pairs(jnp.dot(hb, wd_bf[...], preferred_element_type=F32))
            for u in range(width):
                ybuf[(g + u) % MOE_SLOTS] = y[u * MOE_ROWS:(u + 1) * MOE_ROWS]
                out_copy(g + u).start()

        start = 0
        for width in MOE_GROUPS:
            count = (nb - start) // width

            def body(i, c, width=width, start=start):
                process(start + width * i, width)
                return c

            lax.fori_loop(0, count, body, 0)
            start = start + count * width

    @pl.when(e == pl.num_programs(0) - 1)
    def _():
        for j in range(MOE_SLOTS):
            @pl.when(n_total - 1 - j >= 0)
            def _():
                out_copy(n_total - 1 - j).wait()


def _routed_experts(xs, wg, wu, wd, layer, pstarts, nblk, counts, n_total):
    p, dp = xs.shape
    _, e, d, f = wg.shape
    wspec = lambda r, c: pl.BlockSpec((1, 1, r, c), lambda i, ps, nb, cnt, nt: (layer, i, 0, 0))
    grid_spec = pltpu.PrefetchScalarGridSpec(
        num_scalar_prefetch=4,
        grid=(e,),
        in_specs=[pl.BlockSpec(memory_space=pl.ANY), wspec(d, f), wspec(d, f), wspec(f, d)],
        out_specs=pl.BlockSpec(memory_space=pl.ANY),
        scratch_shapes=[
            pltpu.VMEM((MOE_SLOTS, MOE_ROWS, dp), I32), pltpu.VMEM((MOE_SLOTS, MOE_ROWS, dp), I32),
            pltpu.SemaphoreType.DMA((MOE_SLOTS,)), pltpu.SemaphoreType.DMA((MOE_SLOTS,)),
            pltpu.VMEM((d, f), BF16), pltpu.VMEM((d, f), BF16), pltpu.VMEM((f, d), BF16),
        ],
    )
    return pl.pallas_call(
        _experts_kernel,
        out_shape=jax.ShapeDtypeStruct((p, dp), I32),
        grid_spec=grid_spec,
        compiler_params=_cparams("arbitrary"),
        name="routed_experts",
    )(pstarts, nblk, counts, n_total, xs, wg, wu, wd)


def _combine_kernel(yg_ref, w_ref, hf_ref, sg_ref, su_ref, sd_ref, x_ref, gf_ref, lg_ref, lb_ref, o_ref, *, alpha):
    w = w_ref[...]
    r_lo, r_hi = _unpack_bf16_pairs(yg_ref[0])
    r_lo, r_hi = w[:, 0:1] * r_lo, w[:, 0:1] * r_hi
    for k in range(1, TOP_K):
        y_lo, y_hi = _unpack_bf16_pairs(yg_ref[k])
        r_lo, r_hi = r_lo + w[:, k:k + 1] * y_lo, r_hi + w[:, k:k + 1] * y_hi
    routed = jnp.concatenate([r_lo, r_hi], axis=1)
    hf = jnp.concatenate(_unpack_bf16_pairs(hf_ref[...]), axis=1).astype(BF16)
    g = jnp.dot(hf, sg_ref[...], preferred_element_type=F32)
    u = jnp.dot(hf, su_ref[...], preferred_element_type=F32)
    shared = jnp.dot((_silu(g) * u).astype(BF16), sd_ref[...], preferred_element_type=F32)
    o_ref[...] = _layer_norm(alpha * x_ref[...] + gf_ref[0] * (routed + shared), lg_ref[...], lb_ref[...])


def _combine(yg, w_tk, hf, sg_bf, su_bf, sd_bf, x1, gf, ln_g, ln_b, alpha, tm, per_b):
    t, d = x1.shape
    f = sg_bf.shape[1]
    row = pl.BlockSpec((1, d), lambda i: (0, 0))
    tile = pl.BlockSpec((tm, d), lambda i: (i, 0))
    return pl.pallas_call(
        functools.partial(_combine_kernel, alpha=alpha),
        out_shape=jax.ShapeDtypeStruct((t, d), F32),
        grid=(t // tm,),
        in_specs=[
            pl.BlockSpec((TOP_K, tm, d // 2), lambda i: (0, i, 0)),
            pl.BlockSpec((tm, TOP_K), lambda i: (i, 0)),
            pl.BlockSpec((tm, d // 2), lambda i: (i, 0)),
            pl.BlockSpec((d, f), lambda i: (0, 0)),
            pl.BlockSpec((d, f), lambda i: (0, 0)),
            pl.BlockSpec((f, d), lambda i: (0, 0)),
            tile,
            pl.BlockSpec((1, 1, d), lambda i: (i // per_b, 0, 0)),
            row, row,
        ],
        out_specs=tile,
        compiler_params=_cparams("parallel"),
        name="moe_combine",
    )(yg, w_tk, hf, sg_bf, su_bf, sd_bf, x1, gf, ln_g.reshape(1, d), ln_b.reshape(1, d))


def _moe_layer(x1, hf, scf, shf, gf, router_w, router_bias, wg, wu, wd, layer, sg, su, sd, ln_g, ln_b, alpha, tm):
    b, n, d = x1.shape
    t = b * n
    e = router_w.shape[1]
    idx_t, w_t, rank_t, cnt = _route(x1, scf, shf, router_w.T, router_bias, tm)

    counts = cnt[:, 0]
    padded = (counts + MOE_ROWS - 1) // MOE_ROWS * MOE_ROWS
    pends = jnp.cumsum(padded)
    pstarts = pends - padded
    pos_t = _slot_positions(idx_t, rank_t, pstarts)
    p = t * TOP_K + e * MOE_ROWS
    pos3 = pos_t.reshape(TOP_K, t // SC_SCATTER_CHUNK, SC_SCATTER_CHUNK).transpose(1, 0, 2)

    hf2 = hf.reshape(t, d // 2)
    xs = _scatter_rows(hf2, pos3, p)
    yb = _routed_experts(xs, wg, wu, wd, layer, pstarts.astype(I32), (padded // MOE_ROWS).astype(I32), counts,
                         (pends[-1:] // MOE_ROWS).astype(I32))
    yg = _gather_rows(yb, pos_t.reshape(-1)).reshape(TOP_K, t, d // 2)
    out = _combine(yg, w_t.T, hf2, sg.astype(BF16), su.astype(BF16), sd.astype(BF16), x1.reshape(t, d), gf,
                   ln_g, ln_b, alpha, tm, n // tm)
    return out.reshape(b, n, d)


def _rope_tables(n):
    rows = n // GRID_W
    row_pos = jnp.repeat(jnp.arange(rows, dtype=F32), GRID_W)
    col_pos = jnp.tile(jnp.arange(GRID_W, dtype=F32), rows)
    half = LANES // 4
    lane = jnp.arange(LANES)
    in_blk = lane % half
    freq = ROPE_THETA ** (-(2.0 * (in_blk % (half // 2)).astype(F32)) / half)
    use_col = (lane // half) % 2 == 1
    pos = jnp.where(use_col[None, :], col_pos[:, None], row_pos[:, None])
    ang = pos * freq[None, :]
    lo = (in_blk < half // 2)[None, :]
    sin = jnp.sin(ang)
    return jnp.cos(ang), jnp.where(lo, -sin, 0.0), jnp.where(lo, 0.0, sin)


def kernel(x, c, ctx, c_ctx, w_mod, b_mod, ln_g, ln_b, attn_w_in, attn_w_out, attn_lambda, attn_subln_g,
           sgu_w_in, sgu_b_in, sgu_norm_g, sgu_norm_b, sgu_w_s, sgu_b_s, sgu_w_out,
           router_w, router_bias, exp_w_gate, exp_w_up, exp_w_down, sh_w_gate, sh_w_up, sh_w_down):
    b, n, d = x.shape
    depth = w_mod.shape[0]
    assert b <= 7 and d == ATT_HEADS * LANES and n % GRID_W == 0
    alpha = (2 * depth) ** 0.25
    head_dim = d // ATT_HEADS // 2
    tm = TOKEN_TILE if n % TOKEN_TILE == 0 else TOKEN_TILE // 2

    cs = jnp.zeros((8, d), F32).at[:b].set(c).at[b].set(c_ctx)
    mods = _modulation(cs, w_mod, b_mod)

    def mod_vec(i, j):
        return mods[i, :, j * d:(j + 1) * d].reshape(8, 1, d)

    for i in range(depth):
        sh_m, sc_m, g_m, sh_f, sc_f, g_f = (mod_vec(i, j) for j in range(6))
        if i % N_MIXERS == 0:
            a = i // N_MIXERS
            lam_init = 0.8 - 0.6 * math.exp(-0.3 * i)
            w_in_bf = attn_w_in[a].astype(BF16)
            cos, slo, shi = _rope_tables(n)
            q_scale = head_dim ** -0.5 * math.log2(math.e)
            q, k, vt = _qkv_proj(x, sc_m, sh_m, w_in_bf, cos, slo, shi, q_scale, tm)
            kc, vct = _kv_ctx_proj(ctx, sc_m[b:b + 1], sh_m[b:b + 1], w_in_bf[:, d:])
            pre = _diff_attention(q, k, vt, kc, vct, attn_lambda[a], attn_subln_g[a], lam_init,
                                  min(ATT_QUERY_TILE, n), min(ATT_KEY_CHUNK, n // 2))
            w_out_bf = attn_w_out[a].astype(BF16)
        else:
            s = i // N_MIXERS
            pre = _sgu_mixer(x, sc_m, sh_m, sgu_w_in[s].astype(BF16), sgu_b_in[s], sgu_norm_g[s], sgu_norm_b[s],
                             sgu_w_s[s].astype(BF16), sgu_b_s[s].T, tm)
            w_out_bf = sgu_w_out[s].astype(BF16)
        x1, hf = _post_mixer(pre, w_out_bf, x, g_m, ln_g[i, 0], ln_b[i, 0], sc_f, sh_f, alpha, tm)
        x = _moe_layer(x1, hf, sc_f, sh_f, g_f, router_w[i], router_bias[i], exp_w_gate, exp_w_up, exp_w_down, i,
                       sh_w_gate[i], sh_w_up[i], sh_w_down[i], ln_g[i, 1], ln_b[i, 1], alpha, tm)
    return x
```

```python
import functools
import math

import jax
import jax.numpy as jnp
from jax import lax
from jax.experimental import pallas as pl
from jax.experimental.pallas import tpu as pltpu
from jax.experimental.pallas import tpu_sc as plsc

F32 = jnp.float32
BF16 = jnp.bfloat16
I32 = jnp.int32

GRID_W = 64
ATT_HEADS = 8
ROPE_THETA = 10000.0
SGU_CHUNK = 128
SGU_GROUPS = 8
TOP_K = 8
N_GROUPS = 8
TOPK_GROUPS = 4
ROUTED_SCALE = 2.5
LN_EPS = 1e-5
N_MIXERS = 2

LANES = 128
MOE_ROWS = 256
SLOT_TILE = 2048
ROUTE_SLAB = 512
MOE_GROUPS = (2, 1)
MOE_LOOKAHEAD = 6
MOE_SLOTS = MOE_LOOKAHEAD + MOE_GROUPS[0]
TOKEN_TILE = 512
ATT_QUERY_TILE = 4096
ATT_KEY_CHUNK = 256
ATT_STRIP = 16
SC_CORES = 2
SC_SUBCORES = 16
SC_CHUNK = 64
SC_SCATTER_CHUNK = 128
VMEM_LIMIT = 56 * 1024 * 1024
NEG_INF = float("-inf")


def _cparams(*sem):
    return pltpu.CompilerParams(dimension_semantics=sem, vmem_limit_bytes=VMEM_LIMIT)


def _layer_norm(z, g, b):
    mu = jnp.mean(z, axis=-1, keepdims=True)
    zc = z - mu
    var = jnp.mean(zc * zc, axis=-1, keepdims=True)
    return zc * lax.rsqrt(var + LN_EPS) * g + b


def _silu(x):
    return x * (1.0 / (1.0 + jnp.exp(-x)))


_HIGH_HALF = -65536


def _pack_bf16_pairs(y):
    w = y.shape[1] // 2
    bits = lax.bitcast_convert_type(y.astype(BF16).astype(F32), I32)
    return lax.shift_right_logical(bits[:, :w], 16) | (bits[:, w:] & _HIGH_HALF)


def _unpack_bf16_pairs(p):
    return (lax.bitcast_convert_type(lax.shift_left(p, 16), F32),
            lax.bitcast_convert_type(p & _HIGH_HALF, F32))


def _mod_kernel(cs_ref, w_ref, b_ref, o_ref):
    s = _silu(cs_ref[...])
    o_ref[0] = jnp.dot(s, w_ref[0], precision=lax.Precision.HIGHEST,
                       preferred_element_type=F32) + b_ref[0]


def _modulation(cs, w_mod, b_mod):
    depth, d, n6 = w_mod.shape
    tn = n6 // 4
    return pl.pallas_call(
        _mod_kernel,
        out_shape=jax.ShapeDtypeStruct((depth, 8, n6), F32),
        grid=(depth, n6 // tn),
        in_specs=[
            pl.BlockSpec((8, d), lambda l, j: (0, 0)),
            pl.BlockSpec((1, d, tn), lambda l, j: (l, 0, j)),
            pl.BlockSpec((1, 1, tn), lambda l, j: (l, 0, j)),
        ],
        out_specs=pl.BlockSpec((1, 8, tn), lambda l, j: (l, 0, j)),
        compiler_params=_cparams("parallel", "parallel"),
        name="modulation",
    )(cs, w_mod, b_mod.reshape(depth, 1, n6))


def _rope(xh, cos, sin_lo, sin_hi):
    return xh * cos + pltpu.roll(xh, LANES - 16, 1) * sin_lo + pltpu.roll(xh, 16, 1) * sin_hi


def _qkv_kernel(x_ref, sc_ref, sh_ref, w_ref, cos_ref, slo_ref, shi_ref, q_ref, k_ref, v_ref, *, q_scale):
    d = x_ref.shape[-1]
    h = (x_ref[0] * (1.0 + sc_ref[0]) + sh_ref[0]).astype(BF16)
    cos, slo, shi = cos_ref[...], slo_ref[...], shi_ref[...]
    q = jnp.dot(h, w_ref[:, 0:d], preferred_element_type=F32)
    for hd in range(ATT_HEADS):
        q_ref[0, hd] = (_rope(q[:, hd * LANES:(hd + 1) * LANES], cos, slo, shi) * q_scale).astype(BF16)
    k = jnp.dot(h, w_ref[:, d:2 * d], preferred_element_type=F32)
    for hd in range(ATT_HEADS):
        k_ref[0, hd] = _rope(k[:, hd * LANES:(hd + 1) * LANES], cos, slo, shi).astype(BF16)
    v = jnp.dot(h, w_ref[:, 2 * d:3 * d], preferred_element_type=F32)
    for hd in range(ATT_HEADS):
        v_ref[0, hd] = v[:, hd * LANES:(hd + 1) * LANES].T.astype(BF16)


def _qkv_proj(x, sc, sh, w_bf, cos, slo, shi, q_scale, tn):
    b, n, d = x.shape
    hd_shape = jax.ShapeDtypeStruct((b, ATT_HEADS, n, LANES), BF16)
    vec = pl.BlockSpec((1, 1, d), lambda bi, i: (bi, 0, 0))
    tab = pl.BlockSpec((tn, LANES), lambda bi, i: (i, 0))
    out = pl.BlockSpec((1, ATT_HEADS, tn, LANES), lambda bi, i: (bi, 0, i, 0))
    out_t = pl.BlockSpec((1, ATT_HEADS, LANES, tn), lambda bi, i: (bi, 0, 0, i))
    return pl.pallas_call(
        functools.partial(_qkv_kernel, q_scale=q_scale),
        out_shape=(hd_shape, hd_shape, jax.ShapeDtypeStruct((b, ATT_HEADS, LANES, n), BF16)),
        grid=(b, n // tn),
        in_specs=[
            pl.BlockSpec((1, tn, d), lambda bi, i: (bi, i, 0)),
            vec, vec,
            pl.BlockSpec((d, 3 * d), lambda bi, i: (0, 0)),
            tab, tab, tab,
        ],
        out_specs=(out, out, out_t),
        compiler_params=_cparams("parallel", "parallel"),
        name="qkv_proj",
    )(x, sc, sh, w_bf, cos, slo, shi)


def _kv_ctx_kernel(x_ref, sc_ref, sh_ref, w_ref, k_ref, v_ref):
    d = x_ref.shape[-1]
    h = (x_ref[0] * (1.0 + sc_ref[0]) + sh_ref[0]).astype(BF16)
    k = jnp.dot(h, w_ref[:, 0:d], preferred_element_type=F32)
    v = jnp.dot(h, w_ref[:, d:2 * d], preferred_element_type=F32)
    for hd in range(ATT_HEADS):
        k_ref[0, hd] = k[:, hd * LANES:(hd + 1) * LANES].astype(BF16)
        v_ref[0, hd] = v[:, hd * LANES:(hd + 1) * LANES].T.astype(BF16)


def _kv_ctx_proj(ctx, sc, sh, w_kv_bf):
    b, c, d = ctx.shape
    hd_shape = jax.ShapeDtypeStruct((b, ATT_HEADS, c, LANES), BF16)
    vec = pl.BlockSpec((1, 1, d), lambda bi: (0, 0, 0))
    out = pl.BlockSpec((1, ATT_HEADS, c, LANES), lambda bi: (bi, 0, 0, 0))
    out_t = pl.BlockSpec((1, ATT_HEADS, LANES, c), lambda bi: (bi, 0, 0, 0))
    return pl.pallas_call(
        _kv_ctx_kernel,
        out_shape=(hd_shape, jax.ShapeDtypeStruct((b, ATT_HEADS, LANES, c), BF16)),
        grid=(b,),
        in_specs=[
            pl.BlockSpec((1, c, d), lambda bi: (bi, 0, 0)),
            vec, vec,
            pl.BlockSpec((d, 2 * d), lambda bi: (0, 0)),
        ],
        out_specs=(out, out_t),
        compiler_params=_cparams("parallel"),
        name="kv_ctx_proj",
    )(ctx, sc, sh, w_kv_bf)


def _attn_finish(acc_a, l_a, acc_b, l_b, lamp_ref, g_ref, lam_init):
    lp = lamp_ref[...]
    lam = (jnp.exp(jnp.sum(lp[0:1] * lp[1:2], axis=-1, keepdims=True))
           - jnp.exp(jnp.sum(lp[2:3] * lp[3:4], axis=-1, keepdims=True)) + lam_init)
    o = acc_a / l_a - lam * (acc_b / l_b)
    o = o * lax.rsqrt(jnp.mean(o * o, axis=0, keepdims=True) + LN_EPS) * g_ref[...] * (1.0 - lam_init)
    return o.T.astype(BF16)


def _attn_kernel(q_ref, k_ref, vt_ref, kc_ref, vct_ref, lamp_ref, g_ref, o_ref,
                 s00, s01, s10, s11, p0, p1, acc0, acc1, *, tk, lam_init):
    q = q_ref[0, 0]
    tq = q.shape[0]
    half = LANES // 2
    lane = lax.broadcasted_iota(jnp.int32, q.shape, 1)
    zero = jnp.zeros_like(q)
    qs = (jnp.where(lane < half, q, zero), jnp.where(lane >= half, q, zero))
    n_chunks = k_ref.shape[2] // tk
    s_scr = ((s00, s01), (s10, s11))
    p_scr, acc_scr = (p0, p1), (acc0, acc1)
    nt = (((1,), (1,)), ((), ()))

    def scores(slot, kc):
        width = kc.shape[0]
        for mp in range(2):
            s_scr[slot][mp][0:width, :] = lax.dot_general(kc, qs[mp], nt, preferred_element_type=F32)

    def absorb(slot, vct, shift, sums):
        width = vct.shape[1]
        sums = list(sums)
        for mp in range(2):
            part = sums[mp]
            for r in range(width // ATT_STRIP):
                lo = r * ATT_STRIP
                tiles = [jnp.exp2(s_scr[slot][mp][lo + 8 * u:lo + 8 * (u + 1), :] - shift[mp])
                         for u in range(ATT_STRIP // 8)]
                p_scr[mp][lo:lo + ATT_STRIP, :] = jnp.concatenate(tiles, axis=0).astype(BF16)
                part = part + functools.reduce(lambda x, y: x + y, tiles)
            sums[mp] = part
            acc_scr[mp][...] += jnp.dot(vct, p_scr[mp][0:width, :], preferred_element_type=F32)
        return tuple(sums)

    def k_chunk(j):
        return k_ref[0, 0, pl.ds(pl.multiple_of(j * tk, tk), tk), :]

    def vt_chunk(j):
        return vt_ref[0, 0, :, pl.ds(pl.multiple_of(j * tk, tk), tk)]

    scores(0, k_chunk(0))
    shift = tuple(jnp.broadcast_to(jnp.max(s_scr[0][mp][...], axis=0, keepdims=True), (8, tq)) for mp in range(2))
    for mp in range(2):
        acc_scr[mp][...] = jnp.zeros((LANES, tq), F32)
    sums = (jnp.zeros((8, tq), F32), jnp.zeros((8, tq), F32))

    def pair(jj, sums):
        j0 = 2 * jj
        scores(1, k_chunk(j0 + 1))
        sums = absorb(0, vt_chunk(j0), shift, sums)
        scores(0, k_chunk(j0 + 2))
        return absorb(1, vt_chunk(j0 + 1), shift, sums)

    sums = lax.fori_loop(0, n_chunks // 2 - 1, pair, sums)
    scores(1, k_chunk(n_chunks - 1))
    sums = absorb(0, vt_chunk(n_chunks - 2), shift, sums)
    scores(0, kc_ref[0, 0])
    sums = absorb(1, vt_chunk(n_chunks - 1), shift, sums)
    sums = absorb(0, vct_ref[0, 0], shift, sums)

    tot = [jnp.sum(sums[mp], axis=0, keepdims=True) for mp in range(2)]
    bad = sum(jnp.sum(jnp.where(jnp.isfinite(x), 0.0, 1.0)) for x in (tot[0], tot[1], acc0[...], acc1[...]))

    @pl.when(bad == 0.0)
    def _():
        o_ref[0] = _attn_finish(acc0[...], tot[0], acc1[...], tot[1], lamp_ref, g_ref, lam_init)

    @pl.when(bad != 0.0)
    def _():
        def update(carry, kc, vct):
            new = []
            for mp in range(2):
                m, l, acc = carry[mp]
                s = lax.dot_general(kc, qs[mp], nt, preferred_element_type=F32)
                mn = jnp.maximum(m, jnp.max(s, axis=0, keepdims=True))
                a = jnp.exp2(m - mn)
                p = jnp.exp2(s - mn)
                new.append((mn, a * l + jnp.sum(p, axis=0, keepdims=True),
                            a * acc + jnp.dot(vct, p.astype(BF16), preferred_element_type=F32)))
            return tuple(new)

        init = tuple((jnp.full((1, tq), NEG_INF, F32), jnp.zeros((1, tq), F32), jnp.zeros((LANES, tq), F32))
                     for _ in range(2))
        carry = lax.fori_loop(0, n_chunks, lambda j, c: update(c, k_chunk(j), vt_chunk(j)), init)
        (_, l_a, acc_a), (_, l_b, acc_b) = update(carry, kc_ref[0, 0], vct_ref[0, 0])
        o_ref[0] = _attn_finish(acc_a, l_a, acc_b, l_b, lamp_ref, g_ref, lam_init)


def _diff_attention(q, k, vt, kc, vct, lam_p, subln_g, lam_init, tq, tk):
    b, h, n, _ = q.shape
    c = kc.shape[2]
    assert n % tk == 0 and (n // tk) % 2 == 0 and c <= tk and tk % ATT_STRIP == 0 and c % ATT_STRIP == 0
    spec = lambda r, cols: pl.BlockSpec((1, 1, r, cols), lambda bi, hi, i: (bi, hi, 0, 0))
    return pl.pallas_call(
        functools.partial(_attn_kernel, tk=tk, lam_init=lam_init),
        out_shape=jax.ShapeDtypeStruct((b, n, h * LANES), BF16),
        grid=(b, h, n // tq),
        in_specs=[
            pl.BlockSpec((1, 1, tq, LANES), lambda bi, hi, i: (bi, hi, i, 0)),
            spec(n, LANES), spec(LANES, n), spec(c, LANES), spec(LANES, c),
            pl.BlockSpec(lam_p.shape, lambda bi, hi, i: (0, 0)),
            pl.BlockSpec((LANES, 1), lambda bi, hi, i: (0, 0)),
        ],
        out_specs=pl.BlockSpec((1, tq, LANES), lambda bi, hi, i: (bi, i, hi)),
        scratch_shapes=[
            *[pltpu.VMEM((tk, tq), F32)] * 4,
            *[pltpu.VMEM((tk, tq), BF16)] * 2,
            *[pltpu.VMEM((LANES, tq), F32)] * 2,
        ],
        compiler_params=_cparams("parallel", "parallel", "parallel"),
        name="diff_attention",
    )(q, k, vt, kc, vct, lam_p, subln_g.reshape(LANES, 1))


def _sgu_kernel(x_ref, sc_ref, sh_ref, w_ref, b_ref, ng_ref, nb_ref, ws_ref, bs_ref, t_ref):
    f = t_ref.shape[-1]
    cg = f // SGU_GROUPS
    tm = x_ref.shape[1]
    h = (x_ref[0] * (1.0 + sc_ref[0]) + sh_ref[0]).astype(BF16)
    z = jnp.dot(h, w_ref[...], preferred_element_type=F32) + b_ref[...]
    z = 0.5 * z * (1.0 + lax.erf(z * (2.0 ** -0.5)))
    u = z[:, :f]
    v = _layer_norm(z[:, f:], ng_ref[...], nb_ref[...]).astype(BF16)
    for c in range(tm // SGU_CHUNK):
        rows = slice(c * SGU_CHUNK, (c + 1) * SGU_CHUNK)
        for g in range(SGU_GROUPS):
            cols = slice(g * cg, (g + 1) * cg)
            vm = jnp.dot(ws_ref[g], v[rows, cols], preferred_element_type=F32) + bs_ref[:, g:g + 1]
            t_ref[0, rows, cols] = (u[rows, cols] * vm).astype(BF16)


def _sgu_mixer(x, sc, sh, w_in_bf, b_in, norm_g, norm_b, w_s_bf, b_s_t, tm):
    b, n, d = x.shape
    f2 = w_in_bf.shape[1]
    f = f2 // 2
    vec = pl.BlockSpec((1, 1, d), lambda bi, i: (bi, 0, 0))
    full2 = lambda a: pl.BlockSpec(a.shape, lambda bi, i: (0,) * a.ndim)
    b_in2, ng2, nb2 = b_in.reshape(1, f2), norm_g.reshape(1, f), norm_b.reshape(1, f)
    return pl.pallas_call(
        _sgu_kernel,
        out_shape=jax.ShapeDtypeStruct((b, n, f), BF16),
        grid=(b, n // tm),
        in_specs=[
            pl.BlockSpec((1, tm, d), lambda bi, i: (bi, i, 0)),
            vec, vec,
            full2(w_in_bf), full2(b_in2), full2(ng2), full2(nb2), full2(w_s_bf), full2(b_s_t),
        ],
        out_specs=pl.BlockSpec((1, tm, f), lambda bi, i: (bi, i, 0)),
        compiler_params=_cparams("parallel", "parallel"),
        name="sgu_mixer",
    )(x, sc, sh, w_in_bf, b_in2, ng2, nb2, w_s_bf, b_s_t)


def _post_kernel(pre_ref, w_ref, x_ref, gm_ref, lg_ref, lb_ref, scf_ref, shf_ref, x1_ref, hf_ref, *, alpha):
    y = jnp.dot(pre_ref[0], w_ref[...], preferred_element_type=F32)
    x1 = _layer_norm(alpha * x_ref[0] + gm_ref[0] * y, lg_ref[...], lb_ref[...])
    x1_ref[0] = x1
    hf_ref[0] = _pack_bf16_pairs(x1 * (1.0 + scf_ref[0]) + shf_ref[0])


def _post_mixer(pre, w_bf, x, gm, ln_g, ln_b, scf, shf, alpha, tm):
    b, n, d = x.shape
    kd = pre.shape[-1]
    vec = pl.BlockSpec((1, 1, d), lambda bi, i: (bi, 0, 0))
    row = pl.BlockSpec((1, d), lambda bi, i: (0, 0))
    tile = pl.BlockSpec((1, tm, d), lambda bi, i: (bi, i, 0))
    return pl.pallas_call(
        functools.partial(_post_kernel, alpha=alpha),
        out_shape=(jax.ShapeDtypeStruct((b, n, d), F32), jax.ShapeDtypeStruct((b, n, d // 2), I32)),
        grid=(b, n // tm),
        in_specs=[
            pl.BlockSpec((1, tm, kd), lambda bi, i: (bi, i, 0)),
            pl.BlockSpec((kd, d), lambda bi, i: (0, 0)),
            tile, vec, row, row, vec, vec,
        ],
        out_specs=(tile, pl.BlockSpec((1, tm, d // 2), lambda bi, i: (bi, i, 0))),
        compiler_params=_cparams("parallel", "parallel"),
        name="post_mixer",
    )(pre, w_bf, x, gm, ln_g.reshape(1, d), ln_b.reshape(1, d), scf, shf)


def _route_select(scores, choice):
    e, w = scores.shape
    ge = e // N_GROUPS
    g3 = choice.reshape(N_GROUPS, ge, w)
    ri = lax.broadcasted_iota(jnp.int32, g3.shape, 1).astype(F32)
    m1 = jnp.max(g3, axis=1, keepdims=True)
    first = jnp.min(jnp.where(g3 == m1, ri, float(ge)), axis=1, keepdims=True)
    m2 = jnp.max(jnp.where(ri == first, NEG_INF, g3), axis=1, keepdims=True)
    gs = m1 + m2

    gi = lax.broadcasted_iota(jnp.int32, gs.shape, 0).astype(F32)
    gsel = jnp.zeros(gs.shape, F32)
    cur = gs
    for _ in range(TOPK_GROUPS):
        m = jnp.max(cur, axis=0, keepdims=True)
        f = jnp.min(jnp.where(cur == m, gi, float(N_GROUPS)), axis=0, keepdims=True)
        hit = gi == f
        gsel = jnp.where(hit, 1.0, gsel)
        cur = jnp.where(hit, NEG_INF, cur)
    emask = jnp.broadcast_to(gsel, g3.shape).reshape(e, w)
    masked = jnp.where(emask > 0.5, choice, NEG_INF)

    ei = lax.broadcasted_iota(jnp.int32, (e, w), 0).astype(F32)
    onehot = jnp.zeros((e, w), F32)
    idxs, ws = [], []
    for _ in range(TOP_K):
        m = jnp.max(masked, axis=0, keepdims=True)
        f = jnp.min(jnp.where(masked == m, ei, float(e)), axis=0, keepdims=True)
        hit = ei == f
        idxs.append(f)
        ws.append(jnp.sum(jnp.where(hit, scores, 0.0), axis=0, keepdims=True))
        masked = jnp.where(hit, NEG_INF, masked)
        onehot = jnp.where(hit, 1.0, onehot)
    return idxs, ws, onehot


def _route_kernel(x_ref, sc_ref, sh_ref, wr_ref, rb_ref, idx_ref, w_ref, rank_ref, cnt_ref, carry_ref):
    i = pl.program_id(0)
    e = wr_ref.shape[1]
    tm = x_ref.shape[0]

    @pl.when(i == 0)
    def _():
        carry_ref[...] = jnp.zeros_like(carry_ref)

    h = x_ref[...] * (1.0 + sc_ref[0]) + sh_ref[0]
    h_hi = h.astype(BF16)
    h_lo = (h - h_hi.astype(F32)).astype(BF16)
    nt = (((1,), (1,)), ((), ()))
    logits = (lax.dot_general(wr_ref[0], h_hi, nt, preferred_element_type=F32)
              + (lax.dot_general(wr_ref[0], h_lo, nt, preferred_element_type=F32)
                 + lax.dot_general(wr_ref[1], h_hi, nt, preferred_element_type=F32)))
    scores = 1.0 / (1.0 + jnp.exp(-logits))
    choice = scores + rb_ref[...]

    slab = min(ROUTE_SLAB, tm)
    slabs = [slice(j * slab, (j + 1) * slab) for j in range(tm // slab)]
    picks = [_route_select(scores[:, sl], choice[:, sl]) for sl in slabs]
    onehot = jnp.concatenate([pk[2] for pk in picks], axis=1)

    r_i = lax.broadcasted_iota(jnp.int32, (tm, tm), 0)
    c_i = lax.broadcasted_iota(jnp.int32, (tm, tm), 1)
    upper = jnp.where(r_i < c_i, 1.0, 0.0).astype(BF16)
    rk = jnp.dot(onehot.astype(BF16), upper, preferred_element_type=F32) + carry_ref[...]
    carry_ref[...] += jnp.sum(onehot, axis=1, keepdims=True)

    ei = lax.broadcasted_iota(jnp.int32, (e, slab), 0).astype(F32)
    for sl, (idxs, ws, _) in zip(slabs, picks):
        wsum = functools.reduce(lambda x, y: x + y, ws)
        for k in range(TOP_K):
            idx_ref[k:k + 1, sl] = idxs[k].astype(jnp.int32)
            w_ref[k:k + 1, sl] = ws[k] / wsum * ROUTED_SCALE
            rank_ref[k:k + 1, sl] = jnp.sum(jnp.where(ei == idxs[k], rk[:, sl], 0.0), axis=0,
                                            keepdims=True).astype(jnp.int32)
    cnt_ref[...] = jnp.broadcast_to(carry_ref[...], cnt_ref.shape).astype(jnp.int32)


def _route(x1, scf, shf, wr_t, rbias, tm):
    b, n, d = x1.shape
    t = b * n
    e = wr_t.shape[0]
    per_b = n // tm
    wr_hi = wr_t.astype(BF16)
    wr_split = jnp.stack([wr_hi, (wr_t - wr_hi.astype(F32)).astype(BF16)])
    vec = pl.BlockSpec((1, 1, d), lambda i: (i // per_b, 0, 0))
    out_t = pl.BlockSpec((TOP_K, tm), lambda i: (0, i))
    return pl.pallas_call(
        _route_kernel,
        out_shape=(jax.ShapeDtypeStruct((TOP_K, t), jnp.int32), jax.ShapeDtypeStruct((TOP_K, t), F32),
                   jax.ShapeDtypeStruct((TOP_K, t), jnp.int32), jax.ShapeDtypeStruct((e, LANES), jnp.int32)),
        grid=(t // tm,),
        in_specs=[
            pl.BlockSpec((tm, d), lambda i: (i, 0)),
            vec, vec,
            pl.BlockSpec((2, e, d), lambda i: (0, 0, 0)),
            pl.BlockSpec((e, 1), lambda i: (0, 0)),
        ],
        out_specs=(out_t, out_t, out_t, pl.BlockSpec((e, LANES), lambda i: (0, 0))),
        scratch_shapes=[pltpu.VMEM((e, 1), F32)],
        compiler_params=_cparams("arbitrary"),
        name="route",
    )(x1.reshape(t, d), scf, shf, wr_split, rbias.reshape(e, 1))


def _slot_kernel(idx_ref, rank_ref, ps_ref, pos_ref):
    e, w = ps_ref.shape[0], idx_ref.shape[1]
    ei = lax.broadcasted_iota(jnp.int32, (e, w), 0)
    starts = ps_ref[...]
    for k in range(TOP_K):
        base = jnp.sum(jnp.where(ei == idx_ref[k:k + 1, :], starts, 0.0), axis=0, keepdims=True)
        pos_ref[k:k + 1, :] = base.astype(jnp.int32) + rank_ref[k:k + 1, :]


def _slot_positions(idx_t, rank_t, pstarts):
    k, t = idx_t.shape
    e = pstarts.shape[0]
    w = min(SLOT_TILE, t)
    tile = pl.BlockSpec((k, w), lambda i: (0, i))
    return pl.pallas_call(
        _slot_kernel,
        out_shape=jax.ShapeDtypeStruct((k, t), I32),
        grid=(t // w,),
        in_specs=[tile, tile, pl.BlockSpec((e, 1), lambda i: (0, 0))],
        out_specs=tile,
        compiler_params=_cparams("parallel"),
        name="slot_positions",
    )(idx_t, rank_t, pstarts.astype(F32).reshape(e, 1))


def _gather_rows(table, idx):
    m = idx.shape[0]
    w = table.shape[1]
    workers = SC_CORES * SC_SUBCORES
    n_ch = m // (workers * SC_CHUNK)
    assert m % (workers * SC_CHUNK) == 0 and n_ch % 2 == 0
    mesh = plsc.VectorSubcoreMesh(core_axis_name="c", subcore_axis_name="s",
                                  num_cores=SC_CORES, num_subcores=SC_SUBCORES)

    @functools.partial(
        pl.kernel, mesh=mesh,
        out_type=jax.ShapeDtypeStruct((m, w), table.dtype),
        scratch_types=[
            pltpu.VMEM((n_ch, SC_CHUNK), I32),
            pltpu.VMEM((SC_CHUNK, w), table.dtype), pltpu.VMEM((SC_CHUNK, w), table.dtype),
            pltpu.SemaphoreType.DMA, pltpu.SemaphoreType.DMA, pltpu.SemaphoreType.DMA, pltpu.SemaphoreType.DMA,
        ],
        name="sc_gather_rows",
    )
    def gather(table_hbm, idx_hbm, out_hbm, idx_all, buf0, buf1, gsem0, gsem1, wsem0, wsem1):
        first = (lax.axis_index("s") * SC_CORES + lax.axis_index("c")) * n_ch
        bufs, gsem, wsem = (buf0, buf1), (gsem0, gsem1), (wsem0, wsem1)
        pltpu.sync_copy(idx_hbm.at[pl.ds(first, n_ch)], idx_all)

        def gather_copy(j, s):
            return pltpu.make_async_copy(table_hbm.at[idx_all.at[j]], bufs[s], gsem[s])

        def write_copy(j, s):
            rows = pl.ds(pl.multiple_of((first + j) * SC_CHUNK, SC_CHUNK), SC_CHUNK)
            return pltpu.make_async_copy(bufs[s], out_hbm.at[rows], wsem[s])

        gather_copy(0, 0).start()

        @pl.loop(0, n_ch, step=2)
        def _(jj):
            for s in range(2):
                j = jj + s

                @pl.when(j >= 1)
                def _():
                    write_copy(j - 1, 1 - s).wait()

                @pl.when(j + 1 < n_ch)
                def _():
                    gather_copy(j + 1, 1 - s).start()

                gather_copy(j, s).wait()
                write_copy(j, s).start()

        write_copy(n_ch - 1, 1).wait()

    return gather(table, idx.reshape(m // SC_CHUNK, SC_CHUNK))


def _scatter_rows(src, pos3, p):
    w = src.shape[1]
    n_chunks, k, ch = pos3.shape
    workers = SC_CORES * SC_SUBCORES
    per_w = n_chunks // workers
    assert ch == SC_SCATTER_CHUNK and n_chunks % workers == 0 and src.shape[0] == n_chunks * ch
    mesh = plsc.VectorSubcoreMesh(core_axis_name="c", subcore_axis_name="s",
                                  num_cores=SC_CORES, num_subcores=SC_SUBCORES)

    @functools.partial(
        pl.kernel, mesh=mesh,
        out_type=jax.ShapeDtypeStruct((p, w), src.dtype),
        scratch_types=[
            pltpu.VMEM((k, ch), I32),
            pltpu.VMEM((ch, w), src.dtype),
            pltpu.SemaphoreType.DMA,
        ],
        name="sc_scatter_rows",
    )
    def scatter(src_hbm, pos_hbm, out_hbm, idx_v, rows_v, sem):
        first = (lax.axis_index("s") * SC_CORES + lax.axis_index("c")) * per_w

        @pl.loop(0, per_w)
        def _(j):
            c = first + j
            pltpu.sync_copy(pos_hbm.at[c], idx_v)
            pltpu.sync_copy(src_hbm.at[pl.ds(pl.multiple_of(c * ch, ch), ch)], rows_v)
            copies = [pltpu.async_copy(rows_v, out_hbm.at[idx_v.at[kk]], sem) for kk in range(k)]
            for cp in copies:
                cp.wait()

    return scatter(src, pos3)


def _experts_kernel(ps_ref, nb_ref, cnt_ref, nt_ref, xs_hbm, wg_ref, wu_ref, wd_ref, y_hbm,
                    xbuf, ybuf, in_sem, out_sem, wg_bf, wu_bf, wd_bf):
    e = pl.program_id(0)
    nb, cnt, n_total = nb_ref[e], cnt_ref[e], nt_ref[0]
    g0 = ps_ref[e] // MOE_ROWS

    def in_copy(g):
        rows = pl.ds(pl.multiple_of(g * MOE_ROWS, MOE_ROWS), MOE_ROWS)
        return pltpu.make_async_copy(xs_hbm.at[rows], xbuf.at[g % MOE_SLOTS], in_sem.at[g % MOE_SLOTS])

    def out_copy(g):
        rows = pl.ds(pl.multiple_of(g * MOE_ROWS, MOE_ROWS), MOE_ROWS)
        return pltpu.make_async_copy(ybuf.at[g % MOE_SLOTS], y_hbm.at[rows], out_sem.at[g % MOE_SLOTS])

    @pl.when(e == 0)
    def _():
        for j in range(MOE_LOOKAHEAD):
            @pl.when(j < n_total)
            def _():
                in_copy(j).start()

    @pl.when(nb > 0)
    def _():
        wg_bf[...] = wg_ref[0, 0].astype(BF16)
        wu_bf[...] = wu_ref[0, 0].astype(BF16)
        wd_bf[...] = wd_ref[0, 0].astype(BF16)

        def process(b, width):
            g = g0 + b
            for u in range(width):
                @pl.when(g + u + MOE_LOOKAHEAD < n_total)
                def _():
                    in_copy(g + u + MOE_LOOKAHEAD).start()

            for u in range(width):
                in_copy(g + u).wait()

                @pl.when(g + u >= MOE_SLOTS)
                def _():
                    out_copy(g + u - MOE_SLOTS).wait()

            packed = jnp.concatenate([xbuf[(g + u) % MOE_SLOTS] for u in range(width)], axis=0)
            row = lax.broadcasted_iota(I32, (width * MOE_ROWS, 1), 0) + b * MOE_ROWS
            x_lo, x_hi = (v.astype(BF16) for v in _unpack_bf16_pairs(jnp.where(row < cnt, packed, 0)))
            half = x_lo.shape[1]

            def up(w_bf):
                return (jnp.dot(x_lo, w_bf[:half, :], preferred_element_type=F32)
                        + jnp.dot(x_hi, w_bf[half:, :], preferred_element_type=F32))

            hb = (_silu(up(wg_bf)) * up(wu_bf)).astype(BF16)
            y = _pack_bf16_pairs(jnp.dot(hb, wd_bf[...], preferred_element_type=F32))
            for u in range(width):
                ybuf[(g + u) % MOE_SLOTS] = y[u * MOE_ROWS:(u + 1) * MOE_ROWS]
                out_copy(g + u).start()

        start = 0
        for width in MOE_GROUPS:
            count = (nb - start) // width

            def body(i, c, width=width, start=start):
                process(start + width * i, width)
                return c

            lax.fori_loop(0, count, body, 0)
            start = start + count * width

    @pl.when(e == pl.num_programs(0) - 1)
    def _():
        for j in range(MOE_SLOTS):
            @pl.when(n_total - 1 - j >= 0)
            def _():
                out_copy(n_total - 1 - j).wait()


def _routed_experts(xs, wg, wu, wd, layer, pstarts, nblk, counts, n_total):
    p, dp = xs.shape
    _, e, d, f = wg.shape
    wspec = lambda r, c: pl.BlockSpec((1, 1, r, c), lambda i, ps, nb, cnt, nt: (layer, i, 0, 0))
    grid_spec = pltpu.PrefetchScalarGridSpec(
        num_scalar_prefetch=4,
        grid=(e,),
        in_specs=[pl.BlockSpec(memory_space=pl.ANY), wspec(d, f), wspec(d, f), wspec(f, d)],
        out_specs=pl.BlockSpec(memory_space=pl.ANY),
        scratch_shapes=[
            pltpu.VMEM((MOE_SLOTS, MOE_ROWS, dp), I32), pltpu.VMEM((MOE_SLOTS, MOE_ROWS, dp), I32),
            pltpu.SemaphoreType.DMA((MOE_SLOTS,)), pltpu.SemaphoreType.DMA((MOE_SLOTS,)),
            pltpu.VMEM((d, f), BF16), pltpu.VMEM((d, f), BF16), pltpu.VMEM((f, d), BF16),
        ],
    )
    return pl.pallas_call(
        _experts_kernel,
        out_shape=jax.ShapeDtypeStruct((p, dp), I32),
        grid_spec=grid_spec,
        compiler_params=_cparams("arbitrary"),
        name="routed_experts",
    )(pstarts, nblk, counts, n_total, xs, wg, wu, wd)


def _combine_kernel(yg_ref, w_ref, hf_ref, sg_ref, su_ref, sd_ref, x_ref, gf_ref, lg_ref, lb_ref, o_ref, *, alpha):
    w = w_ref[...]
    r_lo, r_hi = _unpack_bf16_pairs(yg_ref[0])
    r_lo, r_hi = w[:, 0:1] * r_lo, w[:, 0:1] * r_hi
    for k in range(1, TOP_K):
        y_lo, y_hi = _unpack_bf16_pairs(yg_ref[k])
        r_lo, r_hi = r_lo + w[:, k:k + 1] * y_lo, r_hi + w[:, k:k + 1] * y_hi
    routed = jnp.concatenate([r_lo, r_hi], axis=1)
    hf = jnp.concatenate(_unpack_bf16_pairs(hf_ref[...]), axis=1).astype(BF16)
    g = jnp.dot(hf, sg_ref[...], preferred_element_type=F32)
    u = jnp.dot(hf, su_ref[...], preferred_element_type=F32)
    shared = jnp.dot((_silu(g) * u).astype(BF16), sd_ref[...], preferred_element_type=F32)
    o_ref[...] = _layer_norm(alpha * x_ref[...] + gf_ref[0] * (routed + shared), lg_ref[...], lb_ref[...])


def _combine(yg, w_tk, hf, sg_bf, su_bf, sd_bf, x1, gf, ln_g, ln_b, alpha, tm, per_b):
    t, d = x1.shape
    f = sg_bf.shape[1]
    row = pl.BlockSpec((1, d), lambda i: (0, 0))
    tile = pl.BlockSpec((tm, d), lambda i: (i, 0))
    return pl.pallas_call(
        functools.partial(_combine_kernel, alpha=alpha),
        out_shape=jax.ShapeDtypeStruct((t, d), F32),
        grid=(t // tm,),
        in_specs=[
            pl.BlockSpec((TOP_K, tm, d // 2), lambda i: (0, i, 0)),
            pl.BlockSpec((tm, TOP_K), lambda i: (i, 0)),
            pl.BlockSpec((tm, d // 2), lambda i: (i, 0)),
            pl.BlockSpec((d, f), lambda i: (0, 0)),
            pl.BlockSpec((d, f), lambda i: (0, 0)),
            pl.BlockSpec((f, d), lambda i: (0, 0)),
            tile,
            pl.BlockSpec((1, 1, d), lambda i: (i // per_b, 0, 0)),
            row, row,
        ],
        out_specs=tile,
        compiler_params=_cparams("parallel"),
        name="moe_combine",
    )(yg, w_tk, hf, sg_bf, su_bf, sd_bf, x1, gf, ln_g.reshape(1, d), ln_b.reshape(1, d))


def _moe_layer(x1, hf, scf, shf, gf, router_w, router_bias, wg, wu, wd, layer, sg, su, sd, ln_g, ln_b, alpha, tm):
    b, n, d = x1.shape
    t = b * n
    e = router_w.shape[1]
    idx_t, w_t, rank_t, cnt = _route(x1, scf, shf, router_w.T, router_bias, tm)

    counts = cnt[:, 0]
    padded = (counts + MOE_ROWS - 1) // MOE_ROWS * MOE_ROWS
    pends = jnp.cumsum(padded)
    pstarts = pends - padded
    pos_t = _slot_positions(idx_t, rank_t, pstarts)
    p = t * TOP_K + e * MOE_ROWS
    pos3 = pos_t.reshape(TOP_K, t // SC_SCATTER_CHUNK, SC_SCATTER_CHUNK).transpose(1, 0, 2)

    hf2 = hf.reshape(t, d // 2)
    xs = _scatter_rows(hf2, pos3, p)
    yb = _routed_experts(xs, wg, wu, wd, layer, pstarts.astype(I32), (padded // MOE_ROWS).astype(I32), counts,
                         (pends[-1:] // MOE_ROWS).astype(I32))
    yg = _gather_rows(yb, pos_t.reshape(-1)).reshape(TOP_K, t, d // 2)
    out = _combine(yg, w_t.T, hf2, sg.astype(BF16), su.astype(BF16), sd.astype(BF16), x1.reshape(t, d), gf,
                   ln_g, ln_b, alpha, tm, n // tm)
    return out.reshape(b, n, d)


def _rope_tables(n):
    rows = n // GRID_W
    row_pos = jnp.repeat(jnp.arange(rows, dtype=F32), GRID_W)
    col_pos = jnp.tile(jnp.arange(GRID_W, dtype=F32), rows)
    half = LANES // 4
    lane = jnp.arange(LANES)
    in_blk = lane % half
    freq = ROPE_THETA ** (-(2.0 * (in_blk % (half // 2)).astype(F32)) / half)
    use_col = (lane // half) % 2 == 1
    pos = jnp.where(use_col[None, :], col_pos[:, None], row_pos[:, None])
    ang = pos * freq[None, :]
    lo = (in_blk < half // 2)[None, :]
    sin = jnp.sin(ang)
    return jnp.cos(ang), jnp.where(lo, -sin, 0.0), jnp.where(lo, 0.0, sin)


def kernel(x, c, ctx, c_ctx, w_mod, b_mod, ln_g, ln_b, attn_w_in, attn_w_out, attn_lambda, attn_subln_g,
           sgu_w_in, sgu_b_in, sgu_norm_g, sgu_norm_b, sgu_w_s, sgu_b_s, sgu_w_out,
           router_w, router_bias, exp_w_gate, exp_w_up, exp_w_down, sh_w_gate, sh_w_up, sh_w_down):
    b, n, d = x.shape
    depth = w_mod.shape[0]
    assert b <= 7 and d == ATT_HEADS * LANES and n % GRID_W == 0
    alpha = (2 * depth) ** 0.25
    head_dim = d // ATT_HEADS // 2
    tm = TOKEN_TILE if n % TOKEN_TILE == 0 else TOKEN_TILE // 2

    cs = jnp.zeros((8, d), F32).at[:b].set(c).at[b].set(c_ctx)
    mods = _modulation(cs, w_mod, b_mod)

    def mod_vec(i, j):
        return mods[i, :, j * d:(j + 1) * d].reshape(8, 1, d)

    for i in range(depth):
        sh_m, sc_m, g_m, sh_f, sc_f, g_f = (mod_vec(i, j) for j in range(6))
        if i % N_MIXERS == 0:
            a = i // N_MIXERS
            lam_init = 0.8 - 0.6 * math.exp(-0.3 * i)
            w_in_bf = attn_w_in[a].astype(BF16)
            cos, slo, shi = _rope_tables(n)
            q_scale = head_dim ** -0.5 * math.log2(math.e)
            q, k, vt = _qkv_proj(x, sc_m, sh_m, w_in_bf, cos, slo, shi, q_scale, tm)
            kc, vct = _kv_ctx_proj(ctx, sc_m[b:b + 1], sh_m[b:b + 1], w_in_bf[:, d:])
            pre = _diff_attention(q, k, vt, kc, vct, attn_lambda[a], attn_subln_g[a], lam_init,
                                  min(ATT_QUERY_TILE, n), min(ATT_KEY_CHUNK, n // 2))
            w_out_bf = attn_w_out[a].astype(BF16)
        else:
            s = i // N_MIXERS
            pre = _sgu_mixer(x, sc_m, sh_m, sgu_w_in[s].astype(BF16), sgu_b_in[s], sgu_norm_g[s], sgu_norm_b[s],
                             sgu_w_s[s].astype(BF16), sgu_b_s[s].T, tm)
            w_out_bf = sgu_w_out[s].astype(BF16)
        x1, hf = _post_mixer(pre, w_out_bf, x, g_m, ln_g[i, 0], ln_b[i, 0], sc_f, sh_f, alpha, tm)
        x = _moe_layer(x1, hf, sc_f, sh_f, g_f, router_w[i], router_bias[i], exp_w_gate, exp_w_up, exp_w_down, i,
                       sh_w_gate[i], sh_w_up[i], sh_w_down[i], ln_g[i, 1], ln_b[i, 1], alpha, tm)
    return x
```
